```python
import math
import jax, jax.numpy as jnp
from jax import lax
import numpy as np

D_MODEL = 1024
BATCH = 8
SEQ = 2048
DEPTH = 1
DEC_BATCH = 128
DEC_SEQ = 8
PAST_LEN = 8192
PAGE_SIZE = 128

HEAD_DIM = 64
N_Q_HEADS = (D_MODEL // 2) // HEAD_DIM
N_KV_HEADS = N_Q_HEADS // 4
Q_PER_KV = N_Q_HEADS // N_KV_HEADS
ATTN_WIDTH = N_Q_HEADS * HEAD_DIM
KV_WIDTH = N_KV_HEADS * HEAD_DIM
WINDOW = 128
SSM_WIDTH = D_MODEL // 2
SSM_GROUP = 16
SSM_GROUPS = SSM_WIDTH // SSM_GROUP
SSM_STATE = 64
DT_MIN = 1e-3
DT_MAX = 1e-1
MIX_WIDTH = ATTN_WIDTH + SSM_WIDTH
IN_WIDTH = ATTN_WIDTH + 2 * KV_WIDTH + SSM_WIDTH
N_MEM = 256
CA_HEADS = 4
CA_HEAD_DIM = 128
CA_WIDTH = CA_HEADS * CA_HEAD_DIM
N_EXPERTS = 32
TOP_K = 4
D_FF = D_MODEL
SWIGLU_ALPHA = 1.702
SWIGLU_LIMIT = 7.0
MOE_BLOCK = 256
RMS_EPS = 1e-5

kernel_name = 'hymba_swa_s5_moe_decode_step'

F32 = jnp.float32


def rms_norm(x, g):
    xf = x.astype(F32)
    y = xf * lax.rsqrt(jnp.mean(xf * xf, axis=-1, keepdims=True) + RMS_EPS)
    return (y * g.astype(F32)).astype(x.dtype)


def sink_attend(q, k, v, mask, sink):
    s = jnp.einsum('...qhgd,...khd->...hgqk', q, k).astype(F32) * (HEAD_DIM ** -0.5)
    s = jnp.where(mask, s, -jnp.inf)
    sk = jnp.broadcast_to(sink.astype(F32)[:, :, None, None], s.shape[:-1] + (1,))
    p = jax.nn.softmax(jnp.concatenate([s, sk], axis=-1), axis=-1)[..., :-1]
    return jnp.einsum('...hgqk,...khd->...qhgd', p.astype(v.dtype), v)


def window_attn_prompt(q, k, v, sink):
    b, L = q.shape[:2]
    nb = L // WINDOW
    qb = q.reshape(b, nb, WINDOW, N_KV_HEADS, Q_PER_KV, HEAD_DIM)
    kb = k.reshape(b, nb, WINDOW, N_KV_HEADS, HEAD_DIM)
    vb = v.reshape(b, nb, WINDOW, N_KV_HEADS, HEAD_DIM)
    pad = ((0, 0), (1, 0), (0, 0), (0, 0), (0, 0))
    kk = jnp.concatenate([jnp.pad(kb, pad)[:, :-1], kb], axis=2)
    vv = jnp.concatenate([jnp.pad(vb, pad)[:, :-1], vb], axis=2)
    qi = jnp.arange(WINDOW)[:, None]
    kj = jnp.arange(2 * WINDOW)[None, :]
    rel = qi + WINDOW - kj
    band = (rel >= 0) & (rel < WINDOW)
    exists = (jnp.arange(nb)[:, None, None] > 0) | (kj >= WINDOW)[None]
    mask = band[None] & exists
    o = sink_attend(qb, kk, vv, mask[:, None, None], sink)
    return o.reshape(b, L, ATTN_WIDTH)


def window_attn_sample(q, k, v, buf_k, buf_v, sink):
    b, T = q.shape[:2]
    n = buf_k.shape[1]
    kk = jnp.concatenate([buf_k, k], axis=1)
    vv = jnp.concatenate([buf_v, v], axis=1)
    rel = jnp.arange(T)[:, None] - (jnp.arange(n + T)[None, :] - n)
    mask = (rel >= 0) & (rel < WINDOW)
    o = sink_attend(q, kk, vv, mask, sink)
    return o.reshape(b, T, ATTN_WIDTH), kk[:, -n:], vv[:, -n:]


def ssm_discretize(lam_re, lam_im, log_dt, b_re, b_im):
    lr = lam_re.astype(F32)
    li = lam_im.astype(F32)
    dt = jnp.exp(log_dt.astype(F32))[:, None]
    mag = jnp.exp(lr * dt)
    ang = li * dt
    ab_re = mag * jnp.cos(ang)
    ab_im = mag * jnp.sin(ang)
    nr = ab_re - 1.0
    ni = ab_im
    den = lr * lr + li * li
    f_re = (nr * lr + ni * li) / den
    f_im = (ni * lr - nr * li) / den
    br = b_re.astype(F32)
    bi = b_im.astype(F32)
    bb_re = f_re[..., None] * br - f_im[..., None] * bi
    bb_im = f_re[..., None] * bi + f_im[..., None] * br
    return ab_re, ab_im, bb_re, bb_im


def ssm_combine(e1, e2):
    a1r, a1i, b1r, b1i = e1
    a2r, a2i, b2r, b2i = e2
    return (a1r * a2r - a1i * a2i,
            a1r * a2i + a1i * a2r,
            a2r * b1r - a2i * b1i + b2r,
            a2r * b1i + a2i * b1r + b2i)


def ssm_mix(u, h0_re, h0_im, lam_re, lam_im, log_dt, b_re, b_im, c_re, c_im, d_skip, w_glu):
    b, L, _ = u.shape
    ab_re, ab_im, bb_re, bb_im = ssm_discretize(lam_re, lam_im, log_dt, b_re, b_im)
    ug = u.astype(F32).reshape(b, L, SSM_GROUPS, SSM_GROUP).transpose(1, 0, 2, 3)
    v_re = jnp.einsum('lbgc,gnc->lbgn', ug, bb_re)
    v_im = jnp.einsum('lbgc,gnc->lbgn', ug, bb_im)
    if h0_re is not None:
        hr = h0_re.astype(F32)
        hi = h0_im.astype(F32)
        v_re = v_re.at[0].add(ab_re * hr - ab_im * hi)
        v_im = v_im.at[0].add(ab_re * hi + ab_im * hr)
    a_re = jnp.broadcast_to(ab_re[None, None], (L, 1, SSM_GROUPS, SSM_STATE))
    a_im = jnp.broadcast_to(ab_im[None, None], (L, 1, SSM_GROUPS, SSM_STATE))
    _, _, h_re, h_im = lax.associative_scan(ssm_combine, (a_re, a_im, v_re, v_im), axis=0)
    y = (jnp.einsum('lbgn,gcn->lbgc', h_re, c_re.astype(F32))
         - jnp.einsum('lbgn,gcn->lbgc', h_im, c_im.astype(F32))
         + d_skip.astype(F32) * ug)
    y = y.transpose(1, 0, 2, 3).reshape(b, L, SSM_WIDTH)
    g = jax.nn.gelu(y)
    out = g * jax.nn.sigmoid(g @ w_glu.astype(F32))
    return out.astype(u.dtype), h_re[-1], h_im[-1]


def memory_kv(mem, g_mem, w_mk, w_mv):
    b = mem.shape[0]
    m = rms_norm(mem, g_mem)
    k = (m @ w_mk).reshape(b, N_MEM, CA_HEADS, CA_HEAD_DIM)
    v = (m @ w_mv).reshape(b, N_MEM, CA_HEADS, CA_HEAD_DIM)
    return k, v


def cross_attend(xn, mk, mv, w_cq, w_co):
    b, L, _ = xn.shape
    q = (xn @ w_cq).reshape(b, L, CA_HEADS, CA_HEAD_DIM)
    s = jnp.einsum('blhd,bmhd->bhlm', q, mk).astype(F32) * (CA_HEAD_DIM ** -0.5)
    p = jax.nn.softmax(s, axis=-1)
    o = jnp.einsum('bhlm,bmhd->blhd', p.astype(mv.dtype), mv).reshape(b, L, CA_WIDTH)
    return o @ w_co


def expert_ffn(xb, w_gate, b_gate, w_up, b_up, w_down, b_down):
    gate = jnp.minimum(xb @ w_gate + b_gate, SWIGLU_LIMIT)
    up = jnp.clip(xb @ w_up + b_up, -SWIGLU_LIMIT, SWIGLU_LIMIT)
    h = gate * jax.nn.sigmoid(SWIGLU_ALPHA * gate) * (up + 1.0)
    return h @ w_down + b_down


def moe(x, w_router, b_router, w_gate, b_gate, w_up, b_up, w_down, b_down):
    shp = x.shape
    xt = x.reshape(-1, D_MODEL)
    n = xt.shape[0]
    logits = (xt @ w_router + b_router).astype(F32)
    top_v, top_i = lax.top_k(logits, TOP_K)
    top_w = jax.nn.softmax(top_v, axis=-1)
    m = n * TOP_K
    flat_e = top_i.reshape(-1)
    order = jnp.argsort(flat_e)
    sorted_e = flat_e[order]
    counts = jnp.bincount(flat_e, length=N_EXPERTS)
    padded = (counts + MOE_BLOCK - 1) // MOE_BLOCK * MOE_BLOCK
    pad_end = jnp.cumsum(padded)
    pad_start = pad_end - padded
    start = jnp.cumsum(counts) - counts
    dest = pad_start[sorted_e] + jnp.arange(m) - start[sorted_e]
    n_blocks = -(-(m + N_EXPERTS * (MOE_BLOCK - 1)) // MOE_BLOCK)
    n_slots = n_blocks * MOE_BLOCK
    slot_tok = jnp.zeros((n_slots,), jnp.int32).at[dest].set((order // TOP_K).astype(jnp.int32))
    slot_w = jnp.zeros((n_slots,), F32).at[dest].set(top_w.reshape(-1)[order])
    blk_e = jnp.minimum(jnp.searchsorted(pad_end, jnp.arange(n_blocks) * MOE_BLOCK, side='right'),
                        N_EXPERTS - 1)
    xs = xt[slot_tok].reshape(n_blocks, MOE_BLOCK, D_MODEL)

    def run_block(args):
        xb, e = args
        return expert_ffn(xb, w_gate[e], b_gate[e], w_up[e], b_up[e], w_down[e], b_down[e])

    ys = lax.map(run_block, (xs, blk_e)).reshape(n_slots, D_MODEL)
    y = jnp.zeros((n, D_MODEL), F32).at[slot_tok].add(ys.astype(F32) * slot_w[:, None])
    return y.astype(x.dtype).reshape(shp)


def layer_forward(h, mem_k, mem_v, win_k, win_v, h0_re, h0_im, p):
    b, L, _ = h.shape
    a = rms_norm(h, p['g_mix'])
    proj = a @ p['w_in']
    q = proj[..., :ATTN_WIDTH].reshape(b, L, N_KV_HEADS, Q_PER_KV, HEAD_DIM)
    k = proj[..., ATTN_WIDTH:ATTN_WIDTH + KV_WIDTH].reshape(b, L, N_KV_HEADS, HEAD_DIM)
    v = proj[..., ATTN_WIDTH + KV_WIDTH:ATTN_WIDTH + 2 * KV_WIDTH].reshape(b, L, N_KV_HEADS, HEAD_DIM)
    u = proj[..., ATTN_WIDTH + 2 * KV_WIDTH:]
    sink = p['sink'].reshape(N_KV_HEADS, Q_PER_KV)
    if win_k is None:
        att = window_attn_prompt(q, k, v, sink)
        new_k, new_v = k[:, -WINDOW:], v[:, -WINDOW:]
    else:
        att, new_k, new_v = window_attn_sample(q, k, v, win_k, win_v, sink)
    ssm, s_re, s_im = ssm_mix(u, h0_re, h0_im, p['lam_re'], p['lam_im'], p['log_dt'], p['b_re'],
                              p['b_im'], p['c_re'], p['c_im'], p['d_skip'], p['w_glu'])
    mix = jnp.concatenate([rms_norm(att, p['g_attn_out']), rms_norm(ssm, p['g_ssm_out'])], axis=-1)
    h = h + mix @ p['w_out']
    h = h + cross_attend(rms_norm(h, p['g_ca']), mem_k, mem_v, p['w_cq'], p['w_co'])
    h = h + moe(rms_norm(h, p['g_moe']), p['w_router'], p['b_router'], p['w_gate'], p['b_gate'],
                p['w_up'], p['b_up'], p['w_down'], p['b_down'])
    return h, new_k, new_v, s_re, s_im


def setup_inputs(seed: int = 0) -> dict:
    key = jax.random.key(seed)
    ks = jax.random.split(key, 40)
    nrm = lambda i, shape, s: jax.random.normal(ks[i], shape, F32) * s
    gain = lambda i, shape: 1.0 + 0.02 * jax.random.normal(ks[i], shape, F32)
    lam_im_base = math.pi * jnp.arange(SSM_STATE, dtype=F32)
    return {
        'x_prompt': nrm(0, (BATCH, SEQ, D_MODEL), 1.0),
        'x_sample': nrm(1, (DEC_BATCH, DEC_SEQ, D_MODEL), 1.0),
        'mem_prompt': nrm(2, (BATCH, N_MEM, D_MODEL), 1.0),
        'cache_win_k': nrm(3, (DEPTH, DEC_BATCH, WINDOW, N_KV_HEADS, HEAD_DIM), 1.0),
        'cache_win_v': nrm(4, (DEPTH, DEC_BATCH, WINDOW, N_KV_HEADS, HEAD_DIM), 1.0),
        'state_ssm_re': nrm(5, (DEPTH, DEC_BATCH, SSM_GROUPS, SSM_STATE), 0.3),
        'state_ssm_im': nrm(6, (DEPTH, DEC_BATCH, SSM_GROUPS, SSM_STATE), 0.3),
        'cache_mem_k': nrm(7, (DEPTH, DEC_BATCH, N_MEM, CA_HEADS, CA_HEAD_DIM), 1.0),
        'cache_mem_v': nrm(8, (DEPTH, DEC_BATCH, N_MEM, CA_HEADS, CA_HEAD_DIM), 1.0),
        'g_mix': gain(9, (DEPTH, D_MODEL)),
        'w_in': nrm(10, (DEPTH, D_MODEL, IN_WIDTH), D_MODEL ** -0.5),
        'sink': nrm(11, (DEPTH, N_Q_HEADS), 0.5),
        'lam_re': -0.5 + nrm(12, (DEPTH, SSM_GROUPS, SSM_STATE), 0.01),
        'lam_im': lam_im_base + nrm(13, (DEPTH, SSM_GROUPS, SSM_STATE), 0.01),
        'log_dt': jax.random.uniform(ks[14], (DEPTH, SSM_GROUPS), F32, math.log(DT_MIN), math.log(DT_MAX)),
        'b_re': nrm(15, (DEPTH, SSM_GROUPS, SSM_STATE, SSM_GROUP), (2 * SSM_GROUP) ** -0.5),
        'b_im': nrm(16, (DEPTH, SSM_GROUPS, SSM_STATE, SSM_GROUP), (2 * SSM_GROUP) ** -0.5),
        'c_re': nrm(17, (DEPTH, SSM_GROUPS, SSM_GROUP, SSM_STATE), SSM_STATE ** -0.5),
        'c_im': nrm(18, (DEPTH, SSM_GROUPS, SSM_GROUP, SSM_STATE), SSM_STATE ** -0.5),
        'd_skip': nrm(19, (DEPTH, SSM_GROUPS, SSM_GROUP), 1.0),
        'w_glu': nrm(20, (DEPTH, SSM_WIDTH, SSM_WIDTH), SSM_WIDTH ** -0.5),
        'g_attn_out': gain(21, (DEPTH, ATTN_WIDTH)),
        'g_ssm_out': gain(22, (DEPTH, SSM_WIDTH)),
        'w_out': nrm(23, (DEPTH, MIX_WIDTH, D_MODEL), MIX_WIDTH ** -0.5),
        'g_ca': gain(24, (DEPTH, D_MODEL)),
        'g_mem': gain(25, (DEPTH, D_MODEL)),
        'w_mk': nrm(26, (DEPTH, D_MODEL, CA_WIDTH), D_MODEL ** -0.5),
        'w_mv': nrm(27, (DEPTH, D_MODEL, CA_WIDTH), D_MODEL ** -0.5),
        'w_cq': nrm(28, (DEPTH, D_MODEL, CA_WIDTH), D_MODEL ** -0.5),
        'w_co': nrm(29, (DEPTH, CA_WIDTH, D_MODEL), CA_WIDTH ** -0.5),
        'g_moe': gain(30, (DEPTH, D_MODEL)),
        'w_router': nrm(31, (DEPTH, D_MODEL, N_EXPERTS), D_MODEL ** -0.5),
        'b_router': nrm(32, (DEPTH, N_EXPERTS), 0.01),
        'w_gate': nrm(33, (DEPTH, N_EXPERTS, D_MODEL, D_FF), D_MODEL ** -0.5),
        'b_gate': nrm(34, (DEPTH, N_EXPERTS, D_FF), 0.01),
        'w_up': nrm(35, (DEPTH, N_EXPERTS, D_MODEL, D_FF), D_MODEL ** -0.5),
        'b_up': nrm(36, (DEPTH, N_EXPERTS, D_FF), 0.01),
        'w_down': nrm(37, (DEPTH, N_EXPERTS, D_FF, D_MODEL), D_FF ** -0.5),
        'b_down': nrm(38, (DEPTH, N_EXPERTS, D_MODEL), 0.01),
        'g_final': gain(39, (D_MODEL,)),
    }


def reference(x_prompt, x_sample, mem_prompt, cache_win_k, cache_win_v, state_ssm_re, state_ssm_im,
              cache_mem_k, cache_mem_v, g_mix, w_in, sink, lam_re, lam_im, log_dt, b_re, b_im,
              c_re, c_im, d_skip, w_glu, g_attn_out, g_ssm_out, w_out, g_ca, g_mem, w_mk, w_mv,
              w_cq, w_co, g_moe, w_router, b_router, w_gate, b_gate, w_up, b_up, w_down, b_down,
              g_final):
    hp, hs = x_prompt, x_sample
    p_wk, p_wv, p_sr, p_si, p_mk, p_mv = [], [], [], [], [], []
    s_wk, s_wv, s_sr, s_si = [], [], [], []
    for l in range(DEPTH):
        p = {'g_mix': g_mix[l], 'w_in': w_in[l], 'sink': sink[l], 'lam_re': lam_re[l],
             'lam_im': lam_im[l], 'log_dt': log_dt[l], 'b_re': b_re[l], 'b_im': b_im[l],
             'c_re': c_re[l], 'c_im': c_im[l], 'd_skip': d_skip[l], 'w_glu': w_glu[l],
             'g_attn_out': g_attn_out[l], 'g_ssm_out': g_ssm_out[l], 'w_out': w_out[l],
             'g_ca': g_ca[l], 'w_cq': w_cq[l], 'w_co': w_co[l], 'g_moe': g_moe[l],
             'w_router': w_router[l], 'b_router': b_router[l], 'w_gate': w_gate[l],
             'b_gate': b_gate[l], 'w_up': w_up[l], 'b_up': b_up[l], 'w_down': w_down[l],
             'b_down': b_down[l]}
        mk, mv = memory_kv(mem_prompt, g_mem[l], w_mk[l], w_mv[l])
        hp, wk, wv, sr, si = layer_forward(hp, mk, mv, None, None, None, None, p)
        p_wk.append(wk); p_wv.append(wv); p_sr.append(sr); p_si.append(si)
        p_mk.append(mk); p_mv.append(mv)
        hs, wk, wv, sr, si = layer_forward(hs, cache_mem_k[l], cache_mem_v[l], cache_win_k[l],
                                           cache_win_v[l], state_ssm_re[l], state_ssm_im[l], p)
        s_wk.append(wk); s_wv.append(wv); s_sr.append(sr); s_si.append(si)
    y_prompt = rms_norm(hp, g_final)
    y_sample = rms_norm(hs, g_final)
    return (y_prompt, y_sample,
            jnp.stack(p_wk), jnp.stack(p_wv), jnp.stack(p_sr), jnp.stack(p_si),
            jnp.stack(p_mk), jnp.stack(p_mv),
            jnp.stack(s_wk), jnp.stack(s_wv), jnp.stack(s_sr), jnp.stack(s_si))
```

```python
import functools
import math

import jax
import jax.numpy as jnp
from jax import lax
from jax.experimental import pallas as pl
from jax.experimental.pallas import tpu as pltpu

F32 = jnp.float32
BF16 = jnp.bfloat16

D_MODEL = 1024
HEAD_DIM = 64
N_Q_HEADS = 8
N_KV_HEADS = 2
Q_PER_KV = 4
ATTN_WIDTH = 512
KV_WIDTH = 128
WINDOW = 128
SSM_WIDTH = 512
SSM_GROUP = 16
SSM_GROUPS = 32
SSM_STATE = 64
SSM_LANES = SSM_GROUPS * SSM_STATE
SSM_HALF = SSM_LANES // 2
IN_WIDTH = ATTN_WIDTH + 2 * KV_WIDTH + SSM_WIDTH
N_MEM = 256
CA_HEADS = 4
CA_HEAD_DIM = 128
CA_WIDTH = 512
N_EXPERTS = 32
TOP_K = 4
SWIGLU_ALPHA = 1.702
SWIGLU_LIMIT = 7.0
RMS_EPS = 1e-5

SUBLANES = 8
LANES = 128
MOE_BLOCK = 256
VMEM_LIMIT = 56 * 1024 * 1024


def _cparams(sem, vmem=None):
    return pltpu.CompilerParams(dimension_semantics=sem, vmem_limit_bytes=vmem)


def _rms(x, g):
    return x * lax.rsqrt(jnp.mean(x * x, axis=-1, keepdims=True) + RMS_EPS) * g


def _full(shape):
    return pl.BlockSpec(shape, lambda *_: (0,) * len(shape))


def _disc_kernel(lr_ref, li_ref, ldt_ref, bre_ref, bim_ref, are_ref, aim_ref, bbre_ref, bbim_ref):
    lr = lr_ref[...]
    li = li_ref[...]
    dt = jnp.exp(ldt_ref[...])
    mag = jnp.exp(lr * dt)
    ang = li * dt
    ab_re = mag * jnp.cos(ang)
    ab_im = mag * jnp.sin(ang)
    nr = ab_re - 1.0
    ni = ab_im
    den = lr * lr + li * li
    f_re = (nr * lr + ni * li) / den
    f_im = (ni * lr - nr * li) / den
    are_ref[...] = ab_re
    aim_ref[...] = ab_im
    br = bre_ref[...]
    bi = bim_ref[...]
    bbre_ref[...] = f_re[:, None, :] * br - f_im[:, None, :] * bi
    bbim_ref[...] = f_re[:, None, :] * bi + f_im[:, None, :] * br


def _discretize(lam_re, lam_im, log_dt, b_re_t, b_im_t):
    g, n = lam_re.shape
    c = b_re_t.shape[1]
    return pl.pallas_call(
        _disc_kernel,
        out_shape=(jax.ShapeDtypeStruct((g, n), F32), jax.ShapeDtypeStruct((g, n), F32),
                   jax.ShapeDtypeStruct((g, c, n), F32), jax.ShapeDtypeStruct((g, c, n), F32)),
        name="ssm_discretize",
    )(lam_re, lam_im, log_dt.reshape(g, 1), b_re_t, b_im_t)


def _inproj_kernel(x_ref, g_ref, w_ref, q_ref, k_ref, v_ref, u_ref):
    a = _rms(x_ref[...], g_ref[...]).astype(BF16)
    proj = jnp.dot(a, w_ref[...], preferred_element_type=F32)
    q_ref[...] = proj[:, :ATTN_WIDTH].astype(BF16)
    k_ref[...] = proj[:, ATTN_WIDTH:ATTN_WIDTH + KV_WIDTH]
    v_ref[...] = proj[:, ATTN_WIDTH + KV_WIDTH:ATTN_WIDTH + 2 * KV_WIDTH]
    u_ref[...] = proj[:, ATTN_WIDTH + 2 * KV_WIDTH:]


def _inproj(x, g_mix, w_in_bf, tm):
    t = x.shape[0]
    row = lambda w: pl.BlockSpec((tm, w), lambda i: (i, 0))
    return pl.pallas_call(
        _inproj_kernel,
        grid=(t // tm,),
        in_specs=[row(D_MODEL), _full((1, D_MODEL)), _full((D_MODEL, IN_WIDTH))],
        out_specs=(row(ATTN_WIDTH), row(KV_WIDTH), row(KV_WIDTH), row(SSM_WIDTH)),
        out_shape=(jax.ShapeDtypeStruct((t, ATTN_WIDTH), BF16), jax.ShapeDtypeStruct((t, KV_WIDTH), F32),
                   jax.ShapeDtypeStruct((t, KV_WIDTH), F32), jax.ShapeDtypeStruct((t, SSM_WIDTH), F32)),
        compiler_params=_cparams(("arbitrary",)),
        name="in_projection",
    )(x, g_mix, w_in_bf)


def _sink_softmax_pv(s, sink, v_bf):
    m = jnp.maximum(jnp.max(s, axis=-1, keepdims=True), sink)
    p = jnp.exp(s - m)
    denom = jnp.sum(p, axis=-1, keepdims=True) + jnp.exp(sink - m)
    return jnp.dot(p.astype(BF16), v_bf, preferred_element_type=F32) / denom


def _attn_prompt_kernel(sink_ref, q_ref, kc_ref, kp_ref, vc_ref, vp_ref, g_ref, o_ref):
    i = pl.program_id(1)
    q = q_ref[...]
    kk = jnp.concatenate([kp_ref[...], kc_ref[...]], axis=0).astype(BF16)
    vv = jnp.concatenate([vp_ref[...], vc_ref[...]], axis=0).astype(BF16)
    qi = lax.broadcasted_iota(jnp.int32, (WINDOW, 2 * WINDOW), 0)
    kj = lax.broadcasted_iota(jnp.int32, (WINDOW, 2 * WINDOW), 1)
    rel = qi + WINDOW - kj
    mask = (rel >= 0) & (rel < WINDOW) & ((kj >= WINDOW) | (i > 0))
    outs = []
    for h in range(N_Q_HEADS):
        j = h // Q_PER_KV
        qh = q[:, h * HEAD_DIM:(h + 1) * HEAD_DIM]
        kh = kk[:, j * HEAD_DIM:(j + 1) * HEAD_DIM]
        vh = vv[:, j * HEAD_DIM:(j + 1) * HEAD_DIM]
        s = lax.dot_general(qh, kh, (((1,), (1,)), ((), ())), preferred_element_type=F32) * (HEAD_DIM ** -0.5)
        s = jnp.where(mask, s, -jnp.inf)
        outs.append(_sink_softmax_pv(s, sink_ref[h], vh))
    att = jnp.concatenate(outs, axis=-1)
    o_ref[...] = _rms(att, g_ref[...]).astype(BF16)


def _attn_prompt(sink, q, k, v, g_attn_out, b, l):
    nb = l // WINDOW
    cur = lambda w: pl.BlockSpec((WINDOW, w), lambda bi, i: (bi * nb + i, 0))
    prev = lambda w: pl.BlockSpec((WINDOW, w), lambda bi, i: (bi * nb + jnp.maximum(i - 1, 0), 0))
    return pl.pallas_call(
        _attn_prompt_kernel,
        grid=(b, nb),
        in_specs=[pl.BlockSpec(memory_space=pltpu.SMEM), cur(ATTN_WIDTH), cur(KV_WIDTH), prev(KV_WIDTH),
                  cur(KV_WIDTH), prev(KV_WIDTH), _full((1, ATTN_WIDTH))],
        out_specs=cur(ATTN_WIDTH),
        out_shape=jax.ShapeDtypeStruct((b * l, ATTN_WIDTH), BF16),
        compiler_params=_cparams(("arbitrary", "arbitrary")),
        name="window_attn_prompt",
    )(sink, q, k, k, v, v, g_attn_out)


def _attn_sample_kernel(sink_ref, q_ref, kn_ref, vn_ref, ck_ref, cv_ref, g_ref, o_ref, *, bt, t):
    nk = WINDOW + t
    rows = Q_PER_KV * t
    ri = lax.broadcasted_iota(jnp.int32, (rows, nk), 0)
    kj = lax.broadcasted_iota(jnp.int32, (rows, nk), 1)
    tq = ri % t
    mask = ((kj < WINDOW) & (kj > tq)) | ((kj >= WINDOW) & ((kj - WINDOW) <= tq))
    rg = lax.broadcasted_iota(jnp.int32, (rows, 1), 0) // t
    q_all = q_ref[...]
    kn_all = kn_ref[...]
    vn_all = vn_ref[...]
    outs_b = []
    for bi in range(bt):
        q = q_all[bi * t:(bi + 1) * t]
        kk = jnp.concatenate([ck_ref[bi], kn_all[bi * t:(bi + 1) * t]], axis=0).astype(BF16)
        vv = jnp.concatenate([cv_ref[bi], vn_all[bi * t:(bi + 1) * t]], axis=0).astype(BF16)
        outs = []
        for j in range(N_KV_HEADS):
            qj = jnp.concatenate(
                [q[:, (j * Q_PER_KV + g) * HEAD_DIM:(j * Q_PER_KV + g + 1) * HEAD_DIM] for g in range(Q_PER_KV)],
                axis=0)
            sink = jnp.zeros((rows, 1), F32)
            for g in range(Q_PER_KV):
                sink = jnp.where(rg == g, sink_ref[j * Q_PER_KV + g], sink)
            kh = kk[:, j * HEAD_DIM:(j + 1) * HEAD_DIM]
            vh = vv[:, j * HEAD_DIM:(j + 1) * HEAD_DIM]
            s = lax.dot_general(qj, kh, (((1,), (1,)), ((), ())), preferred_element_type=F32) * (HEAD_DIM ** -0.5)
            s = jnp.where(mask, s, -jnp.inf)
            o = _sink_softmax_pv(s, sink, vh)
            outs.extend(o[g * t:(g + 1) * t] for g in range(Q_PER_KV))
        outs_b.append(jnp.concatenate(outs, axis=-1))
    att = jnp.concatenate(outs_b, axis=0)
    o_ref[...] = _rms(att, g_ref[...]).astype(BF16)


def _attn_sample(sink, q, k, v, cache_k, cache_v, g_attn_out, b, t, bt):
    row = lambda w: pl.BlockSpec((bt * t, w), lambda i: (i, 0))
    cache = pl.BlockSpec((bt, WINDOW, KV_WIDTH), lambda i: (i, 0, 0))
    return pl.pallas_call(
        functools.partial(_attn_sample_kernel, bt=bt, t=t),
        grid=(b // bt,),
        in_specs=[pl.BlockSpec(memory_space=pltpu.SMEM), row(ATTN_WIDTH), row(KV_WIDTH), row(KV_WIDTH),
                  cache, cache, _full((1, ATTN_WIDTH))],
        out_specs=row(ATTN_WIDTH),
        out_shape=jax.ShapeDtypeStruct((b * t, ATTN_WIDTH), BF16),
        compiler_params=_cparams(("arbitrary",)),
        name="window_attn_sample",
    )(sink, q, k, v, cache_k, cache_v, g_attn_out)


def _ssm_kernel(u_ref, perm_ref, permt_ref, wb_ref, wc_ref, are_ref, aim_ref, d_ref, wglu_ref, g_ref,
                h0re_ref, h0im_ref, o_ref, hre_ref, him_ref, vre, vim, sre, sim, *, nb, s):
    c = pl.program_id(0)
    r = nb * s

    @pl.when(c == 0)
    def _():
        sre[...] = h0re_ref[...]
        sim[...] = h0im_ref[...]

    u = u_ref[...].reshape(r, SSM_WIDTH)
    u_hi = u.astype(BF16)
    u_lo = (u - u_hi.astype(F32)).astype(BF16)
    perm = perm_ref[...]
    u_tb = (jnp.dot(perm, u_hi, preferred_element_type=F32) + jnp.dot(perm, u_lo, preferred_element_type=F32))
    u_bf = u_tb.astype(BF16)
    half_w = SSM_WIDTH // 2
    for h in range(2):
        vv = jnp.dot(u_bf[:, h * half_w:(h + 1) * half_w], wb_ref[h], preferred_element_type=F32)
        vre[:, h * SSM_HALF:(h + 1) * SSM_HALF] = vv[:, :SSM_HALF]
        vim[:, h * SSM_HALF:(h + 1) * SSM_HALF] = vv[:, SSM_HALF:]

    lane_chunk = SSM_LANES // 2
    for rg in range(nb // SUBLANES):
        for lc in range(SSM_LANES // lane_chunk):
            lanes = slice(lc * lane_chunk, (lc + 1) * lane_chunk)
            rows0 = slice(rg * SUBLANES, (rg + 1) * SUBLANES)
            ar = jnp.broadcast_to(are_ref[:, lanes], (SUBLANES, lane_chunk))
            ai = jnp.broadcast_to(aim_ref[:, lanes], (SUBLANES, lane_chunk))

            def step(t, carry, lanes=lanes, rg=rg, ar=ar, ai=ai):
                hr, hi = carry
                row = pl.multiple_of(t * nb + rg * SUBLANES, SUBLANES)
                nhr = ar * hr - ai * hi + vre[pl.ds(row, SUBLANES), lanes]
                nhi = ar * hi + ai * hr + vim[pl.ds(row, SUBLANES), lanes]
                vre[pl.ds(row, SUBLANES), lanes] = nhr
                vim[pl.ds(row, SUBLANES), lanes] = nhi
                return nhr, nhi

            hr, hi = lax.fori_loop(0, s, step, (sre[rows0, lanes], sim[rows0, lanes]), unroll=True)
            sre[rows0, lanes] = hr
            sim[rows0, lanes] = hi

    ys = []
    for h in range(2):
        hcat = jnp.concatenate([vre[:, h * SSM_HALF:(h + 1) * SSM_HALF], vim[:, h * SSM_HALF:(h + 1) * SSM_HALF]],
                               axis=1).astype(BF16)
        ys.append(jnp.dot(hcat, wc_ref[h], preferred_element_type=F32))
    y = jnp.concatenate(ys, axis=1) + d_ref[...] * u_tb
    g = jax.nn.gelu(y)
    z = jnp.dot(g.astype(BF16), wglu_ref[...], preferred_element_type=F32)
    out = g * jax.nn.sigmoid(z)
    on = _rms(out, g_ref[...]).astype(BF16)
    o_bt = jnp.dot(permt_ref[...], on, preferred_element_type=F32).astype(BF16)
    o_ref[...] = o_bt.reshape(nb, s, SSM_WIDTH)

    @pl.when(c == pl.num_programs(0) - 1)
    def _():
        hre_ref[...] = sre[...]
        him_ref[...] = sim[...]


def _ssm(u, h0_re, h0_im, ssm_w, s):
    nb, l, _ = u.shape
    r = nb * s
    ridx = jnp.arange(r)
    src = (ridx % nb) * s + ridx // nb
    perm = (src[:, None] == ridx[None, :]).astype(BF16)
    wb, wc, a_re, a_im, d_skip, w_glu, g_out = ssm_w
    seq_blk = pl.BlockSpec((nb, s, SSM_WIDTH), lambda c: (0, c, 0))
    st_blk = _full((nb, SSM_LANES))
    return pl.pallas_call(
        functools.partial(_ssm_kernel, nb=nb, s=s),
        grid=(l // s,),
        in_specs=[seq_blk, _full((r, r)), _full((r, r)), _full(wb.shape), _full(wc.shape),
                  _full((1, SSM_LANES)), _full((1, SSM_LANES)), _full((1, SSM_WIDTH)),
                  _full((SSM_WIDTH, SSM_WIDTH)), _full((1, SSM_WIDTH)), st_blk, st_blk],
        out_specs=(seq_blk, st_blk, st_blk),
        out_shape=(jax.ShapeDtypeStruct((nb, l, SSM_WIDTH), BF16),
                   jax.ShapeDtypeStruct((nb, SSM_LANES), F32), jax.ShapeDtypeStruct((nb, SSM_LANES), F32)),
        scratch_shapes=[pltpu.VMEM((r, SSM_LANES), F32), pltpu.VMEM((r, SSM_LANES), F32),
                        pltpu.VMEM((nb, SSM_LANES), F32), pltpu.VMEM((nb, SSM_LANES), F32)],
        compiler_params=_cparams(("arbitrary",), VMEM_LIMIT),
        name="ssm_scan",
    )(u, perm, perm.T, wb, wc, a_re, a_im, d_skip, w_glu, g_out, h0_re, h0_im)


def _outproj_kernel(x_ref, att_ref, ssm_ref, wa_ref, ws_ref, gca_ref, wcq_ref, h_ref, qc_ref):
    h = (x_ref[...] + jnp.dot(att_ref[...], wa_ref[...], preferred_element_type=F32)
         + jnp.dot(ssm_ref[...], ws_ref[...], preferred_element_type=F32))
    h_ref[...] = h
    xn = _rms(h, gca_ref[...]).astype(BF16)
    qc_ref[...] = jnp.dot(xn, wcq_ref[...], preferred_element_type=F32).astype(BF16)


def _outproj(x, att, ssm, w_att, w_ssm, g_ca, w_cq, tm):
    t = x.shape[0]
    row = lambda w: pl.BlockSpec((tm, w), lambda i: (i, 0))
    return pl.pallas_call(
        _outproj_kernel,
        grid=(t // tm,),
        in_specs=[row(D_MODEL), row(ATTN_WIDTH), row(SSM_WIDTH), _full((ATTN_WIDTH, D_MODEL)),
                  _full((SSM_WIDTH, D_MODEL)), _full((1, D_MODEL)), _full((D_MODEL, CA_WIDTH))],
        out_specs=(row(D_MODEL), row(CA_WIDTH)),
        out_shape=(jax.ShapeDtypeStruct((t, D_MODEL), F32), jax.ShapeDtypeStruct((t, CA_WIDTH), BF16)),
        compiler_params=_cparams(("arbitrary",)),
        name="out_projection",
    )(x, att, ssm, w_att, w_ssm, g_ca, w_cq)


def _memkv_kernel(m_ref, g_ref, w_ref, k_ref, v_ref):
    m = _rms(m_ref[...], g_ref[...]).astype(BF16)
    kv = jnp.dot(m, w_ref[...], preferred_element_type=F32)
    k_ref[...] = kv[:, :CA_WIDTH]
    v_ref[...] = kv[:, CA_WIDTH:]


def _memkv(mem, g_mem, w_mkv_bf, tm):
    t = mem.shape[0]
    row = lambda w: pl.BlockSpec((tm, w), lambda i: (i, 0))
    return pl.pallas_call(
        _memkv_kernel,
        grid=(t // tm,),
        in_specs=[row(D_MODEL), _full((1, D_MODEL)), _full((D_MODEL, 2 * CA_WIDTH))],
        out_specs=(row(CA_WIDTH), row(CA_WIDTH)),
        out_shape=(jax.ShapeDtypeStruct((t, CA_WIDTH), F32), jax.ShapeDtypeStruct((t, CA_WIDTH), F32)),
        compiler_params=_cparams(("arbitrary",)),
        name="memory_kv",
    )(mem, g_mem, w_mkv_bf)


def _xattn_kernel(qc_ref, mk_ref, mv_ref, h_ref, wco_ref, gmoe_ref, wr_ref, br_ref,
                  h2_ref, xn_ref, ti_ref, tw_ref, *, nbat, rows):
    q_all = qc_ref[...]
    outs_b = []
    for bi in range(nbat):
        q = q_all[bi * rows:(bi + 1) * rows]
        mk = mk_ref[bi].astype(BF16)
        mv = mv_ref[bi].astype(BF16)
        outs = []
        for hd in range(CA_HEADS):
            sl = slice(hd * CA_HEAD_DIM, (hd + 1) * CA_HEAD_DIM)
            s = lax.dot_general(q[:, sl], mk[:, sl], (((1,), (1,)), ((), ())),
                                preferred_element_type=F32) * (CA_HEAD_DIM ** -0.5)
            m = jnp.max(s, axis=-1, keepdims=True)
            p = jnp.exp(s - m)
            denom = jnp.sum(p, axis=-1, keepdims=True)
            outs.append(jnp.dot(p.astype(BF16), mv[:, sl], preferred_element_type=F32) / denom)
        outs_b.append(jnp.concatenate(outs, axis=-1))
    o = jnp.concatenate(outs_b, axis=0).astype(BF16)
    h2 = h_ref[...] + jnp.dot(o, wco_ref[...], preferred_element_type=F32)
    h2_ref[...] = h2
    xn = _rms(h2, gmoe_ref[...])
    xn_ref[...] = xn
    logits = jnp.dot(xn, wr_ref[...], preferred_element_type=F32, precision=lax.Precision.HIGHEST) + br_ref[...]
    n = logits.shape[0]
    eidx = lax.broadcasted_iota(jnp.int32, (n, N_EXPERTS), 1).astype(F32)
    lane = lax.broadcasted_iota(jnp.int32, (n, LANES), 1)
    ti = jnp.zeros((n, LANES), F32)
    tv = jnp.zeros((n, LANES), F32)
    work = logits
    vals = []
    for k in range(TOP_K):
        mx = jnp.max(work, axis=-1, keepdims=True)
        idx = jnp.min(jnp.where(work == mx, eidx, float(N_EXPERTS)), axis=-1, keepdims=True)
        vals.append(mx)
        ti = jnp.where(lane == k, idx, ti)
        work = jnp.where(eidx == idx, -jnp.inf, work)
    ex = [jnp.exp(v - vals[0]) for v in vals]
    tot = ex[0] + ex[1] + ex[2] + ex[3]
    for k in range(TOP_K):
        tv = jnp.where(lane == k, ex[k] / tot, tv)
    ti_ref[...] = ti.astype(jnp.int32)
    tw_ref[...] = tv


def _xattn(qc, mk, mv, h1, w_co, g_moe, w_router, b_router, nbat, rows):
    t = qc.shape[0]
    tm = nbat * rows
    row = lambda w: pl.BlockSpec((tm, w), lambda i: (i, 0))
    seq_rows = t // mk.shape[0]
    mem = pl.BlockSpec((nbat, N_MEM, CA_WIDTH), lambda i: ((i * tm) // (seq_rows * nbat), 0, 0))
    return pl.pallas_call(
        functools.partial(_xattn_kernel, nbat=nbat, rows=rows),
        grid=(t // tm,),
        in_specs=[row(CA_WIDTH), mem, mem, row(D_MODEL), _full((CA_WIDTH, D_MODEL)), _full((1, D_MODEL)),
                  _full((D_MODEL, N_EXPERTS)), _full((1, N_EXPERTS))],
        out_specs=(row(D_MODEL), row(D_MODEL), row(LANES), row(LANES)),
        out_shape=(jax.ShapeDtypeStruct((t, D_MODEL), F32), jax.ShapeDtypeStruct((t, D_MODEL), F32),
                   jax.ShapeDtypeStruct((t, LANES), jnp.int32), jax.ShapeDtypeStruct((t, LANES), F32)),
        compiler_params=_cparams(("arbitrary",), VMEM_LIMIT),
        name="cross_attn_router",
    )(qc, mk, mv, h1, w_co, g_moe, w_router, b_router)


def _row_copy(src, src_row, dst, dst_row, sem):
    return pltpu.make_async_copy(src.at[pl.ds(src_row, 1)], dst.at[pl.ds(dst_row, 1)], sem)


def _scatter_kernel(dest_hbm, x_ref, xs_hbm, idx, isem, sem, *, tm):
    i = pl.program_id(0)
    n = pl.num_programs(0)
    slot = i % 2

    def idx_copy(step, sl):
        return pltpu.make_async_copy(dest_hbm.at[step], idx.at[sl], isem.at[sl])

    @pl.when(i == 0)
    def _():
        idx_copy(0, 0).start()

    @pl.when(i + 1 < n)
    def _():
        idx_copy(i + 1, 1 - slot).start()

    idx_copy(i, slot).wait()

    def issue(r, carry):
        for k in range(TOP_K):
            _row_copy(x_ref, r, xs_hbm, idx[slot, r * TOP_K + k], sem).start()
        return carry

    lax.fori_loop(0, tm, issue, 0)

    def drain(r, carry):
        for k in range(TOP_K):
            _row_copy(x_ref, r, xs_hbm, idx[slot, r * TOP_K + k], sem).wait()
        return carry

    lax.fori_loop(0, tm, drain, 0)


def _scatter_rows(dest, x, n_slots, tm):
    t = x.shape[0]
    return pl.pallas_call(
        functools.partial(_scatter_kernel, tm=tm),
        grid=(t // tm,),
        in_specs=[pl.BlockSpec(memory_space=pl.ANY), pl.BlockSpec((tm, D_MODEL), lambda i: (i, 0))],
        out_specs=pl.BlockSpec(memory_space=pl.ANY),
        out_shape=jax.ShapeDtypeStruct((n_slots, D_MODEL), F32),
        scratch_shapes=[pltpu.SMEM((2, tm * TOP_K), jnp.int32), pltpu.SemaphoreType.DMA((2,)),
                        pltpu.SemaphoreType.DMA(())],
        compiler_params=_cparams(("arbitrary",)),
        name="moe_scatter",
    )(dest.reshape(t // tm, tm * TOP_K), x)


def _expert_kernel(blk_e_ref, nused_ref, xs_ref, wg_ref, bg_ref, wu_ref, bu_ref, wd_ref, bd_ref, ys_ref,
                   wg_bf, wu_bf, wd_bf):
    i = pl.program_id(0)
    active = i < nused_ref[0]
    prev_e = blk_e_ref[jnp.maximum(i - 1, 0)]
    new_expert = (i == 0) | (blk_e_ref[i] != prev_e)

    @pl.when(active & new_expert)
    def _():
        wg_bf[...] = wg_ref[0].astype(BF16)
        wu_bf[...] = wu_ref[0].astype(BF16)
        wd_bf[...] = wd_ref[0].astype(BF16)

    @pl.when(active)
    def _():
        x = xs_ref[...].astype(BF16)
        gate = jnp.minimum(jnp.dot(x, wg_bf[...], preferred_element_type=F32) + bg_ref[0], SWIGLU_LIMIT)
        up = jnp.clip(jnp.dot(x, wu_bf[...], preferred_element_type=F32) + bu_ref[0], -SWIGLU_LIMIT, SWIGLU_LIMIT)
        h = gate * jax.nn.sigmoid(SWIGLU_ALPHA * gate) * (up + 1.0)
        ys_ref[...] = jnp.dot(h.astype(BF16), wd_bf[...], preferred_element_type=F32) + bd_ref[0]


def _experts(blk_e, n_used, xs, w_gate, b_gate, w_up, b_up, w_down, b_down):
    n_slots = xs.shape[0]
    n_blocks = n_slots // MOE_BLOCK
    d_ff = w_gate.shape[-1]
    blk = lambda i, be, nu: (jnp.minimum(i, nu[0] - 1), 0)
    wspec = lambda a, b: pl.BlockSpec((1, a, b), lambda i, be, nu: (be[i], 0, 0))
    grid_spec = pltpu.PrefetchScalarGridSpec(
        num_scalar_prefetch=2,
        grid=(n_blocks,),
        in_specs=[pl.BlockSpec((MOE_BLOCK, D_MODEL), blk), wspec(D_MODEL, d_ff), wspec(1, d_ff),
                  wspec(D_MODEL, d_ff), wspec(1, d_ff), wspec(d_ff, D_MODEL), wspec(1, D_MODEL)],
        out_specs=pl.BlockSpec((MOE_BLOCK, D_MODEL), blk),
        scratch_shapes=[pltpu.VMEM((D_MODEL, d_ff), BF16), pltpu.VMEM((D_MODEL, d_ff), BF16),
                        pltpu.VMEM((d_ff, D_MODEL), BF16)],
    )
    return pl.pallas_call(
        _expert_kernel,
        grid_spec=grid_spec,
        out_shape=jax.ShapeDtypeStruct((n_slots, D_MODEL), F32),
        compiler_params=_cparams(("arbitrary",), VMEM_LIMIT),
        name="moe_experts",
    )(blk_e, n_used, xs, w_gate, b_gate.reshape(N_EXPERTS, 1, d_ff), w_up, b_up.reshape(N_EXPERTS, 1, d_ff),
      w_down, b_down.reshape(N_EXPERTS, 1, D_MODEL))


def _combine_kernel(dest_hbm, ys_hbm, h_ref, tw_ref, g_ref, o_ref, idx, isem, buf, sem, *, tm):
    i = pl.program_id(0)
    n = pl.num_programs(0)
    slot = i % 2

    def idx_copy(step, sl):
        return pltpu.make_async_copy(dest_hbm.at[step], idx.at[sl], isem.at[sl])

    def issue(sl):
        def body(r, carry):
            for k in range(TOP_K):
                _row_copy(ys_hbm, idx[sl, r * TOP_K + k], buf.at[sl, k], r, sem.at[sl]).start()
            return carry
        lax.fori_loop(0, tm, body, 0)

    @pl.when(i == 0)
    def _():
        idx_copy(0, 0).start()
        idx_copy(0, 0).wait()
        issue(0)

        @pl.when(n > 1)
        def _():
            idx_copy(1, 1).start()

    @pl.when(i + 1 < n)
    def _():
        idx_copy(i + 1, 1 - slot).wait()
        issue(1 - slot)

    @pl.when(i + 2 < n)
    def _():
        idx_copy(i + 2, slot).start()

    def drain(r, carry):
        for k in range(TOP_K):
            _row_copy(ys_hbm, 0, buf.at[slot, k], r, sem.at[slot]).wait()
        return carry

    lax.fori_loop(0, tm, drain, 0)
    tw = tw_ref[...]
    y = h_ref[...]
    for k in range(TOP_K):
        y = y + tw[:, k:k + 1] * buf[slot, k]
    o_ref[...] = _rms(y, g_ref[...])


def _combine(dest, ys, h2, tw, g_final, tm):
    t = h2.shape[0]
    row = lambda w: pl.BlockSpec((tm, w), lambda i: (i, 0))
    return pl.pallas_call(
        functools.partial(_combine_kernel, tm=tm),
        grid=(t // tm,),
        in_specs=[pl.BlockSpec(memory_space=pl.ANY), pl.BlockSpec(memory_space=pl.ANY), row(D_MODEL), row(LANES),
                  _full((1, D_MODEL))],
        out_specs=row(D_MODEL),
        out_shape=jax.ShapeDtypeStruct((t, D_MODEL), F32),
        scratch_shapes=[pltpu.SMEM((2, tm * TOP_K), jnp.int32), pltpu.SemaphoreType.DMA((2,)),
                        pltpu.VMEM((2, TOP_K, tm, D_MODEL), F32), pltpu.SemaphoreType.DMA((2,))],
        compiler_params=_cparams(("arbitrary",), VMEM_LIMIT),
        name="moe_combine",
    )(dest.reshape(t // tm, tm * TOP_K), ys, h2, tw, g_final)


def _route(top_i):
    t = top_i.shape[0]
    m = t * TOP_K
    flat_e = top_i.reshape(m)
    onehot = (flat_e[:, None] == jnp.arange(N_EXPERTS, dtype=jnp.int32)[None, :]).astype(jnp.int32)
    csum = jnp.cumsum(onehot, axis=0)
    counts = csum[-1]
    rank = jnp.sum(onehot * csum, axis=1) - 1
    padded = (counts + MOE_BLOCK - 1) // MOE_BLOCK * MOE_BLOCK
    pad_end = jnp.cumsum(padded)
    pad_start = pad_end - padded
    dest = jnp.sum(onehot * pad_start[None, :], axis=1) + rank
    n_blocks = -(-(m + N_EXPERTS * (MOE_BLOCK - 1)) // MOE_BLOCK)
    blk_e = jnp.minimum(jnp.searchsorted(pad_end, jnp.arange(n_blocks, dtype=jnp.int32) * MOE_BLOCK, side='right'),
                        N_EXPERTS - 1).astype(jnp.int32)
    n_used = (pad_end[-1] // MOE_BLOCK).astype(jnp.int32).reshape(1)
    return dest.astype(jnp.int32), blk_e, n_used, n_blocks * MOE_BLOCK


def _block_diag(blocks):
    g, a, b = blocks.shape
    eye = jnp.eye(g, dtype=blocks.dtype)
    return (blocks[:, :, None, :] * eye[:, None, :, None]).reshape(g * a, g * b)


def kernel(x_prompt, x_sample, mem_prompt, cache_win_k, cache_win_v, state_ssm_re, state_ssm_im, cache_mem_k,
           cache_mem_v, g_mix, w_in, sink, lam_re, lam_im, log_dt, b_re, b_im, c_re, c_im, d_skip, w_glu,
           g_attn_out, g_ssm_out, w_out, g_ca, g_mem, w_mk, w_mv, w_cq, w_co, g_moe, w_router, b_router,
           w_gate, b_gate, w_up, b_up, w_down, b_down, g_final):
    depth = g_mix.shape[0]
    assert depth == 1
    bp, lp, _ = x_prompt.shape
    bs, ls, _ = x_sample.shape
    tp, ts = bp * lp, bs * ls

    w_in_bf = w_in[0].astype(BF16)
    w_att_bf = w_out[0, :ATTN_WIDTH].astype(BF16)
    w_ssm_bf = w_out[0, ATTN_WIDTH:].astype(BF16)
    w_cq_bf = w_cq[0].astype(BF16)
    w_co_bf = w_co[0].astype(BF16)
    w_mkv_bf = jnp.concatenate([w_mk[0], w_mv[0]], axis=1).astype(BF16)
    w_glu_bf = w_glu[0].astype(BF16)
    row = lambda a: a.reshape(1, -1)

    a_re, a_im, bb_re, bb_im = _discretize(lam_re[0], lam_im[0], log_dt[0], b_re[0].transpose(0, 2, 1),
                                           b_im[0].transpose(0, 2, 1))
    hg = SSM_GROUPS // 2
    wb = jnp.stack([jnp.concatenate([_block_diag(bb_re[h * hg:(h + 1) * hg]),
                                     _block_diag(bb_im[h * hg:(h + 1) * hg])], axis=1) for h in range(2)]).astype(BF16)
    cre_t = c_re[0].transpose(0, 2, 1)
    cim_t = c_im[0].transpose(0, 2, 1)
    wc = jnp.stack([jnp.concatenate([_block_diag(cre_t[h * hg:(h + 1) * hg]),
                                     -_block_diag(cim_t[h * hg:(h + 1) * hg])], axis=0) for h in range(2)]).astype(BF16)
    ssm_w = (wb, wc, a_re.reshape(1, SSM_LANES), a_im.reshape(1, SSM_LANES), row(d_skip[0]), w_glu_bf,
             row(g_ssm_out[0]))

    xp = x_prompt.reshape(tp, D_MODEL)
    xs_ = x_sample.reshape(ts, D_MODEL)

    qp, kp, vp, up = _inproj(xp, row(g_mix[0]), w_in_bf, 512)
    qs, ks, vs, us = _inproj(xs_, row(g_mix[0]), w_in_bf, 256)

    att_p = _attn_prompt(sink[0], qp, kp, vp, row(g_attn_out[0]), bp, lp)
    ck = cache_win_k[0].reshape(bs, WINDOW, KV_WIDTH)
    cv = cache_win_v[0].reshape(bs, WINDOW, KV_WIDTH)
    att_s = _attn_sample(sink[0], qs, ks, vs, ck, cv, row(g_attn_out[0]), bs, ls, 16)

    zero_state = jnp.zeros((bp, SSM_LANES), F32)
    ssm_p, p_sre, p_sim = _ssm(up.reshape(bp, lp, SSM_WIDTH), zero_state, zero_state, ssm_w, 32)
    ssm_s, s_sre, s_sim = _ssm(us.reshape(bs, ls, SSM_WIDTH), state_ssm_re[0].reshape(bs, SSM_LANES),
                               state_ssm_im[0].reshape(bs, SSM_LANES), ssm_w, ls)

    h1p, qcp = _outproj(xp, att_p, ssm_p.reshape(tp, SSM_WIDTH), w_att_bf, w_ssm_bf, row(g_ca[0]), w_cq_bf, 512)
    h1s, qcs = _outproj(xs_, att_s, ssm_s.reshape(ts, SSM_WIDTH), w_att_bf, w_ssm_bf, row(g_ca[0]), w_cq_bf, 256)

    mk_p, mv_p = _memkv(mem_prompt.reshape(bp * N_MEM, D_MODEL), row(g_mem[0]), w_mkv_bf, 512)

    h2p, xnp_, tip, twp = _xattn(qcp, mk_p.reshape(bp, N_MEM, CA_WIDTH), mv_p.reshape(bp, N_MEM, CA_WIDTH), h1p,
                                 w_co_bf, row(g_moe[0]), w_router[0], row(b_router[0]), 1, 512)
    h2s, xns_, tis, tws = _xattn(qcs, cache_mem_k[0].reshape(bs, N_MEM, CA_WIDTH),
                                 cache_mem_v[0].reshape(bs, N_MEM, CA_WIDTH), h1s, w_co_bf, row(g_moe[0]),
                                 w_router[0], row(b_router[0]), 8, ls)

    h2 = jnp.concatenate([h2p, h2s], axis=0)
    xn = jnp.concatenate([xnp_, xns_], axis=0)
    top_i = jnp.concatenate([tip, tis], axis=0)[:, :TOP_K]
    tw = jnp.concatenate([twp, tws], axis=0)
    dest, blk_e, n_used, n_slots = _route(top_i)
    xs_sorted = _scatter_rows(dest, xn, n_slots, 256)
    ys = _experts(blk_e, n_used, xs_sorted, w_gate[0], b_gate[0], w_up[0], b_up[0], w_down[0], b_down[0])
    y = _combine(dest, ys, h2, tw, row(g_final), 128)

    y_prompt = y[:tp].reshape(bp, lp, D_MODEL)
    y_sample = y[tp:].reshape(bs, ls, D_MODEL)

    kp4 = kp.reshape(bp, lp, N_KV_HEADS, HEAD_DIM)
    vp4 = vp.reshape(bp, lp, N_KV_HEADS, HEAD_DIM)
    p_win_k = kp4[:, -WINDOW:][None]
    p_win_v = vp4[:, -WINDOW:][None]
    s_win_k = jnp.concatenate([cache_win_k[0], ks.reshape(bs, ls, N_KV_HEADS, HEAD_DIM)], axis=1)[:, -WINDOW:][None]
    s_win_v = jnp.concatenate([cache_win_v[0], vs.reshape(bs, ls, N_KV_HEADS, HEAD_DIM)], axis=1)[:, -WINDOW:][None]
    st = lambda a, b: a.reshape(1, b, SSM_GROUPS, SSM_STATE)
    p_mem_k = mk_p.reshape(1, bp, N_MEM, CA_HEADS, CA_HEAD_DIM)
    p_mem_v = mv_p.reshape(1, bp, N_MEM, CA_HEADS, CA_HEAD_DIM)
    return (y_prompt, y_sample, p_win_k, p_win_v, st(p_sre, bp), st(p_sim, bp), p_mem_k, p_mem_v,
            s_win_k, s_win_v, st(s_sre, bs), st(s_sim, bs))
```

```python
import functools
import math

import jax
import jax.numpy as jnp
from jax import lax
from jax.experimental import pallas as pl
from jax.experimental.pallas import tpu as pltpu

F32 = jnp.float32
BF16 = jnp.bfloat16

D_MODEL = 1024
HEAD_DIM = 64
N_Q_HEADS = 8
N_KV_HEADS = 2
Q_PER_KV = 4
ATTN_WIDTH = 512
KV_WIDTH = 128
WINDOW = 128
SSM_WIDTH = 512
SSM_GROUP = 16
SSM_GROUPS = 32
SSM_STATE = 64
SSM_LANES = SSM_GROUPS * SSM_STATE
SSM_HALF = SSM_LANES // 2
IN_WIDTH = ATTN_WIDTH + 2 * KV_WIDTH + SSM_WIDTH
N_MEM = 256
CA_HEADS = 4
CA_HEAD_DIM = 128
CA_WIDTH = 512
N_EXPERTS = 32
TOP_K = 4
SWIGLU_ALPHA = 1.702
SWIGLU_LIMIT = 7.0
RMS_EPS = 1e-5

SUBLANES = 8
LANES = 128
MOE_BLOCK = 256
VMEM_LIMIT = 56 * 1024 * 1024


def _cparams(sem, vmem=None):
    return pltpu.CompilerParams(dimension_semantics=sem, vmem_limit_bytes=vmem)


def _rms(x, g):
    return x * lax.rsqrt(jnp.mean(x * x, axis=-1, keepdims=True) + RMS_EPS) * g


def _full(shape):
    return pl.BlockSpec(shape, lambda *_: (0,) * len(shape))


def _disc_kernel(lr_ref, li_ref, ldt_ref, bre_ref, bim_ref, are_ref, aim_ref, bbre_ref, bbim_ref):
    lr = lr_ref[...]
    li = li_ref[...]
    dt = jnp.exp(ldt_ref[...])
    mag = jnp.exp(lr * dt)
    ang = li * dt
    ab_re = mag * jnp.cos(ang)
    ab_im = mag * jnp.sin(ang)
    nr = ab_re - 1.0
    ni = ab_im
    den = lr * lr + li * li
    f_re = (nr * lr + ni * li) / den
    f_im = (ni * lr - nr * li) / den
    are_ref[...] = ab_re
    aim_ref[...] = ab_im
    br = bre_ref[...]
    bi = bim_ref[...]
    bbre_ref[...] = f_re[:, None, :] * br - f_im[:, None, :] * bi
    bbim_ref[...] = f_re[:, None, :] * bi + f_im[:, None, :] * br


def _discretize(lam_re, lam_im, log_dt, b_re_t, b_im_t):
    g, n = lam_re.shape
    c = b_re_t.shape[1]
    return pl.pallas_call(
        _disc_kernel,
        out_shape=(jax.ShapeDtypeStruct((g, n), F32), jax.ShapeDtypeStruct((g, n), F32),
                   jax.ShapeDtypeStruct((g, c, n), F32), jax.ShapeDtypeStruct((g, c, n), F32)),
        name="ssm_discretize",
    )(lam_re, lam_im, log_dt.reshape(g, 1), b_re_t, b_im_t)


def _inproj_kernel(x_ref, g_ref, w_ref, q_ref, k_ref, v_ref, u_ref):
    a = _rms(x_ref[...], g_ref[...]).astype(BF16)
    proj = jnp.dot(a, w_ref[...], preferred_element_type=F32)
    q_ref[...] = proj[:, :ATTN_WIDTH].astype(BF16)
    k_ref[...] = proj[:, ATTN_WIDTH:ATTN_WIDTH + KV_WIDTH]
    v_ref[...] = proj[:, ATTN_WIDTH + KV_WIDTH:ATTN_WIDTH + 2 * KV_WIDTH]
    u_ref[...] = proj[:, ATTN_WIDTH + 2 * KV_WIDTH:]


def _inproj(x, g_mix, w_in_bf, tm):
    t = x.shape[0]
    row = lambda w: pl.BlockSpec((tm, w), lambda i: (i, 0))
    return pl.pallas_call(
        _inproj_kernel,
        grid=(t // tm,),
        in_specs=[row(D_MODEL), _full((1, D_MODEL)), _full((D_MODEL, IN_WIDTH))],
        out_specs=(row(ATTN_WIDTH), row(KV_WIDTH), row(KV_WIDTH), row(SSM_WIDTH)),
        out_shape=(jax.ShapeDtypeStruct((t, ATTN_WIDTH), BF16), jax.ShapeDtypeStruct((t, KV_WIDTH), F32),
                   jax.ShapeDtypeStruct((t, KV_WIDTH), F32), jax.ShapeDtypeStruct((t, SSM_WIDTH), F32)),
        compiler_params=_cparams(("arbitrary",)),
        name="in_projection",
    )(x, g_mix, w_in_bf)


def _sink_softmax_pv(s, sink, v_bf):
    m = jnp.maximum(jnp.max(s, axis=-1, keepdims=True), sink)
    p = jnp.exp(s - m)
    denom = jnp.sum(p, axis=-1, keepdims=True) + jnp.exp(sink - m)
    return jnp.dot(p.astype(BF16), v_bf, preferred_element_type=F32) / denom


def _attn_prompt_kernel(sink_ref, q_ref, kc_ref, kp_ref, vc_ref, vp_ref, g_ref, o_ref):
    i = pl.program_id(1)
    q = q_ref[...]
    kk = jnp.concatenate([kp_ref[...], kc_ref[...]], axis=0).astype(BF16)
    vv = jnp.concatenate([vp_ref[...], vc_ref[...]], axis=0).astype(BF16)
    qi = lax.broadcasted_iota(jnp.int32, (WINDOW, 2 * WINDOW), 0)
    kj = lax.broadcasted_iota(jnp.int32, (WINDOW, 2 * WINDOW), 1)
    rel = qi + WINDOW - kj
    mask = (rel >= 0) & (rel < WINDOW) & ((kj >= WINDOW) | (i > 0))
    outs = []
    for h in range(N_Q_HEADS):
        j = h // Q_PER_KV
        qh = q[:, h * HEAD_DIM:(h + 1) * HEAD_DIM]
        kh = kk[:, j * HEAD_DIM:(j + 1) * HEAD_DIM]
        vh = vv[:, j * HEAD_DIM:(j + 1) * HEAD_DIM]
        s = lax.dot_general(qh, kh, (((1,), (1,)), ((), ())), preferred_element_type=F32) * (HEAD_DIM ** -0.5)
        s = jnp.where(mask, s, -jnp.inf)
        outs.append(_sink_softmax_pv(s, sink_ref[h], vh))
    att = jnp.concatenate(outs, axis=-1)
    o_ref[...] = _rms(att, g_ref[...]).astype(BF16)


def _attn_prompt(sink, q, k, v, g_attn_out, b, l):
    nb = l // WINDOW
    cur = lambda w: pl.BlockSpec((WINDOW, w), lambda bi, i: (bi * nb + i, 0))
    prev = lambda w: pl.BlockSpec((WINDOW, w), lambda bi, i: (bi * nb + jnp.maximum(i - 1, 0), 0))
    return pl.pallas_call(
        _attn_prompt_kernel,
        grid=(b, nb),
        in_specs=[pl.BlockSpec(memory_space=pltpu.SMEM), cur(ATTN_WIDTH), cur(KV_WIDTH), prev(KV_WIDTH),
                  cur(KV_WIDTH), prev(KV_WIDTH), _full((1, ATTN_WIDTH))],
        out_specs=cur(ATTN_WIDTH),
        out_shape=jax.ShapeDtypeStruct((b * l, ATTN_WIDTH), BF16),
        compiler_params=_cparams(("arbitrary", "arbitrary")),
        name="window_attn_prompt",
    )(sink, q, k, k, v, v, g_attn_out)


def _attn_sample_kernel(sink_ref, q_ref, kn_ref, vn_ref, ck_ref, cv_ref, g_ref, o_ref, *, bt, t):
    nk = WINDOW + t
    rows = Q_PER_KV * t
    ri = lax.broadcasted_iota(jnp.int32, (rows, nk), 0)
    kj = lax.broadcasted_iota(jnp.int32, (rows, nk), 1)
    tq = ri % t
    mask = ((kj < WINDOW) & (kj > tq)) | ((kj >= WINDOW) & ((kj - WINDOW) <= tq))
    rg = lax.broadcasted_iota(jnp.int32, (rows, 1), 0) // t
    q_all = q_ref[...]
    kn_all = kn_ref[...]
    vn_all = vn_ref[...]
    outs_b = []
    for bi in range(bt):
        q = q_all[bi * t:(bi + 1) * t]
        kk = jnp.concatenate([ck_ref[bi], kn_all[bi * t:(bi + 1) * t]], axis=0).astype(BF16)
        vv = jnp.concatenate([cv_ref[bi], vn_all[bi * t:(bi + 1) * t]], axis=0).astype(BF16)
        outs = []
        for j in range(N_KV_HEADS):
            qj = jnp.concatenate(
                [q[:, (j * Q_PER_KV + g) * HEAD_DIM:(j * Q_PER_KV + g + 1) * HEAD_DIM] for g in range(Q_PER_KV)],
                axis=0)
            sink = jnp.zeros((rows, 1), F32)
            for g in range(Q_PER_KV):
                sink = jnp.where(rg == g, sink_ref[j * Q_PER_KV + g], sink)
            kh = kk[:, j * HEAD_DIM:(j + 1) * HEAD_DIM]
            vh = vv[:, j * HEAD_DIM:(j + 1) * HEAD_DIM]
            s = lax.dot_general(qj, kh, (((1,), (1,)), ((), ())), preferred_element_type=F32) * (HEAD_DIM ** -0.5)
            s = jnp.where(mask, s, -jnp.inf)
            o = _sink_softmax_pv(s, sink, vh)
            outs.extend(o[g * t:(g + 1) * t] for g in range(Q_PER_KV))
        outs_b.append(jnp.concatenate(outs, axis=-1))
    att = jnp.concatenate(outs_b, axis=0)
    o_ref[...] = _rms(att, g_ref[...]).astype(BF16)


def _attn_sample(sink, q, k, v, cache_k, cache_v, g_attn_out, b, t, bt):
    row = lambda w: pl.BlockSpec((bt * t, w), lambda i: (i, 0))
    cache = pl.BlockSpec((bt, WINDOW, KV_WIDTH), lambda i: (i, 0, 0))
    return pl.pallas_call(
        functools.partial(_attn_sample_kernel, bt=bt, t=t),
        grid=(b // bt,),
        in_specs=[pl.BlockSpec(memory_space=pltpu.SMEM), row(ATTN_WIDTH), row(KV_WIDTH), row(KV_WIDTH),
                  cache, cache, _full((1, ATTN_WIDTH))],
        out_specs=row(ATTN_WIDTH),
        out_shape=jax.ShapeDtypeStruct((b * t, ATTN_WIDTH), BF16),
        compiler_params=_cparams(("arbitrary",)),
        name="window_attn_sample",
    )(sink, q, k, v, cache_k, cache_v, g_attn_out)


def _ssm_kernel(u_ref, perm_ref, permt_ref, wb_ref, wc_ref, are_ref, aim_ref, d_ref, wglu_ref, g_ref,
                h0re_ref, h0im_ref, o_ref, hre_ref, him_ref, vre, vim, sre, sim, *, nb, s):
    c = pl.program_id(0)
    r = nb * s

    @pl.when(c == 0)
    def _():
        sre[...] = h0re_ref[...]
        sim[...] = h0im_ref[...]

    u = u_ref[...].reshape(r, SSM_WIDTH)
    u_hi = u.astype(BF16)
    u_lo = (u - u_hi.astype(F32)).astype(BF16)
    perm = perm_ref[...]
    u_tb = (jnp.dot(perm, u_hi, preferred_element_type=F32) + jnp.dot(perm, u_lo, preferred_element_type=F32))
    u_bf = u_tb.astype(BF16)
    half_w = SSM_WIDTH // 2
    for h in range(2):
        vv = jnp.dot(u_bf[:, h * half_w:(h + 1) * half_w], wb_ref[h], preferred_element_type=F32)
        vre[:, h * SSM_HALF:(h + 1) * SSM_HALF] = vv[:, :SSM_HALF]
        vim[:, h * SSM_HALF:(h + 1) * SSM_HALF] = vv[:, SSM_HALF:]

    lane_chunk = SSM_LANES // 2
    for rg in range(nb // SUBLANES):
        for lc in range(SSM_LANES // lane_chunk):
            lanes = slice(lc * lane_chunk, (lc + 1) * lane_chunk)
            rows0 = slice(rg * SUBLANES, (rg + 1) * SUBLANES)
            ar = jnp.broadcast_to(are_ref[:, lanes], (SUBLANES, lane_chunk))
            ai = jnp.broadcast_to(aim_ref[:, lanes], (SUBLANES, lane_chunk))

            def step(t, carry, lanes=lanes, rg=rg, ar=ar, ai=ai):
                hr, hi = carry
                row = pl.multiple_of(t * nb + rg * SUBLANES, SUBLANES)
                nhr = ar * hr - ai * hi + vre[pl.ds(row, SUBLANES), lanes]
                nhi = ar * hi + ai * hr + vim[pl.ds(row, SUBLANES), lanes]
                vre[pl.ds(row, SUBLANES), lanes] = nhr
                vim[pl.ds(row, SUBLANES), lanes] = nhi
                return nhr, nhi

            hr, hi = lax.fori_loop(0, s, step, (sre[rows0, lanes], sim[rows0, lanes]), unroll=True)
            sre[rows0, lanes] = hr
            sim[rows0, lanes] = hi

    ys = []
    for h in range(2):
        hcat = jnp.concatenate([vre[:, h * SSM_HALF:(h + 1) * SSM_HALF], vim[:, h * SSM_HALF:(h + 1) * SSM_HALF]],
                               axis=1).astype(BF16)
        ys.append(jnp.dot(hcat, wc_ref[h], preferred_element_type=F32))
    y = jnp.concatenate(ys, axis=1) + d_ref[...] * u_tb
    g = jax.nn.gelu(y)
    z = jnp.dot(g.astype(BF16), wglu_ref[...], preferred_element_type=F32)
    out = g * jax.nn.sigmoid(z)
    on = _rms(out, g_ref[...]).astype(BF16)
    o_bt = jnp.dot(permt_ref[...], on, preferred_element_type=F32).astype(BF16)
    o_ref[...] = o_bt.reshape(nb, s, SSM_WIDTH)

    @pl.when(c == pl.num_programs(0) - 1)
    def _():
        hre_ref[...] = sre[...]
        him_ref[...] = sim[...]


def _ssm(u, h0_re, h0_im, ssm_w, s):
    nb, l, _ = u.shape
    r = nb * s
    ridx = jnp.arange(r)
    src = (ridx % nb) * s + ridx // nb
    perm = (src[:, None] == ridx[None, :]).astype(BF16)
    wb, wc, a_re, a_im, d_skip, w_glu, g_out = ssm_w
    seq_blk = pl.BlockSpec((nb, s, SSM_WIDTH), lambda c: (0, c, 0))
    st_blk = _full((nb, SSM_LANES))
    return pl.pallas_call(
        functools.partial(_ssm_kernel, nb=nb, s=s),
        grid=(l // s,),
        in_specs=[seq_blk, _full((r, r)), _full((r, r)), _full(wb.shape), _full(wc.shape),
                  _full((1, SSM_LANES)), _full((1, SSM_LANES)), _full((1, SSM_WIDTH)),
                  _full((SSM_WIDTH, SSM_WIDTH)), _full((1, SSM_WIDTH)), st_blk, st_blk],
        out_specs=(seq_blk, st_blk, st_blk),
        out_shape=(jax.ShapeDtypeStruct((nb, l, SSM_WIDTH), BF16),
                   jax.ShapeDtypeStruct((nb, SSM_LANES), F32), jax.ShapeDtypeStruct((nb, SSM_LANES), F32)),
        scratch_shapes=[pltpu.VMEM((r, SSM_LANES), F32), pltpu.VMEM((r, SSM_LANES), F32),
                        pltpu.VMEM((nb, SSM_LANES), F32), pltpu.VMEM((nb, SSM_LANES), F32)],
        compiler_params=_cparams(("arbitrary",), VMEM_LIMIT),
        name="ssm_scan",
    )(u, perm, perm.T, wb, wc, a_re, a_im, d_skip, w_glu, g_out, h0_re, h0_im)


def _outproj_kernel(x_ref, att_ref, ssm_ref, wa_ref, ws_ref, gca_ref, wcq_ref, h_ref, qc_ref):
    h = (x_ref[...] + jnp.dot(att_ref[...], wa_ref[...], preferred_element_type=F32)
         + jnp.dot(ssm_ref[...], ws_ref[...], preferred_element_type=F32))
    h_ref[...] = h
    xn = _rms(h, gca_ref[...]).astype(BF16)
    qc_ref[...] = jnp.dot(xn, wcq_ref[...], preferred_element_type=F32).astype(BF16)


def _outproj(x, att, ssm, w_att, w_ssm, g_ca, w_cq, tm):
    t = x.shape[0]
    row = lambda w: pl.BlockSpec((tm, w), lambda i: (i, 0))
    return pl.pallas_call(
        _outproj_kernel,
        grid=(t // tm,),
        in_specs=[row(D_MODEL), row(ATTN_WIDTH), row(SSM_WIDTH), _full((ATTN_WIDTH, D_MODEL)),
                  _full((SSM_WIDTH, D_MODEL)), _full((1, D_MODEL)), _full((D_MODEL, CA_WIDTH))],
        out_specs=(row(D_MODEL), row(CA_WIDTH)),
        out_shape=(jax.ShapeDtypeStruct((t, D_MODEL), F32), jax.ShapeDtypeStruct((t, CA_WIDTH), BF16)),
        compiler_params=_cparams(("arbitrary",)),
        name="out_projection",
    )(x, att, ssm, w_att, w_ssm, g_ca, w_cq)


def _memkv_kernel(m_ref, g_ref, w_ref, k_ref, v_ref):
    m = _rms(m_ref[...], g_ref[...]).astype(BF16)
    kv = jnp.dot(m, w_ref[...], preferred_element_type=F32)
    k_ref[...] = kv[:, :CA_WIDTH]
    v_ref[...] = kv[:, CA_WIDTH:]


def _memkv(mem, g_mem, w_mkv_bf, tm):
    t = mem.shape[0]
    row = lambda w: pl.BlockSpec((tm, w), lambda i: (i, 0))
    return pl.pallas_call(
        _memkv_kernel,
        grid=(t // tm,),
        in_specs=[row(D_MODEL), _full((1, D_MODEL)), _full((D_MODEL, 2 * CA_WIDTH))],
        out_specs=(row(CA_WIDTH), row(CA_WIDTH)),
        out_shape=(jax.ShapeDtypeStruct((t, CA_WIDTH), F32), jax.ShapeDtypeStruct((t, CA_WIDTH), F32)),
        compiler_params=_cparams(("arbitrary",)),
        name="memory_kv",
    )(mem, g_mem, w_mkv_bf)


def _xattn_kernel(qc_ref, mk_ref, mv_ref, h_ref, wco_ref, gmoe_ref, wr_ref, br_ref, tri_ref, cin_ref,
                  h2_ref, xn_ref, ti_ref, tw_ref, cout_ref, cnt, *, nbat, rows, head_split):
    step = pl.program_id(0)

    @pl.when(step == 0)
    def _():
        cnt[...] = cin_ref[...]

    q_all = qc_ref[...]
    outs_b = []
    for bi in range(nbat):
        q = q_all[bi * rows:(bi + 1) * rows]
        outs = []
        for hd in range(CA_HEADS):
            sl = slice(hd * CA_HEAD_DIM, (hd + 1) * CA_HEAD_DIM)
            if head_split:
                mk = mk_ref[bi, :, hd, :].astype(BF16)
                mv = mv_ref[bi, :, hd, :].astype(BF16)
            else:
                mk = mk_ref[bi, :, sl].astype(BF16)
                mv = mv_ref[bi, :, sl].astype(BF16)
            s = lax.dot_general(q[:, sl], mk, (((1,), (1,)), ((), ())),
                                preferred_element_type=F32) * (CA_HEAD_DIM ** -0.5)
            m = jnp.max(s, axis=-1, keepdims=True)
            p = jnp.exp(s - m)
            denom = jnp.sum(p, axis=-1, keepdims=True)
            outs.append(jnp.dot(p.astype(BF16), mv, preferred_element_type=F32) / denom)
        outs_b.append(jnp.concatenate(outs, axis=-1))
    o = jnp.concatenate(outs_b, axis=0).astype(BF16)
    h2 = h_ref[...] + jnp.dot(o, wco_ref[...], preferred_element_type=F32)
    h2_ref[...] = h2
    xn = _rms(h2, gmoe_ref[...])
    xn_ref[...] = xn
    logits = jnp.dot(xn, wr_ref[...], preferred_element_type=F32, precision=lax.Precision.HIGHEST) + br_ref[...]
    n = logits.shape[0]
    eidx = lax.broadcasted_iota(jnp.int32, (n, N_EXPERTS), 1).astype(F32)
    lane = lax.broadcasted_iota(jnp.int32, (n, LANES), 1)
    ti = jnp.zeros((n, LANES), F32)
    tv = jnp.zeros((n, LANES), F32)
    work = logits
    vals = []
    hits = []
    for k in range(TOP_K):
        mx = jnp.max(work, axis=-1, keepdims=True)
        idx = jnp.min(jnp.where(work == mx, eidx, float(N_EXPERTS)), axis=-1, keepdims=True)
        vals.append(mx)
        hits.append(eidx == idx)
        ti = jnp.where(lane == k, idx, ti)
        work = jnp.where(hits[k], -jnp.inf, work)
    ex = [jnp.exp(v - vals[0]) for v in vals]
    tot = ex[0] + ex[1] + ex[2] + ex[3]
    for k in range(TOP_K):
        tv = jnp.where(lane == k, ex[k] / tot, tv)
    sel = jnp.zeros((n, N_EXPERTS), F32)
    for k in range(TOP_K):
        sel = sel + jnp.where(hits[k], 1.0, 0.0)
    before = jnp.dot(tri_ref[...], sel.astype(BF16), preferred_element_type=F32) + cnt[...]
    for k in range(TOP_K):
        rank = jnp.sum(jnp.where(hits[k], before, 0.0), axis=-1, keepdims=True)
        ti = jnp.where(lane == TOP_K + k, rank, ti)
    cnt[...] = cnt[...] + jnp.sum(sel, axis=0, keepdims=True)
    ti_ref[...] = ti.astype(jnp.int32)
    tw_ref[...] = tv

    @pl.when(step == pl.num_programs(0) - 1)
    def _():
        cout_ref[...] = cnt[...]


def _xattn(qc, mk, mv, h1, w_co, g_moe, w_router, b_router, counts_in, nbat, rows):
    t = qc.shape[0]
    tm = nbat * rows
    row = lambda w: pl.BlockSpec((tm, w), lambda i: (i, 0))
    seq_rows = t // mk.shape[0]
    head_split = mk.ndim == 4
    mem_blk = (nbat,) + mk.shape[1:]
    mem = pl.BlockSpec(mem_blk, lambda i: ((i * tm) // (seq_rows * nbat),) + (0,) * (len(mem_blk) - 1))
    tri = jnp.tri(tm, k=-1, dtype=BF16)
    return pl.pallas_call(
        functools.partial(_xattn_kernel, nbat=nbat, rows=rows, head_split=head_split),
        grid=(t // tm,),
        in_specs=[row(CA_WIDTH), mem, mem, row(D_MODEL), _full((CA_WIDTH, D_MODEL)), _full((1, D_MODEL)),
                  _full((D_MODEL, N_EXPERTS)), _full((1, N_EXPERTS)), _full((tm, tm)), _full((1, N_EXPERTS))],
        out_specs=(row(D_MODEL), row(D_MODEL), row(LANES), row(LANES), _full((1, N_EXPERTS))),
        out_shape=(jax.ShapeDtypeStruct((t, D_MODEL), F32), jax.ShapeDtypeStruct((t, D_MODEL), F32),
                   jax.ShapeDtypeStruct((t, LANES), jnp.int32), jax.ShapeDtypeStruct((t, LANES), F32),
                   jax.ShapeDtypeStruct((1, N_EXPERTS), F32)),
        scratch_shapes=[pltpu.VMEM((1, N_EXPERTS), F32)],
        compiler_params=_cparams(("arbitrary",), VMEM_LIMIT),
        name="cross_attn_router",
    )(qc, mk, mv, h1, w_co, g_moe, w_router, b_router, tri, counts_in)


def _row_copy(src, src_row, dst, dst_row, sem):
    return pltpu.make_async_copy(src.at[pl.ds(src_row, 1)], dst.at[pl.ds(dst_row, 1)], sem)


def _scatter_kernel(dest_hbm, x_ref, *rest, tm):
    xs_hbm, idx, isem, sem = rest[-4:]
    i = pl.program_id(0)
    n = pl.num_programs(0)
    slot = i % 2

    def idx_copy(step, sl):
        return pltpu.make_async_copy(dest_hbm.at[step], idx.at[sl], isem.at[sl])

    @pl.when(i == 0)
    def _():
        idx_copy(0, 0).start()

    @pl.when(i + 1 < n)
    def _():
        idx_copy(i + 1, 1 - slot).start()

    idx_copy(i, slot).wait()

    def issue(r, carry):
        for k in range(TOP_K):
            _row_copy(x_ref, r, xs_hbm, idx[slot, r * TOP_K + k], sem).start()
        return carry

    lax.fori_loop(0, tm, issue, 0, unroll=4)

    for k in range(TOP_K):
        pltpu.make_async_copy(x_ref, xs_hbm.at[pl.ds(0, tm)], sem).wait()


def _scatter_rows(dest, x, n_slots, tm, xs_prev=None):
    t = x.shape[0]
    args = [dest.reshape(t // tm, tm * TOP_K), x]
    in_specs = [pl.BlockSpec(memory_space=pl.ANY), pl.BlockSpec((tm, D_MODEL), lambda i: (i, 0))]
    aliases = {}
    if xs_prev is not None:
        args.append(xs_prev)
        in_specs.append(pl.BlockSpec(memory_space=pl.ANY))
        aliases = {2: 0}
    return pl.pallas_call(
        functools.partial(_scatter_kernel, tm=tm),
        grid=(t // tm,),
        in_specs=in_specs,
        out_specs=pl.BlockSpec(memory_space=pl.ANY),
        out_shape=jax.ShapeDtypeStruct((n_slots, D_MODEL), F32),
        scratch_shapes=[pltpu.SMEM((2, tm * TOP_K), jnp.int32), pltpu.SemaphoreType.DMA((2,)),
                        pltpu.SemaphoreType.DMA(())],
        input_output_aliases=aliases,
        compiler_params=_cparams(("arbitrary",)),
        name="moe_scatter",
    )(*args)


def _expert_kernel(blk_e_ref, nused_ref, xs_ref, wg_ref, bg_ref, wu_ref, bu_ref, wd_ref, bd_ref, ys_ref,
                   wg_bf, wu_bf, wd_bf):
    i = pl.program_id(0)
    active = i < nused_ref[0]
    prev_e = blk_e_ref[jnp.maximum(i - 1, 0)]
    new_expert = (i == 0) | (blk_e_ref[i] != prev_e)

    @pl.when(active & new_expert)
    def _():
        wg_bf[...] = wg_ref[0].astype(BF16)
        wu_bf[...] = wu_ref[0].astype(BF16)
        wd_bf[...] = wd_ref[0].astype(BF16)

    @pl.when(active)
    def _():
        x = xs_ref[...].astype(BF16)
        gate = jnp.minimum(jnp.dot(x, wg_bf[...], preferred_element_type=F32) + bg_ref[0], SWIGLU_LIMIT)
        up = jnp.clip(jnp.dot(x, wu_bf[...], preferred_element_type=F32) + bu_ref[0], -SWIGLU_LIMIT, SWIGLU_LIMIT)
        h = gate * jax.nn.sigmoid(SWIGLU_ALPHA * gate) * (up + 1.0)
        ys_ref[...] = jnp.dot(h.astype(BF16), wd_bf[...], preferred_element_type=F32) + bd_ref[0]


def _experts(blk_e, n_used, xs, w_gate, b_gate, w_up, b_up, w_down, b_down):
    n_slots = xs.shape[0]
    n_blocks = n_slots // MOE_BLOCK
    d_ff = w_gate.shape[-1]
    blk = lambda i, be, nu: (jnp.minimum(i, nu[0] - 1), 0)
    wspec = lambda a, b: pl.BlockSpec((1, a, b), lambda i, be, nu: (be[i], 0, 0))
    grid_spec = pltpu.PrefetchScalarGridSpec(
        num_scalar_prefetch=2,
        grid=(n_blocks,),
        in_specs=[pl.BlockSpec((MOE_BLOCK, D_MODEL), blk), wspec(D_MODEL, d_ff), wspec(1, d_ff),
                  wspec(D_MODEL, d_ff), wspec(1, d_ff), wspec(d_ff, D_MODEL), wspec(1, D_MODEL)],
        out_specs=pl.BlockSpec((MOE_BLOCK, D_MODEL), blk),
        scratch_shapes=[pltpu.VMEM((D_MODEL, d_ff), BF16), pltpu.VMEM((D_MODEL, d_ff), BF16),
                        pltpu.VMEM((d_ff, D_MODEL), BF16)],
    )
    return pl.pallas_call(
        _expert_kernel,
        grid_spec=grid_spec,
        out_shape=jax.ShapeDtypeStruct((n_slots, D_MODEL), F32),
        compiler_params=_cparams(("arbitrary",), VMEM_LIMIT),
        name="moe_experts",
    )(blk_e, n_used, xs, w_gate, b_gate.reshape(N_EXPERTS, 1, d_ff), w_up, b_up.reshape(N_EXPERTS, 1, d_ff),
      w_down, b_down.reshape(N_EXPERTS, 1, D_MODEL))


def _combine_kernel(dest_hbm, ys_hbm, h_ref, tw_ref, g_ref, o_ref, idx, isem, buf, sem, *, tm):
    i = pl.program_id(0)
    n = pl.num_programs(0)
    slot = i % 2

    def idx_copy(step, sl):
        return pltpu.make_async_copy(dest_hbm.at[step], idx.at[sl], isem.at[sl])

    def issue(sl):
        def body(r, carry):
            for k in range(TOP_K):
                _row_copy(ys_hbm, idx[sl, r * TOP_K + k], buf.at[sl, k], r, sem.at[sl]).start()
            return carry
        lax.fori_loop(0, tm, body, 0, unroll=4)

    @pl.when(i == 0)
    def _():
        idx_copy(0, 0).start()
        idx_copy(0, 0).wait()
        issue(0)

        @pl.when(n > 1)
        def _():
            idx_copy(1, 1).start()

    @pl.when(i + 1 < n)
    def _():
        idx_copy(i + 1, 1 - slot).wait()
        issue(1 - slot)

    @pl.when(i + 2 < n)
    def _():
        idx_copy(i + 2, slot).start()

    for k in range(TOP_K):
        pltpu.make_async_copy(ys_hbm.at[pl.ds(0, tm)], buf.at[slot, k], sem.at[slot]).wait()
    tw = tw_ref[...]
    y = h_ref[...]
    for k in range(TOP_K):
        y = y + tw[:, k:k + 1] * buf[slot, k]
    o_ref[...] = _rms(y, g_ref[...])


def _combine(dest, ys, h2, tw, g_final, tm):
    t = h2.shape[0]
    row = lambda w: pl.BlockSpec((tm, w), lambda i: (i, 0))
    return pl.pallas_call(
        functools.partial(_combine_kernel, tm=tm),
        grid=(t // tm,),
        in_specs=[pl.BlockSpec(memory_space=pl.ANY), pl.BlockSpec(memory_space=pl.ANY), row(D_MODEL), row(LANES),
                  _full((1, D_MODEL))],
        out_specs=row(D_MODEL),
        out_shape=jax.ShapeDtypeStruct((t, D_MODEL), F32),
        scratch_shapes=[pltpu.SMEM((2, tm * TOP_K), jnp.int32), pltpu.SemaphoreType.DMA((2,)),
                        pltpu.VMEM((2, TOP_K, tm, D_MODEL), F32), pltpu.SemaphoreType.DMA((2,))],
        compiler_params=_cparams(("arbitrary",), VMEM_LIMIT),
        name="moe_combine",
    )(dest.reshape(t // tm, tm * TOP_K), ys, h2, tw, g_final)


def _route(counts, m):
    counts = counts.reshape(N_EXPERTS).astype(jnp.int32)
    padded = (counts + MOE_BLOCK - 1) // MOE_BLOCK * MOE_BLOCK
    pad_end = jnp.cumsum(padded)
    pad_start = pad_end - padded
    n_blocks = -(-(m + N_EXPERTS * (MOE_BLOCK - 1)) // MOE_BLOCK)
    blk_start = jnp.arange(n_blocks, dtype=jnp.int32) * MOE_BLOCK
    blk_e = jnp.minimum(jnp.sum((pad_end[None, :] <= blk_start[:, None]).astype(jnp.int32), axis=1),
                        N_EXPERTS - 1).astype(jnp.int32)
    n_used = (pad_end[-1] // MOE_BLOCK).astype(jnp.int32).reshape(1)
    return pad_start, blk_e, n_used, n_blocks * MOE_BLOCK


def _dest(ti, pad_start):
    e = ti[:, :TOP_K]
    onehot = e[:, :, None] == jnp.arange(N_EXPERTS, dtype=jnp.int32)[None, None, :]
    start = jnp.sum(jnp.where(onehot, pad_start[None, None, :], 0), axis=-1)
    return (start + ti[:, TOP_K:2 * TOP_K]).reshape(-1)


def _block_diag(blocks):
    g, a, b = blocks.shape
    eye = jnp.eye(g, dtype=blocks.dtype)
    return (blocks[:, :, None, :] * eye[:, None, :, None]).reshape(g * a, g * b)


def kernel(x_prompt, x_sample, mem_prompt, cache_win_k, cache_win_v, state_ssm_re, state_ssm_im, cache_mem_k,
           cache_mem_v, g_mix, w_in, sink, lam_re, lam_im, log_dt, b_re, b_im, c_re, c_im, d_skip, w_glu,
           g_attn_out, g_ssm_out, w_out, g_ca, g_mem, w_mk, w_mv, w_cq, w_co, g_moe, w_router, b_router,
           w_gate, b_gate, w_up, b_up, w_down, b_down, g_final):
    depth = g_mix.shape[0]
    assert depth == 1
    bp, lp, _ = x_prompt.shape
    bs, ls, _ = x_sample.shape
    tp, ts = bp * lp, bs * ls

    w_in_bf = w_in[0].astype(BF16)
    w_att_bf = w_out[0, :ATTN_WIDTH].astype(BF16)
    w_ssm_bf = w_out[0, ATTN_WIDTH:].astype(BF16)
    w_cq_bf = w_cq[0].astype(BF16)
    w_co_bf = w_co[0].astype(BF16)
    w_mkv_bf = jnp.concatenate([w_mk[0], w_mv[0]], axis=1).astype(BF16)
    w_glu_bf = w_glu[0].astype(BF16)
    row = lambda a: a.reshape(1, -1)

    a_re, a_im, bb_re, bb_im = _discretize(lam_re[0], lam_im[0], log_dt[0], b_re[0].transpose(0, 2, 1),
                                           b_im[0].transpose(0, 2, 1))
    hg = SSM_GROUPS // 2
    wb = jnp.stack([jnp.concatenate([_block_diag(bb_re[h * hg:(h + 1) * hg]),
                                     _block_diag(bb_im[h * hg:(h + 1) * hg])], axis=1) for h in range(2)]).astype(BF16)
    cre_t = c_re[0].transpose(0, 2, 1)
    cim_t = c_im[0].transpose(0, 2, 1)
    wc = jnp.stack([jnp.concatenate([_block_diag(cre_t[h * hg:(h + 1) * hg]),
                                     -_block_diag(cim_t[h * hg:(h + 1) * hg])], axis=0) for h in range(2)]).astype(BF16)
    ssm_w = (wb, wc, a_re.reshape(1, SSM_LANES), a_im.reshape(1, SSM_LANES), row(d_skip[0]), w_glu_bf,
             row(g_ssm_out[0]))

    xp = x_prompt.reshape(tp, D_MODEL)
    xs_ = x_sample.reshape(ts, D_MODEL)

    qp, kp, vp, up = _inproj(xp, row(g_mix[0]), w_in_bf, 512)
    qs, ks, vs, us = _inproj(xs_, row(g_mix[0]), w_in_bf, 256)

    att_p = _attn_prompt(sink[0], qp, kp, vp, row(g_attn_out[0]), bp, lp)
    ck = cache_win_k[0].reshape(bs, WINDOW, KV_WIDTH)
    cv = cache_win_v[0].reshape(bs, WINDOW, KV_WIDTH)
    att_s = _attn_sample(sink[0], qs, ks, vs, ck, cv, row(g_attn_out[0]), bs, ls, 16)

    zero_state = jnp.zeros((bp, SSM_LANES), F32)
    ssm_p, p_sre, p_sim = _ssm(up.reshape(bp, lp, SSM_WIDTH), zero_state, zero_state, ssm_w, 32)
    ssm_s, s_sre, s_sim = _ssm(us.reshape(bs, ls, SSM_WIDTH), state_ssm_re[0].reshape(bs, SSM_LANES),
                               state_ssm_im[0].reshape(bs, SSM_LANES), ssm_w, ls)

    h1p, qcp = _outproj(xp, att_p, ssm_p.reshape(tp, SSM_WIDTH), w_att_bf, w_ssm_bf, row(g_ca[0]), w_cq_bf, 512)
    h1s, qcs = _outproj(xs_, att_s, ssm_s.reshape(ts, SSM_WIDTH), w_att_bf, w_ssm_bf, row(g_ca[0]), w_cq_bf, 256)

    mk_p, mv_p = _memkv(mem_prompt.reshape(bp * N_MEM, D_MODEL), row(g_mem[0]), w_mkv_bf, 512)

    no_counts = jnp.zeros((1, N_EXPERTS), F32)
    h2p, xnp_, tip, twp, cnt_p = _xattn(qcp, mk_p.reshape(bp, N_MEM, CA_WIDTH), mv_p.reshape(bp, N_MEM, CA_WIDTH),
                                        h1p, w_co_bf, row(g_moe[0]), w_router[0], row(b_router[0]), no_counts,
                                        1, 512)
    h2s, xns_, tis, tws, cnt = _xattn(qcs, cache_mem_k[0], cache_mem_v[0], h1s, w_co_bf, row(g_moe[0]),
                                      w_router[0], row(b_router[0]), cnt_p, 4, ls)

    pad_start, blk_e, n_used, n_slots = _route(cnt, (tp + ts) * TOP_K)
    dest_p = _dest(tip, pad_start)
    dest_s = _dest(tis, pad_start)
    xs_sorted = _scatter_rows(dest_p, xnp_, n_slots, 256)
    xs_sorted = _scatter_rows(dest_s, xns_, n_slots, 256, xs_prev=xs_sorted)
    ys = _experts(blk_e, n_used, xs_sorted, w_gate[0], b_gate[0], w_up[0], b_up[0], w_down[0], b_down[0])
    y_prompt = _combine(dest_p, ys, h2p, twp, row(g_final), 128).reshape(bp, lp, D_MODEL)
    y_sample = _combine(dest_s, ys, h2s, tws, row(g_final), 128).reshape(bs, ls, D_MODEL)

    kp4 = kp.reshape(bp, lp, N_KV_HEADS, HEAD_DIM)
    vp4 = vp.reshape(bp, lp, N_KV_HEADS, HEAD_DIM)
    p_win_k = kp4[:, -WINDOW:][None]
    p_win_v = vp4[:, -WINDOW:][None]
    s_win_k = jnp.concatenate([cache_win_k[0], ks.reshape(bs, ls, N_KV_HEADS, HEAD_DIM)], axis=1)[:, -WINDOW:][None]
    s_win_v = jnp.concatenate([cache_win_v[0], vs.reshape(bs, ls, N_KV_HEADS, HEAD_DIM)], axis=1)[:, -WINDOW:][None]
    st = lambda a, b: a.reshape(1, b, SSM_GROUPS, SSM_STATE)
    p_mem_k = mk_p.reshape(1, bp, N_MEM, CA_HEADS, CA_HEAD_DIM)
    p_mem_v = mv_p.reshape(1, bp, N_MEM, CA_HEADS, CA_HEAD_DIM)
    return (y_prompt, y_sample, p_win_k, p_win_v, st(p_sre, bp), st(p_sim, bp), p_mem_k, p_mem_v,
            s_win_k, s_win_v, st(s_sre, bs), st(s_sim, bs))
```

```python
import functools
import math

import jax
import jax.numpy as jnp
from jax import lax
from jax.experimental import pallas as pl
from jax.experimental.pallas import tpu as pltpu

F32 = jnp.float32
BF16 = jnp.bfloat16

D_MODEL = 1024
HEAD_DIM = 64
N_Q_HEADS = 8
N_KV_HEADS = 2
Q_PER_KV = 4
ATTN_WIDTH = 512
KV_WIDTH = 128
WINDOW = 128
SSM_WIDTH = 512
SSM_GROUP = 16
SSM_GROUPS = 32
SSM_STATE = 64
SSM_LANES = SSM_GROUPS * SSM_STATE
SSM_HALF = SSM_LANES // 2
IN_WIDTH = ATTN_WIDTH + 2 * KV_WIDTH + SSM_WIDTH
N_MEM = 256
CA_HEADS = 4
CA_HEAD_DIM = 128
CA_WIDTH = 512
N_EXPERTS = 32
TOP_K = 4
SWIGLU_ALPHA = 1.702
SWIGLU_LIMIT = 7.0
RMS_EPS = 1e-5

SUBLANES = 8
LANES = 128
MOE_BLOCK = 256
DMA_BATCH = 16
VMEM_LIMIT = 56 * 1024 * 1024


def _cparams(sem, vmem=None):
    return pltpu.CompilerParams(dimension_semantics=sem, vmem_limit_bytes=vmem)


def _rms(x, g):
    return x * lax.rsqrt(jnp.mean(x * x, axis=-1, keepdims=True) + RMS_EPS) * g


def _full(shape):
    return pl.BlockSpec(shape, lambda *_: (0,) * len(shape))


def _disc_kernel(lr_ref, li_ref, ldt_ref, bre_ref, bim_ref, are_ref, aim_ref, bbre_ref, bbim_ref):
    lr = lr_ref[...]
    li = li_ref[...]
    dt = jnp.exp(ldt_ref[...])
    mag = jnp.exp(lr * dt)
    ang = li * dt
    ab_re = mag * jnp.cos(ang)
    ab_im = mag * jnp.sin(ang)
    nr = ab_re - 1.0
    ni = ab_im
    den = lr * lr + li * li
    f_re = (nr * lr + ni * li) / den
    f_im = (ni * lr - nr * li) / den
    are_ref[...] = ab_re
    aim_ref[...] = ab_im
    br = bre_ref[...]
    bi = bim_ref[...]
    bbre_ref[...] = f_re[:, None, :] * br - f_im[:, None, :] * bi
    bbim_ref[...] = f_re[:, None, :] * bi + f_im[:, None, :] * br


def _discretize(lam_re, lam_im, log_dt, b_re_t, b_im_t):
    g, n = lam_re.shape
    c = b_re_t.shape[1]
    return pl.pallas_call(
        _disc_kernel,
        out_shape=(jax.ShapeDtypeStruct((g, n), F32), jax.ShapeDtypeStruct((g, n), F32),
                   jax.ShapeDtypeStruct((g, c, n), F32), jax.ShapeDtypeStruct((g, c, n), F32)),
        name="ssm_discretize",
    )(lam_re, lam_im, log_dt.reshape(g, 1), b_re_t, b_im_t)


def _inproj_kernel(x_ref, g_ref, w_ref, q_ref, k_ref, v_ref, u_ref):
    a = _rms(x_ref[...], g_ref[...]).astype(BF16)
    proj = jnp.dot(a, w_ref[...], preferred_element_type=F32)
    q_ref[...] = proj[:, :ATTN_WIDTH].astype(BF16)
    k_ref[...] = proj[:, ATTN_WIDTH:ATTN_WIDTH + KV_WIDTH]
    v_ref[...] = proj[:, ATTN_WIDTH + KV_WIDTH:ATTN_WIDTH + 2 * KV_WIDTH]
    u_ref[...] = proj[:, ATTN_WIDTH + 2 * KV_WIDTH:]


def _inproj(x, g_mix, w_in_bf, tm):
    t = x.shape[0]
    row = lambda w: pl.BlockSpec((tm, w), lambda i: (i, 0))
    return pl.pallas_call(
        _inproj_kernel,
        grid=(t // tm,),
        in_specs=[row(D_MODEL), _full((1, D_MODEL)), _full((D_MODEL, IN_WIDTH))],
        out_specs=(row(ATTN_WIDTH), row(KV_WIDTH), row(KV_WIDTH), row(SSM_WIDTH)),
        out_shape=(jax.ShapeDtypeStruct((t, ATTN_WIDTH), BF16), jax.ShapeDtypeStruct((t, KV_WIDTH), F32),
                   jax.ShapeDtypeStruct((t, KV_WIDTH), F32), jax.ShapeDtypeStruct((t, SSM_WIDTH), F32)),
        compiler_params=_cparams(("arbitrary",)),
        name="in_projection",
    )(x, g_mix, w_in_bf)


def _sink_softmax_pv(s, sink, v_bf):
    m = jnp.maximum(jnp.max(s, axis=-1, keepdims=True), sink)
    p = jnp.exp(s - m)
    denom = jnp.sum(p, axis=-1, keepdims=True) + jnp.exp(sink - m)
    return jnp.dot(p.astype(BF16), v_bf, preferred_element_type=F32) / denom


def _attn_prompt_kernel(sink_ref, q_ref, kc_ref, kp_ref, vc_ref, vp_ref, g_ref, o_ref):
    i = pl.program_id(1)
    q = q_ref[...]
    kk = jnp.concatenate([kp_ref[...], kc_ref[...]], axis=0).astype(BF16)
    vv = jnp.concatenate([vp_ref[...], vc_ref[...]], axis=0).astype(BF16)
    qi = lax.broadcasted_iota(jnp.int32, (WINDOW, 2 * WINDOW), 0)
    kj = lax.broadcasted_iota(jnp.int32, (WINDOW, 2 * WINDOW), 1)
    rel = qi + WINDOW - kj
    mask = (rel >= 0) & (rel < WINDOW) & ((kj >= WINDOW) | (i > 0))
    outs = []
    for h in range(N_Q_HEADS):
        j = h // Q_PER_KV
        qh = q[:, h * HEAD_DIM:(h + 1) * HEAD_DIM]
        kh = kk[:, j * HEAD_DIM:(j + 1) * HEAD_DIM]
        vh = vv[:, j * HEAD_DIM:(j + 1) * HEAD_DIM]
        s = lax.dot_general(qh, kh, (((1,), (1,)), ((), ())), preferred_element_type=F32) * (HEAD_DIM ** -0.5)
        s = jnp.where(mask, s, -jnp.inf)
        outs.append(_sink_softmax_pv(s, sink_ref[h], vh))
    att = jnp.concatenate(outs, axis=-1)
    o_ref[...] = _rms(att, g_ref[...]).astype(BF16)


def _attn_prompt(sink, q, k, v, g_attn_out, b, l):
    nb = l // WINDOW
    cur = lambda w: pl.BlockSpec((WINDOW, w), lambda bi, i: (bi * nb + i, 0))
    prev = lambda w: pl.BlockSpec((WINDOW, w), lambda bi, i: (bi * nb + jnp.maximum(i - 1, 0), 0))
    return pl.pallas_call(
        _attn_prompt_kernel,
        grid=(b, nb),
        in_specs=[pl.BlockSpec(memory_space=pltpu.SMEM), cur(ATTN_WIDTH), cur(KV_WIDTH), prev(KV_WIDTH),
                  cur(KV_WIDTH), prev(KV_WIDTH), _full((1, ATTN_WIDTH))],
        out_specs=cur(ATTN_WIDTH),
        out_shape=jax.ShapeDtypeStruct((b * l, ATTN_WIDTH), BF16),
        compiler_params=_cparams(("arbitrary", "arbitrary")),
        name="window_attn_prompt",
    )(sink, q, k, k, v, v, g_attn_out)


def _attn_sample_kernel(sink_ref, q_ref, kn_ref, vn_ref, ck_ref, cv_ref, g_ref, o_ref, *, bt, t):
    nk = WINDOW + t
    rows = Q_PER_KV * t
    ri = lax.broadcasted_iota(jnp.int32, (rows, nk), 0)
    kj = lax.broadcasted_iota(jnp.int32, (rows, nk), 1)
    tq = ri % t
    mask = ((kj < WINDOW) & (kj > tq)) | ((kj >= WINDOW) & ((kj - WINDOW) <= tq))
    rg = lax.broadcasted_iota(jnp.int32, (rows, 1), 0) // t
    q_all = q_ref[...]
    kn_all = kn_ref[...]
    vn_all = vn_ref[...]
    outs_b = []
    for bi in range(bt):
        q = q_all[bi * t:(bi + 1) * t]
        kk = jnp.concatenate([ck_ref[bi], kn_all[bi * t:(bi + 1) * t]], axis=0).astype(BF16)
        vv = jnp.concatenate([cv_ref[bi], vn_all[bi * t:(bi + 1) * t]], axis=0).astype(BF16)
        outs = []
        for j in range(N_KV_HEADS):
            qj = jnp.concatenate(
                [q[:, (j * Q_PER_KV + g) * HEAD_DIM:(j * Q_PER_KV + g + 1) * HEAD_DIM] for g in range(Q_PER_KV)],
                axis=0)
            sink = jnp.zeros((rows, 1), F32)
            for g in range(Q_PER_KV):
                sink = jnp.where(rg == g, sink_ref[j * Q_PER_KV + g], sink)
            kh = kk[:, j * HEAD_DIM:(j + 1) * HEAD_DIM]
            vh = vv[:, j * HEAD_DIM:(j + 1) * HEAD_DIM]
            s = lax.dot_general(qj, kh, (((1,), (1,)), ((), ())), preferred_element_type=F32) * (HEAD_DIM ** -0.5)
            s = jnp.where(mask, s, -jnp.inf)
            o = _sink_softmax_pv(s, sink, vh)
            outs.extend(o[g * t:(g + 1) * t] for g in range(Q_PER_KV))
        outs_b.append(jnp.concatenate(outs, axis=-1))
    att = jnp.concatenate(outs_b, axis=0)
    o_ref[...] = _rms(att, g_ref[...]).astype(BF16)


def _attn_sample(sink, q, k, v, cache_k, cache_v, g_attn_out, b, t, bt):
    row = lambda w: pl.BlockSpec((bt * t, w), lambda i: (i, 0))
    cache = pl.BlockSpec((bt, WINDOW, KV_WIDTH), lambda i: (i, 0, 0))
    return pl.pallas_call(
        functools.partial(_attn_sample_kernel, bt=bt, t=t),
        grid=(b // bt,),
        in_specs=[pl.BlockSpec(memory_space=pltpu.SMEM), row(ATTN_WIDTH), row(KV_WIDTH), row(KV_WIDTH),
                  cache, cache, _full((1, ATTN_WIDTH))],
        out_specs=row(ATTN_WIDTH),
        out_shape=jax.ShapeDtypeStruct((b * t, ATTN_WIDTH), BF16),
        compiler_params=_cparams(("arbitrary",)),
        name="window_attn_sample",
    )(sink, q, k, v, cache_k, cache_v, g_attn_out)


def _ssm_kernel(u_ref, perm_ref, permt_ref, wb_ref, wc_ref, are_ref, aim_ref, d_ref, wglu_ref, g_ref,
                h0re_ref, h0im_ref, o_ref, hre_ref, him_ref, vre, vim, sre, sim, *, nb, s):
    c = pl.program_id(0)
    r = nb * s

    @pl.when(c == 0)
    def _():
        sre[...] = h0re_ref[...]
        sim[...] = h0im_ref[...]

    u = u_ref[...].reshape(r, SSM_WIDTH)
    u_hi = u.astype(BF16)
    u_lo = (u - u_hi.astype(F32)).astype(BF16)
    perm = perm_ref[...]
    u_tb = (jnp.dot(perm, u_hi, preferred_element_type=F32) + jnp.dot(perm, u_lo, preferred_element_type=F32))
    u_bf = u_tb.astype(BF16)
    half_w = SSM_WIDTH // 2
    for h in range(2):
        vv = jnp.dot(u_bf[:, h * half_w:(h + 1) * half_w], wb_ref[h], preferred_element_type=F32)
        vre[:, h * SSM_HALF:(h + 1) * SSM_HALF] = vv[:, :SSM_HALF]
        vim[:, h * SSM_HALF:(h + 1) * SSM_HALF] = vv[:, SSM_HALF:]

    lane_chunk = SSM_LANES // 2
    for rg in range(nb // SUBLANES):
        for lc in range(SSM_LANES // lane_chunk):
            lanes = slice(lc * lane_chunk, (lc + 1) * lane_chunk)
            rows0 = slice(rg * SUBLANES, (rg + 1) * SUBLANES)
            ar = jnp.broadcast_to(are_ref[:, lanes], (SUBLANES, lane_chunk))
            ai = jnp.broadcast_to(aim_ref[:, lanes], (SUBLANES, lane_chunk))

            def step(t, carry, lanes=lanes, rg=rg, ar=ar, ai=ai):
                hr, hi = carry
                row = pl.multiple_of(t * nb + rg * SUBLANES, SUBLANES)
                nhr = ar * hr - ai * hi + vre[pl.ds(row, SUBLANES), lanes]
                nhi = ar * hi + ai * hr + vim[pl.ds(row, SUBLANES), lanes]
                vre[pl.ds(row, SUBLANES), lanes] = nhr
                vim[pl.ds(row, SUBLANES), lanes] = nhi
                return nhr, nhi

            hr, hi = lax.fori_loop(0, s, step, (sre[rows0, lanes], sim[rows0, lanes]), unroll=True)
            sre[rows0, lanes] = hr
            sim[rows0, lanes] = hi

    ys = []
    for h in range(2):
        hcat = jnp.concatenate([vre[:, h * SSM_HALF:(h + 1) * SSM_HALF], vim[:, h * SSM_HALF:(h + 1) * SSM_HALF]],
                               axis=1).astype(BF16)
        ys.append(jnp.dot(hcat, wc_ref[h], preferred_element_type=F32))
    y = jnp.concatenate(ys, axis=1) + d_ref[...] * u_tb
    g = jax.nn.gelu(y)
    z = jnp.dot(g.astype(BF16), wglu_ref[...], preferred_element_type=F32)
    out = g * jax.nn.sigmoid(z)
    on = _rms(out, g_ref[...]).astype(BF16)
    o_bt = jnp.dot(permt_ref[...], on, preferred_element_type=F32).astype(BF16)
    o_ref[...] = o_bt.reshape(nb, s, SSM_WIDTH)

    @pl.when(c == pl.num_programs(0) - 1)
    def _():
        hre_ref[...] = sre[...]
        him_ref[...] = sim[...]


def _ssm(u, h0_re, h0_im, ssm_w, s):
    nb, l, _ = u.shape
    r = nb * s
    ridx = jnp.arange(r)
    src = (ridx % nb) * s + ridx // nb
    perm = (src[:, None] == ridx[None, :]).astype(BF16)
    wb, wc, a_re, a_im, d_skip, w_glu, g_out = ssm_w
    seq_blk = pl.BlockSpec((nb, s, SSM_WIDTH), lambda c: (0, c, 0))
    st_blk = _full((nb, SSM_LANES))
    return pl.pallas_call(
        functools.partial(_ssm_kernel, nb=nb, s=s),
        grid=(l // s,),
        in_specs=[seq_blk, _full((r, r)), _full((r, r)), _full(wb.shape), _full(wc.shape),
                  _full((1, SSM_LANES)), _full((1, SSM_LANES)), _full((1, SSM_WIDTH)),
                  _full((SSM_WIDTH, SSM_WIDTH)), _full((1, SSM_WIDTH)), st_blk, st_blk],
        out_specs=(seq_blk, st_blk, st_blk),
        out_shape=(jax.ShapeDtypeStruct((nb, l, SSM_WIDTH), BF16),
                   jax.ShapeDtypeStruct((nb, SSM_LANES), F32), jax.ShapeDtypeStruct((nb, SSM_LANES), F32)),
        scratch_shapes=[pltpu.VMEM((r, SSM_LANES), F32), pltpu.VMEM((r, SSM_LANES), F32),
                        pltpu.VMEM((nb, SSM_LANES), F32), pltpu.VMEM((nb, SSM_LANES), F32)],
        compiler_params=_cparams(("arbitrary",), VMEM_LIMIT),
        name="ssm_scan",
    )(u, perm, perm.T, wb, wc, a_re, a_im, d_skip, w_glu, g_out, h0_re, h0_im)


def _outproj_kernel(x_ref, att_ref, ssm_ref, wa_ref, ws_ref, gca_ref, wcq_ref, h_ref, qc_ref):
    h = (x_ref[...] + jnp.dot(att_ref[...], wa_ref[...], preferred_element_type=F32)
         + jnp.dot(ssm_ref[...], ws_ref[...], preferred_element_type=F32))
    h_ref[...] = h
    xn = _rms(h, gca_ref[...]).astype(BF16)
    qc_ref[...] = jnp.dot(xn, wcq_ref[...], preferred_element_type=F32).astype(BF16)


def _outproj(x, att, ssm, w_att, w_ssm, g_ca, w_cq, tm):
    t = x.shape[0]
    row = lambda w: pl.BlockSpec((tm, w), lambda i: (i, 0))
    return pl.pallas_call(
        _outproj_kernel,
        grid=(t // tm,),
        in_specs=[row(D_MODEL), row(ATTN_WIDTH), row(SSM_WIDTH), _full((ATTN_WIDTH, D_MODEL)),
                  _full((SSM_WIDTH, D_MODEL)), _full((1, D_MODEL)), _full((D_MODEL, CA_WIDTH))],
        out_specs=(row(D_MODEL), row(CA_WIDTH)),
        out_shape=(jax.ShapeDtypeStruct((t, D_MODEL), F32), jax.ShapeDtypeStruct((t, CA_WIDTH), BF16)),
        compiler_params=_cparams(("arbitrary",)),
        name="out_projection",
    )(x, att, ssm, w_att, w_ssm, g_ca, w_cq)


def _memkv_kernel(m_ref, g_ref, w_ref, k_ref, v_ref):
    m = _rms(m_ref[...], g_ref[...]).astype(BF16)
    kv = jnp.dot(m, w_ref[...], preferred_element_type=F32)
    k_ref[...] = kv[:, :CA_WIDTH]
    v_ref[...] = kv[:, CA_WIDTH:]


def _memkv(mem, g_mem, w_mkv_bf, tm):
    t = mem.shape[0]
    row = lambda w: pl.BlockSpec((tm, w), lambda i: (i, 0))
    return pl.pallas_call(
        _memkv_kernel,
        grid=(t // tm,),
        in_specs=[row(D_MODEL), _full((1, D_MODEL)), _full((D_MODEL, 2 * CA_WIDTH))],
        out_specs=(row(CA_WIDTH), row(CA_WIDTH)),
        out_shape=(jax.ShapeDtypeStruct((t, CA_WIDTH), F32), jax.ShapeDtypeStruct((t, CA_WIDTH), F32)),
        compiler_params=_cparams(("arbitrary",)),
        name="memory_kv",
    )(mem, g_mem, w_mkv_bf)


def _xattn_kernel(qc_ref, mk_ref, mv_ref, h_ref, wco_ref, gmoe_ref, wr_ref, br_ref, tri_ref, cin_ref,
                  h2_ref, xn_ref, ti_ref, tw_ref, cout_ref, cnt, *scratch, nbat, rows, head_split):
    step = pl.program_id(0)
    nsteps = pl.num_programs(0)

    @pl.when(step == 0)
    def _():
        cnt[...] = cin_ref[...]

    if head_split:
        mbuf, msem = scratch
        slot = step % 2

        def head_copies(s, sl):
            b0 = s * nbat
            return [pltpu.make_async_copy(src.at[pl.ds(b0, nbat), :, hd, :], mbuf.at[sl, kv, hd], msem.at[sl])
                    for kv, src in enumerate((mk_ref, mv_ref)) for hd in range(CA_HEADS)]

        @pl.when(step == 0)
        def _():
            for c in head_copies(0, 0):
                c.start()

        @pl.when(step + 1 < nsteps)
        def _():
            for c in head_copies(step + 1, 1 - slot):
                c.start()

        for c in head_copies(step, slot):
            c.wait()

    q_all = qc_ref[...]
    outs_b = []
    for bi in range(nbat):
        q = q_all[bi * rows:(bi + 1) * rows]
        outs = []
        for hd in range(CA_HEADS):
            sl = slice(hd * CA_HEAD_DIM, (hd + 1) * CA_HEAD_DIM)
            if head_split:
                mk = mbuf[slot, 0, hd, bi].astype(BF16)
                mv = mbuf[slot, 1, hd, bi].astype(BF16)
            else:
                mk = mk_ref[bi, :, sl].astype(BF16)
                mv = mv_ref[bi, :, sl].astype(BF16)
            s = lax.dot_general(q[:, sl], mk, (((1,), (1,)), ((), ())),
                                preferred_element_type=F32) * (CA_HEAD_DIM ** -0.5)
            m = jnp.max(s, axis=-1, keepdims=True)
            p = jnp.exp(s - m)
            denom = jnp.sum(p, axis=-1, keepdims=True)
            outs.append(jnp.dot(p.astype(BF16), mv, preferred_element_type=F32) / denom)
        outs_b.append(jnp.concatenate(outs, axis=-1))
    o = jnp.concatenate(outs_b, axis=0).astype(BF16)
    h2 = h_ref[...] + jnp.dot(o, wco_ref[...], preferred_element_type=F32)
    h2_ref[...] = h2
    xn = _rms(h2, gmoe_ref[...])
    xn_ref[...] = xn
    logits = jnp.dot(xn.astype(BF16), wr_ref[...], preferred_element_type=F32) + br_ref[...]
    n = logits.shape[0]
    eidx = lax.broadcasted_iota(jnp.int32, (n, N_EXPERTS), 1).astype(F32)
    lane = lax.broadcasted_iota(jnp.int32, (n, LANES), 1)
    ti = jnp.zeros((n, LANES), F32)
    tv = jnp.zeros((n, LANES), F32)
    work = logits
    vals = []
    hits = []
    for k in range(TOP_K):
        mx = jnp.max(work, axis=-1, keepdims=True)
        idx = jnp.min(jnp.where(work == mx, eidx, float(N_EXPERTS)), axis=-1, keepdims=True)
        vals.append(mx)
        hits.append(eidx == idx)
        ti = jnp.where(lane == k, idx, ti)
        work = jnp.where(hits[k], -jnp.inf, work)
    ex = [jnp.exp(v - vals[0]) for v in vals]
    tot = ex[0] + ex[1] + ex[2] + ex[3]
    for k in range(TOP_K):
        tv = jnp.where(lane == k, ex[k] / tot, tv)
    sel = jnp.zeros((n, N_EXPERTS), F32)
    for k in range(TOP_K):
        sel = sel + jnp.where(hits[k], 1.0, 0.0)
    before = jnp.dot(tri_ref[...], sel.astype(BF16), preferred_element_type=F32) + cnt[...]
    for k in range(TOP_K):
        rank = jnp.sum(jnp.where(hits[k], before, 0.0), axis=-1, keepdims=True)
        ti = jnp.where(lane == TOP_K + k, rank, ti)
    cnt[...] = cnt[...] + jnp.sum(sel, axis=0, keepdims=True)
    ti_ref[...] = ti.astype(jnp.int32)
    tw_ref[...] = tv

    @pl.when(step == pl.num_programs(0) - 1)
    def _():
        cout_ref[...] = cnt[...]


def _xattn(qc, mk, mv, h1, w_co, g_moe, w_router, b_router, counts_in, nbat, rows):
    t = qc.shape[0]
    tm = nbat * rows
    row = lambda w: pl.BlockSpec((tm, w), lambda i: (i, 0))
    seq_rows = t // mk.shape[0]
    head_split = mk.ndim == 4
    scratch = [pltpu.VMEM((1, N_EXPERTS), F32)]
    if head_split:
        assert seq_rows == rows
        mem = pl.BlockSpec(memory_space=pl.ANY)
        scratch += [pltpu.VMEM((2, 2, CA_HEADS, nbat, N_MEM, CA_HEAD_DIM), F32), pltpu.SemaphoreType.DMA((2,))]
    else:
        mem = pl.BlockSpec((nbat, N_MEM, CA_WIDTH), lambda i: ((i * tm) // (seq_rows * nbat), 0, 0))
    tri = jnp.tri(tm, k=-1, dtype=BF16)
    return pl.pallas_call(
        functools.partial(_xattn_kernel, nbat=nbat, rows=rows, head_split=head_split),
        grid=(t // tm,),
        in_specs=[row(CA_WIDTH), mem, mem, row(D_MODEL), _full((CA_WIDTH, D_MODEL)), _full((1, D_MODEL)),
                  _full((D_MODEL, N_EXPERTS)), _full((1, N_EXPERTS)), _full((tm, tm)), _full((1, N_EXPERTS))],
        out_specs=(row(D_MODEL), row(D_MODEL), row(LANES), row(LANES), _full((1, N_EXPERTS))),
        out_shape=(jax.ShapeDtypeStruct((t, D_MODEL), F32), jax.ShapeDtypeStruct((t, D_MODEL), F32),
                   jax.ShapeDtypeStruct((t, LANES), jnp.int32), jax.ShapeDtypeStruct((t, LANES), F32),
                   jax.ShapeDtypeStruct((1, N_EXPERTS), F32)),
        scratch_shapes=scratch,
        compiler_params=_cparams(("arbitrary",), VMEM_LIMIT),
        name="cross_attn_router",
    )(qc, mk, mv, h1, w_co, g_moe, w_router, b_router, tri, counts_in)


def _row_copy(src, src_row, dst, dst_row, sem):
    return pltpu.make_async_copy(src.at[pl.ds(src_row, 1)], dst.at[pl.ds(dst_row, 1)], sem)


def _scatter_kernel(dest_hbm, x_ref, *rest, tm):
    xs_hbm, idx, isem, sem = rest[-4:]
    i = pl.program_id(0)
    n = pl.num_programs(0)
    slot = i % 2
    per_step = tm * TOP_K

    def idx_copy(step, sl):
        return pltpu.make_async_copy(dest_hbm.at[step], idx.at[pl.ds(sl * per_step, per_step)], isem.at[sl])

    @pl.when(i == 0)
    def _():
        idx_copy(0, 0).start()

    @pl.when(i + 1 < n)
    def _():
        idx_copy(i + 1, 1 - slot).start()

    idx_copy(i, slot).wait()
    off = slot * per_step

    def issue(j, carry):
        row0 = pl.multiple_of(j * SUBLANES, SUBLANES)
        for half in range(2):
            base = off + j * (SUBLANES * TOP_K) + half * DMA_BATCH
            dst = [idx[base + q] for q in range(DMA_BATCH)]
            for q in range(DMA_BATCH):
                s = (half * DMA_BATCH + q) // TOP_K
                _row_copy(x_ref, row0 + s, xs_hbm, dst[q], sem).start(priority=q % 2)
        return carry

    lax.fori_loop(0, tm // SUBLANES, issue, 0)

    for k in range(TOP_K):
        pltpu.make_async_copy(x_ref, xs_hbm.at[pl.ds(0, tm)], sem).wait()


def _scatter_rows(dest, x, n_slots, tm, xs_prev=None):
    t = x.shape[0]
    args = [dest.reshape(t // tm, tm * TOP_K), x]
    in_specs = [pl.BlockSpec(memory_space=pl.ANY), pl.BlockSpec((tm, D_MODEL), lambda i: (i, 0))]
    aliases = {}
    if xs_prev is not None:
        args.append(xs_prev)
        in_specs.append(pl.BlockSpec(memory_space=pl.ANY))
        aliases = {2: 0}
    return pl.pallas_call(
        functools.partial(_scatter_kernel, tm=tm),
        grid=(t // tm,),
        in_specs=in_specs,
        out_specs=pl.BlockSpec(memory_space=pl.ANY),
        out_shape=jax.ShapeDtypeStruct((n_slots, D_MODEL), F32),
        scratch_shapes=[pltpu.SMEM((2 * tm * TOP_K,), jnp.int32), pltpu.SemaphoreType.DMA((2,)),
                        pltpu.SemaphoreType.DMA(())],
        input_output_aliases=aliases,
        compiler_params=_cparams(("arbitrary",)),
        name="moe_scatter",
    )(*args)


def _expert_kernel(blk_e_ref, nused_ref, xs_ref, wg_ref, bg_ref, wu_ref, bu_ref, wd_ref, bd_ref, ys_ref,
                   wg_bf, wu_bf, wd_bf):
    i = pl.program_id(0)
    active = i < nused_ref[0]
    prev_e = blk_e_ref[jnp.maximum(i - 1, 0)]
    new_expert = (i == 0) | (blk_e_ref[i] != prev_e)

    @pl.when(active & new_expert)
    def _():
        wg_bf[...] = wg_ref[0].astype(BF16)
        wu_bf[...] = wu_ref[0].astype(BF16)
        wd_bf[...] = wd_ref[0].astype(BF16)

    @pl.when(active)
    def _():
        x = xs_ref[...].astype(BF16)
        gate = jnp.minimum(jnp.dot(x, wg_bf[...], preferred_element_type=F32) + bg_ref[0], SWIGLU_LIMIT)
        up = jnp.clip(jnp.dot(x, wu_bf[...], preferred_element_type=F32) + bu_ref[0], -SWIGLU_LIMIT, SWIGLU_LIMIT)
        h = gate * jax.nn.sigmoid(SWIGLU_ALPHA * gate) * (up + 1.0)
        ys_ref[...] = jnp.dot(h.astype(BF16), wd_bf[...], preferred_element_type=F32) + bd_ref[0]


def _experts(blk_e, n_used, xs, w_gate, b_gate, w_up, b_up, w_down, b_down):
    n_slots = xs.shape[0]
    n_blocks = n_slots // MOE_BLOCK
    d_ff = w_gate.shape[-1]
    blk = lambda i, be, nu: (jnp.minimum(i, jnp.maximum(nu[0] - 1, 0)), 0)
    wspec = lambda a, b: pl.BlockSpec((1, a, b), lambda i, be, nu: (be[i], 0, 0))
    grid_spec = pltpu.PrefetchScalarGridSpec(
        num_scalar_prefetch=2,
        grid=(n_blocks,),
        in_specs=[pl.BlockSpec((MOE_BLOCK, D_MODEL), blk), wspec(D_MODEL, d_ff), wspec(1, d_ff),
                  wspec(D_MODEL, d_ff), wspec(1, d_ff), wspec(d_ff, D_MODEL), wspec(1, D_MODEL)],
        out_specs=pl.BlockSpec((MOE_BLOCK, D_MODEL), blk),
        scratch_shapes=[pltpu.VMEM((D_MODEL, d_ff), BF16), pltpu.VMEM((D_MODEL, d_ff), BF16),
                        pltpu.VMEM((d_ff, D_MODEL), BF16)],
    )
    return pl.pallas_call(
        _expert_kernel,
        grid_spec=grid_spec,
        out_shape=jax.ShapeDtypeStruct((n_slots, D_MODEL), F32),
        compiler_params=_cparams(("arbitrary",), VMEM_LIMIT),
        name="moe_experts",
    )(blk_e, n_used, xs, w_gate, b_gate.reshape(N_EXPERTS, 1, d_ff), w_up, b_up.reshape(N_EXPERTS, 1, d_ff),
      w_down, b_down.reshape(N_EXPERTS, 1, D_MODEL))


def _combine_kernel(dest_hbm, ys_hbm, h_ref, tw_ref, g_ref, o_ref, idx, isem, buf, sem, *, tm):
    i = pl.program_id(0)
    n = pl.num_programs(0)
    slot = i % 2
    per_step = tm * TOP_K

    def idx_copy(step, sl):
        return pltpu.make_async_copy(dest_hbm.at[step], idx.at[pl.ds(sl * per_step, per_step)], isem.at[sl])

    def issue(sl):
        def body(j, carry):
            row0 = pl.multiple_of(j * SUBLANES, SUBLANES)
            for half in range(2):
                base = sl * per_step + j * (SUBLANES * TOP_K) + half * DMA_BATCH
                src = [idx[base + q] for q in range(DMA_BATCH)]
                for q in range(DMA_BATCH):
                    s, k = divmod(half * DMA_BATCH + q, TOP_K)
                    _row_copy(ys_hbm, src[q], buf.at[sl, k], row0 + s, sem.at[sl]).start(priority=q % 2)
            return carry
        lax.fori_loop(0, tm // SUBLANES, body, 0)

    @pl.when(i == 0)
    def _():
        idx_copy(0, 0).start()
        idx_copy(0, 0).wait()
        issue(0)

        @pl.when(n > 1)
        def _():
            idx_copy(1, 1).start()

    @pl.when(i + 1 < n)
    def _():
        idx_copy(i + 1, 1 - slot).wait()
        issue(1 - slot)

    @pl.when(i + 2 < n)
    def _():
        idx_copy(i + 2, slot).start()

    for k in range(TOP_K):
        pltpu.make_async_copy(ys_hbm.at[pl.ds(0, tm)], buf.at[slot, k], sem.at[slot]).wait()
    tw = tw_ref[...]
    y = h_ref[...]
    for k in range(TOP_K):
        y = y + tw[:, k:k + 1] * buf[slot, k]
    o_ref[...] = _rms(y, g_ref[...])


def _combine(dest, ys, h2, tw, g_final, tm):
    t = h2.shape[0]
    row = lambda w: pl.BlockSpec((tm, w), lambda i: (i, 0))
    return pl.pallas_call(
        functools.partial(_combine_kernel, tm=tm),
        grid=(t // tm,),
        in_specs=[pl.BlockSpec(memory_space=pl.ANY), pl.BlockSpec(memory_space=pl.ANY), row(D_MODEL), row(LANES),
                  _full((1, D_MODEL))],
        out_specs=row(D_MODEL),
        out_shape=jax.ShapeDtypeStruct((t, D_MODEL), F32),
        scratch_shapes=[pltpu.SMEM((2 * tm * TOP_K,), jnp.int32), pltpu.SemaphoreType.DMA((2,)),
                        pltpu.VMEM((2, TOP_K, tm, D_MODEL), F32), pltpu.SemaphoreType.DMA((2,))],
        compiler_params=_cparams(("arbitrary",), VMEM_LIMIT),
        name="moe_combine",
    )(dest.reshape(t // tm, tm * TOP_K), ys, h2, tw, g_final)


def _route(counts, m):
    counts = counts.reshape(N_EXPERTS).astype(jnp.int32)
    padded = (counts + MOE_BLOCK - 1) // MOE_BLOCK * MOE_BLOCK
    pad_end = jnp.cumsum(padded)
    pad_start = pad_end - padded
    n_blocks = -(-(m + N_EXPERTS * (MOE_BLOCK - 1)) // MOE_BLOCK)
    blk_start = jnp.arange(n_blocks, dtype=jnp.int32) * MOE_BLOCK
    blk_e = jnp.minimum(jnp.sum((pad_end[None, :] <= blk_start[:, None]).astype(jnp.int32), axis=1),
                        N_EXPERTS - 1).astype(jnp.int32)
    n_used = (pad_end[-1] // MOE_BLOCK).astype(jnp.int32).reshape(1)
    return pad_start, blk_e, n_used, n_blocks * MOE_BLOCK


def _dest(ti, pad_start):
    e = ti[:, :TOP_K]
    onehot = e[:, :, None] == jnp.arange(N_EXPERTS, dtype=jnp.int32)[None, None, :]
    start = jnp.sum(jnp.where(onehot, pad_start[None, None, :], 0), axis=-1)
    return (start + ti[:, TOP_K:2 * TOP_K]).reshape(-1)


def _block_diag(blocks):
    g, a, b = blocks.shape
    eye = jnp.eye(g, dtype=blocks.dtype)
    return (blocks[:, :, None, :] * eye[:, None, :, None]).reshape(g * a, g * b)


def kernel(x_prompt, x_sample, mem_prompt, cache_win_k, cache_win_v, state_ssm_re, state_ssm_im, cache_mem_k,
           cache_mem_v, g_mix, w_in, sink, lam_re, lam_im, log_dt, b_re, b_im, c_re, c_im, d_skip, w_glu,
           g_attn_out, g_ssm_out, w_out, g_ca, g_mem, w_mk, w_mv, w_cq, w_co, g_moe, w_router, b_router,
           w_gate, b_gate, w_up, b_up, w_down, b_down, g_final):
    depth = g_mix.shape[0]
    assert depth == 1
    bp, lp, _ = x_prompt.shape
    bs, ls, _ = x_sample.shape
    tp, ts = bp * lp, bs * ls

    w_in_bf = w_in[0].astype(BF16)
    w_att_bf = w_out[0, :ATTN_WIDTH].astype(BF16)
    w_ssm_bf = w_out[0, ATTN_WIDTH:].astype(BF16)
    w_cq_bf = w_cq[0].astype(BF16)
    w_co_bf = w_co[0].astype(BF16)
    w_mkv_bf = jnp.concatenate([w_mk[0], w_mv[0]], axis=1).astype(BF16)
    w_glu_bf = w_glu[0].astype(BF16)
    row = lambda a: a.reshape(1, -1)

    a_re, a_im, bb_re, bb_im = _discretize(lam_re[0], lam_im[0], log_dt[0], b_re[0].transpose(0, 2, 1),
                                           b_im[0].transpose(0, 2, 1))
    hg = SSM_GROUPS // 2
    wb = jnp.stack([jnp.concatenate([_block_diag(bb_re[h * hg:(h + 1) * hg]),
                                     _block_diag(bb_im[h * hg:(h + 1) * hg])], axis=1) for h in range(2)]).astype(BF16)
    cre_t = c_re[0].transpose(0, 2, 1)
    cim_t = c_im[0].transpose(0, 2, 1)
    wc = jnp.stack([jnp.concatenate([_block_diag(cre_t[h * hg:(h + 1) * hg]),
                                     -_block_diag(cim_t[h * hg:(h + 1) * hg])], axis=0) for h in range(2)]).astype(BF16)
    ssm_w = (wb, wc, a_re.reshape(1, SSM_LANES), a_im.reshape(1, SSM_LANES), row(d_skip[0]), w_glu_bf,
             row(g_ssm_out[0]))

    xp = x_prompt.reshape(tp, D_MODEL)
    xs_ = x_sample.reshape(ts, D_MODEL)

    qp, kp, vp, up = _inproj(xp, row(g_mix[0]), w_in_bf, 512)
    qs, ks, vs, us = _inproj(xs_, row(g_mix[0]), w_in_bf, 256)

    att_p = _attn_prompt(sink[0], qp, kp, vp, row(g_attn_out[0]), bp, lp)
    ck = cache_win_k[0].reshape(bs, WINDOW, KV_WIDTH)
    cv = cache_win_v[0].reshape(bs, WINDOW, KV_WIDTH)
    att_s = _attn_sample(sink[0], qs, ks, vs, ck, cv, row(g_attn_out[0]), bs, ls, 16)

    zero_state = jnp.zeros((bp, SSM_LANES), F32)
    ssm_p, p_sre, p_sim = _ssm(up.reshape(bp, lp, SSM_WIDTH), zero_state, zero_state, ssm_w, 32)
    ssm_s, s_sre, s_sim = _ssm(us.reshape(bs, ls, SSM_WIDTH), state_ssm_re[0].reshape(bs, SSM_LANES),
                               state_ssm_im[0].reshape(bs, SSM_LANES), ssm_w, ls)

    h1p, qcp = _outproj(xp, att_p, ssm_p.reshape(tp, SSM_WIDTH), w_att_bf, w_ssm_bf, row(g_ca[0]), w_cq_bf, 512)
    h1s, qcs = _outproj(xs_, att_s, ssm_s.reshape(ts, SSM_WIDTH), w_att_bf, w_ssm_bf, row(g_ca[0]), w_cq_bf, 256)

    mk_p, mv_p = _memkv(mem_prompt.reshape(bp * N_MEM, D_MODEL), row(g_mem[0]), w_mkv_bf, 512)

    no_counts = jnp.zeros((1, N_EXPERTS), F32)
    w_router_bf = w_router[0].astype(BF16)
    h2p, xnp_, tip, twp, cnt_p = _xattn(qcp, mk_p.reshape(bp, N_MEM, CA_WIDTH), mv_p.reshape(bp, N_MEM, CA_WIDTH),
                                        h1p, w_co_bf, row(g_moe[0]), w_router_bf, row(b_router[0]), no_counts,
                                        1, 512)
    h2s, xns_, tis, tws, cnt = _xattn(qcs, cache_mem_k[0], cache_mem_v[0], h1s, w_co_bf, row(g_moe[0]),
                                      w_router_bf, row(b_router[0]), cnt_p, 8, ls)

    pad_start, blk_e, n_used, n_slots = _route(cnt, (tp + ts) * TOP_K)
    dest_p = _dest(tip, pad_start)
    dest_s = _dest(tis, pad_start)
    xs_sorted = _scatter_rows(dest_p, xnp_, n_slots, 256)
    xs_sorted = _scatter_rows(dest_s, xns_, n_slots, 256, xs_prev=xs_sorted)
    ys = _experts(blk_e, n_used, xs_sorted, w_gate[0], b_gate[0], w_up[0], b_up[0], w_down[0], b_down[0])
    y_prompt = _combine(dest_p, ys, h2p, twp, row(g_final), 128).reshape(bp, lp, D_MODEL)
    y_sample = _combine(dest_s, ys, h2s, tws, row(g_final), 128).reshape(bs, ls, D_MODEL)

    kp4 = kp.reshape(bp, lp, N_KV_HEADS, HEAD_DIM)
    vp4 = vp.reshape(bp, lp, N_KV_HEADS, HEAD_DIM)
    p_win_k = kp4[:, -WINDOW:][None]
    p_win_v = vp4[:, -WINDOW:][None]
    s_win_k = jnp.concatenate([cache_win_k[0], ks.reshape(bs, ls, N_KV_HEADS, HEAD_DIM)], axis=1)[:, -WINDOW:][None]
    s_win_v = jnp.concatenate([cache_win_v[0], vs.reshape(bs, ls, N_KV_HEADS, HEAD_DIM)], axis=1)[:, -WINDOW:][None]
    st = lambda a, b: a.reshape(1, b, SSM_GROUPS, SSM_STATE)
    p_mem_k = mk_p.reshape(1, bp, N_MEM, CA_HEADS, CA_HEAD_DIM)
    p_mem_v = mv_p.reshape(1, bp, N_MEM, CA_HEADS, CA_HEAD_DIM)
    return (y_prompt, y_sample, p_win_k, p_win_v, st(p_sre, bp), st(p_sim, bp), p_mem_k, p_mem_v,
            s_win_k, s_win_v, st(s_sre, bs), st(s_sim, bs))
```

```python
import functools
import math

import jax
import jax.numpy as jnp
from jax import lax
from jax.experimental import pallas as pl
from jax.experimental.pallas import tpu as pltpu

F32 = jnp.float32
BF16 = jnp.bfloat16

D_MODEL = 1024
HEAD_DIM = 64
N_Q_HEADS = 8
N_KV_HEADS = 2
Q_PER_KV = 4
ATTN_WIDTH = 512
KV_WIDTH = 128
WINDOW = 128
SSM_WIDTH = 512
SSM_GROUP = 16
SSM_GROUPS = 32
SSM_STATE = 64
SSM_LANES = SSM_GROUPS * SSM_STATE
SSM_HALF = SSM_LANES // 2
IN_WIDTH = ATTN_WIDTH + 2 * KV_WIDTH + SSM_WIDTH
N_MEM = 256
CA_HEADS = 4
CA_HEAD_DIM = 128
CA_WIDTH = 512
N_EXPERTS = 32
TOP_K = 4
SWIGLU_ALPHA = 1.702
SWIGLU_LIMIT = 7.0
RMS_EPS = 1e-5

SUBLANES = 8
LANES = 128
MOE_BLOCK = 256
DMA_BATCH = 16
VMEM_LIMIT = 56 * 1024 * 1024


def _cparams(sem, vmem=None):
    return pltpu.CompilerParams(dimension_semantics=sem, vmem_limit_bytes=vmem)


def _rms(x, g):
    return x * lax.rsqrt(jnp.mean(x * x, axis=-1, keepdims=True) + RMS_EPS) * g


def _full(shape):
    return pl.BlockSpec(shape, lambda *_: (0,) * len(shape))


LANE_TILES = D_MODEL // LANES


def _store_row_tiles(ref, x):
    for c in range(LANE_TILES):
        ref[:, c] = x[:, c * LANES:(c + 1) * LANES].reshape(x.shape[0] // SUBLANES, SUBLANES, LANES)


def _load_row_tiles(ref):
    rows = ref.shape[0] * SUBLANES
    return jnp.concatenate([ref[:, c].reshape(rows, LANES) for c in range(LANE_TILES)], axis=1)


def _row_tile_shape(rows):
    return jax.ShapeDtypeStruct((rows // SUBLANES, LANE_TILES, SUBLANES, LANES), F32)


def _row_tile_spec(tm):
    return pl.BlockSpec((tm // SUBLANES, LANE_TILES, SUBLANES, LANES), lambda i: (i, 0, 0, 0))


def _disc_kernel(lr_ref, li_ref, ldt_ref, bre_ref, bim_ref, are_ref, aim_ref, bbre_ref, bbim_ref):
    lr = lr_ref[...]
    li = li_ref[...]
    dt = jnp.exp(ldt_ref[...])
    mag = jnp.exp(lr * dt)
    ang = li * dt
    ab_re = mag * jnp.cos(ang)
    ab_im = mag * jnp.sin(ang)
    nr = ab_re - 1.0
    ni = ab_im
    den = lr * lr + li * li
    f_re = (nr * lr + ni * li) / den
    f_im = (ni * lr - nr * li) / den
    are_ref[...] = ab_re
    aim_ref[...] = ab_im
    br = bre_ref[...]
    bi = bim_ref[...]
    bbre_ref[...] = f_re[:, None, :] * br - f_im[:, None, :] * bi
    bbim_ref[...] = f_re[:, None, :] * bi + f_im[:, None, :] * br


def _discretize(lam_re, lam_im, log_dt, b_re_t, b_im_t):
    g, n = lam_re.shape
    c = b_re_t.shape[1]
    return pl.pallas_call(
        _disc_kernel,
        out_shape=(jax.ShapeDtypeStruct((g, n), F32), jax.ShapeDtypeStruct((g, n), F32),
                   jax.ShapeDtypeStruct((g, c, n), F32), jax.ShapeDtypeStruct((g, c, n), F32)),
        name="ssm_discretize",
    )(lam_re, lam_im, log_dt.reshape(g, 1), b_re_t, b_im_t)


def _inproj_kernel(x_ref, g_ref, w_ref, q_ref, k_ref, v_ref, u_ref):
    a = _rms(x_ref[...], g_ref[...]).astype(BF16)
    proj = jnp.dot(a, w_ref[...], preferred_element_type=F32)
    q_ref[...] = proj[:, :ATTN_WIDTH].astype(BF16)
    k_ref[...] = proj[:, ATTN_WIDTH:ATTN_WIDTH + KV_WIDTH]
    v_ref[...] = proj[:, ATTN_WIDTH + KV_WIDTH:ATTN_WIDTH + 2 * KV_WIDTH]
    u_ref[...] = proj[:, ATTN_WIDTH + 2 * KV_WIDTH:]


def _inproj(x, g_mix, w_in_bf, tm):
    t = x.shape[0]
    row = lambda w: pl.BlockSpec((tm, w), lambda i: (i, 0))
    return pl.pallas_call(
        _inproj_kernel,
        grid=(t // tm,),
        in_specs=[row(D_MODEL), _full((1, D_MODEL)), _full((D_MODEL, IN_WIDTH))],
        out_specs=(row(ATTN_WIDTH), row(KV_WIDTH), row(KV_WIDTH), row(SSM_WIDTH)),
        out_shape=(jax.ShapeDtypeStruct((t, ATTN_WIDTH), BF16), jax.ShapeDtypeStruct((t, KV_WIDTH), F32),
                   jax.ShapeDtypeStruct((t, KV_WIDTH), F32), jax.ShapeDtypeStruct((t, SSM_WIDTH), F32)),
        compiler_params=_cparams(("arbitrary",)),
        name="in_projection",
    )(x, g_mix, w_in_bf)


def _sink_softmax_pv(s, sink, v_bf):
    m = jnp.maximum(jnp.max(s, axis=-1, keepdims=True), sink)
    p = jnp.exp(s - m)
    denom = jnp.sum(p, axis=-1, keepdims=True) + jnp.exp(sink - m)
    return jnp.dot(p.astype(BF16), v_bf, preferred_element_type=F32) / denom


def _attn_prompt_kernel(sink_ref, q_ref, kc_ref, kp_ref, vc_ref, vp_ref, g_ref, o_ref):
    i = pl.program_id(1)
    q = q_ref[...]
    kk = jnp.concatenate([kp_ref[...], kc_ref[...]], axis=0).astype(BF16)
    vv = jnp.concatenate([vp_ref[...], vc_ref[...]], axis=0).astype(BF16)
    rows = Q_PER_KV * WINDOW
    qi = lax.broadcasted_iota(jnp.int32, (rows, 2 * WINDOW), 0) & (WINDOW - 1)
    kj = lax.broadcasted_iota(jnp.int32, (rows, 2 * WINDOW), 1)
    rel = qi + WINDOW - kj
    mask = (rel >= 0) & (rel < WINDOW) & ((kj >= WINDOW) | (i > 0))
    head_of_row = lax.broadcasted_iota(jnp.int32, (rows, 1), 0) // WINDOW
    low_q = lax.broadcasted_iota(jnp.int32, (WINDOW, LANES), 1) < HEAD_DIM
    zero_q = jnp.zeros((WINDOW, LANES), BF16)
    pairs = []
    for j in range(N_KV_HEADS):
        kh = kk[:, j * HEAD_DIM:(j + 1) * HEAD_DIM]
        vh = vv[:, j * HEAD_DIM:(j + 1) * HEAD_DIM]
        k2 = jnp.concatenate([kh, kh], axis=1)
        v2 = jnp.concatenate([vh, vh], axis=1)
        stacked = []
        for p in range(Q_PER_KV // 2):
            qp = q[:, (j * Q_PER_KV + 2 * p) * HEAD_DIM:(j * Q_PER_KV + 2 * p + 2) * HEAD_DIM]
            stacked += [jnp.where(low_q, qp, zero_q), jnp.where(low_q, zero_q, qp)]
        qs = jnp.concatenate(stacked, axis=0)
        sink = jnp.zeros((rows, 1), F32)
        for g in range(Q_PER_KV):
            sink = jnp.where(head_of_row == g, sink_ref[j * Q_PER_KV + g], sink)
        s = lax.dot_general(qs, k2, (((1,), (1,)), ((), ())), preferred_element_type=F32) * (HEAD_DIM ** -0.5)
        s = jnp.where(mask, s, -jnp.inf)
        o = _sink_softmax_pv(s, sink, v2)
        for p in range(Q_PER_KV // 2):
            pairs.append(jnp.where(low_q, o[2 * p * WINDOW:(2 * p + 1) * WINDOW],
                                   o[(2 * p + 1) * WINDOW:(2 * p + 2) * WINDOW]))
    att = jnp.concatenate(pairs, axis=-1)
    o_ref[...] = _rms(att, g_ref[...]).astype(BF16)


def _attn_prompt(sink, q, k, v, g_attn_out, b, l):
    nb = l // WINDOW
    cur = lambda w: pl.BlockSpec((WINDOW, w), lambda bi, i: (bi * nb + i, 0))
    prev = lambda w: pl.BlockSpec((WINDOW, w), lambda bi, i: (bi * nb + jnp.maximum(i - 1, 0), 0))
    return pl.pallas_call(
        _attn_prompt_kernel,
        grid=(b, nb),
        in_specs=[pl.BlockSpec(memory_space=pltpu.SMEM), cur(ATTN_WIDTH), cur(KV_WIDTH), prev(KV_WIDTH),
                  cur(KV_WIDTH), prev(KV_WIDTH), _full((1, ATTN_WIDTH))],
        out_specs=cur(ATTN_WIDTH),
        out_shape=jax.ShapeDtypeStruct((b * l, ATTN_WIDTH), BF16),
        compiler_params=_cparams(("arbitrary", "arbitrary")),
        name="window_attn_prompt",
    )(sink, q, k, k, v, v, g_attn_out)


def _attn_sample_kernel(sink_ref, q_ref, kn_ref, vn_ref, ck_ref, cv_ref, g_ref, o_ref, *, bt, t):
    nk = WINDOW + t
    rows = Q_PER_KV * t
    ri = lax.broadcasted_iota(jnp.int32, (rows, nk), 0)
    kj = lax.broadcasted_iota(jnp.int32, (rows, nk), 1)
    tq = ri % t
    mask = ((kj < WINDOW) & (kj > tq)) | ((kj >= WINDOW) & ((kj - WINDOW) <= tq))
    rg = lax.broadcasted_iota(jnp.int32, (rows, 1), 0) // t
    q_all = q_ref[...]
    kn_all = kn_ref[...]
    vn_all = vn_ref[...]
    outs_b = []
    for bi in range(bt):
        q = q_all[bi * t:(bi + 1) * t]
        kk = jnp.concatenate([ck_ref[bi], kn_all[bi * t:(bi + 1) * t]], axis=0).astype(BF16)
        vv = jnp.concatenate([cv_ref[bi], vn_all[bi * t:(bi + 1) * t]], axis=0).astype(BF16)
        outs = []
        for j in range(N_KV_HEADS):
            qj = jnp.concatenate(
                [q[:, (j * Q_PER_KV + g) * HEAD_DIM:(j * Q_PER_KV + g + 1) * HEAD_DIM] for g in range(Q_PER_KV)],
                axis=0)
            sink = jnp.zeros((rows, 1), F32)
            for g in range(Q_PER_KV):
                sink = jnp.where(rg == g, sink_ref[j * Q_PER_KV + g], sink)
            kh = kk[:, j * HEAD_DIM:(j + 1) * HEAD_DIM]
            vh = vv[:, j * HEAD_DIM:(j + 1) * HEAD_DIM]
            s = lax.dot_general(qj, kh, (((1,), (1,)), ((), ())), preferred_element_type=F32) * (HEAD_DIM ** -0.5)
            s = jnp.where(mask, s, -jnp.inf)
            o = _sink_softmax_pv(s, sink, vh)
            outs.extend(o[g * t:(g + 1) * t] for g in range(Q_PER_KV))
        outs_b.append(jnp.concatenate(outs, axis=-1))
    att = jnp.concatenate(outs_b, axis=0)
    o_ref[...] = _rms(att, g_ref[...]).astype(BF16)


def _attn_sample(sink, q, k, v, cache_k, cache_v, g_attn_out, b, t, bt):
    row = lambda w: pl.BlockSpec((bt * t, w), lambda i: (i, 0))
    cache = pl.BlockSpec((bt, WINDOW, KV_WIDTH), lambda i: (i, 0, 0))
    return pl.pallas_call(
        functools.partial(_attn_sample_kernel, bt=bt, t=t),
        grid=(b // bt,),
        in_specs=[pl.BlockSpec(memory_space=pltpu.SMEM), row(ATTN_WIDTH), row(KV_WIDTH), row(KV_WIDTH),
                  cache, cache, _full((1, ATTN_WIDTH))],
        out_specs=row(ATTN_WIDTH),
        out_shape=jax.ShapeDtypeStruct((b * t, ATTN_WIDTH), BF16),
        compiler_params=_cparams(("arbitrary",)),
        name="window_attn_sample",
    )(sink, q, k, v, cache_k, cache_v, g_attn_out)


def _ssm_kernel(u_ref, perm_ref, permt_ref, wb_ref, wc_ref, are_ref, aim_ref, d_ref, wglu_ref, g_ref,
                h0re_ref, h0im_ref, o_ref, hre_ref, him_ref, vre, vim, sre, sim, *, nb, s):
    c = pl.program_id(0)
    r = nb * s

    @pl.when(c == 0)
    def _():
        sre[...] = h0re_ref[...]
        sim[...] = h0im_ref[...]

    u = u_ref[...].reshape(r, SSM_WIDTH)
    u_hi = u.astype(BF16)
    u_lo = (u - u_hi.astype(F32)).astype(BF16)
    perm = perm_ref[...]
    u_tb = (jnp.dot(perm, u_hi, preferred_element_type=F32) + jnp.dot(perm, u_lo, preferred_element_type=F32))
    u_bf = u_tb.astype(BF16)
    half_w = SSM_WIDTH // 2
    for h in range(2):
        vv = jnp.dot(u_bf[:, h * half_w:(h + 1) * half_w], wb_ref[h], preferred_element_type=F32)
        vre[:, h * SSM_HALF:(h + 1) * SSM_HALF] = vv[:, :SSM_HALF]
        vim[:, h * SSM_HALF:(h + 1) * SSM_HALF] = vv[:, SSM_HALF:]

    lane_chunk = SSM_LANES // 2
    for rg in range(nb // SUBLANES):
        for lc in range(SSM_LANES // lane_chunk):
            lanes = slice(lc * lane_chunk, (lc + 1) * lane_chunk)
            rows0 = slice(rg * SUBLANES, (rg + 1) * SUBLANES)
            ar = jnp.broadcast_to(are_ref[:, lanes], (SUBLANES, lane_chunk))
            ai = jnp.broadcast_to(aim_ref[:, lanes], (SUBLANES, lane_chunk))

            def step(t, carry, lanes=lanes, rg=rg, ar=ar, ai=ai):
                hr, hi = carry
                row = pl.multiple_of(t * nb + rg * SUBLANES, SUBLANES)
                nhr = ar * hr - ai * hi + vre[pl.ds(row, SUBLANES), lanes]
                nhi = ar * hi + ai * hr + vim[pl.ds(row, SUBLANES), lanes]
                vre[pl.ds(row, SUBLANES), lanes] = nhr
                vim[pl.ds(row, SUBLANES), lanes] = nhi
                return nhr, nhi

            hr, hi = lax.fori_loop(0, s, step, (sre[rows0, lanes], sim[rows0, lanes]), unroll=True)
            sre[rows0, lanes] = hr
            sim[rows0, lanes] = hi

    ys = []
    for h in range(2):
        hcat = jnp.concatenate([vre[:, h * SSM_HALF:(h + 1) * SSM_HALF], vim[:, h * SSM_HALF:(h + 1) * SSM_HALF]],
                               axis=1).astype(BF16)
        ys.append(jnp.dot(hcat, wc_ref[h], preferred_element_type=F32))
    y = jnp.concatenate(ys, axis=1) + d_ref[...] * u_tb
    g = jax.nn.gelu(y)
    z = jnp.dot(g.astype(BF16), wglu_ref[...], preferred_element_type=F32)
    out = g * jax.nn.sigmoid(z)
    on = _rms(out, g_ref[...]).astype(BF16)
    o_bt = jnp.dot(permt_ref[...], on, preferred_element_type=F32).astype(BF16)
    o_ref[...] = o_bt.reshape(nb, s, SSM_WIDTH)

    @pl.when(c == pl.num_programs(0) - 1)
    def _():
        hre_ref[...] = sre[...]
        him_ref[...] = sim[...]


def _ssm(u, h0_re, h0_im, ssm_w, s):
    nb, l, _ = u.shape
    r = nb * s
    ridx = jnp.arange(r)
    src = (ridx % nb) * s + ridx // nb
    perm = (src[:, None] == ridx[None, :]).astype(BF16)
    wb, wc, a_re, a_im, d_skip, w_glu, g_out = ssm_w
    seq_blk = pl.BlockSpec((nb, s, SSM_WIDTH), lambda c: (0, c, 0))
    st_blk = _full((nb, SSM_LANES))
    return pl.pallas_call(
        functools.partial(_ssm_kernel, nb=nb, s=s),
        grid=(l // s,),
        in_specs=[seq_blk, _full((r, r)), _full((r, r)), _full(wb.shape), _full(wc.shape),
                  _full((1, SSM_LANES)), _full((1, SSM_LANES)), _full((1, SSM_WIDTH)),
                  _full((SSM_WIDTH, SSM_WIDTH)), _full((1, SSM_WIDTH)), st_blk, st_blk],
        out_specs=(seq_blk, st_blk, st_blk),
        out_shape=(jax.ShapeDtypeStruct((nb, l, SSM_WIDTH), BF16),
                   jax.ShapeDtypeStruct((nb, SSM_LANES), F32), jax.ShapeDtypeStruct((nb, SSM_LANES), F32)),
        scratch_shapes=[pltpu.VMEM((r, SSM_LANES), F32), pltpu.VMEM((r, SSM_LANES), F32),
                        pltpu.VMEM((nb, SSM_LANES), F32), pltpu.VMEM((nb, SSM_LANES), F32)],
        compiler_params=_cparams(("arbitrary",), VMEM_LIMIT),
        name="ssm_scan",
    )(u, perm, perm.T, wb, wc, a_re, a_im, d_skip, w_glu, g_out, h0_re, h0_im)


def _outproj_kernel(x_ref, att_ref, ssm_ref, wa_ref, ws_ref, gca_ref, wcq_ref, h_ref, qc_ref):
    h = (x_ref[...] + jnp.dot(att_ref[...], wa_ref[...], preferred_element_type=F32)
         + jnp.dot(ssm_ref[...], ws_ref[...], preferred_element_type=F32))
    h_ref[...] = h
    xn = _rms(h, gca_ref[...]).astype(BF16)
    qc_ref[...] = jnp.dot(xn, wcq_ref[...], preferred_element_type=F32).astype(BF16)


def _outproj(x, att, ssm, w_att, w_ssm, g_ca, w_cq, tm):
    t = x.shape[0]
    row = lambda w: pl.BlockSpec((tm, w), lambda i: (i, 0))
    return pl.pallas_call(
        _outproj_kernel,
        grid=(t // tm,),
        in_specs=[row(D_MODEL), row(ATTN_WIDTH), row(SSM_WIDTH), _full((ATTN_WIDTH, D_MODEL)),
                  _full((SSM_WIDTH, D_MODEL)), _full((1, D_MODEL)), _full((D_MODEL, CA_WIDTH))],
        out_specs=(row(D_MODEL), row(CA_WIDTH)),
        out_shape=(jax.ShapeDtypeStruct((t, D_MODEL), F32), jax.ShapeDtypeStruct((t, CA_WIDTH), BF16)),
        compiler_params=_cparams(("arbitrary",)),
        name="out_projection",
    )(x, att, ssm, w_att, w_ssm, g_ca, w_cq)


def _memkv_kernel(m_ref, g_ref, w_ref, k_ref, v_ref):
    m = _rms(m_ref[...], g_ref[...]).astype(BF16)
    kv = jnp.dot(m, w_ref[...], preferred_element_type=F32)
    k_ref[...] = kv[:, :CA_WIDTH]
    v_ref[...] = kv[:, CA_WIDTH:]


def _memkv(mem, g_mem, w_mkv_bf, tm):
    t = mem.shape[0]
    row = lambda w: pl.BlockSpec((tm, w), lambda i: (i, 0))
    return pl.pallas_call(
        _memkv_kernel,
        grid=(t // tm,),
        in_specs=[row(D_MODEL), _full((1, D_MODEL)), _full((D_MODEL, 2 * CA_WIDTH))],
        out_specs=(row(CA_WIDTH), row(CA_WIDTH)),
        out_shape=(jax.ShapeDtypeStruct((t, CA_WIDTH), F32), jax.ShapeDtypeStruct((t, CA_WIDTH), F32)),
        compiler_params=_cparams(("arbitrary",)),
        name="memory_kv",
    )(mem, g_mem, w_mkv_bf)


def _xattn_kernel(qc_ref, mk_ref, mv_ref, h_ref, wco_ref, gmoe_ref, wr_ref, br_ref, tri_ref, cin_ref,
                  h2_ref, xn_ref, ti_ref, tw_ref, cout_ref, cnt, *scratch, nbat, rows, head_split):
    step = pl.program_id(0)
    nsteps = pl.num_programs(0)

    @pl.when(step == 0)
    def _():
        cnt[...] = cin_ref[...]

    if head_split:
        mbuf, msem = scratch
        slot = step % 2

        def head_copies(s, sl):
            b0 = s * nbat
            return [pltpu.make_async_copy(src.at[pl.ds(b0, nbat), :, hd, :], mbuf.at[sl, kv, hd], msem.at[sl])
                    for kv, src in enumerate((mk_ref, mv_ref)) for hd in range(CA_HEADS)]

        @pl.when(step == 0)
        def _():
            for c in head_copies(0, 0):
                c.start()

        @pl.when(step + 1 < nsteps)
        def _():
            for c in head_copies(step + 1, 1 - slot):
                c.start()

        for c in head_copies(step, slot):
            c.wait()

    q_all = qc_ref[...]
    outs_b = []
    for bi in range(nbat):
        q = q_all[bi * rows:(bi + 1) * rows]
        outs = []
        for hd in range(CA_HEADS):
            sl = slice(hd * CA_HEAD_DIM, (hd + 1) * CA_HEAD_DIM)
            if head_split:
                mk = mbuf[slot, 0, hd, bi].astype(BF16)
                mv = mbuf[slot, 1, hd, bi].astype(BF16)
            else:
                mk = mk_ref[bi, :, sl].astype(BF16)
                mv = mv_ref[bi, :, sl].astype(BF16)
            s = lax.dot_general(q[:, sl], mk, (((1,), (1,)), ((), ())),
                                preferred_element_type=F32) * (CA_HEAD_DIM ** -0.5)
            m = jnp.max(s, axis=-1, keepdims=True)
            p = jnp.exp(s - m)
            denom = jnp.sum(p, axis=-1, keepdims=True)
            outs.append(jnp.dot(p.astype(BF16), mv, preferred_element_type=F32) / denom)
        outs_b.append(jnp.concatenate(outs, axis=-1))
    o = jnp.concatenate(outs_b, axis=0).astype(BF16)
    h2 = h_ref[...] + jnp.dot(o, wco_ref[...], preferred_element_type=F32)
    h2_ref[...] = h2
    xn = _rms(h2, gmoe_ref[...])
    _store_row_tiles(xn_ref, xn)
    logits = jnp.dot(xn.astype(BF16), wr_ref[...], preferred_element_type=F32) + br_ref[...]
    n = logits.shape[0]
    eidx = lax.broadcasted_iota(jnp.int32, (n, N_EXPERTS), 1).astype(F32)
    lane = lax.broadcasted_iota(jnp.int32, (n, LANES), 1)
    ti = jnp.zeros((n, LANES), F32)
    tv = jnp.zeros((n, LANES), F32)
    work = logits
    vals = []
    hits = []
    for k in range(TOP_K):
        mx = jnp.max(work, axis=-1, keepdims=True)
        idx = jnp.min(jnp.where(work == mx, eidx, float(N_EXPERTS)), axis=-1, keepdims=True)
        vals.append(mx)
        hits.append(eidx == idx)
        ti = jnp.where(lane == k, idx, ti)
        work = jnp.where(hits[k], -jnp.inf, work)
    ex = [jnp.exp(v - vals[0]) for v in vals]
    tot = ex[0] + ex[1] + ex[2] + ex[3]
    for k in range(TOP_K):
        tv = jnp.where(lane == k, ex[k] / tot, tv)
    sel = jnp.zeros((n, N_EXPERTS), F32)
    for k in range(TOP_K):
        sel = sel + jnp.where(hits[k], 1.0, 0.0)
    before = jnp.dot(tri_ref[...], sel.astype(BF16), preferred_element_type=F32) + cnt[...]
    for k in range(TOP_K):
        rank = jnp.sum(jnp.where(hits[k], before, 0.0), axis=-1, keepdims=True)
        ti = jnp.where(lane == TOP_K + k, rank, ti)
    cnt[...] = cnt[...] + jnp.sum(sel, axis=0, keepdims=True)
    ti_ref[...] = ti.astype(jnp.int32)
    tw_ref[...] = tv

    @pl.when(step == pl.num_programs(0) - 1)
    def _():
        cout_ref[...] = cnt[...]


def _xattn(qc, mk, mv, h1, w_co, g_moe, w_router, b_router, counts_in, nbat, rows):
    t = qc.shape[0]
    tm = nbat * rows
    row = lambda w: pl.BlockSpec((tm, w), lambda i: (i, 0))
    seq_rows = t // mk.shape[0]
    head_split = mk.ndim == 4
    scratch = [pltpu.VMEM((1, N_EXPERTS), F32)]
    if head_split:
        assert seq_rows == rows
        mem = pl.BlockSpec(memory_space=pl.ANY)
        scratch += [pltpu.VMEM((2, 2, CA_HEADS, nbat, N_MEM, CA_HEAD_DIM), F32), pltpu.SemaphoreType.DMA((2,))]
    else:
        mem = pl.BlockSpec((nbat, N_MEM, CA_WIDTH), lambda i: ((i * tm) // (seq_rows * nbat), 0, 0))
    tri = jnp.tri(tm, k=-1, dtype=BF16)
    return pl.pallas_call(
        functools.partial(_xattn_kernel, nbat=nbat, rows=rows, head_split=head_split),
        grid=(t // tm,),
        in_specs=[row(CA_WIDTH), mem, mem, row(D_MODEL), _full((CA_WIDTH, D_MODEL)), _full((1, D_MODEL)),
                  _full((D_MODEL, N_EXPERTS)), _full((1, N_EXPERTS)), _full((tm, tm)), _full((1, N_EXPERTS))],
        out_specs=(row(D_MODEL), _row_tile_spec(tm), row(LANES), row(LANES), _full((1, N_EXPERTS))),
        out_shape=(jax.ShapeDtypeStruct((t, D_MODEL), F32), _row_tile_shape(t),
                   jax.ShapeDtypeStruct((t, LANES), jnp.int32), jax.ShapeDtypeStruct((t, LANES), F32),
                   jax.ShapeDtypeStruct((1, N_EXPERTS), F32)),
        scratch_shapes=scratch,
        compiler_params=_cparams(("arbitrary",), VMEM_LIMIT),
        name="cross_attn_router",
    )(qc, mk, mv, h1, w_co, g_moe, w_router, b_router, tri, counts_in)


def _scatter_kernel(dest_hbm, x_ref, *rest, tm):
    xs_hbm, idx, isem, sem = rest[-4:]
    i = pl.program_id(0)
    n = pl.num_programs(0)
    slot = i % 2
    per_step = tm * TOP_K

    def idx_copy(step, sl):
        return pltpu.make_async_copy(dest_hbm.at[step], idx.at[pl.ds(sl * per_step, per_step)], isem.at[sl])

    @pl.when(i == 0)
    def _():
        idx_copy(0, 0).start()

    @pl.when(i + 1 < n)
    def _():
        idx_copy(i + 1, 1 - slot).start()

    idx_copy(i, slot).wait()
    off = slot * per_step

    def issue(j, carry):
        for half in range(2):
            base = off + j * (SUBLANES * TOP_K) + half * DMA_BATCH
            dst = [idx[base + q] for q in range(DMA_BATCH)]
            for q in range(DMA_BATCH):
                s = (half * DMA_BATCH + q) // TOP_K
                pltpu.make_async_copy(x_ref.at[j, :, s, :], xs_hbm.at[dst[q]], sem).start(priority=q % 2)
        return carry

    lax.fori_loop(0, tm // SUBLANES, issue, 0)

    for k in range(TOP_K):
        pltpu.make_async_copy(xs_hbm.at[pl.ds(0, tm)], xs_hbm.at[pl.ds(0, tm)], sem).wait()


def _scatter_rows(dest, x, n_slots, tm, xs_prev=None):
    t = x.shape[0] * SUBLANES
    args = [dest.reshape(t // tm, tm * TOP_K), x]
    in_specs = [pl.BlockSpec(memory_space=pl.ANY), _row_tile_spec(tm)]
    aliases = {}
    if xs_prev is not None:
        args.append(xs_prev)
        in_specs.append(pl.BlockSpec(memory_space=pl.ANY))
        aliases = {2: 0}
    return pl.pallas_call(
        functools.partial(_scatter_kernel, tm=tm),
        grid=(t // tm,),
        in_specs=in_specs,
        out_specs=pl.BlockSpec(memory_space=pl.ANY),
        out_shape=jax.ShapeDtypeStruct((n_slots, LANE_TILES, LANES), F32),
        scratch_shapes=[pltpu.SMEM((2 * tm * TOP_K,), jnp.int32), pltpu.SemaphoreType.DMA((2,)),
                        pltpu.SemaphoreType.DMA(())],
        input_output_aliases=aliases,
        compiler_params=_cparams(("arbitrary",)),
        name="moe_scatter",
    )(*args)


def _expert_kernel(blk_e_ref, w_idx_ref, nused_ref, xs_hbm, wg_ref, bg_ref, wu_ref, bu_ref, wd_ref, bd_ref, ys_hbm,
                   wg_bf, wu_bf, wd_bf, xbuf, ybuf, xsem, ysem):
    i = pl.program_id(0)
    n_used = nused_ref[0]
    active = i < n_used
    slot = i % 2
    tiles = MOE_BLOCK // SUBLANES

    def x_copies(blk, sl):
        return [pltpu.make_async_copy(xs_hbm.at[pl.ds(blk * tiles, tiles), s], xbuf.at[sl, :, :, s, :], xsem.at[sl])
                for s in range(SUBLANES)]

    def y_copies(blk, sl):
        return [pltpu.make_async_copy(ybuf.at[sl, :, :, s, :], ys_hbm.at[pl.ds(blk * tiles, tiles), s], ysem.at[sl])
                for s in range(SUBLANES)]

    @pl.when((i == 0) & active)
    def _():
        for c in x_copies(0, 0):
            c.start()

    prev_e = blk_e_ref[jnp.maximum(i - 1, 0)]
    new_expert = (i == 0) | (blk_e_ref[i] != prev_e)

    @pl.when(active & new_expert)
    def _():
        wg_bf[...] = wg_ref[0].astype(BF16)
        wu_bf[...] = wu_ref[0].astype(BF16)
        wd_bf[...] = wd_ref[0].astype(BF16)

    @pl.when(active)
    def _():
        for c in x_copies(i, slot):
            c.wait()

        @pl.when(i + 1 < n_used)
        def _():
            for c in x_copies(i + 1, 1 - slot):
                c.start()

        x = _load_row_tiles(xbuf.at[slot]).astype(BF16)
        gate = jnp.minimum(jnp.dot(x, wg_bf[...], preferred_element_type=F32) + bg_ref[0], SWIGLU_LIMIT)
        up = jnp.clip(jnp.dot(x, wu_bf[...], preferred_element_type=F32) + bu_ref[0], -SWIGLU_LIMIT, SWIGLU_LIMIT)
        h = gate * jax.nn.sigmoid(SWIGLU_ALPHA * gate) * (up + 1.0)
        y = jnp.dot(h.astype(BF16), wd_bf[...], preferred_element_type=F32) + bd_ref[0]

        @pl.when(i >= 1)
        def _():
            for c in y_copies(i - 1, 1 - slot):
                c.wait()

        _store_row_tiles(ybuf.at[slot], y)
        for c in y_copies(i, slot):
            c.start()

        @pl.when(i == n_used - 1)
        def _():
            for c in y_copies(i, slot):
                c.wait()


def _experts(blk_e, w_idx, n_used, xs, w_gate, b_gate, w_up, b_up, w_down, b_down):
    n_slots = xs.shape[0]
    n_blocks = n_slots // MOE_BLOCK
    d_ff = w_gate.shape[-1]
    tiles = MOE_BLOCK // SUBLANES
    wspec = lambda a, b: pl.BlockSpec((1, a, b), lambda i, be, wi, nu: (wi[i], 0, 0))
    bspec = lambda b: pl.BlockSpec((1, 1, b), lambda i, be, wi, nu: (be[i], 0, 0))
    any_spec = pl.BlockSpec(memory_space=pl.ANY)
    grid_spec = pltpu.PrefetchScalarGridSpec(
        num_scalar_prefetch=3,
        grid=(n_blocks,),
        in_specs=[any_spec, wspec(D_MODEL, d_ff), bspec(d_ff), wspec(D_MODEL, d_ff), bspec(d_ff),
                  wspec(d_ff, D_MODEL), bspec(D_MODEL)],
        out_specs=any_spec,
        scratch_shapes=[pltpu.VMEM((D_MODEL, d_ff), BF16), pltpu.VMEM((D_MODEL, d_ff), BF16),
                        pltpu.VMEM((d_ff, D_MODEL), BF16),
                        pltpu.VMEM((2, tiles, LANE_TILES, SUBLANES, LANES), F32),
                        pltpu.VMEM((2, tiles, LANE_TILES, SUBLANES, LANES), F32),
                        pltpu.SemaphoreType.DMA((2,)), pltpu.SemaphoreType.DMA((2,))],
    )
    ys = pl.pallas_call(
        _expert_kernel,
        grid_spec=grid_spec,
        out_shape=jax.ShapeDtypeStruct((n_slots // SUBLANES, SUBLANES, LANE_TILES, LANES), F32),
        compiler_params=_cparams(("arbitrary",), VMEM_LIMIT),
        name="moe_experts",
    )(blk_e, w_idx, n_used, xs.reshape(n_slots // SUBLANES, SUBLANES, LANE_TILES, LANES), w_gate,
      b_gate.reshape(N_EXPERTS, 1, d_ff), w_up, b_up.reshape(N_EXPERTS, 1, d_ff), w_down,
      b_down.reshape(N_EXPERTS, 1, D_MODEL))
    return ys.reshape(n_slots, LANE_TILES, LANES)


def _combine_kernel(dest_hbm, ys_hbm, h_ref, tw_ref, g_ref, o_ref, idx, isem, buf, sem, *, tm):
    i = pl.program_id(0)
    n = pl.num_programs(0)
    slot = i % 2
    per_step = tm * TOP_K

    def idx_copy(step, sl):
        return pltpu.make_async_copy(dest_hbm.at[step], idx.at[pl.ds(sl * per_step, per_step)], isem.at[sl])

    def issue(sl):
        def body(j, carry):
            for half in range(2):
                base = sl * per_step + j * (SUBLANES * TOP_K) + half * DMA_BATCH
                src = [idx[base + q] for q in range(DMA_BATCH)]
                for q in range(DMA_BATCH):
                    s, k = divmod(half * DMA_BATCH + q, TOP_K)
                    pltpu.make_async_copy(ys_hbm.at[src[q]], buf.at[sl, k, j, :, s, :],
                                          sem.at[sl]).start(priority=q % 2)
            return carry
        lax.fori_loop(0, tm // SUBLANES, body, 0)

    @pl.when(i == 0)
    def _():
        idx_copy(0, 0).start()
        idx_copy(0, 0).wait()
        issue(0)

        @pl.when(n > 1)
        def _():
            idx_copy(1, 1).start()

    @pl.when(i + 1 < n)
    def _():
        idx_copy(i + 1, 1 - slot).wait()
        issue(1 - slot)

    @pl.when(i + 2 < n)
    def _():
        idx_copy(i + 2, slot).start()

    for k in range(TOP_K):
        pltpu.make_async_copy(ys_hbm.at[pl.ds(0, tm)], ys_hbm.at[pl.ds(0, tm)], sem.at[slot]).wait()
    tw = tw_ref[...]
    y = h_ref[...]
    for k in range(TOP_K):
        y = y + tw[:, k:k + 1] * _load_row_tiles(buf.at[slot, k])
    o_ref[...] = _rms(y, g_ref[...])


def _combine(dest, ys, h2, tw, g_final, tm):
    t = h2.shape[0]
    row = lambda w: pl.BlockSpec((tm, w), lambda i: (i, 0))
    return pl.pallas_call(
        functools.partial(_combine_kernel, tm=tm),
        grid=(t // tm,),
        in_specs=[pl.BlockSpec(memory_space=pl.ANY), pl.BlockSpec(memory_space=pl.ANY), row(D_MODEL), row(LANES),
                  _full((1, D_MODEL))],
        out_specs=row(D_MODEL),
        out_shape=jax.ShapeDtypeStruct((t, D_MODEL), F32),
        scratch_shapes=[pltpu.SMEM((2 * tm * TOP_K,), jnp.int32), pltpu.SemaphoreType.DMA((2,)),
                        pltpu.VMEM((2, TOP_K, tm // SUBLANES, LANE_TILES, SUBLANES, LANES), F32),
                        pltpu.SemaphoreType.DMA((2,))],
        compiler_params=_cparams(("arbitrary",), VMEM_LIMIT),
        name="moe_combine",
    )(dest.reshape(t // tm, tm * TOP_K), ys, h2, tw, g_final)


def _route(counts, m):
    counts = counts.reshape(N_EXPERTS).astype(jnp.int32)
    padded = (counts + MOE_BLOCK - 1) // MOE_BLOCK * MOE_BLOCK
    pad_end = jnp.cumsum(padded)
    pad_start = pad_end - padded
    n_blocks = -(-(m + N_EXPERTS * (MOE_BLOCK - 1)) // MOE_BLOCK)
    blk_start = jnp.arange(n_blocks, dtype=jnp.int32) * MOE_BLOCK
    owner = lambda start: jnp.minimum(jnp.sum((pad_end[None, :] <= start[:, None]).astype(jnp.int32), axis=1),
                                      N_EXPERTS - 1).astype(jnp.int32)
    n_used = pad_end[-1] // MOE_BLOCK
    last_start = jnp.maximum(n_used - 1, 0) * MOE_BLOCK
    blk_e = owner(jnp.minimum(blk_start, last_start))
    first = jnp.concatenate([jnp.ones((1,), bool), blk_e[1:] != blk_e[:-1]])
    region_end = jnp.sum(jnp.where(blk_e[:, None] == jnp.arange(N_EXPERTS)[None, :], pad_end[None, :], 0), axis=1)
    w_idx = jnp.where(first, blk_e, owner(jnp.minimum(region_end, last_start))).astype(jnp.int32)
    return pad_start, blk_e, w_idx, n_used.astype(jnp.int32).reshape(1), n_blocks * MOE_BLOCK


def _dest(ti, pad_start):
    e = ti[:, :TOP_K]
    onehot = e[:, :, None] == jnp.arange(N_EXPERTS, dtype=jnp.int32)[None, None, :]
    start = jnp.sum(jnp.where(onehot, pad_start[None, None, :], 0), axis=-1)
    return (start + ti[:, TOP_K:2 * TOP_K]).reshape(-1)


def _block_diag(blocks):
    g, a, b = blocks.shape
    eye = jnp.eye(g, dtype=blocks.dtype)
    return (blocks[:, :, None, :] * eye[:, None, :, None]).reshape(g * a, g * b)


def kernel(x_prompt, x_sample, mem_prompt, cache_win_k, cache_win_v, state_ssm_re, state_ssm_im, cache_mem_k,
           cache_mem_v, g_mix, w_in, sink, lam_re, lam_im, log_dt, b_re, b_im, c_re, c_im, d_skip, w_glu,
           g_attn_out, g_ssm_out, w_out, g_ca, g_mem, w_mk, w_mv, w_cq, w_co, g_moe, w_router, b_router,
           w_gate, b_gate, w_up, b_up, w_down, b_down, g_final):
    depth = g_mix.shape[0]
    assert depth == 1
    bp, lp, _ = x_prompt.shape
    bs, ls, _ = x_sample.shape
    tp, ts = bp * lp, bs * ls

    w_in_bf = w_in[0].astype(BF16)
    w_att_bf = w_out[0, :ATTN_WIDTH].astype(BF16)
    w_ssm_bf = w_out[0, ATTN_WIDTH:].astype(BF16)
    w_cq_bf = w_cq[0].astype(BF16)
    w_co_bf = w_co[0].astype(BF16)
    w_mkv_bf = jnp.concatenate([w_mk[0], w_mv[0]], axis=1).astype(BF16)
    w_glu_bf = w_glu[0].astype(BF16)
    row = lambda a: a.reshape(1, -1)

    a_re, a_im, bb_re, bb_im = _discretize(lam_re[0], lam_im[0], log_dt[0], b_re[0].transpose(0, 2, 1),
                                           b_im[0].transpose(0, 2, 1))
    hg = SSM_GROUPS // 2
    wb = jnp.stack([jnp.concatenate([_block_diag(bb_re[h * hg:(h + 1) * hg]),
                                     _block_diag(bb_im[h * hg:(h + 1) * hg])], axis=1) for h in range(2)]).astype(BF16)
    cre_t = c_re[0].transpose(0, 2, 1)
    cim_t = c_im[0].transpose(0, 2, 1)
    wc = jnp.stack([jnp.concatenate([_block_diag(cre_t[h * hg:(h + 1) * hg]),
                                     -_block_diag(cim_t[h * hg:(h + 1) * hg])], axis=0) for h in range(2)]).astype(BF16)
    ssm_w = (wb, wc, a_re.reshape(1, SSM_LANES), a_im.reshape(1, SSM_LANES), row(d_skip[0]), w_glu_bf,
             row(g_ssm_out[0]))

    xp = x_prompt.reshape(tp, D_MODEL)
    xs_ = x_sample.reshape(ts, D_MODEL)

    qp, kp, vp, up = _inproj(xp, row(g_mix[0]), w_in_bf, 512)
    qs, ks, vs, us = _inproj(xs_, row(g_mix[0]), w_in_bf, 256)

    att_p = _attn_prompt(sink[0], qp, kp, vp, row(g_attn_out[0]), bp, lp)
    ck = cache_win_k[0].reshape(bs, WINDOW, KV_WIDTH)
    cv = cache_win_v[0].reshape(bs, WINDOW, KV_WIDTH)
    att_s = _attn_sample(sink[0], qs, ks, vs, ck, cv, row(g_attn_out[0]), bs, ls, 16)

    zero_state = jnp.zeros((bp, SSM_LANES), F32)
    ssm_p, p_sre, p_sim = _ssm(up.reshape(bp, lp, SSM_WIDTH), zero_state, zero_state, ssm_w, 32)
    ssm_s, s_sre, s_sim = _ssm(us.reshape(bs, ls, SSM_WIDTH), state_ssm_re[0].reshape(bs, SSM_LANES),
                               state_ssm_im[0].reshape(bs, SSM_LANES), ssm_w, ls)

    h1p, qcp = _outproj(xp, att_p, ssm_p.reshape(tp, SSM_WIDTH), w_att_bf, w_ssm_bf, row(g_ca[0]), w_cq_bf, 512)
    h1s, qcs = _outproj(xs_, att_s, ssm_s.reshape(ts, SSM_WIDTH), w_att_bf, w_ssm_bf, row(g_ca[0]), w_cq_bf, 256)

    mk_p, mv_p = _memkv(mem_prompt.reshape(bp * N_MEM, D_MODEL), row(g_mem[0]), w_mkv_bf, 512)

    no_counts = jnp.zeros((1, N_EXPERTS), F32)
    w_router_bf = w_router[0].astype(BF16)
    h2p, xnp_, tip, twp, cnt_p = _xattn(qcp, mk_p.reshape(bp, N_MEM, CA_WIDTH), mv_p.reshape(bp, N_MEM, CA_WIDTH),
                                        h1p, w_co_bf, row(g_moe[0]), w_router_bf, row(b_router[0]), no_counts,
                                        1, 512)
    h2s, xns_, tis, tws, cnt = _xattn(qcs, cache_mem_k[0], cache_mem_v[0], h1s, w_co_bf, row(g_moe[0]),
                                      w_router_bf, row(b_router[0]), cnt_p, 8, ls)

    pad_start, blk_e, w_idx, n_used, n_slots = _route(cnt, (tp + ts) * TOP_K)
    dest_p = _dest(tip, pad_start)
    dest_s = _dest(tis, pad_start)
    xs_sorted = _scatter_rows(dest_p, xnp_, n_slots, 256)
    xs_sorted = _scatter_rows(dest_s, xns_, n_slots, 256, xs_prev=xs_sorted)
    ys = _experts(blk_e, w_idx, n_used, xs_sorted, w_gate[0], b_gate[0], w_up[0], b_up[0], w_down[0], b_down[0])
    y_prompt = _combine(dest_p, ys, h2p, twp, row(g_final), 128).reshape(bp, lp, D_MODEL)
    y_sample = _combine(dest_s, ys, h2s, tws, row(g_final), 128).reshape(bs, ls, D_MODEL)

    kp4 = kp.reshape(bp, lp, N_KV_HEADS, HEAD_DIM)
    vp4 = vp.reshape(bp, lp, N_KV_HEADS, HEAD_DIM)
    p_win_k = kp4[:, -WINDOW:][None]
    p_win_v = vp4[:, -WINDOW:][None]
    s_win_k = jnp.concatenate([cache_win_k[0], ks.reshape(bs, ls, N_KV_HEADS, HEAD_DIM)], axis=1)[:, -WINDOW:][None]
    s_win_v = jnp.concatenate([cache_win_v[0], vs.reshape(bs, ls, N_KV_HEADS, HEAD_DIM)], axis=1)[:, -WINDOW:][None]
    st = lambda a, b: a.reshape(1, b, SSM_GROUPS, SSM_STATE)
    p_mem_k = mk_p.reshape(1, bp, N_MEM, CA_HEADS, CA_HEAD_DIM)
    p_mem_v = mv_p.reshape(1, bp, N_MEM, CA_HEADS, CA_HEAD_DIM)
    return (y_prompt, y_sample, p_win_k, p_win_v, st(p_sre, bp), st(p_sim, bp), p_mem_k, p_mem_v,
            s_win_k, s_win_v, st(s_sre, bs), st(s_sim, bs))
```

```python
import functools
import math

import jax
import jax.numpy as jnp
from jax import lax
from jax.experimental import pallas as pl
from jax.experimental.pallas import tpu as pltpu

F32 = jnp.float32
BF16 = jnp.bfloat16

D_MODEL = 1024
HEAD_DIM = 64
N_Q_HEADS = 8
N_KV_HEADS = 2
Q_PER_KV = 4
ATTN_WIDTH = 512
KV_WIDTH = 128
WINDOW = 128
SSM_WIDTH = 512
SSM_GROUP = 16
SSM_GROUPS = 32
SSM_STATE = 64
SSM_LANES = SSM_GROUPS * SSM_STATE
SSM_HALF = SSM_LANES // 2
IN_WIDTH = ATTN_WIDTH + 2 * KV_WIDTH + SSM_WIDTH
N_MEM = 256
CA_HEADS = 4
CA_HEAD_DIM = 128
CA_WIDTH = 512
N_EXPERTS = 32
TOP_K = 4
SWIGLU_ALPHA = 1.702
SWIGLU_LIMIT = 7.0
RMS_EPS = 1e-5

SUBLANES = 8
LANES = 128
MOE_BLOCK = 256
DMA_BATCH = 16
TAB_RING = 4
DUMP_TILES = 2 * MOE_BLOCK
VMEM_LIMIT = 56 * 1024 * 1024


def _cparams(sem, vmem=None):
    return pltpu.CompilerParams(dimension_semantics=sem, vmem_limit_bytes=vmem)


def _rms(x, g):
    return x * lax.rsqrt(jnp.mean(x * x, axis=-1, keepdims=True) + RMS_EPS) * g


def _full(shape):
    return pl.BlockSpec(shape, lambda *_: (0,) * len(shape))


LANE_TILES = D_MODEL // LANES


def _store_row_tiles(ref, x):
    for c in range(LANE_TILES):
        ref[:, c] = x[:, c * LANES:(c + 1) * LANES].reshape(x.shape[0] // SUBLANES, SUBLANES, LANES)


def _load_row_tiles(ref):
    rows = ref.shape[0] * SUBLANES
    return jnp.concatenate([ref[:, c].reshape(rows, LANES) for c in range(LANE_TILES)], axis=1)


def _disc_kernel(lr_ref, li_ref, ldt_ref, bre_ref, bim_ref, are_ref, aim_ref, bbre_ref, bbim_ref):
    lr = lr_ref[...]
    li = li_ref[...]
    dt = jnp.exp(ldt_ref[...])
    mag = jnp.exp(lr * dt)
    ang = li * dt
    ab_re = mag * jnp.cos(ang)
    ab_im = mag * jnp.sin(ang)
    nr = ab_re - 1.0
    ni = ab_im
    den = lr * lr + li * li
    f_re = (nr * lr + ni * li) / den
    f_im = (ni * lr - nr * li) / den
    are_ref[...] = ab_re
    aim_ref[...] = ab_im
    br = bre_ref[...]
    bi = bim_ref[...]
    bbre_ref[...] = f_re[:, None, :] * br - f_im[:, None, :] * bi
    bbim_ref[...] = f_re[:, None, :] * bi + f_im[:, None, :] * br


def _discretize(lam_re, lam_im, log_dt, b_re_t, b_im_t):
    g, n = lam_re.shape
    c = b_re_t.shape[1]
    return pl.pallas_call(
        _disc_kernel,
        out_shape=(jax.ShapeDtypeStruct((g, n), F32), jax.ShapeDtypeStruct((g, n), F32),
                   jax.ShapeDtypeStruct((g, c, n), F32), jax.ShapeDtypeStruct((g, c, n), F32)),
        name="ssm_discretize",
    )(lam_re, lam_im, log_dt.reshape(g, 1), b_re_t, b_im_t)


def _inproj_kernel(x_ref, g_ref, w_ref, q_ref, k_ref, v_ref, u_ref):
    a = _rms(x_ref[...], g_ref[...]).astype(BF16)
    proj = jnp.dot(a, w_ref[...], preferred_element_type=F32)
    q_ref[...] = proj[:, :ATTN_WIDTH].astype(BF16)
    k_ref[...] = proj[:, ATTN_WIDTH:ATTN_WIDTH + KV_WIDTH]
    v_ref[...] = proj[:, ATTN_WIDTH + KV_WIDTH:ATTN_WIDTH + 2 * KV_WIDTH]
    u_ref[...] = proj[:, ATTN_WIDTH + 2 * KV_WIDTH:]


def _inproj(x, g_mix, w_in_bf, tm):
    t = x.shape[0]
    row = lambda w: pl.BlockSpec((tm, w), lambda i: (i, 0))
    return pl.pallas_call(
        _inproj_kernel,
        grid=(t // tm,),
        in_specs=[row(D_MODEL), _full((1, D_MODEL)), _full((D_MODEL, IN_WIDTH))],
        out_specs=(row(ATTN_WIDTH), row(KV_WIDTH), row(KV_WIDTH), row(SSM_WIDTH)),
        out_shape=(jax.ShapeDtypeStruct((t, ATTN_WIDTH), BF16), jax.ShapeDtypeStruct((t, KV_WIDTH), F32),
                   jax.ShapeDtypeStruct((t, KV_WIDTH), F32), jax.ShapeDtypeStruct((t, SSM_WIDTH), F32)),
        compiler_params=_cparams(("arbitrary",)),
        name="in_projection",
    )(x, g_mix, w_in_bf)


def _sink_softmax_pv(s, sink, v_bf):
    m = jnp.maximum(jnp.max(s, axis=-1, keepdims=True), sink)
    p = jnp.exp(s - m)
    denom = jnp.sum(p, axis=-1, keepdims=True) + jnp.exp(sink - m)
    return jnp.dot(p.astype(BF16), v_bf, preferred_element_type=F32) / denom


def _attn_prompt_kernel(sink_ref, q_ref, kc_ref, kp_ref, vc_ref, vp_ref, g_ref, o_ref):
    i = pl.program_id(1)
    q = q_ref[...]
    kk = jnp.concatenate([kp_ref[...], kc_ref[...]], axis=0).astype(BF16)
    vv = jnp.concatenate([vp_ref[...], vc_ref[...]], axis=0).astype(BF16)
    rows = Q_PER_KV * WINDOW
    qi = lax.broadcasted_iota(jnp.int32, (rows, 2 * WINDOW), 0) & (WINDOW - 1)
    kj = lax.broadcasted_iota(jnp.int32, (rows, 2 * WINDOW), 1)
    rel = qi + WINDOW - kj
    mask = (rel >= 0) & (rel < WINDOW) & ((kj >= WINDOW) | (i > 0))
    head_of_row = lax.broadcasted_iota(jnp.int32, (rows, 1), 0) // WINDOW
    low_q = lax.broadcasted_iota(jnp.int32, (WINDOW, LANES), 1) < HEAD_DIM
    zero_q = jnp.zeros((WINDOW, LANES), BF16)
    pairs = []
    for j in range(N_KV_HEADS):
        kh = kk[:, j * HEAD_DIM:(j + 1) * HEAD_DIM]
        vh = vv[:, j * HEAD_DIM:(j + 1) * HEAD_DIM]
        k2 = jnp.concatenate([kh, kh], axis=1)
        v2 = jnp.concatenate([vh, vh], axis=1)
        stacked = []
        for p in range(Q_PER_KV // 2):
            qp = q[:, (j * Q_PER_KV + 2 * p) * HEAD_DIM:(j * Q_PER_KV + 2 * p + 2) * HEAD_DIM]
            stacked += [jnp.where(low_q, qp, zero_q), jnp.where(low_q, zero_q, qp)]
        qs = jnp.concatenate(stacked, axis=0)
        sink = jnp.zeros((rows, 1), F32)
        for g in range(Q_PER_KV):
            sink = jnp.where(head_of_row == g, sink_ref[j * Q_PER_KV + g], sink)
        s = lax.dot_general(qs, k2, (((1,), (1,)), ((), ())), preferred_element_type=F32) * (HEAD_DIM ** -0.5)
        s = jnp.where(mask, s, -jnp.inf)
        o = _sink_softmax_pv(s, sink, v2)
        for p in range(Q_PER_KV // 2):
            pairs.append(jnp.where(low_q, o[2 * p * WINDOW:(2 * p + 1) * WINDOW],
                                   o[(2 * p + 1) * WINDOW:(2 * p + 2) * WINDOW]))
    att = jnp.concatenate(pairs, axis=-1)
    o_ref[...] = _rms(att, g_ref[...]).astype(BF16)


def _attn_prompt(sink, q, k, v, g_attn_out, b, l):
    nb = l // WINDOW
    cur = lambda w: pl.BlockSpec((WINDOW, w), lambda bi, i: (bi * nb + i, 0))
    prev = lambda w: pl.BlockSpec((WINDOW, w), lambda bi, i: (bi * nb + jnp.maximum(i - 1, 0), 0))
    return pl.pallas_call(
        _attn_prompt_kernel,
        grid=(b, nb),
        in_specs=[pl.BlockSpec(memory_space=pltpu.SMEM), cur(ATTN_WIDTH), cur(KV_WIDTH), prev(KV_WIDTH),
                  cur(KV_WIDTH), prev(KV_WIDTH), _full((1, ATTN_WIDTH))],
        out_specs=cur(ATTN_WIDTH),
        out_shape=jax.ShapeDtypeStruct((b * l, ATTN_WIDTH), BF16),
        compiler_params=_cparams(("arbitrary", "arbitrary")),
        name="window_attn_prompt",
    )(sink, q, k, k, v, v, g_attn_out)


def _attn_sample_kernel(sink_ref, q_ref, kn_ref, vn_ref, ck_ref, cv_ref, g_ref, o_ref, *, bt, t):
    nk = WINDOW + t
    rows = Q_PER_KV * t
    ri = lax.broadcasted_iota(jnp.int32, (rows, nk), 0)
    kj = lax.broadcasted_iota(jnp.int32, (rows, nk), 1)
    tq = ri % t
    mask = ((kj < WINDOW) & (kj > tq)) | ((kj >= WINDOW) & ((kj - WINDOW) <= tq))
    rg = lax.broadcasted_iota(jnp.int32, (rows, 1), 0) // t
    q_all = q_ref[...]
    kn_all = kn_ref[...]
    vn_all = vn_ref[...]
    outs_b = []
    for bi in range(bt):
        q = q_all[bi * t:(bi + 1) * t]
        kk = jnp.concatenate([ck_ref[bi], kn_all[bi * t:(bi + 1) * t]], axis=0).astype(BF16)
        vv = jnp.concatenate([cv_ref[bi], vn_all[bi * t:(bi + 1) * t]], axis=0).astype(BF16)
        outs = []
        for j in range(N_KV_HEADS):
            qj = jnp.concatenate(
                [q[:, (j * Q_PER_KV + g) * HEAD_DIM:(j * Q_PER_KV + g + 1) * HEAD_DIM] for g in range(Q_PER_KV)],
                axis=0)
            sink = jnp.zeros((rows, 1), F32)
            for g in range(Q_PER_KV):
                sink = jnp.where(rg == g, sink_ref[j * Q_PER_KV + g], sink)
            kh = kk[:, j * HEAD_DIM:(j + 1) * HEAD_DIM]
            vh = vv[:, j * HEAD_DIM:(j + 1) * HEAD_DIM]
            s = lax.dot_general(qj, kh, (((1,), (1,)), ((), ())), preferred_element_type=F32) * (HEAD_DIM ** -0.5)
            s = jnp.where(mask, s, -jnp.inf)
            o = _sink_softmax_pv(s, sink, vh)
            outs.extend(o[g * t:(g + 1) * t] for g in range(Q_PER_KV))
        outs_b.append(jnp.concatenate(outs, axis=-1))
    att = jnp.concatenate(outs_b, axis=0)
    o_ref[...] = _rms(att, g_ref[...]).astype(BF16)


def _attn_sample(sink, q, k, v, cache_k, cache_v, g_attn_out, b, t, bt):
    row = lambda w: pl.BlockSpec((bt * t, w), lambda i: (i, 0))
    cache = pl.BlockSpec((bt, WINDOW, KV_WIDTH), lambda i: (i, 0, 0))
    return pl.pallas_call(
        functools.partial(_attn_sample_kernel, bt=bt, t=t),
        grid=(b // bt,),
        in_specs=[pl.BlockSpec(memory_space=pltpu.SMEM), row(ATTN_WIDTH), row(KV_WIDTH), row(KV_WIDTH),
                  cache, cache, _full((1, ATTN_WIDTH))],
        out_specs=row(ATTN_WIDTH),
        out_shape=jax.ShapeDtypeStruct((b * t, ATTN_WIDTH), BF16),
        compiler_params=_cparams(("arbitrary",)),
        name="window_attn_sample",
    )(sink, q, k, v, cache_k, cache_v, g_attn_out)


def _ssm_kernel(u_ref, perm_ref, permt_ref, wb_ref, wc_ref, are_ref, aim_ref, d_ref, wglu_ref, g_ref,
                h0re_ref, h0im_ref, o_ref, hre_ref, him_ref, vre, vim, sre, sim, *, nb, s):
    c = pl.program_id(0)
    r = nb * s

    @pl.when(c == 0)
    def _():
        sre[...] = h0re_ref[...]
        sim[...] = h0im_ref[...]

    u = u_ref[...].reshape(r, SSM_WIDTH)
    u_hi = u.astype(BF16)
    u_lo = (u - u_hi.astype(F32)).astype(BF16)
    perm = perm_ref[...]
    u_tb = (jnp.dot(perm, u_hi, preferred_element_type=F32) + jnp.dot(perm, u_lo, preferred_element_type=F32))
    u_bf = u_tb.astype(BF16)
    half_w = SSM_WIDTH // 2
    for h in range(2):
        vv = jnp.dot(u_bf[:, h * half_w:(h + 1) * half_w], wb_ref[h], preferred_element_type=F32)
        vre[:, h * SSM_HALF:(h + 1) * SSM_HALF] = vv[:, :SSM_HALF]
        vim[:, h * SSM_HALF:(h + 1) * SSM_HALF] = vv[:, SSM_HALF:]

    lane_chunk = SSM_LANES // 2
    for rg in range(nb // SUBLANES):
        for lc in range(SSM_LANES // lane_chunk):
            lanes = slice(lc * lane_chunk, (lc + 1) * lane_chunk)
            rows0 = slice(rg * SUBLANES, (rg + 1) * SUBLANES)
            ar = jnp.broadcast_to(are_ref[:, lanes], (SUBLANES, lane_chunk))
            ai = jnp.broadcast_to(aim_ref[:, lanes], (SUBLANES, lane_chunk))

            def step(t, carry, lanes=lanes, rg=rg, ar=ar, ai=ai):
                hr, hi = carry
                row = pl.multiple_of(t * nb + rg * SUBLANES, SUBLANES)
                nhr = ar * hr - ai * hi + vre[pl.ds(row, SUBLANES), lanes]
                nhi = ar * hi + ai * hr + vim[pl.ds(row, SUBLANES), lanes]
                vre[pl.ds(row, SUBLANES), lanes] = nhr
                vim[pl.ds(row, SUBLANES), lanes] = nhi
                return nhr, nhi

            hr, hi = lax.fori_loop(0, s, step, (sre[rows0, lanes], sim[rows0, lanes]), unroll=True)
            sre[rows0, lanes] = hr
            sim[rows0, lanes] = hi

    ys = []
    for h in range(2):
        hcat = jnp.concatenate([vre[:, h * SSM_HALF:(h + 1) * SSM_HALF], vim[:, h * SSM_HALF:(h + 1) * SSM_HALF]],
                               axis=1).astype(BF16)
        ys.append(jnp.dot(hcat, wc_ref[h], preferred_element_type=F32))
    y = jnp.concatenate(ys, axis=1) + d_ref[...] * u_tb
    g = jax.nn.gelu(y)
    z = jnp.dot(g.astype(BF16), wglu_ref[...], preferred_element_type=F32)
    out = g * jax.nn.sigmoid(z)
    on = _rms(out, g_ref[...]).astype(BF16)
    o_bt = jnp.dot(permt_ref[...], on, preferred_element_type=F32).astype(BF16)
    o_ref[...] = o_bt.reshape(nb, s, SSM_WIDTH)

    @pl.when(c == pl.num_programs(0) - 1)
    def _():
        hre_ref[...] = sre[...]
        him_ref[...] = sim[...]


def _ssm(u, h0_re, h0_im, ssm_w, s):
    nb, l, _ = u.shape
    r = nb * s
    ridx = jnp.arange(r)
    src = (ridx % nb) * s + ridx // nb
    perm = (src[:, None] == ridx[None, :]).astype(BF16)
    wb, wc, a_re, a_im, d_skip, w_glu, g_out = ssm_w
    seq_blk = pl.BlockSpec((nb, s, SSM_WIDTH), lambda c: (0, c, 0))
    st_blk = _full((nb, SSM_LANES))
    return pl.pallas_call(
        functools.partial(_ssm_kernel, nb=nb, s=s),
        grid=(l // s,),
        in_specs=[seq_blk, _full((r, r)), _full((r, r)), _full(wb.shape), _full(wc.shape),
                  _full((1, SSM_LANES)), _full((1, SSM_LANES)), _full((1, SSM_WIDTH)),
                  _full((SSM_WIDTH, SSM_WIDTH)), _full((1, SSM_WIDTH)), st_blk, st_blk],
        out_specs=(seq_blk, st_blk, st_blk),
        out_shape=(jax.ShapeDtypeStruct((nb, l, SSM_WIDTH), BF16),
                   jax.ShapeDtypeStruct((nb, SSM_LANES), F32), jax.ShapeDtypeStruct((nb, SSM_LANES), F32)),
        scratch_shapes=[pltpu.VMEM((r, SSM_LANES), F32), pltpu.VMEM((r, SSM_LANES), F32),
                        pltpu.VMEM((nb, SSM_LANES), F32), pltpu.VMEM((nb, SSM_LANES), F32)],
        compiler_params=_cparams(("arbitrary",), VMEM_LIMIT),
        name="ssm_scan",
    )(u, perm, perm.T, wb, wc, a_re, a_im, d_skip, w_glu, g_out, h0_re, h0_im)


def _outproj_kernel(x_ref, att_ref, ssm_ref, wa_ref, ws_ref, gca_ref, wcq_ref, h_ref, qc_ref):
    h = (x_ref[...] + jnp.dot(att_ref[...], wa_ref[...], preferred_element_type=F32)
         + jnp.dot(ssm_ref[...], ws_ref[...], preferred_element_type=F32))
    h_ref[...] = h
    xn = _rms(h, gca_ref[...]).astype(BF16)
    qc_ref[...] = jnp.dot(xn, wcq_ref[...], preferred_element_type=F32).astype(BF16)


def _outproj(x, att, ssm, w_att, w_ssm, g_ca, w_cq, tm):
    t = x.shape[0]
    row = lambda w: pl.BlockSpec((tm, w), lambda i: (i, 0))
    return pl.pallas_call(
        _outproj_kernel,
        grid=(t // tm,),
        in_specs=[row(D_MODEL), row(ATTN_WIDTH), row(SSM_WIDTH), _full((ATTN_WIDTH, D_MODEL)),
                  _full((SSM_WIDTH, D_MODEL)), _full((1, D_MODEL)), _full((D_MODEL, CA_WIDTH))],
        out_specs=(row(D_MODEL), row(CA_WIDTH)),
        out_shape=(jax.ShapeDtypeStruct((t, D_MODEL), F32), jax.ShapeDtypeStruct((t, CA_WIDTH), BF16)),
        compiler_params=_cparams(("arbitrary",)),
        name="out_projection",
    )(x, att, ssm, w_att, w_ssm, g_ca, w_cq)


def _memkv_kernel(m_ref, g_ref, w_ref, k_ref, v_ref):
    m = _rms(m_ref[...], g_ref[...]).astype(BF16)
    kv = jnp.dot(m, w_ref[...], preferred_element_type=F32)
    k_ref[...] = kv[:, :CA_WIDTH]
    v_ref[...] = kv[:, CA_WIDTH:]


def _memkv(mem, g_mem, w_mkv_bf, tm):
    t = mem.shape[0]
    row = lambda w: pl.BlockSpec((tm, w), lambda i: (i, 0))
    return pl.pallas_call(
        _memkv_kernel,
        grid=(t // tm,),
        in_specs=[row(D_MODEL), _full((1, D_MODEL)), _full((D_MODEL, 2 * CA_WIDTH))],
        out_specs=(row(CA_WIDTH), row(CA_WIDTH)),
        out_shape=(jax.ShapeDtypeStruct((t, CA_WIDTH), F32), jax.ShapeDtypeStruct((t, CA_WIDTH), F32)),
        compiler_params=_cparams(("arbitrary",)),
        name="memory_kv",
    )(mem, g_mem, w_mkv_bf)


def _xattn_kernel(qc_ref, mk_ref, mv_ref, h_ref, wco_ref, gmoe_ref, wr_ref, br_ref, tri_ref, cin_ref, *refs,
                  nbat, rows, head_split, has_prev, tile_off):
    refs = refs[1:] if has_prev else refs
    h2_ref, xn_hbm, ti_ref, tw_ref, cout_ref, cnt, xn_buf, xn_sem = refs[:8]
    step = pl.program_id(0)
    nsteps = pl.num_programs(0)
    tm = nbat * rows

    def xn_copies(s):
        dst0 = tile_off + s * (tm // SUBLANES)
        return [pltpu.make_async_copy(xn_buf.at[:, :, sl, :], xn_hbm.at[pl.ds(dst0, tm // SUBLANES), sl], xn_sem)
                for sl in range(SUBLANES)]

    @pl.when(step == 0)
    def _():
        cnt[...] = cin_ref[...]

    if head_split:
        mbuf, msem = refs[8:]
        slot = step % 2

        def head_copies(s, sl):
            b0 = s * nbat
            return [pltpu.make_async_copy(src.at[pl.ds(b0, nbat), :, hd, :], mbuf.at[sl, kv, hd], msem.at[sl])
                    for kv, src in enumerate((mk_ref, mv_ref)) for hd in range(CA_HEADS)]

        @pl.when(step == 0)
        def _():
            for c in head_copies(0, 0):
                c.start()

        @pl.when(step + 1 < nsteps)
        def _():
            for c in head_copies(step + 1, 1 - slot):
                c.start()

        for c in head_copies(step, slot):
            c.wait()

    q_all = qc_ref[...]
    outs_b = []
    for bi in range(nbat):
        q = q_all[bi * rows:(bi + 1) * rows]
        outs = []
        for hd in range(CA_HEADS):
            sl = slice(hd * CA_HEAD_DIM, (hd + 1) * CA_HEAD_DIM)
            if head_split:
                mk = mbuf[slot, 0, hd, bi].astype(BF16)
                mv = mbuf[slot, 1, hd, bi].astype(BF16)
            else:
                mk = mk_ref[bi, :, sl].astype(BF16)
                mv = mv_ref[bi, :, sl].astype(BF16)
            s = lax.dot_general(q[:, sl], mk, (((1,), (1,)), ((), ())),
                                preferred_element_type=F32) * (CA_HEAD_DIM ** -0.5)
            m = jnp.max(s, axis=-1, keepdims=True)
            p = jnp.exp(s - m)
            denom = jnp.sum(p, axis=-1, keepdims=True)
            outs.append(jnp.dot(p.astype(BF16), mv, preferred_element_type=F32) / denom)
        outs_b.append(jnp.concatenate(outs, axis=-1))
    o = jnp.concatenate(outs_b, axis=0).astype(BF16)
    h2 = h_ref[...] + jnp.dot(o, wco_ref[...], preferred_element_type=F32)
    h2_ref[...] = h2
    xn = _rms(h2, gmoe_ref[...])
    @pl.when(step >= 1)
    def _():
        for c in xn_copies(step - 1):
            c.wait()

    _store_row_tiles(xn_buf, xn)
    for c in xn_copies(step):
        c.start()

    @pl.when(step == nsteps - 1)
    def _():
        for c in xn_copies(step):
            c.wait()

    logits = jnp.dot(xn.astype(BF16), wr_ref[...], preferred_element_type=F32) + br_ref[...]
    n = logits.shape[0]
    eidx = lax.broadcasted_iota(jnp.int32, (n, N_EXPERTS), 1).astype(F32)
    lane = lax.broadcasted_iota(jnp.int32, (n, LANES), 1)
    ti = jnp.zeros((n, LANES), F32)
    tv = jnp.zeros((n, LANES), F32)
    work = logits
    vals = []
    hits = []
    for k in range(TOP_K):
        mx = jnp.max(work, axis=-1, keepdims=True)
        idx = jnp.min(jnp.where(work == mx, eidx, float(N_EXPERTS)), axis=-1, keepdims=True)
        vals.append(mx)
        hits.append(eidx == idx)
        ti = jnp.where(lane == k, idx, ti)
        work = jnp.where(hits[k], -jnp.inf, work)
    ex = [jnp.exp(v - vals[0]) for v in vals]
    tot = ex[0] + ex[1] + ex[2] + ex[3]
    for k in range(TOP_K):
        tv = jnp.where(lane == k, ex[k] / tot, tv)
    sel = jnp.zeros((n, N_EXPERTS), F32)
    for k in range(TOP_K):
        sel = sel + jnp.where(hits[k], 1.0, 0.0)
    before = jnp.dot(tri_ref[...], sel.astype(BF16), preferred_element_type=F32) + cnt[...]
    for k in range(TOP_K):
        rank = jnp.sum(jnp.where(hits[k], before, 0.0), axis=-1, keepdims=True)
        ti = jnp.where(lane == TOP_K + k, rank, ti)
    cnt[...] = cnt[...] + jnp.sum(sel, axis=0, keepdims=True)
    ti_ref[...] = ti.astype(jnp.int32)
    tw_ref[...] = tv

    @pl.when(step == pl.num_programs(0) - 1)
    def _():
        cout_ref[...] = cnt[...]


def _xattn(qc, mk, mv, h1, w_co, g_moe, w_router, b_router, counts_in, nbat, rows, t_total, t_off, xn_prev=None):
    t = qc.shape[0]
    tm = nbat * rows
    row = lambda w: pl.BlockSpec((tm, w), lambda i: (i, 0))
    seq_rows = t // mk.shape[0]
    head_split = mk.ndim == 4
    scratch = [pltpu.VMEM((1, N_EXPERTS), F32), pltpu.VMEM((tm // SUBLANES, LANE_TILES, SUBLANES, LANES), F32),
               pltpu.SemaphoreType.DMA(())]
    any_spec = pl.BlockSpec(memory_space=pl.ANY)
    prev_args, prev_specs, aliases = [], [], {}
    if xn_prev is not None:
        prev_args, prev_specs, aliases = [xn_prev], [any_spec], {10: 1}
    if head_split:
        assert seq_rows == rows
        mem = pl.BlockSpec(memory_space=pl.ANY)
        scratch += [pltpu.VMEM((2, 2, CA_HEADS, nbat, N_MEM, CA_HEAD_DIM), F32), pltpu.SemaphoreType.DMA((2,))]
    else:
        mem = pl.BlockSpec((nbat, N_MEM, CA_WIDTH), lambda i: ((i * tm) // (seq_rows * nbat), 0, 0))
    tri = jnp.tri(tm, k=-1, dtype=BF16)
    return pl.pallas_call(
        functools.partial(_xattn_kernel, nbat=nbat, rows=rows, head_split=head_split, has_prev=xn_prev is not None,
                          tile_off=t_off // SUBLANES),
        grid=(t // tm,),
        in_specs=[row(CA_WIDTH), mem, mem, row(D_MODEL), _full((CA_WIDTH, D_MODEL)), _full((1, D_MODEL)),
                  _full((D_MODEL, N_EXPERTS)), _full((1, N_EXPERTS)), _full((tm, tm)), _full((1, N_EXPERTS))]
        + prev_specs,
        out_specs=(row(D_MODEL), any_spec, row(LANES), row(LANES), _full((1, N_EXPERTS))),
        out_shape=(jax.ShapeDtypeStruct((t, D_MODEL), F32),
                   jax.ShapeDtypeStruct((t_total // SUBLANES, SUBLANES, LANE_TILES, LANES), F32),
                   jax.ShapeDtypeStruct((t, LANES), jnp.int32), jax.ShapeDtypeStruct((t, LANES), F32),
                   jax.ShapeDtypeStruct((1, N_EXPERTS), F32)),
        scratch_shapes=scratch,
        input_output_aliases=aliases,
        compiler_params=_cparams(("arbitrary",), VMEM_LIMIT),
        name="cross_attn_router",
    )(qc, mk, mv, h1, w_co, g_moe, w_router, b_router, tri, counts_in, *prev_args)


def _expert_kernel(blk_e_ref, w_idx_ref, nused_ref, tab_hbm, xn_hbm, wg_ref, bg_ref, wu_ref, bu_ref, wd_ref, bd_ref,
                   yu_hbm, wg_bf, wu_bf, wd_bf, xbuf, ybuf, hb, tab, gsem, ssem, tsem, *, n_blocks):
    i = pl.program_id(0)
    n_used = nused_ref[0]
    active = i < n_used
    slot = i % 2
    row_w = 2 * MOE_BLOCK

    def ring(b):
        return (b + TAB_RING) % TAB_RING

    def tab_copy(b):
        r = ring(b)
        return pltpu.make_async_copy(tab_hbm.at[jnp.minimum(b, n_blocks - 1) + 1], tab.at[pl.ds(r * row_w, row_w)],
                                     tsem.at[r])

    def block_wait(ref, sem):
        pltpu.make_async_copy(ref.at[pl.ds(0, MOE_BLOCK)], ref.at[pl.ds(0, MOE_BLOCK)], sem).wait()

    def issue_gathers(b, sl):
        off = ring(b) * row_w
        for lo in range(0, MOE_BLOCK, DMA_BATCH):
            ids = [tab[off + lo + q] for q in range(DMA_BATCH)]
            for q in range(DMA_BATCH):
                j, s = divmod(lo + q, SUBLANES)
                pltpu.make_async_copy(xn_hbm.at[ids[q]], xbuf.at[sl, j, :, s, :], gsem.at[sl]).start(priority=q % 2)

    def issue_scatters(b, sl):
        off = ring(b) * row_w + MOE_BLOCK
        for lo in range(0, MOE_BLOCK, DMA_BATCH):
            ids = [tab[off + lo + q] for q in range(DMA_BATCH)]
            for q in range(DMA_BATCH):
                j, s = divmod(lo + q, SUBLANES)
                pltpu.make_async_copy(ybuf.at[sl, j, :, s, :], yu_hbm.at[ids[q]], ssem).start(priority=q % 2)

    @pl.when((i == 0) & active)
    def _():
        ybuf[...] = jnp.zeros(ybuf.shape, F32)
        for b in (-1, 0, 1):
            tab_copy(b).start()
            tab_copy(b).wait()
        tab_copy(2).start()
        issue_gathers(0, 0)

    prev_e = blk_e_ref[jnp.maximum(i - 1, 0)]
    new_expert = (i == 0) | (blk_e_ref[i] != prev_e)

    @pl.when(active & new_expert)
    def _():
        wg_bf[...] = wg_ref[0].astype(BF16)
        wu_bf[...] = wu_ref[0].astype(BF16)
        wd_bf[...] = wd_ref[0].astype(BF16)

    @pl.when(active)
    def _():
        block_wait(xn_hbm, gsem.at[slot])

        @pl.when(i >= 1)
        def _():
            tab_copy(i + 1).wait()
            tab_copy(i + 2).start()

        issue_scatters(i - 1, 1 - slot)
        x = _load_row_tiles(xbuf.at[slot]).astype(BF16)
        gate = jnp.minimum(jnp.dot(x, wg_bf[...], preferred_element_type=F32) + bg_ref[0], SWIGLU_LIMIT)
        up = jnp.clip(jnp.dot(x, wu_bf[...], preferred_element_type=F32) + bu_ref[0], -SWIGLU_LIMIT, SWIGLU_LIMIT)
        hb[...] = (gate * jax.nn.sigmoid(SWIGLU_ALPHA * gate) * (up + 1.0)).astype(BF16)
        block_wait(yu_hbm, ssem)
        issue_gathers(i + 1, 1 - slot)
        y = jnp.dot(hb[...], wd_bf[...], preferred_element_type=F32) + bd_ref[0]
        _store_row_tiles(ybuf.at[slot], y)

        @pl.when(i == n_used - 1)
        def _():
            issue_scatters(i, slot)
            block_wait(yu_hbm, ssem)
            block_wait(xn_hbm, gsem.at[1 - slot])
            tab_copy(i + 2).wait()


def _experts(blk_e, w_idx, n_used, tab, xn, n_out_tiles, w_gate, b_gate, w_up, b_up, w_down, b_down):
    n_blocks = tab.shape[0] - 1
    d_ff = w_gate.shape[-1]
    tiles = MOE_BLOCK // SUBLANES
    wspec = lambda a, b: pl.BlockSpec((1, a, b), lambda i, be, wi, nu: (wi[i], 0, 0))
    bspec = lambda b: pl.BlockSpec((1, 1, b), lambda i, be, wi, nu: (be[i], 0, 0))
    any_spec = pl.BlockSpec(memory_space=pl.ANY)
    grid_spec = pltpu.PrefetchScalarGridSpec(
        num_scalar_prefetch=3,
        grid=(n_blocks,),
        in_specs=[any_spec, any_spec, wspec(D_MODEL, d_ff), bspec(d_ff), wspec(D_MODEL, d_ff), bspec(d_ff),
                  wspec(d_ff, D_MODEL), bspec(D_MODEL)],
        out_specs=any_spec,
        scratch_shapes=[pltpu.VMEM((D_MODEL, d_ff), BF16), pltpu.VMEM((D_MODEL, d_ff), BF16),
                        pltpu.VMEM((d_ff, D_MODEL), BF16),
                        pltpu.VMEM((2, tiles, LANE_TILES, SUBLANES, LANES), F32),
                        pltpu.VMEM((2, tiles, LANE_TILES, SUBLANES, LANES), F32),
                        pltpu.VMEM((MOE_BLOCK, d_ff), BF16),
                        pltpu.SMEM((TAB_RING * 2 * MOE_BLOCK,), jnp.int32),
                        pltpu.SemaphoreType.DMA((2,)), pltpu.SemaphoreType.DMA(()),
                        pltpu.SemaphoreType.DMA((TAB_RING,))],
    )
    return pl.pallas_call(
        functools.partial(_expert_kernel, n_blocks=n_blocks),
        grid_spec=grid_spec,
        out_shape=jax.ShapeDtypeStruct((n_out_tiles, LANE_TILES, LANES), F32),
        compiler_params=_cparams(("arbitrary",), VMEM_LIMIT),
        name="moe_experts",
    )(blk_e, w_idx, n_used, tab, xn, w_gate, b_gate.reshape(N_EXPERTS, 1, d_ff), w_up,
      b_up.reshape(N_EXPERTS, 1, d_ff), w_down, b_down.reshape(N_EXPERTS, 1, D_MODEL))


def _combine_kernel(yu_hbm, h_ref, tw_ref, g_ref, o_ref, buf, sem, *, tm, tile_off):
    i = pl.program_id(0)
    n = pl.num_programs(0)
    slot = i % 2
    tiles = tm // SUBLANES

    def copies(step, sl):
        t0 = tile_off + step * tiles
        return [pltpu.make_async_copy(yu_hbm.at[pl.ds(t0, tiles), s, k], buf.at[sl, k, :, :, s, :], sem.at[sl])
                for s in range(SUBLANES) for k in range(TOP_K)]

    @pl.when(i == 0)
    def _():
        for c in copies(0, 0):
            c.start()

    @pl.when(i + 1 < n)
    def _():
        for c in copies(i + 1, 1 - slot):
            c.start()

    for c in copies(i, slot):
        c.wait()
    tw = tw_ref[...]
    y = h_ref[...]
    for k in range(TOP_K):
        y = y + tw[:, k:k + 1] * _load_row_tiles(buf.at[slot, k])
    o_ref[...] = _rms(y, g_ref[...])


def _combine(yu, h2, tw, g_final, tm, t_off):
    t = h2.shape[0]
    row = lambda w: pl.BlockSpec((tm, w), lambda i: (i, 0))
    return pl.pallas_call(
        functools.partial(_combine_kernel, tm=tm, tile_off=t_off // SUBLANES),
        grid=(t // tm,),
        in_specs=[pl.BlockSpec(memory_space=pl.ANY), row(D_MODEL), row(LANES), _full((1, D_MODEL))],
        out_specs=row(D_MODEL),
        out_shape=jax.ShapeDtypeStruct((t, D_MODEL), F32),
        scratch_shapes=[pltpu.VMEM((2, TOP_K, tm // SUBLANES, LANE_TILES, SUBLANES, LANES), F32),
                        pltpu.SemaphoreType.DMA((2,))],
        compiler_params=_cparams(("arbitrary",), VMEM_LIMIT),
        name="moe_combine",
    )(yu, h2, tw, g_final)


def _route(counts, m):
    counts = counts.reshape(N_EXPERTS).astype(jnp.int32)
    padded = (counts + MOE_BLOCK - 1) // MOE_BLOCK * MOE_BLOCK
    pad_end = jnp.cumsum(padded)
    pad_start = pad_end - padded
    n_blocks = -(-(m + N_EXPERTS * (MOE_BLOCK - 1)) // MOE_BLOCK)
    blk_start = jnp.arange(n_blocks, dtype=jnp.int32) * MOE_BLOCK
    owner = lambda start: jnp.minimum(jnp.sum((pad_end[None, :] <= start[:, None]).astype(jnp.int32), axis=1),
                                      N_EXPERTS - 1).astype(jnp.int32)
    n_used = pad_end[-1] // MOE_BLOCK
    last_start = jnp.maximum(n_used - 1, 0) * MOE_BLOCK
    blk_e = owner(jnp.minimum(blk_start, last_start))
    first = jnp.concatenate([jnp.ones((1,), bool), blk_e[1:] != blk_e[:-1]])
    region_end = jnp.sum(jnp.where(blk_e[:, None] == jnp.arange(N_EXPERTS)[None, :], pad_end[None, :], 0), axis=1)
    w_idx = jnp.where(first, blk_e, owner(jnp.minimum(region_end, last_start))).astype(jnp.int32)
    return pad_start, blk_e, w_idx, n_used.astype(jnp.int32).reshape(1), n_blocks * MOE_BLOCK


def _dest(ti, pad_start):
    e = ti[:, :TOP_K]
    onehot = e[:, :, None] == jnp.arange(N_EXPERTS, dtype=jnp.int32)[None, None, :]
    start = jnp.sum(jnp.where(onehot, pad_start[None, None, :], 0), axis=-1)
    return (start + ti[:, TOP_K:2 * TOP_K]).reshape(-1)


def _slot_table(dest, n_slots):
    m = dest.shape[0]
    n_blocks = n_slots // MOE_BLOCK
    filled_by = jnp.zeros((n_slots,), jnp.int32).at[dest].add(jnp.arange(m, dtype=jnp.int32) + 1) - 1
    filled = filled_by >= 0
    src_tok = jnp.where(filled, filled_by, 0) // TOP_K
    dst_tile = jnp.where(filled, filled_by, m + jnp.arange(n_slots, dtype=jnp.int32) % DUMP_TILES)
    rows = jnp.concatenate([src_tok.reshape(n_blocks, MOE_BLOCK), dst_tile.reshape(n_blocks, MOE_BLOCK)], axis=1)
    before_first = jnp.concatenate([jnp.zeros((MOE_BLOCK,), jnp.int32),
                                    m + MOE_BLOCK + jnp.arange(MOE_BLOCK, dtype=jnp.int32)])[None]
    return jnp.concatenate([before_first, rows], axis=0).astype(jnp.int32)


def _block_diag(blocks):
    g, a, b = blocks.shape
    eye = jnp.eye(g, dtype=blocks.dtype)
    return (blocks[:, :, None, :] * eye[:, None, :, None]).reshape(g * a, g * b)


def kernel(x_prompt, x_sample, mem_prompt, cache_win_k, cache_win_v, state_ssm_re, state_ssm_im, cache_mem_k,
           cache_mem_v, g_mix, w_in, sink, lam_re, lam_im, log_dt, b_re, b_im, c_re, c_im, d_skip, w_glu,
           g_attn_out, g_ssm_out, w_out, g_ca, g_mem, w_mk, w_mv, w_cq, w_co, g_moe, w_router, b_router,
           w_gate, b_gate, w_up, b_up, w_down, b_down, g_final):
    depth = g_mix.shape[0]
    assert depth == 1
    bp, lp, _ = x_prompt.shape
    bs, ls, _ = x_sample.shape
    tp, ts = bp * lp, bs * ls

    w_in_bf = w_in[0].astype(BF16)
    w_att_bf = w_out[0, :ATTN_WIDTH].astype(BF16)
    w_ssm_bf = w_out[0, ATTN_WIDTH:].astype(BF16)
    w_cq_bf = w_cq[0].astype(BF16)
    w_co_bf = w_co[0].astype(BF16)
    w_mkv_bf = jnp.concatenate([w_mk[0], w_mv[0]], axis=1).astype(BF16)
    w_glu_bf = w_glu[0].astype(BF16)
    row = lambda a: a.reshape(1, -1)

    a_re, a_im, bb_re, bb_im = _discretize(lam_re[0], lam_im[0], log_dt[0], b_re[0].transpose(0, 2, 1),
                                           b_im[0].transpose(0, 2, 1))
    hg = SSM_GROUPS // 2
    wb = jnp.stack([jnp.concatenate([_block_diag(bb_re[h * hg:(h + 1) * hg]),
                                     _block_diag(bb_im[h * hg:(h + 1) * hg])], axis=1) for h in range(2)]).astype(BF16)
    cre_t = c_re[0].transpose(0, 2, 1)
    cim_t = c_im[0].transpose(0, 2, 1)
    wc = jnp.stack([jnp.concatenate([_block_diag(cre_t[h * hg:(h + 1) * hg]),
                                     -_block_diag(cim_t[h * hg:(h + 1) * hg])], axis=0) for h in range(2)]).astype(BF16)
    ssm_w = (wb, wc, a_re.reshape(1, SSM_LANES), a_im.reshape(1, SSM_LANES), row(d_skip[0]), w_glu_bf,
             row(g_ssm_out[0]))

    xp = x_prompt.reshape(tp, D_MODEL)
    xs_ = x_sample.reshape(ts, D_MODEL)

    qp, kp, vp, up = _inproj(xp, row(g_mix[0]), w_in_bf, 512)
    qs, ks, vs, us = _inproj(xs_, row(g_mix[0]), w_in_bf, 256)

    att_p = _attn_prompt(sink[0], qp, kp, vp, row(g_attn_out[0]), bp, lp)
    ck = cache_win_k[0].reshape(bs, WINDOW, KV_WIDTH)
    cv = cache_win_v[0].reshape(bs, WINDOW, KV_WIDTH)
    att_s = _attn_sample(sink[0], qs, ks, vs, ck, cv, row(g_attn_out[0]), bs, ls, 16)

    zero_state = jnp.zeros((bp, SSM_LANES), F32)
    ssm_p, p_sre, p_sim = _ssm(up.reshape(bp, lp, SSM_WIDTH), zero_state, zero_state, ssm_w, 32)
    ssm_s, s_sre, s_sim = _ssm(us.reshape(bs, ls, SSM_WIDTH), state_ssm_re[0].reshape(bs, SSM_LANES),
                               state_ssm_im[0].reshape(bs, SSM_LANES), ssm_w, ls)

    h1p, qcp = _outproj(xp, att_p, ssm_p.reshape(tp, SSM_WIDTH), w_att_bf, w_ssm_bf, row(g_ca[0]), w_cq_bf, 512)
    h1s, qcs = _outproj(xs_, att_s, ssm_s.reshape(ts, SSM_WIDTH), w_att_bf, w_ssm_bf, row(g_ca[0]), w_cq_bf, 256)

    mk_p, mv_p = _memkv(mem_prompt.reshape(bp * N_MEM, D_MODEL), row(g_mem[0]), w_mkv_bf, 512)

    no_counts = jnp.zeros((1, N_EXPERTS), F32)
    w_router_bf = w_router[0].astype(BF16)
    tt = tp + ts
    h2p, xn, tip, twp, cnt_p = _xattn(qcp, mk_p.reshape(bp, N_MEM, CA_WIDTH), mv_p.reshape(bp, N_MEM, CA_WIDTH),
                                      h1p, w_co_bf, row(g_moe[0]), w_router_bf, row(b_router[0]), no_counts,
                                      1, 512, tt, 0)
    h2s, xn, tis, tws, cnt = _xattn(qcs, cache_mem_k[0], cache_mem_v[0], h1s, w_co_bf, row(g_moe[0]),
                                    w_router_bf, row(b_router[0]), cnt_p, 8, ls, tt, tp, xn_prev=xn)

    m = tt * TOP_K
    pad_start, blk_e, w_idx, n_used, n_slots = _route(cnt, m)
    tab = _slot_table(jnp.concatenate([_dest(tip, pad_start), _dest(tis, pad_start)]), n_slots)
    yu = _experts(blk_e, w_idx, n_used, tab, xn.reshape(tt, LANE_TILES, LANES), m + DUMP_TILES,
                  w_gate[0], b_gate[0], w_up[0], b_up[0], w_down[0], b_down[0])
    yu = yu.reshape((m + DUMP_TILES) // (SUBLANES * TOP_K), SUBLANES, TOP_K, LANE_TILES, LANES)
    y_prompt = _combine(yu, h2p, twp, row(g_final), 256, 0).reshape(bp, lp, D_MODEL)
    y_sample = _combine(yu, h2s, tws, row(g_final), 256, tp).reshape(bs, ls, D_MODEL)

    kp4 = kp.reshape(bp, lp, N_KV_HEADS, HEAD_DIM)
    vp4 = vp.reshape(bp, lp, N_KV_HEADS, HEAD_DIM)
    p_win_k = kp4[:, -WINDOW:][None]
    p_win_v = vp4[:, -WINDOW:][None]
    s_win_k = jnp.concatenate([cache_win_k[0], ks.reshape(bs, ls, N_KV_HEADS, HEAD_DIM)], axis=1)[:, -WINDOW:][None]
    s_win_v = jnp.concatenate([cache_win_v[0], vs.reshape(bs, ls, N_KV_HEADS, HEAD_DIM)], axis=1)[:, -WINDOW:][None]
    st = lambda a, b: a.reshape(1, b, SSM_GROUPS, SSM_STATE)
    p_mem_k = mk_p.reshape(1, bp, N_MEM, CA_HEADS, CA_HEAD_DIM)
    p_mem_v = mv_p.reshape(1, bp, N_MEM, CA_HEADS, CA_HEAD_DIM)
    return (y_prompt, y_sample, p_win_k, p_win_v, st(p_sre, bp), st(p_sim, bp), p_mem_k, p_mem_v,
            s_win_k, s_win_v, st(s_sre, bs), st(s_sim, bs))
```

```python
import functools
import math

import jax
import jax.numpy as jnp
from jax import lax
from jax.experimental import pallas as pl
from jax.experimental.pallas import tpu as pltpu

F32 = jnp.float32
BF16 = jnp.bfloat16

D_MODEL = 1024
HEAD_DIM = 64
N_Q_HEADS = 8
N_KV_HEADS = 2
Q_PER_KV = 4
ATTN_WIDTH = 512
KV_WIDTH = 128
WINDOW = 128
SSM_WIDTH = 512
SSM_GROUP = 16
SSM_GROUPS = 32
SSM_STATE = 64
SSM_LANES = SSM_GROUPS * SSM_STATE
SSM_HALF = SSM_LANES // 2
IN_WIDTH = ATTN_WIDTH + 2 * KV_WIDTH + SSM_WIDTH
N_MEM = 256
CA_HEADS = 4
CA_HEAD_DIM = 128
CA_WIDTH = 512
N_EXPERTS = 32
TOP_K = 4
SWIGLU_ALPHA = 1.702
SWIGLU_LIMIT = 7.0
RMS_EPS = 1e-5

SUBLANES = 8
LANES = 128
MOE_BLOCK = 256
DMA_BATCH = 16
TAB_RING = 4
DUMP_TILES = 2 * MOE_BLOCK
VMEM_LIMIT = 56 * 1024 * 1024


def _cparams(sem, vmem=None):
    return pltpu.CompilerParams(dimension_semantics=sem, vmem_limit_bytes=vmem)


def _rms(x, g):
    return x * lax.rsqrt(jnp.mean(x * x, axis=-1, keepdims=True) + RMS_EPS) * g


def _full(shape):
    return pl.BlockSpec(shape, lambda *_: (0,) * len(shape))


LANE_TILES = D_MODEL // LANES


def _store_row_tiles(ref, x):
    for c in range(LANE_TILES):
        ref[:, c] = x[:, c * LANES:(c + 1) * LANES].reshape(x.shape[0] // SUBLANES, SUBLANES, LANES)


def _load_row_tiles(ref):
    rows = ref.shape[0] * SUBLANES
    return jnp.concatenate([ref[:, c].reshape(rows, LANES) for c in range(LANE_TILES)], axis=1)


def _disc_kernel(lr_ref, li_ref, ldt_ref, bre_ref, bim_ref, are_ref, aim_ref, bbre_ref, bbim_ref):
    lr = lr_ref[...]
    li = li_ref[...]
    dt = jnp.exp(ldt_ref[...])
    mag = jnp.exp(lr * dt)
    ang = li * dt
    ab_re = mag * jnp.cos(ang)
    ab_im = mag * jnp.sin(ang)
    nr = ab_re - 1.0
    ni = ab_im
    den = lr * lr + li * li
    f_re = (nr * lr + ni * li) / den
    f_im = (ni * lr - nr * li) / den
    are_ref[...] = ab_re
    aim_ref[...] = ab_im
    br = bre_ref[...]
    bi = bim_ref[...]
    bbre_ref[...] = f_re[:, None, :] * br - f_im[:, None, :] * bi
    bbim_ref[...] = f_re[:, None, :] * bi + f_im[:, None, :] * br


def _discretize(lam_re, lam_im, log_dt, b_re_t, b_im_t):
    g, n = lam_re.shape
    c = b_re_t.shape[1]
    return pl.pallas_call(
        _disc_kernel,
        out_shape=(jax.ShapeDtypeStruct((g, n), F32), jax.ShapeDtypeStruct((g, n), F32),
                   jax.ShapeDtypeStruct((g, c, n), F32), jax.ShapeDtypeStruct((g, c, n), F32)),
        name="ssm_discretize",
    )(lam_re, lam_im, log_dt.reshape(g, 1), b_re_t, b_im_t)


def _inproj_kernel(x_ref, g_ref, w_ref, q_ref, k_ref, v_ref, u_ref):
    a = _rms(x_ref[...], g_ref[...]).astype(BF16)
    proj = jnp.dot(a, w_ref[...], preferred_element_type=F32)
    q_ref[...] = proj[:, :ATTN_WIDTH].astype(BF16)
    k_ref[...] = proj[:, ATTN_WIDTH:ATTN_WIDTH + KV_WIDTH]
    v_ref[...] = proj[:, ATTN_WIDTH + KV_WIDTH:ATTN_WIDTH + 2 * KV_WIDTH]
    u_ref[...] = proj[:, ATTN_WIDTH + 2 * KV_WIDTH:]


def _inproj(x, g_mix, w_in_bf, tm):
    t = x.shape[0]
    row = lambda w: pl.BlockSpec((tm, w), lambda i: (i, 0))
    return pl.pallas_call(
        _inproj_kernel,
        grid=(t // tm,),
        in_specs=[row(D_MODEL), _full((1, D_MODEL)), _full((D_MODEL, IN_WIDTH))],
        out_specs=(row(ATTN_WIDTH), row(KV_WIDTH), row(KV_WIDTH), row(SSM_WIDTH)),
        out_shape=(jax.ShapeDtypeStruct((t, ATTN_WIDTH), BF16), jax.ShapeDtypeStruct((t, KV_WIDTH), F32),
                   jax.ShapeDtypeStruct((t, KV_WIDTH), F32), jax.ShapeDtypeStruct((t, SSM_WIDTH), F32)),
        compiler_params=_cparams(("arbitrary",)),
        name="in_projection",
    )(x, g_mix, w_in_bf)


def _sink_softmax_pv(s, sink, v_bf):
    m = jnp.maximum(jnp.max(s, axis=-1, keepdims=True), sink)
    p = jnp.exp(s - m)
    denom = jnp.sum(p, axis=-1, keepdims=True) + jnp.exp(sink - m)
    return jnp.dot(p.astype(BF16), v_bf, preferred_element_type=F32) / denom


def _attn_prompt_kernel(sink_ref, q_ref, kc_ref, kp_ref, vc_ref, vp_ref, g_ref, o_ref):
    i = pl.program_id(1)
    q = q_ref[...]
    kk = jnp.concatenate([kp_ref[...], kc_ref[...]], axis=0).astype(BF16)
    vv = jnp.concatenate([vp_ref[...], vc_ref[...]], axis=0).astype(BF16)
    rows = Q_PER_KV * WINDOW
    qi = lax.broadcasted_iota(jnp.int32, (rows, 2 * WINDOW), 0) & (WINDOW - 1)
    kj = lax.broadcasted_iota(jnp.int32, (rows, 2 * WINDOW), 1)
    rel = qi + WINDOW - kj
    mask = (rel >= 0) & (rel < WINDOW) & ((kj >= WINDOW) | (i > 0))
    head_of_row = lax.broadcasted_iota(jnp.int32, (rows, 1), 0) // WINDOW
    low_q = lax.broadcasted_iota(jnp.int32, (WINDOW, LANES), 1) < HEAD_DIM
    zero_q = jnp.zeros((WINDOW, LANES), BF16)
    pairs = []
    for j in range(N_KV_HEADS):
        kh = kk[:, j * HEAD_DIM:(j + 1) * HEAD_DIM]
        vh = vv[:, j * HEAD_DIM:(j + 1) * HEAD_DIM]
        k2 = jnp.concatenate([kh, kh], axis=1)
        v2 = jnp.concatenate([vh, vh], axis=1)
        stacked = []
        for p in range(Q_PER_KV // 2):
            qp = q[:, (j * Q_PER_KV + 2 * p) * HEAD_DIM:(j * Q_PER_KV + 2 * p + 2) * HEAD_DIM]
            stacked += [jnp.where(low_q, qp, zero_q), jnp.where(low_q, zero_q, qp)]
        qs = jnp.concatenate(stacked, axis=0)
        sink = jnp.zeros((rows, 1), F32)
        for g in range(Q_PER_KV):
            sink = jnp.where(head_of_row == g, sink_ref[j * Q_PER_KV + g], sink)
        s = lax.dot_general(qs, k2, (((1,), (1,)), ((), ())), preferred_element_type=F32) * (HEAD_DIM ** -0.5)
        s = jnp.where(mask, s, -jnp.inf)
        o = _sink_softmax_pv(s, sink, v2)
        for p in range(Q_PER_KV // 2):
            pairs.append(jnp.where(low_q, o[2 * p * WINDOW:(2 * p + 1) * WINDOW],
                                   o[(2 * p + 1) * WINDOW:(2 * p + 2) * WINDOW]))
    att = jnp.concatenate(pairs, axis=-1)
    o_ref[...] = _rms(att, g_ref[...]).astype(BF16)


def _attn_prompt(sink, q, k, v, g_attn_out, b, l):
    nb = l // WINDOW
    cur = lambda w: pl.BlockSpec((WINDOW, w), lambda bi, i: (bi * nb + i, 0))
    prev = lambda w: pl.BlockSpec((WINDOW, w), lambda bi, i: (bi * nb + jnp.maximum(i - 1, 0), 0))
    return pl.pallas_call(
        _attn_prompt_kernel,
        grid=(b, nb),
        in_specs=[pl.BlockSpec(memory_space=pltpu.SMEM), cur(ATTN_WIDTH), cur(KV_WIDTH), prev(KV_WIDTH),
                  cur(KV_WIDTH), prev(KV_WIDTH), _full((1, ATTN_WIDTH))],
        out_specs=cur(ATTN_WIDTH),
        out_shape=jax.ShapeDtypeStruct((b * l, ATTN_WIDTH), BF16),
        compiler_params=_cparams(("arbitrary", "arbitrary")),
        name="window_attn_prompt",
    )(sink, q, k, k, v, v, g_attn_out)


def _attn_sample_kernel(sink_ref, q_ref, kn_ref, vn_ref, ck_ref, cv_ref, g_ref, o_ref, *, bt, t):
    nk = WINDOW + t
    rows = Q_PER_KV * t
    ri = lax.broadcasted_iota(jnp.int32, (rows, nk), 0)
    kj = lax.broadcasted_iota(jnp.int32, (rows, nk), 1)
    tq = ri % t
    mask = ((kj < WINDOW) & (kj > tq)) | ((kj >= WINDOW) & ((kj - WINDOW) <= tq))
    rg = lax.broadcasted_iota(jnp.int32, (rows, 1), 0) // t
    q_all = q_ref[...]
    kn_all = kn_ref[...]
    vn_all = vn_ref[...]
    outs_b = []
    for bi in range(bt):
        q = q_all[bi * t:(bi + 1) * t]
        kk = jnp.concatenate([ck_ref[bi], kn_all[bi * t:(bi + 1) * t]], axis=0).astype(BF16)
        vv = jnp.concatenate([cv_ref[bi], vn_all[bi * t:(bi + 1) * t]], axis=0).astype(BF16)
        outs = []
        for j in range(N_KV_HEADS):
            qj = jnp.concatenate(
                [q[:, (j * Q_PER_KV + g) * HEAD_DIM:(j * Q_PER_KV + g + 1) * HEAD_DIM] for g in range(Q_PER_KV)],
                axis=0)
            sink = jnp.zeros((rows, 1), F32)
            for g in range(Q_PER_KV):
                sink = jnp.where(rg == g, sink_ref[j * Q_PER_KV + g], sink)
            kh = kk[:, j * HEAD_DIM:(j + 1) * HEAD_DIM]
            vh = vv[:, j * HEAD_DIM:(j + 1) * HEAD_DIM]
            s = lax.dot_general(qj, kh, (((1,), (1,)), ((), ())), preferred_element_type=F32) * (HEAD_DIM ** -0.5)
            s = jnp.where(mask, s, -jnp.inf)
            o = _sink_softmax_pv(s, sink, vh)
            outs.extend(o[g * t:(g + 1) * t] for g in range(Q_PER_KV))
        outs_b.append(jnp.concatenate(outs, axis=-1))
    att = jnp.concatenate(outs_b, axis=0)
    o_ref[...] = _rms(att, g_ref[...]).astype(BF16)


def _attn_sample(sink, q, k, v, cache_k, cache_v, g_attn_out, b, t, bt):
    row = lambda w: pl.BlockSpec((bt * t, w), lambda i: (i, 0))
    cache = pl.BlockSpec((bt, WINDOW, KV_WIDTH), lambda i: (i, 0, 0))
    return pl.pallas_call(
        functools.partial(_attn_sample_kernel, bt=bt, t=t),
        grid=(b // bt,),
        in_specs=[pl.BlockSpec(memory_space=pltpu.SMEM), row(ATTN_WIDTH), row(KV_WIDTH), row(KV_WIDTH),
                  cache, cache, _full((1, ATTN_WIDTH))],
        out_specs=row(ATTN_WIDTH),
        out_shape=jax.ShapeDtypeStruct((b * t, ATTN_WIDTH), BF16),
        compiler_params=_cparams(("arbitrary",)),
        name="window_attn_sample",
    )(sink, q, k, v, cache_k, cache_v, g_attn_out)


def _ssm_kernel(u_ref, perm_ref, permt_ref, wb_ref, wc_ref, are_ref, aim_ref, d_ref, wglu_ref, g_ref,
                h0re_ref, h0im_ref, o_ref, hre_ref, him_ref, vre, vim, sre, sim, *, nb, s):
    c = pl.program_id(0)
    r = nb * s

    @pl.when(c == 0)
    def _():
        sre[...] = h0re_ref[...]
        sim[...] = h0im_ref[...]

    u = u_ref[...].reshape(r, SSM_WIDTH)
    u_hi = u.astype(BF16)
    u_lo = (u - u_hi.astype(F32)).astype(BF16)
    perm = perm_ref[...]
    u_tb = (jnp.dot(perm, u_hi, preferred_element_type=F32) + jnp.dot(perm, u_lo, preferred_element_type=F32))
    u_bf = u_tb.astype(BF16)
    half_w = SSM_WIDTH // 2
    for h in range(2):
        vv = jnp.dot(u_bf[:, h * half_w:(h + 1) * half_w], wb_ref[h], preferred_element_type=F32)
        vre[:, h * SSM_HALF:(h + 1) * SSM_HALF] = vv[:, :SSM_HALF]
        vim[:, h * SSM_HALF:(h + 1) * SSM_HALF] = vv[:, SSM_HALF:]

    lane_chunk = SSM_LANES // 2
    for rg in range(nb // SUBLANES):
        for lc in range(SSM_LANES // lane_chunk):
            lanes = slice(lc * lane_chunk, (lc + 1) * lane_chunk)
            rows0 = slice(rg * SUBLANES, (rg + 1) * SUBLANES)
            ar = jnp.broadcast_to(are_ref[:, lanes], (SUBLANES, lane_chunk))
            ai = jnp.broadcast_to(aim_ref[:, lanes], (SUBLANES, lane_chunk))

            def step(t, carry, lanes=lanes, rg=rg, ar=ar, ai=ai):
                hr, hi = carry
                row = pl.multiple_of(t * nb + rg * SUBLANES, SUBLANES)
                nhr = ar * hr - ai * hi + vre[pl.ds(row, SUBLANES), lanes]
                nhi = ar * hi + ai * hr + vim[pl.ds(row, SUBLANES), lanes]
                vre[pl.ds(row, SUBLANES), lanes] = nhr
                vim[pl.ds(row, SUBLANES), lanes] = nhi
                return nhr, nhi

            hr, hi = lax.fori_loop(0, s, step, (sre[rows0, lanes], sim[rows0, lanes]), unroll=True)
            sre[rows0, lanes] = hr
            sim[rows0, lanes] = hi

    ys = []
    for h in range(2):
        hcat = jnp.concatenate([vre[:, h * SSM_HALF:(h + 1) * SSM_HALF], vim[:, h * SSM_HALF:(h + 1) * SSM_HALF]],
                               axis=1).astype(BF16)
        ys.append(jnp.dot(hcat, wc_ref[h], preferred_element_type=F32))
    y = jnp.concatenate(ys, axis=1) + d_ref[...] * u_tb
    g = jax.nn.gelu(y)
    z = jnp.dot(g.astype(BF16), wglu_ref[...], preferred_element_type=F32)
    out = g * jax.nn.sigmoid(z)
    on = _rms(out, g_ref[...]).astype(BF16)
    o_bt = jnp.dot(permt_ref[...], on, preferred_element_type=F32).astype(BF16)
    o_ref[...] = o_bt.reshape(nb, s, SSM_WIDTH)

    @pl.when(c == pl.num_programs(0) - 1)
    def _():
        hre_ref[...] = sre[...]
        him_ref[...] = sim[...]


def _ssm(u, h0_re, h0_im, ssm_w, s):
    nb, l, _ = u.shape
    r = nb * s
    ridx = jnp.arange(r)
    src = (ridx % nb) * s + ridx // nb
    perm = (src[:, None] == ridx[None, :]).astype(BF16)
    wb, wc, a_re, a_im, d_skip, w_glu, g_out = ssm_w
    seq_blk = pl.BlockSpec((nb, s, SSM_WIDTH), lambda c: (0, c, 0))
    st_blk = _full((nb, SSM_LANES))
    return pl.pallas_call(
        functools.partial(_ssm_kernel, nb=nb, s=s),
        grid=(l // s,),
        in_specs=[seq_blk, _full((r, r)), _full((r, r)), _full(wb.shape), _full(wc.shape),
                  _full((1, SSM_LANES)), _full((1, SSM_LANES)), _full((1, SSM_WIDTH)),
                  _full((SSM_WIDTH, SSM_WIDTH)), _full((1, SSM_WIDTH)), st_blk, st_blk],
        out_specs=(seq_blk, st_blk, st_blk),
        out_shape=(jax.ShapeDtypeStruct((nb, l, SSM_WIDTH), BF16),
                   jax.ShapeDtypeStruct((nb, SSM_LANES), F32), jax.ShapeDtypeStruct((nb, SSM_LANES), F32)),
        scratch_shapes=[pltpu.VMEM((r, SSM_LANES), F32), pltpu.VMEM((r, SSM_LANES), F32),
                        pltpu.VMEM((nb, SSM_LANES), F32), pltpu.VMEM((nb, SSM_LANES), F32)],
        compiler_params=_cparams(("arbitrary",), VMEM_LIMIT),
        name="ssm_scan",
    )(u, perm, perm.T, wb, wc, a_re, a_im, d_skip, w_glu, g_out, h0_re, h0_im)


def _outproj_kernel(x_ref, att_ref, ssm_ref, wa_ref, ws_ref, gca_ref, wcq_ref, h_ref, qc_ref):
    h = (x_ref[...] + jnp.dot(att_ref[...], wa_ref[...], preferred_element_type=F32)
         + jnp.dot(ssm_ref[...], ws_ref[...], preferred_element_type=F32))
    h_ref[...] = h
    xn = _rms(h, gca_ref[...]).astype(BF16)
    qc_ref[...] = jnp.dot(xn, wcq_ref[...], preferred_element_type=F32).astype(BF16)


def _outproj(x, att, ssm, w_att, w_ssm, g_ca, w_cq, tm):
    t = x.shape[0]
    row = lambda w: pl.BlockSpec((tm, w), lambda i: (i, 0))
    return pl.pallas_call(
        _outproj_kernel,
        grid=(t // tm,),
        in_specs=[row(D_MODEL), row(ATTN_WIDTH), row(SSM_WIDTH), _full((ATTN_WIDTH, D_MODEL)),
                  _full((SSM_WIDTH, D_MODEL)), _full((1, D_MODEL)), _full((D_MODEL, CA_WIDTH))],
        out_specs=(row(D_MODEL), row(CA_WIDTH)),
        out_shape=(jax.ShapeDtypeStruct((t, D_MODEL), F32), jax.ShapeDtypeStruct((t, CA_WIDTH), BF16)),
        compiler_params=_cparams(("arbitrary",)),
        name="out_projection",
    )(x, att, ssm, w_att, w_ssm, g_ca, w_cq)


def _memkv_kernel(m_ref, g_ref, w_ref, k_ref, v_ref):
    m = _rms(m_ref[...], g_ref[...]).astype(BF16)
    kv = jnp.dot(m, w_ref[...], preferred_element_type=F32)
    k_ref[...] = kv[:, :CA_WIDTH]
    v_ref[...] = kv[:, CA_WIDTH:]


def _memkv(mem, g_mem, w_mkv_bf, tm):
    t = mem.shape[0]
    row = lambda w: pl.BlockSpec((tm, w), lambda i: (i, 0))
    return pl.pallas_call(
        _memkv_kernel,
        grid=(t // tm,),
        in_specs=[row(D_MODEL), _full((1, D_MODEL)), _full((D_MODEL, 2 * CA_WIDTH))],
        out_specs=(row(CA_WIDTH), row(CA_WIDTH)),
        out_shape=(jax.ShapeDtypeStruct((t, CA_WIDTH), F32), jax.ShapeDtypeStruct((t, CA_WIDTH), F32)),
        compiler_params=_cparams(("arbitrary",)),
        name="memory_kv",
    )(mem, g_mem, w_mkv_bf)


def _xattn_kernel(qc_ref, mk_ref, mv_ref, h_ref, wco_ref, gmoe_ref, wr_ref, br_ref, tri_ref, cin_ref, *refs,
                  nbat, rows, head_split, has_prev, tile_off):
    refs = refs[1:] if has_prev else refs
    h2_ref, xn_hbm, ti_ref, tw_ref, cout_ref, cnt, xn_buf, xn_sem = refs[:8]
    step = pl.program_id(0)
    nsteps = pl.num_programs(0)
    tm = nbat * rows

    def xn_copies(s):
        dst0 = tile_off + s * (tm // SUBLANES)
        return [pltpu.make_async_copy(xn_buf.at[:, :, sl, :], xn_hbm.at[pl.ds(dst0, tm // SUBLANES), sl], xn_sem)
                for sl in range(SUBLANES)]

    @pl.when(step == 0)
    def _():
        cnt[...] = cin_ref[...]

    if head_split:
        mbuf, msem = refs[8:]
        slot = step % 2

        def head_copies(s, sl):
            b0 = s * nbat
            return [pltpu.make_async_copy(src.at[pl.ds(b0, nbat), :, hd, :], mbuf.at[sl, kv, hd], msem.at[sl])
                    for kv, src in enumerate((mk_ref, mv_ref)) for hd in range(CA_HEADS)]

        @pl.when(step == 0)
        def _():
            for c in head_copies(0, 0):
                c.start()

        @pl.when(step + 1 < nsteps)
        def _():
            for c in head_copies(step + 1, 1 - slot):
                c.start()

        for c in head_copies(step, slot):
            c.wait()

    q_all = qc_ref[...]
    outs_b = []
    for bi in range(nbat):
        q = q_all[bi * rows:(bi + 1) * rows]
        outs = []
        for hd in range(CA_HEADS):
            sl = slice(hd * CA_HEAD_DIM, (hd + 1) * CA_HEAD_DIM)
            if head_split:
                mk = mbuf[slot, 0, hd, bi].astype(BF16)
                mv = mbuf[slot, 1, hd, bi].astype(BF16)
            else:
                mk = mk_ref[bi, :, sl].astype(BF16)
                mv = mv_ref[bi, :, sl].astype(BF16)
            s = lax.dot_general(q[:, sl], mk, (((1,), (1,)), ((), ())),
                                preferred_element_type=F32) * (CA_HEAD_DIM ** -0.5)
            m = jnp.max(s, axis=-1, keepdims=True)
            p = jnp.exp(s - m)
            denom = jnp.sum(p, axis=-1, keepdims=True)
            outs.append(jnp.dot(p.astype(BF16), mv, preferred_element_type=F32) / denom)
        outs_b.append(jnp.concatenate(outs, axis=-1))
    o = jnp.concatenate(outs_b, axis=0).astype(BF16)
    h2 = h_ref[...] + jnp.dot(o, wco_ref[...], preferred_element_type=F32)
    h2_ref[...] = h2
    xn = _rms(h2, gmoe_ref[...])
    @pl.when(step >= 1)
    def _():
        for c in xn_copies(step - 1):
            c.wait()

    _store_row_tiles(xn_buf, xn)
    for c in xn_copies(step):
        c.start()

    @pl.when(step == nsteps - 1)
    def _():
        for c in xn_copies(step):
            c.wait()

    logits = jnp.dot(xn.astype(BF16), wr_ref[...], preferred_element_type=F32) + br_ref[...]
    n = logits.shape[0]
    eidx = lax.broadcasted_iota(jnp.int32, (n, N_EXPERTS), 1).astype(F32)
    lane = lax.broadcasted_iota(jnp.int32, (n, LANES), 1)
    ti = jnp.zeros((n, LANES), F32)
    tv = jnp.zeros((n, LANES), F32)
    work = logits
    vals = []
    hits = []
    for k in range(TOP_K):
        mx = jnp.max(work, axis=-1, keepdims=True)
        idx = jnp.min(jnp.where(work == mx, eidx, float(N_EXPERTS)), axis=-1, keepdims=True)
        vals.append(mx)
        hits.append(eidx == idx)
        ti = jnp.where(lane == k, idx, ti)
        work = jnp.where(hits[k], -jnp.inf, work)
    ex = [jnp.exp(v - vals[0]) for v in vals]
    tot = ex[0] + ex[1] + ex[2] + ex[3]
    for k in range(TOP_K):
        tv = jnp.where(lane == k, ex[k] / tot, tv)
    sel = jnp.zeros((n, N_EXPERTS), F32)
    for k in range(TOP_K):
        sel = sel + jnp.where(hits[k], 1.0, 0.0)
    before = jnp.dot(tri_ref[...], sel.astype(BF16), preferred_element_type=F32) + cnt[...]
    for k in range(TOP_K):
        rank = jnp.sum(jnp.where(hits[k], before, 0.0), axis=-1, keepdims=True)
        ti = jnp.where(lane == TOP_K + k, rank, ti)
    cnt[...] = cnt[...] + jnp.sum(sel, axis=0, keepdims=True)
    ti_ref[...] = ti.astype(jnp.int32)
    tw_ref[...] = tv

    @pl.when(step == pl.num_programs(0) - 1)
    def _():
        cout_ref[...] = cnt[...]


def _xattn(qc, mk, mv, h1, w_co, g_moe, w_router, b_router, counts_in, nbat, rows, t_total, t_off, xn_prev=None):
    t = qc.shape[0]
    tm = nbat * rows
    row = lambda w: pl.BlockSpec((tm, w), lambda i: (i, 0))
    seq_rows = t // mk.shape[0]
    head_split = mk.ndim == 4
    scratch = [pltpu.VMEM((1, N_EXPERTS), F32), pltpu.VMEM((tm // SUBLANES, LANE_TILES, SUBLANES, LANES), F32),
               pltpu.SemaphoreType.DMA(())]
    any_spec = pl.BlockSpec(memory_space=pl.ANY)
    prev_args, prev_specs, aliases = [], [], {}
    if xn_prev is not None:
        prev_args, prev_specs, aliases = [xn_prev], [any_spec], {10: 1}
    if head_split:
        assert seq_rows == rows
        mem = pl.BlockSpec(memory_space=pl.ANY)
        scratch += [pltpu.VMEM((2, 2, CA_HEADS, nbat, N_MEM, CA_HEAD_DIM), F32), pltpu.SemaphoreType.DMA((2,))]
    else:
        mem = pl.BlockSpec((nbat, N_MEM, CA_WIDTH), lambda i: ((i * tm) // (seq_rows * nbat), 0, 0))
    tri = jnp.tri(tm, k=-1, dtype=BF16)
    return pl.pallas_call(
        functools.partial(_xattn_kernel, nbat=nbat, rows=rows, head_split=head_split, has_prev=xn_prev is not None,
                          tile_off=t_off // SUBLANES),
        grid=(t // tm,),
        in_specs=[row(CA_WIDTH), mem, mem, row(D_MODEL), _full((CA_WIDTH, D_MODEL)), _full((1, D_MODEL)),
                  _full((D_MODEL, N_EXPERTS)), _full((1, N_EXPERTS)), _full((tm, tm)), _full((1, N_EXPERTS))]
        + prev_specs,
        out_specs=(row(D_MODEL), any_spec, row(LANES), row(LANES), _full((1, N_EXPERTS))),
        out_shape=(jax.ShapeDtypeStruct((t, D_MODEL), F32),
                   jax.ShapeDtypeStruct((t_total // SUBLANES, SUBLANES, LANE_TILES, LANES), F32),
                   jax.ShapeDtypeStruct((t, LANES), jnp.int32), jax.ShapeDtypeStruct((t, LANES), F32),
                   jax.ShapeDtypeStruct((1, N_EXPERTS), F32)),
        scratch_shapes=scratch,
        input_output_aliases=aliases,
        compiler_params=_cparams(("arbitrary",), VMEM_LIMIT),
        name="cross_attn_router",
    )(qc, mk, mv, h1, w_co, g_moe, w_router, b_router, tri, counts_in, *prev_args)


def _expert_kernel(blk_e_ref, w_idx_ref, nused_ref, tab_hbm, xn_hbm, wg_ref, bg_ref, wu_ref, bu_ref, wd_ref, bd_ref,
                   yu_hbm, wg_bf, wu_bf, wd_bf, xbuf, ybuf, tab, gsem, ssem, tsem, *, n_blocks):
    i = pl.program_id(0)
    n_used = nused_ref[0]
    active = i < n_used
    slot = i % 2
    row_w = 2 * MOE_BLOCK

    def ring(b):
        return (b + TAB_RING) % TAB_RING

    def tab_copy(b):
        r = ring(b)
        return pltpu.make_async_copy(tab_hbm.at[jnp.minimum(b, n_blocks - 1) + 1], tab.at[pl.ds(r * row_w, row_w)],
                                     tsem.at[r])

    def block_wait(ref, sem):
        pltpu.make_async_copy(ref.at[pl.ds(0, MOE_BLOCK)], ref.at[pl.ds(0, MOE_BLOCK)], sem).wait()

    def issue_rows(off, copy):
        def body(it, carry):
            ids = [tab[off + it * DMA_BATCH + q] for q in range(DMA_BATCH)]
            for q in range(DMA_BATCH):
                copy(ids[q], it * (DMA_BATCH // SUBLANES) + q // SUBLANES, q % SUBLANES).start(priority=q % 2)
            return carry
        lax.fori_loop(0, MOE_BLOCK // DMA_BATCH, body, 0)

    def issue_gathers(b, sl):
        issue_rows(ring(b) * row_w,
                   lambda tok, j, s: pltpu.make_async_copy(xn_hbm.at[tok], xbuf.at[sl, j, :, s, :], gsem.at[sl]))

    def issue_scatters(b, sl):
        issue_rows(ring(b) * row_w + MOE_BLOCK,
                   lambda tile, j, s: pltpu.make_async_copy(ybuf.at[sl, j, :, s, :], yu_hbm.at[tile], ssem))

    @pl.when((i == 0) & active)
    def _():
        ybuf[...] = jnp.zeros(ybuf.shape, F32)
        for b in (-1, 0, 1):
            tab_copy(b).start()
            tab_copy(b).wait()
        tab_copy(2).start()
        issue_gathers(0, 0)

    prev_e = blk_e_ref[jnp.maximum(i - 1, 0)]
    new_expert = (i == 0) | (blk_e_ref[i] != prev_e)

    @pl.when(active & new_expert)
    def _():
        wg_bf[...] = wg_ref[0].astype(BF16)
        wu_bf[...] = wu_ref[0].astype(BF16)
        wd_bf[...] = wd_ref[0].astype(BF16)

    @pl.when(active)
    def _():
        block_wait(xn_hbm, gsem.at[slot])

        @pl.when(i >= 1)
        def _():
            tab_copy(i + 1).wait()
            tab_copy(i + 2).start()

        issue_gathers(i + 1, 1 - slot)
        issue_scatters(i - 1, 1 - slot)
        x = _load_row_tiles(xbuf.at[slot]).astype(BF16)
        gate = jnp.minimum(jnp.dot(x, wg_bf[...], preferred_element_type=F32) + bg_ref[0], SWIGLU_LIMIT)
        up = jnp.clip(jnp.dot(x, wu_bf[...], preferred_element_type=F32) + bu_ref[0], -SWIGLU_LIMIT, SWIGLU_LIMIT)
        h = gate * jax.nn.sigmoid(SWIGLU_ALPHA * gate) * (up + 1.0)
        y = jnp.dot(h.astype(BF16), wd_bf[...], preferred_element_type=F32) + bd_ref[0]
        _store_row_tiles(ybuf.at[slot], y)
        block_wait(yu_hbm, ssem)

        @pl.when(i == n_used - 1)
        def _():
            issue_scatters(i, slot)
            block_wait(yu_hbm, ssem)
            block_wait(xn_hbm, gsem.at[1 - slot])
            tab_copy(i + 2).wait()


def _experts(blk_e, w_idx, n_used, tab, xn, n_out_tiles, w_gate, b_gate, w_up, b_up, w_down, b_down):
    n_blocks = tab.shape[0] - 1
    d_ff = w_gate.shape[-1]
    tiles = MOE_BLOCK // SUBLANES
    wspec = lambda a, b: pl.BlockSpec((1, a, b), lambda i, be, wi, nu: (wi[i], 0, 0))
    bspec = lambda b: pl.BlockSpec((1, 1, b), lambda i, be, wi, nu: (be[i], 0, 0))
    any_spec = pl.BlockSpec(memory_space=pl.ANY)
    grid_spec = pltpu.PrefetchScalarGridSpec(
        num_scalar_prefetch=3,
        grid=(n_blocks,),
        in_specs=[any_spec, any_spec, wspec(D_MODEL, d_ff), bspec(d_ff), wspec(D_MODEL, d_ff), bspec(d_ff),
                  wspec(d_ff, D_MODEL), bspec(D_MODEL)],
        out_specs=any_spec,
        scratch_shapes=[pltpu.VMEM((D_MODEL, d_ff), BF16), pltpu.VMEM((D_MODEL, d_ff), BF16),
                        pltpu.VMEM((d_ff, D_MODEL), BF16),
                        pltpu.VMEM((2, tiles, LANE_TILES, SUBLANES, LANES), F32),
                        pltpu.VMEM((2, tiles, LANE_TILES, SUBLANES, LANES), F32),
                        pltpu.SMEM((TAB_RING * 2 * MOE_BLOCK,), jnp.int32),
                        pltpu.SemaphoreType.DMA((2,)), pltpu.SemaphoreType.DMA(()),
                        pltpu.SemaphoreType.DMA((TAB_RING,))],
    )
    return pl.pallas_call(
        functools.partial(_expert_kernel, n_blocks=n_blocks),
        grid_spec=grid_spec,
        out_shape=jax.ShapeDtypeStruct((n_out_tiles, LANE_TILES, LANES), F32),
        compiler_params=_cparams(("arbitrary",), VMEM_LIMIT),
        name="moe_experts",
    )(blk_e, w_idx, n_used, tab, xn, w_gate, b_gate.reshape(N_EXPERTS, 1, d_ff), w_up,
      b_up.reshape(N_EXPERTS, 1, d_ff), w_down, b_down.reshape(N_EXPERTS, 1, D_MODEL))


def _combine_kernel(yu_hbm, h_ref, tw_ref, g_ref, o_ref, buf, sem, *, tm, tile_off):
    i = pl.program_id(0)
    n = pl.num_programs(0)
    slot = i % 2
    tiles = tm // SUBLANES

    def copies(step, sl):
        t0 = tile_off + step * tiles
        return [pltpu.make_async_copy(yu_hbm.at[pl.ds(t0, tiles), s, k], buf.at[sl, k, :, :, s, :], sem.at[sl])
                for s in range(SUBLANES) for k in range(TOP_K)]

    @pl.when(i == 0)
    def _():
        for c in copies(0, 0):
            c.start()

    @pl.when(i + 1 < n)
    def _():
        for c in copies(i + 1, 1 - slot):
            c.start()

    for c in copies(i, slot):
        c.wait()
    tw = tw_ref[...]
    y = h_ref[...]
    for k in range(TOP_K):
        y = y + tw[:, k:k + 1] * _load_row_tiles(buf.at[slot, k])
    o_ref[...] = _rms(y, g_ref[...])


def _combine(yu, h2, tw, g_final, tm, t_off):
    t = h2.shape[0]
    row = lambda w: pl.BlockSpec((tm, w), lambda i: (i, 0))
    return pl.pallas_call(
        functools.partial(_combine_kernel, tm=tm, tile_off=t_off // SUBLANES),
        grid=(t // tm,),
        in_specs=[pl.BlockSpec(memory_space=pl.ANY), row(D_MODEL), row(LANES), _full((1, D_MODEL))],
        out_specs=row(D_MODEL),
        out_shape=jax.ShapeDtypeStruct((t, D_MODEL), F32),
        scratch_shapes=[pltpu.VMEM((2, TOP_K, tm // SUBLANES, LANE_TILES, SUBLANES, LANES), F32),
                        pltpu.SemaphoreType.DMA((2,))],
        compiler_params=_cparams(("arbitrary",), VMEM_LIMIT),
        name="moe_combine",
    )(yu, h2, tw, g_final)


def _route(counts, m):
    counts = counts.reshape(N_EXPERTS).astype(jnp.int32)
    padded = (counts + MOE_BLOCK - 1) // MOE_BLOCK * MOE_BLOCK
    pad_end = jnp.cumsum(padded)
    pad_start = pad_end - padded
    n_blocks = -(-(m + N_EXPERTS * (MOE_BLOCK - 1)) // MOE_BLOCK)
    blk_start = jnp.arange(n_blocks, dtype=jnp.int32) * MOE_BLOCK
    owner = lambda start: jnp.minimum(jnp.sum((pad_end[None, :] <= start[:, None]).astype(jnp.int32), axis=1),
                                      N_EXPERTS - 1).astype(jnp.int32)
    n_used = pad_end[-1] // MOE_BLOCK
    last_start = jnp.maximum(n_used - 1, 0) * MOE_BLOCK
    blk_e = owner(jnp.minimum(blk_start, last_start))
    first = jnp.concatenate([jnp.ones((1,), bool), blk_e[1:] != blk_e[:-1]])
    region_end = jnp.sum(jnp.where(blk_e[:, None] == jnp.arange(N_EXPERTS)[None, :], pad_end[None, :], 0), axis=1)
    w_idx = jnp.where(first, blk_e, owner(jnp.minimum(region_end, last_start))).astype(jnp.int32)
    return pad_start, blk_e, w_idx, n_used.astype(jnp.int32).reshape(1), n_blocks * MOE_BLOCK


def _dest(ti, pad_start):
    e = ti[:, :TOP_K]
    onehot = e[:, :, None] == jnp.arange(N_EXPERTS, dtype=jnp.int32)[None, None, :]
    start = jnp.sum(jnp.where(onehot, pad_start[None, None, :], 0), axis=-1)
    return (start + ti[:, TOP_K:2 * TOP_K]).reshape(-1)


def _slot_table(dest, n_slots):
    m = dest.shape[0]
    n_blocks = n_slots // MOE_BLOCK
    filled_by = jnp.zeros((n_slots,), jnp.int32).at[dest].add(jnp.arange(m, dtype=jnp.int32) + 1) - 1
    filled = filled_by >= 0
    src_tok = jnp.where(filled, filled_by, 0) // TOP_K
    dst_tile = jnp.where(filled, filled_by, m + jnp.arange(n_slots, dtype=jnp.int32) % DUMP_TILES)
    rows = jnp.concatenate([src_tok.reshape(n_blocks, MOE_BLOCK), dst_tile.reshape(n_blocks, MOE_BLOCK)], axis=1)
    before_first = jnp.concatenate([jnp.zeros((MOE_BLOCK,), jnp.int32),
                                    m + MOE_BLOCK + jnp.arange(MOE_BLOCK, dtype=jnp.int32)])[None]
    return jnp.concatenate([before_first, rows], axis=0).astype(jnp.int32)


def _block_diag(blocks):
    g, a, b = blocks.shape
    eye = jnp.eye(g, dtype=blocks.dtype)
    return (blocks[:, :, None, :] * eye[:, None, :, None]).reshape(g * a, g * b)


def kernel(x_prompt, x_sample, mem_prompt, cache_win_k, cache_win_v, state_ssm_re, state_ssm_im, cache_mem_k,
           cache_mem_v, g_mix, w_in, sink, lam_re, lam_im, log_dt, b_re, b_im, c_re, c_im, d_skip, w_glu,
           g_attn_out, g_ssm_out, w_out, g_ca, g_mem, w_mk, w_mv, w_cq, w_co, g_moe, w_router, b_router,
           w_gate, b_gate, w_up, b_up, w_down, b_down, g_final):
    depth = g_mix.shape[0]
    assert depth == 1
    bp, lp, _ = x_prompt.shape
    bs, ls, _ = x_sample.shape
    tp, ts = bp * lp, bs * ls

    w_in_bf = w_in[0].astype(BF16)
    w_att_bf = w_out[0, :ATTN_WIDTH].astype(BF16)
    w_ssm_bf = w_out[0, ATTN_WIDTH:].astype(BF16)
    w_cq_bf = w_cq[0].astype(BF16)
    w_co_bf = w_co[0].astype(BF16)
    w_mkv_bf = jnp.concatenate([w_mk[0], w_mv[0]], axis=1).astype(BF16)
    w_glu_bf = w_glu[0].astype(BF16)
    row = lambda a: a.reshape(1, -1)

    a_re, a_im, bb_re, bb_im = _discretize(lam_re[0], lam_im[0], log_dt[0], b_re[0].transpose(0, 2, 1),
                                           b_im[0].transpose(0, 2, 1))
    hg = SSM_GROUPS // 2
    wb = jnp.stack([jnp.concatenate([_block_diag(bb_re[h * hg:(h + 1) * hg]),
                                     _block_diag(bb_im[h * hg:(h + 1) * hg])], axis=1) for h in range(2)]).astype(BF16)
    cre_t = c_re[0].transpose(0, 2, 1)
    cim_t = c_im[0].transpose(0, 2, 1)
    wc = jnp.stack([jnp.concatenate([_block_diag(cre_t[h * hg:(h + 1) * hg]),
                                     -_block_diag(cim_t[h * hg:(h + 1) * hg])], axis=0) for h in range(2)]).astype(BF16)
    ssm_w = (wb, wc, a_re.reshape(1, SSM_LANES), a_im.reshape(1, SSM_LANES), row(d_skip[0]), w_glu_bf,
             row(g_ssm_out[0]))

    xp = x_prompt.reshape(tp, D_MODEL)
    xs_ = x_sample.reshape(ts, D_MODEL)

    qp, kp, vp, up = _inproj(xp, row(g_mix[0]), w_in_bf, 512)
    qs, ks, vs, us = _inproj(xs_, row(g_mix[0]), w_in_bf, 256)

    att_p = _attn_prompt(sink[0], qp, kp, vp, row(g_attn_out[0]), bp, lp)
    ck = cache_win_k[0].reshape(bs, WINDOW, KV_WIDTH)
    cv = cache_win_v[0].reshape(bs, WINDOW, KV_WIDTH)
    att_s = _attn_sample(sink[0], qs, ks, vs, ck, cv, row(g_attn_out[0]), bs, ls, 16)

    zero_state = jnp.zeros((bp, SSM_LANES), F32)
    ssm_p, p_sre, p_sim = _ssm(up.reshape(bp, lp, SSM_WIDTH), zero_state, zero_state, ssm_w, 32)
    ssm_s, s_sre, s_sim = _ssm(us.reshape(bs, ls, SSM_WIDTH), state_ssm_re[0].reshape(bs, SSM_LANES),
                               state_ssm_im[0].reshape(bs, SSM_LANES), ssm_w, ls)

    h1p, qcp = _outproj(xp, att_p, ssm_p.reshape(tp, SSM_WIDTH), w_att_bf, w_ssm_bf, row(g_ca[0]), w_cq_bf, 512)
    h1s, qcs = _outproj(xs_, att_s, ssm_s.reshape(ts, SSM_WIDTH), w_att_bf, w_ssm_bf, row(g_ca[0]), w_cq_bf, 256)

    mk_p, mv_p = _memkv(mem_prompt.reshape(bp * N_MEM, D_MODEL), row(g_mem[0]), w_mkv_bf, 512)

    no_counts = jnp.zeros((1, N_EXPERTS), F32)
    w_router_bf = w_router[0].astype(BF16)
    tt = tp + ts
    h2p, xn, tip, twp, cnt_p = _xattn(qcp, mk_p.reshape(bp, N_MEM, CA_WIDTH), mv_p.reshape(bp, N_MEM, CA_WIDTH),
                                      h1p, w_co_bf, row(g_moe[0]), w_router_bf, row(b_router[0]), no_counts,
                                      1, 512, tt, 0)
    h2s, xn, tis, tws, cnt = _xattn(qcs, cache_mem_k[0], cache_mem_v[0], h1s, w_co_bf, row(g_moe[0]),
                                    w_router_bf, row(b_router[0]), cnt_p, 8, ls, tt, tp, xn_prev=xn)

    m = tt * TOP_K
    pad_start, blk_e, w_idx, n_used, n_slots = _route(cnt, m)
    tab = _slot_table(jnp.concatenate([_dest(tip, pad_start), _dest(tis, pad_start)]), n_slots)
    yu = _experts(blk_e, w_idx, n_used, tab, xn.reshape(tt, LANE_TILES, LANES), m + DUMP_TILES,
                  w_gate[0], b_gate[0], w_up[0], b_up[0], w_down[0], b_down[0])
    yu = yu.reshape((m + DUMP_TILES) // (SUBLANES * TOP_K), SUBLANES, TOP_K, LANE_TILES, LANES)
    y_prompt = _combine(yu, h2p, twp, row(g_final), 256, 0).reshape(bp, lp, D_MODEL)
    y_sample = _combine(yu, h2s, tws, row(g_final), 256, tp).reshape(bs, ls, D_MODEL)

    kp4 = kp.reshape(bp, lp, N_KV_HEADS, HEAD_DIM)
    vp4 = vp.reshape(bp, lp, N_KV_HEADS, HEAD_DIM)
    p_win_k = kp4[:, -WINDOW:][None]
    p_win_v = vp4[:, -WINDOW:][None]
    s_win_k = jnp.concatenate([cache_win_k[0], ks.reshape(bs, ls, N_KV_HEADS, HEAD_DIM)], axis=1)[:, -WINDOW:][None]
    s_win_v = jnp.concatenate([cache_win_v[0], vs.reshape(bs, ls, N_KV_HEADS, HEAD_DIM)], axis=1)[:, -WINDOW:][None]
    st = lambda a, b: a.reshape(1, b, SSM_GROUPS, SSM_STATE)
    p_mem_k = mk_p.reshape(1, bp, N_MEM, CA_HEADS, CA_HEAD_DIM)
    p_mem_v = mv_p.reshape(1, bp, N_MEM, CA_HEADS, CA_HEAD_DIM)
    return (y_prompt, y_sample, p_win_k, p_win_v, st(p_sre, bp), st(p_sim, bp), p_mem_k, p_mem_v,
            s_win_k, s_win_v, st(s_sre, bs), st(s_sim, bs))
```

```python
import functools
import math

import jax
import jax.numpy as jnp
from jax import lax
from jax.experimental import pallas as pl
from jax.experimental.pallas import tpu as pltpu

F32 = jnp.float32
BF16 = jnp.bfloat16

D_MODEL = 1024
HEAD_DIM = 64
N_Q_HEADS = 8
N_KV_HEADS = 2
Q_PER_KV = 4
ATTN_WIDTH = 512
KV_WIDTH = 128
WINDOW = 128
SSM_WIDTH = 512
SSM_GROUP = 16
SSM_GROUPS = 32
SSM_STATE = 64
SSM_LANES = SSM_GROUPS * SSM_STATE
SSM_HALF = SSM_LANES // 2
IN_WIDTH = ATTN_WIDTH + 2 * KV_WIDTH + SSM_WIDTH
N_MEM = 256
CA_HEADS = 4
CA_HEAD_DIM = 128
CA_WIDTH = 512
N_EXPERTS = 32
TOP_K = 4
SWIGLU_ALPHA = 1.702
SWIGLU_LIMIT = 7.0
RMS_EPS = 1e-5

SUBLANES = 8
LANES = 128
MOE_BLOCK = 256
DMA_BATCH = 16
TAB_RING = 4
DUMP_TILES = 2 * MOE_BLOCK
VMEM_LIMIT = 56 * 1024 * 1024


def _cparams(sem, vmem=None):
    return pltpu.CompilerParams(dimension_semantics=sem, vmem_limit_bytes=vmem)


def _rms(x, g):
    return x * lax.rsqrt(jnp.mean(x * x, axis=-1, keepdims=True) + RMS_EPS) * g


def _full(shape):
    return pl.BlockSpec(shape, lambda *_: (0,) * len(shape))


LANE_TILES = D_MODEL // LANES


def _store_row_tiles(ref, x):
    for c in range(LANE_TILES):
        ref[:, c] = x[:, c * LANES:(c + 1) * LANES].reshape(x.shape[0] // SUBLANES, SUBLANES, LANES)


def _load_row_tiles(ref):
    rows = ref.shape[0] * SUBLANES
    return jnp.concatenate([ref[:, c].reshape(rows, LANES) for c in range(LANE_TILES)], axis=1)


def _disc_kernel(lr_ref, li_ref, ldt_ref, bre_ref, bim_ref, are_ref, aim_ref, bbre_ref, bbim_ref):
    lr = lr_ref[...]
    li = li_ref[...]
    dt = jnp.exp(ldt_ref[...])
    mag = jnp.exp(lr * dt)
    ang = li * dt
    ab_re = mag * jnp.cos(ang)
    ab_im = mag * jnp.sin(ang)
    nr = ab_re - 1.0
    ni = ab_im
    den = lr * lr + li * li
    f_re = (nr * lr + ni * li) / den
    f_im = (ni * lr - nr * li) / den
    are_ref[...] = ab_re
    aim_ref[...] = ab_im
    br = bre_ref[...]
    bi = bim_ref[...]
    bbre_ref[...] = f_re[:, None, :] * br - f_im[:, None, :] * bi
    bbim_ref[...] = f_re[:, None, :] * bi + f_im[:, None, :] * br


def _discretize(lam_re, lam_im, log_dt, b_re_t, b_im_t):
    g, n = lam_re.shape
    c = b_re_t.shape[1]
    return pl.pallas_call(
        _disc_kernel,
        out_shape=(jax.ShapeDtypeStruct((g, n), F32), jax.ShapeDtypeStruct((g, n), F32),
                   jax.ShapeDtypeStruct((g, c, n), F32), jax.ShapeDtypeStruct((g, c, n), F32)),
        name="ssm_discretize",
    )(lam_re, lam_im, log_dt.reshape(g, 1), b_re_t, b_im_t)


def _inproj_kernel(x_ref, g_ref, w_ref, q_ref, k_ref, v_ref, u_ref):
    a = _rms(x_ref[...], g_ref[...]).astype(BF16)
    proj = jnp.dot(a, w_ref[...], preferred_element_type=F32)
    q_ref[...] = proj[:, :ATTN_WIDTH].astype(BF16)
    k_ref[...] = proj[:, ATTN_WIDTH:ATTN_WIDTH + KV_WIDTH]
    v_ref[...] = proj[:, ATTN_WIDTH + KV_WIDTH:ATTN_WIDTH + 2 * KV_WIDTH]
    u_ref[...] = proj[:, ATTN_WIDTH + 2 * KV_WIDTH:]


def _inproj(x, g_mix, w_in_bf, tm):
    t = x.shape[0]
    row = lambda w: pl.BlockSpec((tm, w), lambda i: (i, 0))
    return pl.pallas_call(
        _inproj_kernel,
        grid=(t // tm,),
        in_specs=[row(D_MODEL), _full((1, D_MODEL)), _full((D_MODEL, IN_WIDTH))],
        out_specs=(row(ATTN_WIDTH), row(KV_WIDTH), row(KV_WIDTH), row(SSM_WIDTH)),
        out_shape=(jax.ShapeDtypeStruct((t, ATTN_WIDTH), BF16), jax.ShapeDtypeStruct((t, KV_WIDTH), F32),
                   jax.ShapeDtypeStruct((t, KV_WIDTH), F32), jax.ShapeDtypeStruct((t, SSM_WIDTH), F32)),
        compiler_params=_cparams(("arbitrary",)),
        name="in_projection",
    )(x, g_mix, w_in_bf)


def _sink_softmax_pv(s, sink, v_bf):
    m = jnp.maximum(jnp.max(s, axis=-1, keepdims=True), sink)
    p = jnp.exp(s - m)
    denom = jnp.sum(p, axis=-1, keepdims=True) + jnp.exp(sink - m)
    return jnp.dot(p.astype(BF16), v_bf, preferred_element_type=F32) / denom


def _attn_prompt_kernel(sink_ref, q_ref, kc_ref, kp_ref, vc_ref, vp_ref, g_ref, o_ref):
    i = pl.program_id(1)
    q = q_ref[...]
    kk = jnp.concatenate([kp_ref[...], kc_ref[...]], axis=0).astype(BF16)
    vv = jnp.concatenate([vp_ref[...], vc_ref[...]], axis=0).astype(BF16)
    rows = Q_PER_KV * WINDOW
    qi = lax.broadcasted_iota(jnp.int32, (rows, 2 * WINDOW), 0) & (WINDOW - 1)
    kj = lax.broadcasted_iota(jnp.int32, (rows, 2 * WINDOW), 1)
    rel = qi + WINDOW - kj
    mask = (rel >= 0) & (rel < WINDOW) & ((kj >= WINDOW) | (i > 0))
    head_of_row = lax.broadcasted_iota(jnp.int32, (rows, 1), 0) // WINDOW
    low_q = lax.broadcasted_iota(jnp.int32, (WINDOW, LANES), 1) < HEAD_DIM
    zero_q = jnp.zeros((WINDOW, LANES), BF16)
    pairs = []
    for j in range(N_KV_HEADS):
        kh = kk[:, j * HEAD_DIM:(j + 1) * HEAD_DIM]
        vh = vv[:, j * HEAD_DIM:(j + 1) * HEAD_DIM]
        k2 = jnp.concatenate([kh, kh], axis=1)
        v2 = jnp.concatenate([vh, vh], axis=1)
        stacked = []
        for p in range(Q_PER_KV // 2):
            qp = q[:, (j * Q_PER_KV + 2 * p) * HEAD_DIM:(j * Q_PER_KV + 2 * p + 2) * HEAD_DIM]
            stacked += [jnp.where(low_q, qp, zero_q), jnp.where(low_q, zero_q, qp)]
        qs = jnp.concatenate(stacked, axis=0)
        sink = jnp.zeros((rows, 1), F32)
        for g in range(Q_PER_KV):
            sink = jnp.where(head_of_row == g, sink_ref[j * Q_PER_KV + g], sink)
        s = lax.dot_general(qs, k2, (((1,), (1,)), ((), ())), preferred_element_type=F32) * (HEAD_DIM ** -0.5)
        s = jnp.where(mask, s, -jnp.inf)
        o = _sink_softmax_pv(s, sink, v2)
        for p in range(Q_PER_KV // 2):
            pairs.append(jnp.where(low_q, o[2 * p * WINDOW:(2 * p + 1) * WINDOW],
                                   o[(2 * p + 1) * WINDOW:(2 * p + 2) * WINDOW]))
    att = jnp.concatenate(pairs, axis=-1)
    o_ref[...] = _rms(att, g_ref[...]).astype(BF16)


def _attn_prompt(sink, q, k, v, g_attn_out, b, l):
    nb = l // WINDOW
    cur = lambda w: pl.BlockSpec((WINDOW, w), lambda bi, i: (bi * nb + i, 0))
    prev = lambda w: pl.BlockSpec((WINDOW, w), lambda bi, i: (bi * nb + jnp.maximum(i - 1, 0), 0))
    return pl.pallas_call(
        _attn_prompt_kernel,
        grid=(b, nb),
        in_specs=[pl.BlockSpec(memory_space=pltpu.SMEM), cur(ATTN_WIDTH), cur(KV_WIDTH), prev(KV_WIDTH),
                  cur(KV_WIDTH), prev(KV_WIDTH), _full((1, ATTN_WIDTH))],
        out_specs=cur(ATTN_WIDTH),
        out_shape=jax.ShapeDtypeStruct((b * l, ATTN_WIDTH), BF16),
        compiler_params=_cparams(("arbitrary", "arbitrary")),
        name="window_attn_prompt",
    )(sink, q, k, k, v, v, g_attn_out)


def _attn_sample_kernel(sink_ref, q_ref, kn_ref, vn_ref, ck_ref, cv_ref, g_ref, o_ref, *, bt, t):
    nk = WINDOW + t
    rows = Q_PER_KV * t
    ri = lax.broadcasted_iota(jnp.int32, (rows, nk), 0)
    kj = lax.broadcasted_iota(jnp.int32, (rows, nk), 1)
    tq = ri % t
    mask = ((kj < WINDOW) & (kj > tq)) | ((kj >= WINDOW) & ((kj - WINDOW) <= tq))
    rg = lax.broadcasted_iota(jnp.int32, (rows, 1), 0) // t
    q_all = q_ref[...]
    kn_all = kn_ref[...]
    vn_all = vn_ref[...]
    outs_b = []
    for bi in range(bt):
        q = q_all[bi * t:(bi + 1) * t]
        kk = jnp.concatenate([ck_ref[bi], kn_all[bi * t:(bi + 1) * t]], axis=0).astype(BF16)
        vv = jnp.concatenate([cv_ref[bi], vn_all[bi * t:(bi + 1) * t]], axis=0).astype(BF16)
        outs = []
        for j in range(N_KV_HEADS):
            qj = jnp.concatenate(
                [q[:, (j * Q_PER_KV + g) * HEAD_DIM:(j * Q_PER_KV + g + 1) * HEAD_DIM] for g in range(Q_PER_KV)],
                axis=0)
            sink = jnp.zeros((rows, 1), F32)
            for g in range(Q_PER_KV):
                sink = jnp.where(rg == g, sink_ref[j * Q_PER_KV + g], sink)
            kh = kk[:, j * HEAD_DIM:(j + 1) * HEAD_DIM]
            vh = vv[:, j * HEAD_DIM:(j + 1) * HEAD_DIM]
            s = lax.dot_general(qj, kh, (((1,), (1,)), ((), ())), preferred_element_type=F32) * (HEAD_DIM ** -0.5)
            s = jnp.where(mask, s, -jnp.inf)
            o = _sink_softmax_pv(s, sink, vh)
            outs.extend(o[g * t:(g + 1) * t] for g in range(Q_PER_KV))
        outs_b.append(jnp.concatenate(outs, axis=-1))
    att = jnp.concatenate(outs_b, axis=0)
    o_ref[...] = _rms(att, g_ref[...]).astype(BF16)


def _attn_sample(sink, q, k, v, cache_k, cache_v, g_attn_out, b, t, bt):
    row = lambda w: pl.BlockSpec((bt * t, w), lambda i: (i, 0))
    cache = pl.BlockSpec((bt, WINDOW, KV_WIDTH), lambda i: (i, 0, 0))
    return pl.pallas_call(
        functools.partial(_attn_sample_kernel, bt=bt, t=t),
        grid=(b // bt,),
        in_specs=[pl.BlockSpec(memory_space=pltpu.SMEM), row(ATTN_WIDTH), row(KV_WIDTH), row(KV_WIDTH),
                  cache, cache, _full((1, ATTN_WIDTH))],
        out_specs=row(ATTN_WIDTH),
        out_shape=jax.ShapeDtypeStruct((b * t, ATTN_WIDTH), BF16),
        compiler_params=_cparams(("arbitrary",)),
        name="window_attn_sample",
    )(sink, q, k, v, cache_k, cache_v, g_attn_out)


def _ssm_project(u_ref, perm_ref, wb_ref, vre, vim, utb, r):
    u = u_ref[...].reshape(r, SSM_WIDTH)
    u_hi = u.astype(BF16)
    u_lo = (u - u_hi.astype(F32)).astype(BF16)
    perm = perm_ref[...]
    u_tb = (jnp.dot(perm, u_hi, preferred_element_type=F32) + jnp.dot(perm, u_lo, preferred_element_type=F32))
    utb[...] = u_tb
    u_bf = u_tb.astype(BF16)
    half_w = SSM_WIDTH // 2
    for h in range(2):
        vv = jnp.dot(u_bf[:, h * half_w:(h + 1) * half_w], wb_ref[h], preferred_element_type=F32)
        vre[:, h * SSM_HALF:(h + 1) * SSM_HALF] = vv[:, :SSM_HALF]
        vim[:, h * SSM_HALF:(h + 1) * SSM_HALF] = vv[:, SSM_HALF:]


def _ssm_scan(are_ref, aim_ref, vre, vim, sre, sim, nb, s):
    lane_chunk = SSM_LANES // 2
    for rg in range(nb // SUBLANES):
        for lc in range(SSM_LANES // lane_chunk):
            lanes = slice(lc * lane_chunk, (lc + 1) * lane_chunk)
            rows0 = slice(rg * SUBLANES, (rg + 1) * SUBLANES)
            ar = jnp.broadcast_to(are_ref[:, lanes], (SUBLANES, lane_chunk))
            ai = jnp.broadcast_to(aim_ref[:, lanes], (SUBLANES, lane_chunk))

            def step(t, carry, lanes=lanes, rg=rg, ar=ar, ai=ai):
                hr, hi = carry
                row = pl.multiple_of(t * nb + rg * SUBLANES, SUBLANES)
                nhr = ar * hr - ai * hi + vre[pl.ds(row, SUBLANES), lanes]
                nhi = ar * hi + ai * hr + vim[pl.ds(row, SUBLANES), lanes]
                vre[pl.ds(row, SUBLANES), lanes] = nhr
                vim[pl.ds(row, SUBLANES), lanes] = nhi
                return nhr, nhi

            hr, hi = lax.fori_loop(0, s, step, (sre[rows0, lanes], sim[rows0, lanes]), unroll=True)
            sre[rows0, lanes] = hr
            sim[rows0, lanes] = hi


def _ssm_output(vre, vim, utb, permt_ref, wc_ref, d_ref, wglu_ref, g_ref, o_ref, nb, s):
    ys = []
    for h in range(2):
        hcat = jnp.concatenate([vre[:, h * SSM_HALF:(h + 1) * SSM_HALF], vim[:, h * SSM_HALF:(h + 1) * SSM_HALF]],
                               axis=1).astype(BF16)
        ys.append(jnp.dot(hcat, wc_ref[h], preferred_element_type=F32))
    y = jnp.concatenate(ys, axis=1) + d_ref[...] * utb[...]
    g = jax.nn.gelu(y)
    z = jnp.dot(g.astype(BF16), wglu_ref[...], preferred_element_type=F32)
    out = g * jax.nn.sigmoid(z)
    on = _rms(out, g_ref[...]).astype(BF16)
    o_bt = jnp.dot(permt_ref[...], on, preferred_element_type=F32).astype(BF16)
    o_ref[...] = o_bt.reshape(nb, s, SSM_WIDTH)


def _ssm_kernel(u_ref, perm_ref, permt_ref, wb_ref, wc_ref, are_ref, aim_ref, d_ref, wglu_ref, g_ref,
                h0re_ref, h0im_ref, o_ref, hre_ref, him_ref, vre, vim, utb, sre, sim, *, nb, s):
    c = pl.program_id(0)

    @pl.when(c == 0)
    def _():
        sre[...] = h0re_ref[...]
        sim[...] = h0im_ref[...]

    _ssm_project(u_ref, perm_ref, wb_ref, vre, vim, utb, nb * s)
    _ssm_scan(are_ref, aim_ref, vre, vim, sre, sim, nb, s)
    _ssm_output(vre, vim, utb, permt_ref, wc_ref, d_ref, wglu_ref, g_ref, o_ref, nb, s)

    @pl.when(c == pl.num_programs(0) - 1)
    def _():
        hre_ref[...] = sre[...]
        him_ref[...] = sim[...]


def _ssm(u, h0_re, h0_im, ssm_w, s):
    nb, l, _ = u.shape
    r = nb * s
    ridx = jnp.arange(r)
    src = (ridx % nb) * s + ridx // nb
    perm = (src[:, None] == ridx[None, :]).astype(BF16)
    wb, wc, a_re, a_im, d_skip, w_glu, g_out = ssm_w
    in_blk = out_blk = pl.BlockSpec((nb, s, SSM_WIDTH), lambda c: (0, c, 0))
    chunk_bufs = [pltpu.VMEM((r, SSM_LANES), F32), pltpu.VMEM((r, SSM_LANES), F32), pltpu.VMEM((r, SSM_WIDTH), F32)]
    st_blk = _full((nb, SSM_LANES))
    return pl.pallas_call(
        functools.partial(_ssm_kernel, nb=nb, s=s),
        grid=(l // s,),
        in_specs=[in_blk, _full((r, r)), _full((r, r)), _full(wb.shape), _full(wc.shape),
                  _full((1, SSM_LANES)), _full((1, SSM_LANES)), _full((1, SSM_WIDTH)),
                  _full((SSM_WIDTH, SSM_WIDTH)), _full((1, SSM_WIDTH)), st_blk, st_blk],
        out_specs=(out_blk, st_blk, st_blk),
        out_shape=(jax.ShapeDtypeStruct((nb, l, SSM_WIDTH), BF16),
                   jax.ShapeDtypeStruct((nb, SSM_LANES), F32), jax.ShapeDtypeStruct((nb, SSM_LANES), F32)),
        scratch_shapes=chunk_bufs + [pltpu.VMEM((nb, SSM_LANES), F32), pltpu.VMEM((nb, SSM_LANES), F32)],
        compiler_params=_cparams(("arbitrary",), VMEM_LIMIT),
        name="ssm_scan",
    )(u, perm, perm.T, wb, wc, a_re, a_im, d_skip, w_glu, g_out, h0_re, h0_im)


def _outproj_kernel(x_ref, att_ref, ssm_ref, wa_ref, ws_ref, gca_ref, wcq_ref, h_ref, qc_ref):
    h = (x_ref[...] + jnp.dot(att_ref[...], wa_ref[...], preferred_element_type=F32)
         + jnp.dot(ssm_ref[...], ws_ref[...], preferred_element_type=F32))
    h_ref[...] = h
    xn = _rms(h, gca_ref[...]).astype(BF16)
    qc_ref[...] = jnp.dot(xn, wcq_ref[...], preferred_element_type=F32).astype(BF16)


def _outproj(x, att, ssm, w_att, w_ssm, g_ca, w_cq, tm):
    t = x.shape[0]
    row = lambda w: pl.BlockSpec((tm, w), lambda i: (i, 0))
    return pl.pallas_call(
        _outproj_kernel,
        grid=(t // tm,),
        in_specs=[row(D_MODEL), row(ATTN_WIDTH), row(SSM_WIDTH), _full((ATTN_WIDTH, D_MODEL)),
                  _full((SSM_WIDTH, D_MODEL)), _full((1, D_MODEL)), _full((D_MODEL, CA_WIDTH))],
        out_specs=(row(D_MODEL), row(CA_WIDTH)),
        out_shape=(jax.ShapeDtypeStruct((t, D_MODEL), F32), jax.ShapeDtypeStruct((t, CA_WIDTH), BF16)),
        compiler_params=_cparams(("arbitrary",)),
        name="out_projection",
    )(x, att, ssm, w_att, w_ssm, g_ca, w_cq)


def _memkv_kernel(m_ref, g_ref, w_ref, k_ref, v_ref):
    m = _rms(m_ref[...], g_ref[...]).astype(BF16)
    kv = jnp.dot(m, w_ref[...], preferred_element_type=F32)
    k_ref[...] = kv[:, :CA_WIDTH]
    v_ref[...] = kv[:, CA_WIDTH:]


def _memkv(mem, g_mem, w_mkv_bf, tm):
    t = mem.shape[0]
    row = lambda w: pl.BlockSpec((tm, w), lambda i: (i, 0))
    return pl.pallas_call(
        _memkv_kernel,
        grid=(t // tm,),
        in_specs=[row(D_MODEL), _full((1, D_MODEL)), _full((D_MODEL, 2 * CA_WIDTH))],
        out_specs=(row(CA_WIDTH), row(CA_WIDTH)),
        out_shape=(jax.ShapeDtypeStruct((t, CA_WIDTH), F32), jax.ShapeDtypeStruct((t, CA_WIDTH), F32)),
        compiler_params=_cparams(("arbitrary",)),
        name="memory_kv",
    )(mem, g_mem, w_mkv_bf)


def _xattn_kernel(qc_ref, mk_ref, mv_ref, h_ref, wco_ref, gmoe_ref, wr_ref, br_ref, tri_ref, cin_ref, *refs,
                  nbat, rows, head_split, has_prev, tile_off):
    refs = refs[1:] if has_prev else refs
    h2_ref, xn_hbm, ti_ref, tw_ref, cout_ref, cnt, xn_buf, xn_sem = refs[:8]
    step = pl.program_id(0)
    nsteps = pl.num_programs(0)
    tm = nbat * rows

    def xn_copies(s):
        dst0 = tile_off + s * (tm // SUBLANES)
        return [pltpu.make_async_copy(xn_buf.at[:, :, sl, :], xn_hbm.at[pl.ds(dst0, tm // SUBLANES), sl], xn_sem)
                for sl in range(SUBLANES)]

    @pl.when(step == 0)
    def _():
        cnt[...] = cin_ref[...]

    if head_split:
        mbuf, msem = refs[8:]
        slot = step % 2

        def head_copies(s, sl):
            b0 = s * nbat
            return [pltpu.make_async_copy(src.at[pl.ds(b0, nbat), :, hd, :], mbuf.at[sl, kv, hd], msem.at[sl])
                    for kv, src in enumerate((mk_ref, mv_ref)) for hd in range(CA_HEADS)]

        @pl.when(step == 0)
        def _():
            for c in head_copies(0, 0):
                c.start()

        @pl.when(step + 1 < nsteps)
        def _():
            for c in head_copies(step + 1, 1 - slot):
                c.start()

        for c in head_copies(step, slot):
            c.wait()

    q_all = qc_ref[...]
    if head_split:
        keys = nbat * N_MEM
        own = (lax.broadcasted_iota(jnp.int32, (keys, tm), 0) // N_MEM
               == lax.broadcasted_iota(jnp.int32, (keys, tm), 1) // rows)
        outs = []
        for hd in range(CA_HEADS):
            sl = slice(hd * CA_HEAD_DIM, (hd + 1) * CA_HEAD_DIM)
            mk = mbuf[slot, 0, hd].reshape(keys, CA_HEAD_DIM).astype(BF16)
            mv = mbuf[slot, 1, hd].reshape(keys, CA_HEAD_DIM).astype(BF16)
            s = lax.dot_general(mk, q_all[:, sl], (((1,), (1,)), ((), ())),
                                preferred_element_type=F32) * (CA_HEAD_DIM ** -0.5)
            s = jnp.where(own, s, -jnp.inf)
            p = jnp.exp(s - jnp.max(s, axis=0, keepdims=True))
            p = p / jnp.sum(p, axis=0, keepdims=True)
            outs.append(lax.dot_general(p.astype(BF16), mv, (((0,), (0,)), ((), ())), preferred_element_type=F32))
        o = jnp.concatenate(outs, axis=-1).astype(BF16)
    else:
        outs_b = []
        for bi in range(nbat):
            q = q_all[bi * rows:(bi + 1) * rows]
            outs = []
            for hd in range(CA_HEADS):
                sl = slice(hd * CA_HEAD_DIM, (hd + 1) * CA_HEAD_DIM)
                mk = mk_ref[bi, :, sl].astype(BF16)
                mv = mv_ref[bi, :, sl].astype(BF16)
                s = lax.dot_general(q[:, sl], mk, (((1,), (1,)), ((), ())),
                                    preferred_element_type=F32) * (CA_HEAD_DIM ** -0.5)
                m = jnp.max(s, axis=-1, keepdims=True)
                p = jnp.exp(s - m)
                denom = jnp.sum(p, axis=-1, keepdims=True)
                outs.append(jnp.dot(p.astype(BF16), mv, preferred_element_type=F32) / denom)
            outs_b.append(jnp.concatenate(outs, axis=-1))
        o = jnp.concatenate(outs_b, axis=0).astype(BF16)
    h2 = h_ref[...] + jnp.dot(o, wco_ref[...], preferred_element_type=F32)
    h2_ref[...] = h2
    xn = _rms(h2, gmoe_ref[...])
    @pl.when(step >= 1)
    def _():
        for c in xn_copies(step - 1):
            c.wait()

    _store_row_tiles(xn_buf, xn)
    for c in xn_copies(step):
        c.start()

    @pl.when(step == nsteps - 1)
    def _():
        for c in xn_copies(step):
            c.wait()

    logits = jnp.dot(xn.astype(BF16), wr_ref[...], preferred_element_type=F32) + br_ref[...]
    n = logits.shape[0]
    eidx = lax.broadcasted_iota(jnp.int32, (n, N_EXPERTS), 1).astype(F32)
    lane = lax.broadcasted_iota(jnp.int32, (n, LANES), 1)
    ti = jnp.zeros((n, LANES), F32)
    tv = jnp.zeros((n, LANES), F32)
    work = logits
    vals = []
    hits = []
    for k in range(TOP_K):
        mx = jnp.max(work, axis=-1, keepdims=True)
        idx = jnp.min(jnp.where(work == mx, eidx, float(N_EXPERTS)), axis=-1, keepdims=True)
        vals.append(mx)
        hits.append(eidx == idx)
        ti = jnp.where(lane == k, idx, ti)
        work = jnp.where(hits[k], -jnp.inf, work)
    ex = [jnp.exp(v - vals[0]) for v in vals]
    tot = ex[0] + ex[1] + ex[2] + ex[3]
    for k in range(TOP_K):
        tv = jnp.where(lane == k, ex[k] / tot, tv)
    sel = jnp.zeros((n, N_EXPERTS), F32)
    for k in range(TOP_K):
        sel = sel + jnp.where(hits[k], 1.0, 0.0)
    before = jnp.dot(tri_ref[...], sel.astype(BF16), preferred_element_type=F32) + cnt[...]
    for k in range(TOP_K):
        rank = jnp.sum(jnp.where(hits[k], before, 0.0), axis=-1, keepdims=True)
        ti = jnp.where(lane == TOP_K + k, rank, ti)
    cnt[...] = cnt[...] + jnp.sum(sel, axis=0, keepdims=True)
    ti_ref[...] = ti.astype(jnp.int32)
    tw_ref[...] = tv

    @pl.when(step == pl.num_programs(0) - 1)
    def _():
        cout_ref[...] = cnt[...]


def _xattn(qc, mk, mv, h1, w_co, g_moe, w_router, b_router, counts_in, nbat, rows, t_total, t_off, xn_prev=None):
    t = qc.shape[0]
    tm = nbat * rows
    row = lambda w: pl.BlockSpec((tm, w), lambda i: (i, 0))
    seq_rows = t // mk.shape[0]
    head_split = mk.ndim == 4
    scratch = [pltpu.VMEM((1, N_EXPERTS), F32), pltpu.VMEM((tm // SUBLANES, LANE_TILES, SUBLANES, LANES), F32),
               pltpu.SemaphoreType.DMA(())]
    any_spec = pl.BlockSpec(memory_space=pl.ANY)
    prev_args, prev_specs, aliases = [], [], {}
    if xn_prev is not None:
        prev_args, prev_specs, aliases = [xn_prev], [any_spec], {10: 1}
    if head_split:
        assert seq_rows == rows
        mem = pl.BlockSpec(memory_space=pl.ANY)
        scratch += [pltpu.VMEM((2, 2, CA_HEADS, nbat, N_MEM, CA_HEAD_DIM), F32), pltpu.SemaphoreType.DMA((2,))]
    else:
        mem = pl.BlockSpec((nbat, N_MEM, CA_WIDTH), lambda i: ((i * tm) // (seq_rows * nbat), 0, 0))
    tri = jnp.tri(tm, k=-1, dtype=BF16)
    return pl.pallas_call(
        functools.partial(_xattn_kernel, nbat=nbat, rows=rows, head_split=head_split, has_prev=xn_prev is not None,
                          tile_off=t_off // SUBLANES),
        grid=(t // tm,),
        in_specs=[row(CA_WIDTH), mem, mem, row(D_MODEL), _full((CA_WIDTH, D_MODEL)), _full((1, D_MODEL)),
                  _full((D_MODEL, N_EXPERTS)), _full((1, N_EXPERTS)), _full((tm, tm)), _full((1, N_EXPERTS))]
        + prev_specs,
        out_specs=(row(D_MODEL), any_spec, row(LANES), row(LANES), _full((1, N_EXPERTS))),
        out_shape=(jax.ShapeDtypeStruct((t, D_MODEL), F32),
                   jax.ShapeDtypeStruct((t_total // SUBLANES, SUBLANES, LANE_TILES, LANES), F32),
                   jax.ShapeDtypeStruct((t, LANES), jnp.int32), jax.ShapeDtypeStruct((t, LANES), F32),
                   jax.ShapeDtypeStruct((1, N_EXPERTS), F32)),
        scratch_shapes=scratch,
        input_output_aliases=aliases,
        compiler_params=_cparams(("arbitrary",), VMEM_LIMIT),
        name="cross_attn_router",
    )(qc, mk, mv, h1, w_co, g_moe, w_router, b_router, tri, counts_in, *prev_args)


def _expert_kernel(blk_e_ref, w_idx_ref, nused_ref, tab_hbm, xn_hbm, wg_ref, bg_ref, wu_ref, bu_ref, wd_ref, bd_ref,
                   yu_hbm, wg_bf, wu_bf, wd_bf, xbuf, ybuf, tab, gsem, ssem, tsem, *, n_blocks):
    i = pl.program_id(0)
    n_used = nused_ref[0]
    active = i < n_used
    slot = i % 2
    row_w = 2 * MOE_BLOCK

    def ring(b):
        return (b + TAB_RING) % TAB_RING

    def tab_copy(b):
        r = ring(b)
        return pltpu.make_async_copy(tab_hbm.at[jnp.minimum(b, n_blocks - 1) + 1], tab.at[pl.ds(r * row_w, row_w)],
                                     tsem.at[r])

    def block_wait(ref, sem):
        pltpu.make_async_copy(ref.at[pl.ds(0, MOE_BLOCK)], ref.at[pl.ds(0, MOE_BLOCK)], sem).wait()

    def issue_rows(off, copy):
        def body(it, carry):
            ids = [tab[off + it * DMA_BATCH + q] for q in range(DMA_BATCH)]
            for q in range(DMA_BATCH):
                copy(ids[q], it * (DMA_BATCH // SUBLANES) + q // SUBLANES, q % SUBLANES).start(priority=q % 2)
            return carry
        lax.fori_loop(0, MOE_BLOCK // DMA_BATCH, body, 0)

    def issue_gathers(b, sl):
        issue_rows(ring(b) * row_w,
                   lambda tok, j, s: pltpu.make_async_copy(xn_hbm.at[tok], xbuf.at[sl, j, :, s, :], gsem.at[sl]))

    def issue_scatters(b, sl):
        issue_rows(ring(b) * row_w + MOE_BLOCK,
                   lambda tile, j, s: pltpu.make_async_copy(ybuf.at[sl, j, :, s, :], yu_hbm.at[tile], ssem))

    @pl.when((i == 0) & active)
    def _():
        ybuf[...] = jnp.zeros(ybuf.shape, F32)
        for b in (-1, 0, 1):
            tab_copy(b).start()
            tab_copy(b).wait()
        tab_copy(2).start()
        issue_gathers(0, 0)

    prev_e = blk_e_ref[jnp.maximum(i - 1, 0)]
    new_expert = (i == 0) | (blk_e_ref[i] != prev_e)

    @pl.when(active & new_expert)
    def _():
        wg_bf[...] = wg_ref[0].astype(BF16)
        wu_bf[...] = wu_ref[0].astype(BF16)
        wd_bf[...] = wd_ref[0].astype(BF16)

    @pl.when(active)
    def _():
        block_wait(xn_hbm, gsem.at[slot])

        @pl.when(i >= 1)
        def _():
            tab_copy(i + 1).wait()
            tab_copy(i + 2).start()

        issue_gathers(i + 1, 1 - slot)
        issue_scatters(i - 1, 1 - slot)
        x = _load_row_tiles(xbuf.at[slot]).astype(BF16)
        gate = jnp.minimum(jnp.dot(x, wg_bf[...], preferred_element_type=F32) + bg_ref[0], SWIGLU_LIMIT)
        up = jnp.clip(jnp.dot(x, wu_bf[...], preferred_element_type=F32) + bu_ref[0], -SWIGLU_LIMIT, SWIGLU_LIMIT)
        h = gate * jax.nn.sigmoid(SWIGLU_ALPHA * gate) * (up + 1.0)
        y = jnp.dot(h.astype(BF16), wd_bf[...], preferred_element_type=F32) + bd_ref[0]
        _store_row_tiles(ybuf.at[slot], y)
        block_wait(yu_hbm, ssem)

        @pl.when(i == n_used - 1)
        def _():
            issue_scatters(i, slot)
            block_wait(yu_hbm, ssem)
            block_wait(xn_hbm, gsem.at[1 - slot])
            tab_copy(i + 2).wait()


def _experts(blk_e, w_idx, n_used, tab, xn, n_out_tiles, w_gate, b_gate, w_up, b_up, w_down, b_down):
    n_blocks = tab.shape[0] - 1
    d_ff = w_gate.shape[-1]
    tiles = MOE_BLOCK // SUBLANES
    wspec = lambda a, b: pl.BlockSpec((1, a, b), lambda i, be, wi, nu: (wi[i], 0, 0))
    bspec = lambda b: pl.BlockSpec((1, 1, b), lambda i, be, wi, nu: (be[i], 0, 0))
    any_spec = pl.BlockSpec(memory_space=pl.ANY)
    grid_spec = pltpu.PrefetchScalarGridSpec(
        num_scalar_prefetch=3,
        grid=(n_blocks,),
        in_specs=[any_spec, any_spec, wspec(D_MODEL, d_ff), bspec(d_ff), wspec(D_MODEL, d_ff), bspec(d_ff),
                  wspec(d_ff, D_MODEL), bspec(D_MODEL)],
        out_specs=any_spec,
        scratch_shapes=[pltpu.VMEM((D_MODEL, d_ff), BF16), pltpu.VMEM((D_MODEL, d_ff), BF16),
                        pltpu.VMEM((d_ff, D_MODEL), BF16),
                        pltpu.VMEM((2, tiles, LANE_TILES, SUBLANES, LANES), F32),
                        pltpu.VMEM((2, tiles, LANE_TILES, SUBLANES, LANES), F32),
                        pltpu.SMEM((TAB_RING * 2 * MOE_BLOCK,), jnp.int32),
                        pltpu.SemaphoreType.DMA((2,)), pltpu.SemaphoreType.DMA(()),
                        pltpu.SemaphoreType.DMA((TAB_RING,))],
    )
    return pl.pallas_call(
        functools.partial(_expert_kernel, n_blocks=n_blocks),
        grid_spec=grid_spec,
        out_shape=jax.ShapeDtypeStruct((n_out_tiles, LANE_TILES, LANES), F32),
        compiler_params=_cparams(("arbitrary",), VMEM_LIMIT),
        name="moe_experts",
    )(blk_e, w_idx, n_used, tab, xn, w_gate, b_gate.reshape(N_EXPERTS, 1, d_ff), w_up,
      b_up.reshape(N_EXPERTS, 1, d_ff), w_down, b_down.reshape(N_EXPERTS, 1, D_MODEL))


def _combine_kernel(yu_hbm, h_ref, tw_ref, g_ref, o_ref, buf, sem, *, tm, tile_off):
    i = pl.program_id(0)
    n = pl.num_programs(0)
    slot = i % 2
    tiles = tm // SUBLANES

    def copies(step, sl):
        t0 = tile_off + step * tiles
        return [pltpu.make_async_copy(yu_hbm.at[pl.ds(t0, tiles), s, k], buf.at[sl, k, :, :, s, :], sem.at[sl])
                for s in range(SUBLANES) for k in range(TOP_K)]

    @pl.when(i == 0)
    def _():
        for c in copies(0, 0):
            c.start()

    @pl.when(i + 1 < n)
    def _():
        for c in copies(i + 1, 1 - slot):
            c.start()

    for c in copies(i, slot):
        c.wait()
    tw = tw_ref[...]
    y = h_ref[...]
    for k in range(TOP_K):
        y = y + tw[:, k:k + 1] * _load_row_tiles(buf.at[slot, k])
    o_ref[...] = _rms(y, g_ref[...])


def _combine(yu, h2, tw, g_final, tm, t_off):
    t = h2.shape[0]
    row = lambda w: pl.BlockSpec((tm, w), lambda i: (i, 0))
    return pl.pallas_call(
        functools.partial(_combine_kernel, tm=tm, tile_off=t_off // SUBLANES),
        grid=(t // tm,),
        in_specs=[pl.BlockSpec(memory_space=pl.ANY), row(D_MODEL), row(LANES), _full((1, D_MODEL))],
        out_specs=row(D_MODEL),
        out_shape=jax.ShapeDtypeStruct((t, D_MODEL), F32),
        scratch_shapes=[pltpu.VMEM((2, TOP_K, tm // SUBLANES, LANE_TILES, SUBLANES, LANES), F32),
                        pltpu.SemaphoreType.DMA((2,))],
        compiler_params=_cparams(("arbitrary",), VMEM_LIMIT),
        name="moe_combine",
    )(yu, h2, tw, g_final)


def _route(counts, m):
    counts = counts.reshape(N_EXPERTS).astype(jnp.int32)
    padded = (counts + MOE_BLOCK - 1) // MOE_BLOCK * MOE_BLOCK
    pad_end = jnp.cumsum(padded)
    pad_start = pad_end - padded
    n_blocks = -(-(m + N_EXPERTS * (MOE_BLOCK - 1)) // MOE_BLOCK)
    blk_start = jnp.arange(n_blocks, dtype=jnp.int32) * MOE_BLOCK
    owner = lambda start: jnp.minimum(jnp.sum((pad_end[None, :] <= start[:, None]).astype(jnp.int32), axis=1),
                                      N_EXPERTS - 1).astype(jnp.int32)
    n_used = pad_end[-1] // MOE_BLOCK
    last_start = jnp.maximum(n_used - 1, 0) * MOE_BLOCK
    blk_e = owner(jnp.minimum(blk_start, last_start))
    first = jnp.concatenate([jnp.ones((1,), bool), blk_e[1:] != blk_e[:-1]])
    region_end = jnp.sum(jnp.where(blk_e[:, None] == jnp.arange(N_EXPERTS)[None, :], pad_end[None, :], 0), axis=1)
    w_idx = jnp.where(first, blk_e, owner(jnp.minimum(region_end, last_start))).astype(jnp.int32)
    return pad_start, blk_e, w_idx, n_used.astype(jnp.int32).reshape(1), n_blocks * MOE_BLOCK


def _dest(ti, pad_start):
    e = ti[:, :TOP_K]
    onehot = e[:, :, None] == jnp.arange(N_EXPERTS, dtype=jnp.int32)[None, None, :]
    start = jnp.sum(jnp.where(onehot, pad_start[None, None, :], 0), axis=-1)
    return (start + ti[:, TOP_K:2 * TOP_K]).reshape(-1)


def _slot_table(dest, n_slots):
    m = dest.shape[0]
    n_blocks = n_slots // MOE_BLOCK
    filled_by = jnp.zeros((n_slots,), jnp.int32).at[dest].add(jnp.arange(m, dtype=jnp.int32) + 1) - 1
    filled = filled_by >= 0
    src_tok = jnp.where(filled, filled_by, 0) // TOP_K
    dst_tile = jnp.where(filled, filled_by, m + jnp.arange(n_slots, dtype=jnp.int32) % DUMP_TILES)
    rows = jnp.concatenate([src_tok.reshape(n_blocks, MOE_BLOCK), dst_tile.reshape(n_blocks, MOE_BLOCK)], axis=1)
    before_first = jnp.concatenate([jnp.zeros((MOE_BLOCK,), jnp.int32),
                                    m + MOE_BLOCK + jnp.arange(MOE_BLOCK, dtype=jnp.int32)])[None]
    return jnp.concatenate([before_first, rows], axis=0).astype(jnp.int32)


def _block_diag(blocks):
    g, a, b = blocks.shape
    eye = jnp.eye(g, dtype=blocks.dtype)
    return (blocks[:, :, None, :] * eye[:, None, :, None]).reshape(g * a, g * b)


def kernel(x_prompt, x_sample, mem_prompt, cache_win_k, cache_win_v, state_ssm_re, state_ssm_im, cache_mem_k,
           cache_mem_v, g_mix, w_in, sink, lam_re, lam_im, log_dt, b_re, b_im, c_re, c_im, d_skip, w_glu,
           g_attn_out, g_ssm_out, w_out, g_ca, g_mem, w_mk, w_mv, w_cq, w_co, g_moe, w_router, b_router,
           w_gate, b_gate, w_up, b_up, w_down, b_down, g_final):
    depth = g_mix.shape[0]
    assert depth == 1
    bp, lp, _ = x_prompt.shape
    bs, ls, _ = x_sample.shape
    tp, ts = bp * lp, bs * ls

    w_in_bf = w_in[0].astype(BF16)
    w_att_bf = w_out[0, :ATTN_WIDTH].astype(BF16)
    w_ssm_bf = w_out[0, ATTN_WIDTH:].astype(BF16)
    w_cq_bf = w_cq[0].astype(BF16)
    w_co_bf = w_co[0].astype(BF16)
    w_mkv_bf = jnp.concatenate([w_mk[0], w_mv[0]], axis=1).astype(BF16)
    w_glu_bf = w_glu[0].astype(BF16)
    row = lambda a: a.reshape(1, -1)

    a_re, a_im, bb_re, bb_im = _discretize(lam_re[0], lam_im[0], log_dt[0], b_re[0].transpose(0, 2, 1),
                                           b_im[0].transpose(0, 2, 1))
    hg = SSM_GROUPS // 2
    wb = jnp.stack([jnp.concatenate([_block_diag(bb_re[h * hg:(h + 1) * hg]),
                                     _block_diag(bb_im[h * hg:(h + 1) * hg])], axis=1) for h in range(2)]).astype(BF16)
    cre_t = c_re[0].transpose(0, 2, 1)
    cim_t = c_im[0].transpose(0, 2, 1)
    wc = jnp.stack([jnp.concatenate([_block_diag(cre_t[h * hg:(h + 1) * hg]),
                                     -_block_diag(cim_t[h * hg:(h + 1) * hg])], axis=0) for h in range(2)]).astype(BF16)
    ssm_w = (wb, wc, a_re.reshape(1, SSM_LANES), a_im.reshape(1, SSM_LANES), row(d_skip[0]), w_glu_bf,
             row(g_ssm_out[0]))

    xp = x_prompt.reshape(tp, D_MODEL)
    xs_ = x_sample.reshape(ts, D_MODEL)

    qp, kp, vp, up = _inproj(xp, row(g_mix[0]), w_in_bf, 512)
    qs, ks, vs, us = _inproj(xs_, row(g_mix[0]), w_in_bf, 256)

    att_p = _attn_prompt(sink[0], qp, kp, vp, row(g_attn_out[0]), bp, lp)
    ck = cache_win_k[0].reshape(bs, WINDOW, KV_WIDTH)
    cv = cache_win_v[0].reshape(bs, WINDOW, KV_WIDTH)
    att_s = _attn_sample(sink[0], qs, ks, vs, ck, cv, row(g_attn_out[0]), bs, ls, 16)

    zero_state = jnp.zeros((bp, SSM_LANES), F32)
    ssm_p, p_sre, p_sim = _ssm(up.reshape(bp, lp, SSM_WIDTH), zero_state, zero_state, ssm_w, 32)
    ssm_s, s_sre, s_sim = _ssm(us.reshape(bs, ls, SSM_WIDTH), state_ssm_re[0].reshape(bs, SSM_LANES),
                               state_ssm_im[0].reshape(bs, SSM_LANES), ssm_w, ls)

    h1p, qcp = _outproj(xp, att_p, ssm_p.reshape(tp, SSM_WIDTH), w_att_bf, w_ssm_bf, row(g_ca[0]), w_cq_bf, 512)
    h1s, qcs = _outproj(xs_, att_s, ssm_s.reshape(ts, SSM_WIDTH), w_att_bf, w_ssm_bf, row(g_ca[0]), w_cq_bf, 256)

    mk_p, mv_p = _memkv(mem_prompt.reshape(bp * N_MEM, D_MODEL), row(g_mem[0]), w_mkv_bf, 512)

    no_counts = jnp.zeros((1, N_EXPERTS), F32)
    w_router_bf = w_router[0].astype(BF16)
    tt = tp + ts
    h2p, xn, tip, twp, cnt_p = _xattn(qcp, mk_p.reshape(bp, N_MEM, CA_WIDTH), mv_p.reshape(bp, N_MEM, CA_WIDTH),
                                      h1p, w_co_bf, row(g_moe[0]), w_router_bf, row(b_router[0]), no_counts,
                                      1, 512, tt, 0)
    h2s, xn, tis, tws, cnt = _xattn(qcs, cache_mem_k[0], cache_mem_v[0], h1s, w_co_bf, row(g_moe[0]),
                                    w_router_bf, row(b_router[0]), cnt_p, 8, ls, tt, tp, xn_prev=xn)

    m = tt * TOP_K
    pad_start, blk_e, w_idx, n_used, n_slots = _route(cnt, m)
    tab = _slot_table(jnp.concatenate([_dest(tip, pad_start), _dest(tis, pad_start)]), n_slots)
    yu = _experts(blk_e, w_idx, n_used, tab, xn.reshape(tt, LANE_TILES, LANES), m + DUMP_TILES,
                  w_gate[0], b_gate[0], w_up[0], b_up[0], w_down[0], b_down[0])
    yu = yu.reshape((m + DUMP_TILES) // (SUBLANES * TOP_K), SUBLANES, TOP_K, LANE_TILES, LANES)
    y_prompt = _combine(yu, h2p, twp, row(g_final), 256, 0).reshape(bp, lp, D_MODEL)
    y_sample = _combine(yu, h2s, tws, row(g_final), 256, tp).reshape(bs, ls, D_MODEL)

    kp4 = kp.reshape(bp, lp, N_KV_HEADS, HEAD_DIM)
    vp4 = vp.reshape(bp, lp, N_KV_HEADS, HEAD_DIM)
    p_win_k = kp4[:, -WINDOW:][None]
    p_win_v = vp4[:, -WINDOW:][None]
    s_win_k = jnp.concatenate([cache_win_k[0], ks.reshape(bs, ls, N_KV_HEADS, HEAD_DIM)], axis=1)[:, -WINDOW:][None]
    s_win_v = jnp.concatenate([cache_win_v[0], vs.reshape(bs, ls, N_KV_HEADS, HEAD_DIM)], axis=1)[:, -WINDOW:][None]
    st = lambda a, b: a.reshape(1, b, SSM_GROUPS, SSM_STATE)
    p_mem_k = mk_p.reshape(1, bp, N_MEM, CA_HEADS, CA_HEAD_DIM)
    p_mem_v = mv_p.reshape(1, bp, N_MEM, CA_HEADS, CA_HEAD_DIM)
    return (y_prompt, y_sample, p_win_k, p_win_v, st(p_sre, bp), st(p_sim, bp), p_mem_k, p_mem_v,
            s_win_k, s_win_v, st(s_sre, bs), st(s_sim, bs))
```

```python
import functools
import math

import jax
import jax.numpy as jnp
from jax import lax
from jax.experimental import pallas as pl
from jax.experimental.pallas import tpu as pltpu
from jax.experimental.pallas import tpu_sc as plsc

F32 = jnp.float32
BF16 = jnp.bfloat16

D_MODEL = 1024
HEAD_DIM = 64
N_Q_HEADS = 8
N_KV_HEADS = 2
Q_PER_KV = 4
ATTN_WIDTH = 512
KV_WIDTH = 128
WINDOW = 128
SSM_WIDTH = 512
SSM_GROUP = 16
SSM_GROUPS = 32
SSM_STATE = 64
SSM_LANES = SSM_GROUPS * SSM_STATE
SSM_HALF = SSM_LANES // 2
IN_WIDTH = ATTN_WIDTH + 2 * KV_WIDTH + SSM_WIDTH
N_MEM = 256
CA_HEADS = 4
CA_HEAD_DIM = 128
CA_WIDTH = 512
N_EXPERTS = 32
TOP_K = 4
SWIGLU_ALPHA = 1.702
SWIGLU_LIMIT = 7.0
RMS_EPS = 1e-5

SUBLANES = 8
LANES = 128
MOE_BLOCK = 256
DMA_BATCH = 16
TAB_RING = 4
ATTN_SEQS = 4
SC_CORES = 2
SC_WORKERS = 32
SC_ROWS = 64
DUMP_TILES = 2 * MOE_BLOCK
VMEM_LIMIT = 56 * 1024 * 1024


def _cparams(sem, vmem=None):
    return pltpu.CompilerParams(dimension_semantics=sem, vmem_limit_bytes=vmem)


def _rms(x, g):
    return x * lax.rsqrt(jnp.mean(x * x, axis=-1, keepdims=True) + RMS_EPS) * g


def _full(shape):
    return pl.BlockSpec(shape, lambda *_: (0,) * len(shape))


LANE_TILES = D_MODEL // LANES


def _store_row_tiles(ref, x):
    for c in range(LANE_TILES):
        ref[:, c] = x[:, c * LANES:(c + 1) * LANES].reshape(x.shape[0] // SUBLANES, SUBLANES, LANES)


def _load_row_tiles(ref):
    rows = ref.shape[0] * SUBLANES
    return jnp.concatenate([ref[:, c].reshape(rows, LANES) for c in range(LANE_TILES)], axis=1)


def _disc_kernel(lr_ref, li_ref, ldt_ref, bre_ref, bim_ref, are_ref, aim_ref, bbre_ref, bbim_ref):
    lr = lr_ref[...]
    li = li_ref[...]
    dt = jnp.exp(ldt_ref[...])
    mag = jnp.exp(lr * dt)
    ang = li * dt
    ab_re = mag * jnp.cos(ang)
    ab_im = mag * jnp.sin(ang)
    nr = ab_re - 1.0
    ni = ab_im
    den = lr * lr + li * li
    f_re = (nr * lr + ni * li) / den
    f_im = (ni * lr - nr * li) / den
    are_ref[...] = ab_re
    aim_ref[...] = ab_im
    br = bre_ref[...]
    bi = bim_ref[...]
    bbre_ref[...] = f_re[:, None, :] * br - f_im[:, None, :] * bi
    bbim_ref[...] = f_re[:, None, :] * bi + f_im[:, None, :] * br


def _discretize(lam_re, lam_im, log_dt, b_re_t, b_im_t):
    g, n = lam_re.shape
    c = b_re_t.shape[1]
    return pl.pallas_call(
        _disc_kernel,
        out_shape=(jax.ShapeDtypeStruct((g, n), F32), jax.ShapeDtypeStruct((g, n), F32),
                   jax.ShapeDtypeStruct((g, c, n), F32), jax.ShapeDtypeStruct((g, c, n), F32)),
        name="ssm_discretize",
    )(lam_re, lam_im, log_dt.reshape(g, 1), b_re_t, b_im_t)


def _inproj_kernel(x_ref, g_ref, w_ref, q_ref, k_ref, v_ref, u_ref):
    a = _rms(x_ref[...], g_ref[...]).astype(BF16)
    proj = jnp.dot(a, w_ref[...], preferred_element_type=F32)
    q_ref[...] = proj[:, :ATTN_WIDTH].astype(BF16)
    k_ref[...] = proj[:, ATTN_WIDTH:ATTN_WIDTH + KV_WIDTH]
    v_ref[...] = proj[:, ATTN_WIDTH + KV_WIDTH:ATTN_WIDTH + 2 * KV_WIDTH]
    u_ref[...] = proj[:, ATTN_WIDTH + 2 * KV_WIDTH:]


def _inproj(x, g_mix, w_in_bf, tm):
    t = x.shape[0]
    row = lambda w: pl.BlockSpec((tm, w), lambda i: (i, 0))
    return pl.pallas_call(
        _inproj_kernel,
        grid=(t // tm,),
        in_specs=[row(D_MODEL), _full((1, D_MODEL)), _full((D_MODEL, IN_WIDTH))],
        out_specs=(row(ATTN_WIDTH), row(KV_WIDTH), row(KV_WIDTH), row(SSM_WIDTH)),
        out_shape=(jax.ShapeDtypeStruct((t, ATTN_WIDTH), BF16), jax.ShapeDtypeStruct((t, KV_WIDTH), F32),
                   jax.ShapeDtypeStruct((t, KV_WIDTH), F32), jax.ShapeDtypeStruct((t, SSM_WIDTH), F32)),
        compiler_params=_cparams(("arbitrary",)),
        name="in_projection",
    )(x, g_mix, w_in_bf)


def _sink_softmax_pv(s, sink, v_bf):
    m = jnp.maximum(jnp.max(s, axis=-1, keepdims=True), sink)
    p = jnp.exp(s - m)
    denom = jnp.sum(p, axis=-1, keepdims=True) + jnp.exp(sink - m)
    return jnp.dot(p.astype(BF16), v_bf, preferred_element_type=F32) / denom


def _attn_prompt_kernel(sink_ref, q_ref, kc_ref, kp_ref, vc_ref, vp_ref, g_ref, o_ref):
    i = pl.program_id(1)
    q = q_ref[...]
    kk = jnp.concatenate([kp_ref[...], kc_ref[...]], axis=0).astype(BF16)
    vv = jnp.concatenate([vp_ref[...], vc_ref[...]], axis=0).astype(BF16)
    rows = Q_PER_KV * WINDOW
    qi = lax.broadcasted_iota(jnp.int32, (rows, 2 * WINDOW), 0) & (WINDOW - 1)
    kj = lax.broadcasted_iota(jnp.int32, (rows, 2 * WINDOW), 1)
    rel = qi + WINDOW - kj
    mask = (rel >= 0) & (rel < WINDOW) & ((kj >= WINDOW) | (i > 0))
    head_of_row = lax.broadcasted_iota(jnp.int32, (rows, 1), 0) // WINDOW
    low_q = lax.broadcasted_iota(jnp.int32, (WINDOW, LANES), 1) < HEAD_DIM
    zero_q = jnp.zeros((WINDOW, LANES), BF16)
    pairs = []
    for j in range(N_KV_HEADS):
        kh = kk[:, j * HEAD_DIM:(j + 1) * HEAD_DIM]
        vh = vv[:, j * HEAD_DIM:(j + 1) * HEAD_DIM]
        k2 = jnp.concatenate([kh, kh], axis=1)
        v2 = jnp.concatenate([vh, vh], axis=1)
        stacked = []
        for p in range(Q_PER_KV // 2):
            qp = q[:, (j * Q_PER_KV + 2 * p) * HEAD_DIM:(j * Q_PER_KV + 2 * p + 2) * HEAD_DIM]
            stacked += [jnp.where(low_q, qp, zero_q), jnp.where(low_q, zero_q, qp)]
        qs = jnp.concatenate(stacked, axis=0)
        sink = jnp.zeros((rows, 1), F32)
        for g in range(Q_PER_KV):
            sink = jnp.where(head_of_row == g, sink_ref[j * Q_PER_KV + g], sink)
        s = lax.dot_general(qs, k2, (((1,), (1,)), ((), ())), preferred_element_type=F32) * (HEAD_DIM ** -0.5)
        s = jnp.where(mask, s, -jnp.inf)
        o = _sink_softmax_pv(s, sink, v2)
        for p in range(Q_PER_KV // 2):
            pairs.append(jnp.where(low_q, o[2 * p * WINDOW:(2 * p + 1) * WINDOW],
                                   o[(2 * p + 1) * WINDOW:(2 * p + 2) * WINDOW]))
    att = jnp.concatenate(pairs, axis=-1)
    o_ref[...] = _rms(att, g_ref[...]).astype(BF16)


def _attn_prompt(sink, q, k, v, g_attn_out, b, l):
    nb = l // WINDOW
    cur = lambda w: pl.BlockSpec((WINDOW, w), lambda bi, i: (bi * nb + i, 0))
    prev = lambda w: pl.BlockSpec((WINDOW, w), lambda bi, i: (bi * nb + jnp.maximum(i - 1, 0), 0))
    return pl.pallas_call(
        _attn_prompt_kernel,
        grid=(b, nb),
        in_specs=[pl.BlockSpec(memory_space=pltpu.SMEM), cur(ATTN_WIDTH), cur(KV_WIDTH), prev(KV_WIDTH),
                  cur(KV_WIDTH), prev(KV_WIDTH), _full((1, ATTN_WIDTH))],
        out_specs=cur(ATTN_WIDTH),
        out_shape=jax.ShapeDtypeStruct((b * l, ATTN_WIDTH), BF16),
        compiler_params=_cparams(("arbitrary", "arbitrary")),
        name="window_attn_prompt",
    )(sink, q, k, k, v, v, g_attn_out)


def _attn_sample_kernel(sink_ref, q_ref, kn_ref, vn_ref, ck_ref, cv_ref, g_ref, o_ref, *, bt, t):
    nk = WINDOW + t
    sb = ATTN_SEQS
    keys = sb * nk
    cols = Q_PER_KV * sb * t
    kb = lax.broadcasted_iota(jnp.int32, (sb, nk, cols), 0).reshape(keys, cols)
    kk = lax.broadcasted_iota(jnp.int32, (sb, nk, cols), 1).reshape(keys, cols)
    col = lax.broadcasted_iota(jnp.int32, (keys, cols), 1) % (sb * t)
    cb, ct = col // t, col % t
    mask = (kb == cb) & (((kk < WINDOW) & (kk > ct)) | ((kk >= WINDOW) & ((kk - WINDOW) <= ct)))
    head_of_col = lax.broadcasted_iota(jnp.int32, (1, cols), 1) // (sb * t)
    low = lax.broadcasted_iota(jnp.int32, (sb * t, LANES), 1) < HEAD_DIM
    zero_q = jnp.zeros((sb * t, LANES), BF16)
    swap = lambda x: pltpu.roll(x, HEAD_DIM, 1)
    q_all = q_ref[...]
    kn_all = kn_ref[...]
    vn_all = vn_ref[...]
    outs_b = []
    for b0 in range(0, bt, sb):
        r0 = b0 * t
        newk = kn_all[r0:r0 + sb * t].reshape(sb, t, KV_WIDTH)
        newv = vn_all[r0:r0 + sb * t].reshape(sb, t, KV_WIDTH)
        k_all = jnp.concatenate([ck_ref[b0:b0 + sb], newk], axis=1).reshape(keys, KV_WIDTH).astype(BF16)
        v_all = jnp.concatenate([cv_ref[b0:b0 + sb], newv], axis=1).reshape(keys, KV_WIDTH).astype(BF16)
        head_out = []
        for j in range(N_KV_HEADS):
            mine = low if j == 0 else jnp.logical_not(low)
            qs = []
            for g in range(Q_PER_KV):
                h = j * Q_PER_KV + g
                tile = q_all[r0:r0 + sb * t, (h // 2) * LANES:(h // 2 + 1) * LANES]
                if h % 2 != j:
                    tile = swap(tile.astype(F32)).astype(BF16)
                qs.append(jnp.where(mine, tile, zero_q))
            qj = jnp.concatenate(qs, axis=0)
            sink = jnp.zeros((1, cols), F32)
            for g in range(Q_PER_KV):
                sink = jnp.where(head_of_col == g, sink_ref[j * Q_PER_KV + g], sink)
            s = lax.dot_general(k_all, qj, (((1,), (1,)), ((), ())), preferred_element_type=F32) * (HEAD_DIM ** -0.5)
            s = jnp.where(mask, s, -jnp.inf)
            m = jnp.maximum(jnp.max(s, axis=0, keepdims=True), sink)
            p = jnp.exp(s - m)
            p = p / (jnp.sum(p, axis=0, keepdims=True) + jnp.exp(sink - m))
            o = lax.dot_general(p.astype(BF16), v_all, (((0,), (0,)), ((), ())), preferred_element_type=F32)
            for g in range(Q_PER_KV):
                og = o[g * sb * t:(g + 1) * sb * t]
                head_out.append(og if g % 2 == j else swap(og))
        tiles = [jnp.where(low, head_out[2 * pr], head_out[2 * pr + 1]) for pr in range(N_Q_HEADS // 2)]
        outs_b.append(jnp.concatenate(tiles, axis=-1))
    att = jnp.concatenate(outs_b, axis=0)
    o_ref[...] = _rms(att, g_ref[...]).astype(BF16)


def _attn_sample(sink, q, k, v, cache_k, cache_v, g_attn_out, b, t, bt):
    row = lambda w: pl.BlockSpec((bt * t, w), lambda i: (i, 0))
    cache = pl.BlockSpec((bt, WINDOW, KV_WIDTH), lambda i: (i, 0, 0))
    return pl.pallas_call(
        functools.partial(_attn_sample_kernel, bt=bt, t=t),
        grid=(b // bt,),
        in_specs=[pl.BlockSpec(memory_space=pltpu.SMEM), row(ATTN_WIDTH), row(KV_WIDTH), row(KV_WIDTH),
                  cache, cache, _full((1, ATTN_WIDTH))],
        out_specs=row(ATTN_WIDTH),
        out_shape=jax.ShapeDtypeStruct((b * t, ATTN_WIDTH), BF16),
        compiler_params=_cparams(("arbitrary",)),
        name="window_attn_sample",
    )(sink, q, k, v, cache_k, cache_v, g_attn_out)


def _ssm_project(u_ref, perm_ref, wb_ref, vre, vim, utb, r):
    u = u_ref[...].reshape(r, SSM_WIDTH)
    u_hi = u.astype(BF16)
    u_lo = (u - u_hi.astype(F32)).astype(BF16)
    perm = perm_ref[...]
    u_tb = (jnp.dot(perm, u_hi, preferred_element_type=F32) + jnp.dot(perm, u_lo, preferred_element_type=F32))
    utb[...] = u_tb
    u_bf = u_tb.astype(BF16)
    half_w = SSM_WIDTH // 2
    for h in range(2):
        vv = jnp.dot(u_bf[:, h * half_w:(h + 1) * half_w], wb_ref[h], preferred_element_type=F32)
        vre[:, h * SSM_HALF:(h + 1) * SSM_HALF] = vv[:, :SSM_HALF]
        vim[:, h * SSM_HALF:(h + 1) * SSM_HALF] = vv[:, SSM_HALF:]


def _ssm_scan(are_ref, aim_ref, vre, vim, sre, sim, nb, s):
    lane_chunk = SSM_LANES // 2
    for rg in range(nb // SUBLANES):
        for lc in range(SSM_LANES // lane_chunk):
            lanes = slice(lc * lane_chunk, (lc + 1) * lane_chunk)
            rows0 = slice(rg * SUBLANES, (rg + 1) * SUBLANES)
            ar = jnp.broadcast_to(are_ref[:, lanes], (SUBLANES, lane_chunk))
            ai = jnp.broadcast_to(aim_ref[:, lanes], (SUBLANES, lane_chunk))

            def step(t, carry, lanes=lanes, rg=rg, ar=ar, ai=ai):
                hr, hi = carry
                row = pl.multiple_of(t * nb + rg * SUBLANES, SUBLANES)
                nhr = ar * hr - ai * hi + vre[pl.ds(row, SUBLANES), lanes]
                nhi = ar * hi + ai * hr + vim[pl.ds(row, SUBLANES), lanes]
                vre[pl.ds(row, SUBLANES), lanes] = nhr
                vim[pl.ds(row, SUBLANES), lanes] = nhi
                return nhr, nhi

            hr, hi = lax.fori_loop(0, s, step, (sre[rows0, lanes], sim[rows0, lanes]), unroll=True)
            sre[rows0, lanes] = hr
            sim[rows0, lanes] = hi


def _ssm_output(vre, vim, utb, permt_ref, wc_ref, d_ref, wglu_ref, g_ref, o_ref, nb, s):
    ys = []
    for h in range(2):
        hcat = jnp.concatenate([vre[:, h * SSM_HALF:(h + 1) * SSM_HALF], vim[:, h * SSM_HALF:(h + 1) * SSM_HALF]],
                               axis=1).astype(BF16)
        ys.append(jnp.dot(hcat, wc_ref[h], preferred_element_type=F32))
    y = jnp.concatenate(ys, axis=1) + d_ref[...] * utb[...]
    g = jax.nn.gelu(y)
    z = jnp.dot(g.astype(BF16), wglu_ref[...], preferred_element_type=F32)
    out = g * jax.nn.sigmoid(z)
    on = _rms(out, g_ref[...]).astype(BF16)
    o_bt = jnp.dot(permt_ref[...], on, preferred_element_type=F32).astype(BF16)
    o_ref[...] = o_bt.reshape(nb, s, SSM_WIDTH)


def _ssm_kernel(u_ref, perm_ref, permt_ref, wb_ref, wc_ref, are_ref, aim_ref, d_ref, wglu_ref, g_ref,
                h0re_ref, h0im_ref, o_ref, hre_ref, him_ref, vre, vim, utb, sre, sim, *, nb, s):
    c = pl.program_id(0)

    @pl.when(c == 0)
    def _():
        sre[...] = h0re_ref[...]
        sim[...] = h0im_ref[...]

    _ssm_project(u_ref, perm_ref, wb_ref, vre, vim, utb, nb * s)
    _ssm_scan(are_ref, aim_ref, vre, vim, sre, sim, nb, s)
    _ssm_output(vre, vim, utb, permt_ref, wc_ref, d_ref, wglu_ref, g_ref, o_ref, nb, s)

    @pl.when(c == pl.num_programs(0) - 1)
    def _():
        hre_ref[...] = sre[...]
        him_ref[...] = sim[...]


def _ssm(u, h0_re, h0_im, ssm_w, s):
    nb, l, _ = u.shape
    r = nb * s
    ridx = jnp.arange(r)
    src = (ridx % nb) * s + ridx // nb
    perm = (src[:, None] == ridx[None, :]).astype(BF16)
    wb, wc, a_re, a_im, d_skip, w_glu, g_out = ssm_w
    in_blk = out_blk = pl.BlockSpec((nb, s, SSM_WIDTH), lambda c: (0, c, 0))
    chunk_bufs = [pltpu.VMEM((r, SSM_LANES), F32), pltpu.VMEM((r, SSM_LANES), F32), pltpu.VMEM((r, SSM_WIDTH), F32)]
    st_blk = _full((nb, SSM_LANES))
    return pl.pallas_call(
        functools.partial(_ssm_kernel, nb=nb, s=s),
        grid=(l // s,),
        in_specs=[in_blk, _full((r, r)), _full((r, r)), _full(wb.shape), _full(wc.shape),
                  _full((1, SSM_LANES)), _full((1, SSM_LANES)), _full((1, SSM_WIDTH)),
                  _full((SSM_WIDTH, SSM_WIDTH)), _full((1, SSM_WIDTH)), st_blk, st_blk],
        out_specs=(out_blk, st_blk, st_blk),
        out_shape=(jax.ShapeDtypeStruct((nb, l, SSM_WIDTH), BF16),
                   jax.ShapeDtypeStruct((nb, SSM_LANES), F32), jax.ShapeDtypeStruct((nb, SSM_LANES), F32)),
        scratch_shapes=chunk_bufs + [pltpu.VMEM((nb, SSM_LANES), F32), pltpu.VMEM((nb, SSM_LANES), F32)],
        compiler_params=_cparams(("arbitrary",), VMEM_LIMIT),
        name="ssm_scan",
    )(u, perm, perm.T, wb, wc, a_re, a_im, d_skip, w_glu, g_out, h0_re, h0_im)


def _outproj_kernel(x_ref, att_ref, ssm_ref, wa_ref, ws_ref, gca_ref, wcq_ref, h_ref, qc_ref):
    h = (x_ref[...] + jnp.dot(att_ref[...], wa_ref[...], preferred_element_type=F32)
         + jnp.dot(ssm_ref[...], ws_ref[...], preferred_element_type=F32))
    h_ref[...] = h
    xn = _rms(h, gca_ref[...]).astype(BF16)
    qc_ref[...] = jnp.dot(xn, wcq_ref[...], preferred_element_type=F32).astype(BF16)


def _outproj(x, att, ssm, w_att, w_ssm, g_ca, w_cq, tm):
    t = x.shape[0]
    row = lambda w: pl.BlockSpec((tm, w), lambda i: (i, 0))
    return pl.pallas_call(
        _outproj_kernel,
        grid=(t // tm,),
        in_specs=[row(D_MODEL), row(ATTN_WIDTH), row(SSM_WIDTH), _full((ATTN_WIDTH, D_MODEL)),
                  _full((SSM_WIDTH, D_MODEL)), _full((1, D_MODEL)), _full((D_MODEL, CA_WIDTH))],
        out_specs=(row(D_MODEL), row(CA_WIDTH)),
        out_shape=(jax.ShapeDtypeStruct((t, D_MODEL), F32), jax.ShapeDtypeStruct((t, CA_WIDTH), BF16)),
        compiler_params=_cparams(("arbitrary",)),
        name="out_projection",
    )(x, att, ssm, w_att, w_ssm, g_ca, w_cq)


def _memkv_kernel(m_ref, g_ref, w_ref, k_ref, v_ref):
    m = _rms(m_ref[...], g_ref[...]).astype(BF16)
    kv = jnp.dot(m, w_ref[...], preferred_element_type=F32)
    k_ref[...] = kv[:, :CA_WIDTH]
    v_ref[...] = kv[:, CA_WIDTH:]


def _memkv(mem, g_mem, w_mkv_bf, tm):
    t = mem.shape[0]
    row = lambda w: pl.BlockSpec((tm, w), lambda i: (i, 0))
    return pl.pallas_call(
        _memkv_kernel,
        grid=(t // tm,),
        in_specs=[row(D_MODEL), _full((1, D_MODEL)), _full((D_MODEL, 2 * CA_WIDTH))],
        out_specs=(row(CA_WIDTH), row(CA_WIDTH)),
        out_shape=(jax.ShapeDtypeStruct((t, CA_WIDTH), F32), jax.ShapeDtypeStruct((t, CA_WIDTH), F32)),
        compiler_params=_cparams(("arbitrary",)),
        name="memory_kv",
    )(mem, g_mem, w_mkv_bf)


def _xattn_kernel(qc_ref, mk_ref, mv_ref, h_ref, wco_ref, gmoe_ref, wr_ref, br_ref, tri_ref, cin_ref, *refs,
                  nbat, rows, head_split, has_prev, tile_off):
    refs = refs[1:] if has_prev else refs
    h2_ref, xn_hbm, ti_ref, tw_ref, cout_ref, cnt, xn_buf, xn_sem = refs[:8]
    step = pl.program_id(0)
    nsteps = pl.num_programs(0)
    tm = nbat * rows

    def xn_copies(s):
        dst0 = tile_off + s * (tm // SUBLANES)
        return [pltpu.make_async_copy(xn_buf.at[:, :, sl, :], xn_hbm.at[pl.ds(dst0, tm // SUBLANES), sl], xn_sem)
                for sl in range(SUBLANES)]

    @pl.when(step == 0)
    def _():
        cnt[...] = cin_ref[...]

    if head_split:
        mbuf, msem = refs[8:]
        slot = step % 2

        def head_copies(s, sl):
            b0 = s * nbat
            return [pltpu.make_async_copy(src.at[pl.ds(b0, nbat), :, hd, :], mbuf.at[sl, kv, hd], msem.at[sl])
                    for kv, src in enumerate((mk_ref, mv_ref)) for hd in range(CA_HEADS)]

        @pl.when(step == 0)
        def _():
            for c in head_copies(0, 0):
                c.start()

        @pl.when(step + 1 < nsteps)
        def _():
            for c in head_copies(step + 1, 1 - slot):
                c.start()

        for c in head_copies(step, slot):
            c.wait()

    q_all = qc_ref[...]
    if head_split:
        keys = nbat * N_MEM
        own = (lax.broadcasted_iota(jnp.int32, (keys, tm), 0) // N_MEM
               == lax.broadcasted_iota(jnp.int32, (keys, tm), 1) // rows)
        outs = []
        for hd in range(CA_HEADS):
            sl = slice(hd * CA_HEAD_DIM, (hd + 1) * CA_HEAD_DIM)
            mk = mbuf[slot, 0, hd].reshape(keys, CA_HEAD_DIM).astype(BF16)
            mv = mbuf[slot, 1, hd].reshape(keys, CA_HEAD_DIM).astype(BF16)
            s = lax.dot_general(mk, q_all[:, sl], (((1,), (1,)), ((), ())),
                                preferred_element_type=F32) * (CA_HEAD_DIM ** -0.5)
            s = jnp.where(own, s, -jnp.inf)
            p = jnp.exp(s - jnp.max(s, axis=0, keepdims=True))
            p = p / jnp.sum(p, axis=0, keepdims=True)
            outs.append(lax.dot_general(p.astype(BF16), mv, (((0,), (0,)), ((), ())), preferred_element_type=F32))
        o = jnp.concatenate(outs, axis=-1).astype(BF16)
    else:
        outs_b = []
        for bi in range(nbat):
            q = q_all[bi * rows:(bi + 1) * rows]
            outs = []
            for hd in range(CA_HEADS):
                sl = slice(hd * CA_HEAD_DIM, (hd + 1) * CA_HEAD_DIM)
                mk = mk_ref[bi, :, sl].astype(BF16)
                mv = mv_ref[bi, :, sl].astype(BF16)
                s = lax.dot_general(q[:, sl], mk, (((1,), (1,)), ((), ())),
                                    preferred_element_type=F32) * (CA_HEAD_DIM ** -0.5)
                m = jnp.max(s, axis=-1, keepdims=True)
                p = jnp.exp(s - m)
                denom = jnp.sum(p, axis=-1, keepdims=True)
                outs.append(jnp.dot(p.astype(BF16), mv, preferred_element_type=F32) / denom)
            outs_b.append(jnp.concatenate(outs, axis=-1))
        o = jnp.concatenate(outs_b, axis=0).astype(BF16)
    h2 = h_ref[...] + jnp.dot(o, wco_ref[...], preferred_element_type=F32)
    h2_ref[...] = h2
    xn = _rms(h2, gmoe_ref[...])
    @pl.when(step >= 1)
    def _():
        for c in xn_copies(step - 1):
            c.wait()

    _store_row_tiles(xn_buf, xn)
    for c in xn_copies(step):
        c.start()

    @pl.when(step == nsteps - 1)
    def _():
        for c in xn_copies(step):
            c.wait()

    logits = jnp.dot(xn.astype(BF16), wr_ref[...], preferred_element_type=F32) + br_ref[...]
    n = logits.shape[0]
    eidx = lax.broadcasted_iota(jnp.int32, (n, N_EXPERTS), 1).astype(F32)
    lane = lax.broadcasted_iota(jnp.int32, (n, LANES), 1)
    ti = jnp.zeros((n, LANES), F32)
    tv = jnp.zeros((n, LANES), F32)
    work = logits
    vals = []
    hits = []
    for k in range(TOP_K):
        mx = jnp.max(work, axis=-1, keepdims=True)
        idx = jnp.min(jnp.where(work == mx, eidx, float(N_EXPERTS)), axis=-1, keepdims=True)
        vals.append(mx)
        hits.append(eidx == idx)
        ti = jnp.where(lane == k, idx, ti)
        work = jnp.where(hits[k], -jnp.inf, work)
    ex = [jnp.exp(v - vals[0]) for v in vals]
    tot = ex[0] + ex[1] + ex[2] + ex[3]
    for k in range(TOP_K):
        tv = jnp.where(lane == k, ex[k] / tot, tv)
    sel = jnp.zeros((n, N_EXPERTS), F32)
    for k in range(TOP_K):
        sel = sel + jnp.where(hits[k], 1.0, 0.0)
    before = jnp.dot(tri_ref[...], sel.astype(BF16), preferred_element_type=F32) + cnt[...]
    for k in range(TOP_K):
        rank = jnp.sum(jnp.where(hits[k], before, 0.0), axis=-1, keepdims=True)
        ti = jnp.where(lane == TOP_K + k, rank, ti)
    cnt[...] = cnt[...] + jnp.sum(sel, axis=0, keepdims=True)
    ti_ref[...] = ti.astype(jnp.int32)
    tw_ref[...] = tv

    @pl.when(step == pl.num_programs(0) - 1)
    def _():
        cout_ref[...] = cnt[...]


def _xattn(qc, mk, mv, h1, w_co, g_moe, w_router, b_router, counts_in, nbat, rows, t_total, t_off, xn_prev=None):
    t = qc.shape[0]
    tm = nbat * rows
    row = lambda w: pl.BlockSpec((tm, w), lambda i: (i, 0))
    seq_rows = t // mk.shape[0]
    head_split = mk.ndim == 4
    scratch = [pltpu.VMEM((1, N_EXPERTS), F32), pltpu.VMEM((tm // SUBLANES, LANE_TILES, SUBLANES, LANES), F32),
               pltpu.SemaphoreType.DMA(())]
    any_spec = pl.BlockSpec(memory_space=pl.ANY)
    prev_args, prev_specs, aliases = [], [], {}
    if xn_prev is not None:
        prev_args, prev_specs, aliases = [xn_prev], [any_spec], {10: 1}
    if head_split:
        assert seq_rows == rows
        mem = pl.BlockSpec(memory_space=pl.ANY)
        scratch += [pltpu.VMEM((2, 2, CA_HEADS, nbat, N_MEM, CA_HEAD_DIM), F32), pltpu.SemaphoreType.DMA((2,))]
    else:
        mem = pl.BlockSpec((nbat, N_MEM, CA_WIDTH), lambda i: ((i * tm) // (seq_rows * nbat), 0, 0))
    tri = jnp.tri(tm, k=-1, dtype=BF16)
    return pl.pallas_call(
        functools.partial(_xattn_kernel, nbat=nbat, rows=rows, head_split=head_split, has_prev=xn_prev is not None,
                          tile_off=t_off // SUBLANES),
        grid=(t // tm,),
        in_specs=[row(CA_WIDTH), mem, mem, row(D_MODEL), _full((CA_WIDTH, D_MODEL)), _full((1, D_MODEL)),
                  _full((D_MODEL, N_EXPERTS)), _full((1, N_EXPERTS)), _full((tm, tm)), _full((1, N_EXPERTS))]
        + prev_specs,
        out_specs=(row(D_MODEL), any_spec, row(LANES), row(LANES), _full((1, N_EXPERTS))),
        out_shape=(jax.ShapeDtypeStruct((t, D_MODEL), F32),
                   jax.ShapeDtypeStruct((t_total // SUBLANES, SUBLANES, LANE_TILES, LANES), F32),
                   jax.ShapeDtypeStruct((t, LANES), jnp.int32), jax.ShapeDtypeStruct((t, LANES), F32),
                   jax.ShapeDtypeStruct((1, N_EXPERTS), F32)),
        scratch_shapes=scratch,
        input_output_aliases=aliases,
        compiler_params=_cparams(("arbitrary",), VMEM_LIMIT),
        name="cross_attn_router",
    )(qc, mk, mv, h1, w_co, g_moe, w_router, b_router, tri, counts_in, *prev_args)


def _sc_gather(table, idx):
    n = idx.shape[0]
    per_worker = n // SC_WORKERS
    pieces = per_worker // SC_ROWS
    assert pieces * SC_ROWS * SC_WORKERS == n
    mesh = plsc.VectorSubcoreMesh(core_axis_name="c", subcore_axis_name="s")

    @functools.partial(pl.kernel, mesh=mesh, out_type=jax.ShapeDtypeStruct((n, LANE_TILES, LANES), F32),
                       scratch_types=[pltpu.VMEM((SC_ROWS,), jnp.int32), pltpu.VMEM((SC_ROWS, LANE_TILES, LANES), F32),
                                      pltpu.SemaphoreType.DMA],
                       name="moe_row_gather")
    def gather(table_hbm, idx_hbm, out_hbm, idx_v, rows_v, sem):
        worker = lax.axis_index("s") * SC_CORES + lax.axis_index("c")
        base = worker * per_worker

        @pl.loop(0, pieces)
        def _(piece):
            off = base + piece * SC_ROWS
            pltpu.sync_copy(idx_hbm.at[pl.ds(off, SC_ROWS)], idx_v)
            pltpu.async_copy(table_hbm.at[idx_v], rows_v, sem).wait()
            pltpu.sync_copy(rows_v, out_hbm.at[pl.ds(off, SC_ROWS)])

    return gather(table, idx)


def _expert_block_kernel(blk_e_ref, w_idx_ref, nused_ref, xs_hbm, wg_ref, bg_ref, wu_ref, bu_ref, wd_ref, bd_ref,
                         ys_hbm, wg_bf, wu_bf, wd_bf, xbuf, ybuf, xsem, ysem):
    i = pl.program_id(0)
    n_used = nused_ref[0]
    active = i < n_used
    slot = i % 2
    tiles = MOE_BLOCK // SUBLANES

    def x_copies(blk, sl):
        return [pltpu.make_async_copy(xs_hbm.at[pl.ds(blk * tiles, tiles), s], xbuf.at[sl, :, :, s, :], xsem.at[sl])
                for s in range(SUBLANES)]

    def y_copies(blk, sl):
        return [pltpu.make_async_copy(ybuf.at[sl, :, :, s, :], ys_hbm.at[pl.ds(blk * tiles, tiles), s], ysem.at[sl])
                for s in range(SUBLANES)]

    @pl.when((i == 0) & active)
    def _():
        for c in x_copies(0, 0):
            c.start()

    prev_e = blk_e_ref[jnp.maximum(i - 1, 0)]
    new_expert = (i == 0) | (blk_e_ref[i] != prev_e)

    @pl.when(active & new_expert)
    def _():
        wg_bf[...] = wg_ref[0].astype(BF16)
        wu_bf[...] = wu_ref[0].astype(BF16)
        wd_bf[...] = wd_ref[0].astype(BF16)

    @pl.when(active)
    def _():
        for c in x_copies(i, slot):
            c.wait()

        @pl.when(i + 1 < n_used)
        def _():
            for c in x_copies(i + 1, 1 - slot):
                c.start()

        x = _load_row_tiles(xbuf.at[slot]).astype(BF16)
        gate = jnp.minimum(jnp.dot(x, wg_bf[...], preferred_element_type=F32) + bg_ref[0], SWIGLU_LIMIT)
        up = jnp.clip(jnp.dot(x, wu_bf[...], preferred_element_type=F32) + bu_ref[0], -SWIGLU_LIMIT, SWIGLU_LIMIT)
        h = gate * jax.nn.sigmoid(SWIGLU_ALPHA * gate) * (up + 1.0)
        y = jnp.dot(h.astype(BF16), wd_bf[...], preferred_element_type=F32) + bd_ref[0]

        @pl.when(i >= 1)
        def _():
            for c in y_copies(i - 1, 1 - slot):
                c.wait()

        _store_row_tiles(ybuf.at[slot], y)
        for c in y_copies(i, slot):
            c.start()

        @pl.when(i == n_used - 1)
        def _():
            for c in y_copies(i, slot):
                c.wait()


def _experts_blocks(blk_e, w_idx, n_used, xs, w_gate, b_gate, w_up, b_up, w_down, b_down):
    n_slots = xs.shape[0]
    n_blocks = n_slots // MOE_BLOCK
    d_ff = w_gate.shape[-1]
    tiles = MOE_BLOCK // SUBLANES
    wspec = lambda a, b: pl.BlockSpec((1, a, b), lambda i, be, wi, nu: (wi[i], 0, 0))
    bspec = lambda b: pl.BlockSpec((1, 1, b), lambda i, be, wi, nu: (be[i], 0, 0))
    any_spec = pl.BlockSpec(memory_space=pl.ANY)
    grid_spec = pltpu.PrefetchScalarGridSpec(
        num_scalar_prefetch=3,
        grid=(n_blocks,),
        in_specs=[any_spec, wspec(D_MODEL, d_ff), bspec(d_ff), wspec(D_MODEL, d_ff), bspec(d_ff),
                  wspec(d_ff, D_MODEL), bspec(D_MODEL)],
        out_specs=any_spec,
        scratch_shapes=[pltpu.VMEM((D_MODEL, d_ff), BF16), pltpu.VMEM((D_MODEL, d_ff), BF16),
                        pltpu.VMEM((d_ff, D_MODEL), BF16),
                        pltpu.VMEM((2, tiles, LANE_TILES, SUBLANES, LANES), F32),
                        pltpu.VMEM((2, tiles, LANE_TILES, SUBLANES, LANES), F32),
                        pltpu.SemaphoreType.DMA((2,)), pltpu.SemaphoreType.DMA((2,))],
    )
    ys = pl.pallas_call(
        _expert_block_kernel,
        grid_spec=grid_spec,
        out_shape=jax.ShapeDtypeStruct((n_slots // SUBLANES, SUBLANES, LANE_TILES, LANES), F32),
        compiler_params=_cparams(("arbitrary",), VMEM_LIMIT),
        name="moe_experts",
    )(blk_e, w_idx, n_used, xs.reshape(n_slots // SUBLANES, SUBLANES, LANE_TILES, LANES), w_gate,
      b_gate.reshape(N_EXPERTS, 1, d_ff), w_up, b_up.reshape(N_EXPERTS, 1, d_ff), w_down,
      b_down.reshape(N_EXPERTS, 1, D_MODEL))
    return ys.reshape(n_slots, LANE_TILES, LANES)


def _expert_kernel(blk_e_ref, w_idx_ref, nused_ref, tab_hbm, xn_hbm, wg_ref, bg_ref, wu_ref, bu_ref, wd_ref, bd_ref,
                   yu_hbm, wg_bf, wu_bf, wd_bf, xbuf, ybuf, tab, gsem, ssem, tsem, *, n_blocks):
    i = pl.program_id(0)
    n_used = nused_ref[0]
    active = i < n_used
    slot = i % 2
    row_w = 2 * MOE_BLOCK

    def ring(b):
        return (b + TAB_RING) % TAB_RING

    def tab_copy(b):
        r = ring(b)
        return pltpu.make_async_copy(tab_hbm.at[jnp.minimum(b, n_blocks - 1) + 1], tab.at[pl.ds(r * row_w, row_w)],
                                     tsem.at[r])

    def block_wait(ref, sem):
        pltpu.make_async_copy(ref.at[pl.ds(0, MOE_BLOCK)], ref.at[pl.ds(0, MOE_BLOCK)], sem).wait()

    def issue_rows(off, copy):
        def body(it, carry):
            ids = [tab[off + it * DMA_BATCH + q] for q in range(DMA_BATCH)]
            for q in range(DMA_BATCH):
                copy(ids[q], it * (DMA_BATCH // SUBLANES) + q // SUBLANES, q % SUBLANES).start(priority=q % 2)
            return carry
        lax.fori_loop(0, MOE_BLOCK // DMA_BATCH, body, 0)

    def issue_gathers(b, sl):
        issue_rows(ring(b) * row_w,
                   lambda tok, j, s: pltpu.make_async_copy(xn_hbm.at[tok], xbuf.at[sl, j, :, s, :], gsem.at[sl]))

    def issue_scatters(b, sl):
        issue_rows(ring(b) * row_w + MOE_BLOCK,
                   lambda tile, j, s: pltpu.make_async_copy(ybuf.at[sl, j, :, s, :], yu_hbm.at[tile], ssem))

    @pl.when((i == 0) & active)
    def _():
        ybuf[...] = jnp.zeros(ybuf.shape, F32)
        for b in (-1, 0, 1):
            tab_copy(b).start()
            tab_copy(b).wait()
        tab_copy(2).start()
        issue_gathers(0, 0)

    prev_e = blk_e_ref[jnp.maximum(i - 1, 0)]
    new_expert = (i == 0) | (blk_e_ref[i] != prev_e)

    @pl.when(active & new_expert)
    def _():
        wg_bf[...] = wg_ref[0].astype(BF16)
        wu_bf[...] = wu_ref[0].astype(BF16)
        wd_bf[...] = wd_ref[0].astype(BF16)

    @pl.when(active)
    def _():
        block_wait(xn_hbm, gsem.at[slot])

        @pl.when(i >= 1)
        def _():
            tab_copy(i + 1).wait()
            tab_copy(i + 2).start()

        issue_gathers(i + 1, 1 - slot)
        issue_scatters(i - 1, 1 - slot)
        x = _load_row_tiles(xbuf.at[slot]).astype(BF16)
        gate = jnp.minimum(jnp.dot(x, wg_bf[...], preferred_element_type=F32) + bg_ref[0], SWIGLU_LIMIT)
        up = jnp.clip(jnp.dot(x, wu_bf[...], preferred_element_type=F32) + bu_ref[0], -SWIGLU_LIMIT, SWIGLU_LIMIT)
        h = gate * jax.nn.sigmoid(SWIGLU_ALPHA * gate) * (up + 1.0)
        y = jnp.dot(h.astype(BF16), wd_bf[...], preferred_element_type=F32) + bd_ref[0]
        _store_row_tiles(ybuf.at[slot], y)
        block_wait(yu_hbm, ssem)

        @pl.when(i == n_used - 1)
        def _():
            issue_scatters(i, slot)
            block_wait(yu_hbm, ssem)
            block_wait(xn_hbm, gsem.at[1 - slot])
            tab_copy(i + 2).wait()


def _experts(blk_e, w_idx, n_used, tab, xn, n_out_tiles, w_gate, b_gate, w_up, b_up, w_down, b_down):
    n_blocks = tab.shape[0] - 1
    d_ff = w_gate.shape[-1]
    tiles = MOE_BLOCK // SUBLANES
    wspec = lambda a, b: pl.BlockSpec((1, a, b), lambda i, be, wi, nu: (wi[i], 0, 0))
    bspec = lambda b: pl.BlockSpec((1, 1, b), lambda i, be, wi, nu: (be[i], 0, 0))
    any_spec = pl.BlockSpec(memory_space=pl.ANY)
    grid_spec = pltpu.PrefetchScalarGridSpec(
        num_scalar_prefetch=3,
        grid=(n_blocks,),
        in_specs=[any_spec, any_spec, wspec(D_MODEL, d_ff), bspec(d_ff), wspec(D_MODEL, d_ff), bspec(d_ff),
                  wspec(d_ff, D_MODEL), bspec(D_MODEL)],
        out_specs=any_spec,
        scratch_shapes=[pltpu.VMEM((D_MODEL, d_ff), BF16), pltpu.VMEM((D_MODEL, d_ff), BF16),
                        pltpu.VMEM((d_ff, D_MODEL), BF16),
                        pltpu.VMEM((2, tiles, LANE_TILES, SUBLANES, LANES), F32),
                        pltpu.VMEM((2, tiles, LANE_TILES, SUBLANES, LANES), F32),
                        pltpu.SMEM((TAB_RING * 2 * MOE_BLOCK,), jnp.int32),
                        pltpu.SemaphoreType.DMA((2,)), pltpu.SemaphoreType.DMA(()),
                        pltpu.SemaphoreType.DMA((TAB_RING,))],
    )
    return pl.pallas_call(
        functools.partial(_expert_kernel, n_blocks=n_blocks),
        grid_spec=grid_spec,
        out_shape=jax.ShapeDtypeStruct((n_out_tiles, LANE_TILES, LANES), F32),
        compiler_params=_cparams(("arbitrary",), VMEM_LIMIT),
        name="moe_experts",
    )(blk_e, w_idx, n_used, tab, xn, w_gate, b_gate.reshape(N_EXPERTS, 1, d_ff), w_up,
      b_up.reshape(N_EXPERTS, 1, d_ff), w_down, b_down.reshape(N_EXPERTS, 1, D_MODEL))


def _combine_kernel(yu_hbm, h_ref, tw_ref, g_ref, o_ref, buf, sem, *, tm, tile_off):
    i = pl.program_id(0)
    n = pl.num_programs(0)
    slot = i % 2
    tiles = tm // SUBLANES

    def copies(step, sl):
        t0 = tile_off + step * tiles
        return [pltpu.make_async_copy(yu_hbm.at[pl.ds(t0, tiles), s, k], buf.at[sl, k, :, :, s, :], sem.at[sl])
                for s in range(SUBLANES) for k in range(TOP_K)]

    @pl.when(i == 0)
    def _():
        for c in copies(0, 0):
            c.start()

    @pl.when(i + 1 < n)
    def _():
        for c in copies(i + 1, 1 - slot):
            c.start()

    for c in copies(i, slot):
        c.wait()
    tw = tw_ref[...]
    y = h_ref[...]
    for k in range(TOP_K):
        y = y + tw[:, k:k + 1] * _load_row_tiles(buf.at[slot, k])
    o_ref[...] = _rms(y, g_ref[...])


def _combine(yu, h2, tw, g_final, tm, t_off):
    t = h2.shape[0]
    row = lambda w: pl.BlockSpec((tm, w), lambda i: (i, 0))
    return pl.pallas_call(
        functools.partial(_combine_kernel, tm=tm, tile_off=t_off // SUBLANES),
        grid=(t // tm,),
        in_specs=[pl.BlockSpec(memory_space=pl.ANY), row(D_MODEL), row(LANES), _full((1, D_MODEL))],
        out_specs=row(D_MODEL),
        out_shape=jax.ShapeDtypeStruct((t, D_MODEL), F32),
        scratch_shapes=[pltpu.VMEM((2, TOP_K, tm // SUBLANES, LANE_TILES, SUBLANES, LANES), F32),
                        pltpu.SemaphoreType.DMA((2,))],
        compiler_params=_cparams(("arbitrary",), VMEM_LIMIT),
        name="moe_combine",
    )(yu, h2, tw, g_final)


def _route(counts, m):
    counts = counts.reshape(N_EXPERTS).astype(jnp.int32)
    padded = (counts + MOE_BLOCK - 1) // MOE_BLOCK * MOE_BLOCK
    pad_end = jnp.cumsum(padded)
    pad_start = pad_end - padded
    n_blocks = -(-(m + N_EXPERTS * (MOE_BLOCK - 1)) // MOE_BLOCK)
    blk_start = jnp.arange(n_blocks, dtype=jnp.int32) * MOE_BLOCK
    owner = lambda start: jnp.minimum(jnp.sum((pad_end[None, :] <= start[:, None]).astype(jnp.int32), axis=1),
                                      N_EXPERTS - 1).astype(jnp.int32)
    n_used = pad_end[-1] // MOE_BLOCK
    last_start = jnp.maximum(n_used - 1, 0) * MOE_BLOCK
    blk_e = owner(jnp.minimum(blk_start, last_start))
    first = jnp.concatenate([jnp.ones((1,), bool), blk_e[1:] != blk_e[:-1]])
    region_end = jnp.sum(jnp.where(blk_e[:, None] == jnp.arange(N_EXPERTS)[None, :], pad_end[None, :], 0), axis=1)
    w_idx = jnp.where(first, blk_e, owner(jnp.minimum(region_end, last_start))).astype(jnp.int32)
    return pad_start, blk_e, w_idx, n_used.astype(jnp.int32).reshape(1), n_blocks * MOE_BLOCK


def _dest(ti, pad_start):
    e = ti[:, :TOP_K]
    onehot = e[:, :, None] == jnp.arange(N_EXPERTS, dtype=jnp.int32)[None, None, :]
    start = jnp.sum(jnp.where(onehot, pad_start[None, None, :], 0), axis=-1)
    return (start + ti[:, TOP_K:2 * TOP_K]).reshape(-1)


def _slot_table(dest, n_slots):
    m = dest.shape[0]
    n_blocks = n_slots // MOE_BLOCK
    filled_by = jnp.zeros((n_slots,), jnp.int32).at[dest].add(jnp.arange(m, dtype=jnp.int32) + 1) - 1
    filled = filled_by >= 0
    src_tok = jnp.where(filled, filled_by, 0) // TOP_K
    dst_tile = jnp.where(filled, filled_by, m + jnp.arange(n_slots, dtype=jnp.int32) % DUMP_TILES)
    rows = jnp.concatenate([src_tok.reshape(n_blocks, MOE_BLOCK), dst_tile.reshape(n_blocks, MOE_BLOCK)], axis=1)
    before_first = jnp.concatenate([jnp.zeros((MOE_BLOCK,), jnp.int32),
                                    m + MOE_BLOCK + jnp.arange(MOE_BLOCK, dtype=jnp.int32)])[None]
    return jnp.concatenate([before_first, rows], axis=0).astype(jnp.int32)


def _block_diag(blocks):
    g, a, b = blocks.shape
    eye = jnp.eye(g, dtype=blocks.dtype)
    return (blocks[:, :, None, :] * eye[:, None, :, None]).reshape(g * a, g * b)


def kernel(x_prompt, x_sample, mem_prompt, cache_win_k, cache_win_v, state_ssm_re, state_ssm_im, cache_mem_k,
           cache_mem_v, g_mix, w_in, sink, lam_re, lam_im, log_dt, b_re, b_im, c_re, c_im, d_skip, w_glu,
           g_attn_out, g_ssm_out, w_out, g_ca, g_mem, w_mk, w_mv, w_cq, w_co, g_moe, w_router, b_router,
           w_gate, b_gate, w_up, b_up, w_down, b_down, g_final):
    depth = g_mix.shape[0]
    assert depth == 1
    bp, lp, _ = x_prompt.shape
    bs, ls, _ = x_sample.shape
    tp, ts = bp * lp, bs * ls

    w_in_bf = w_in[0].astype(BF16)
    w_att_bf = w_out[0, :ATTN_WIDTH].astype(BF16)
    w_ssm_bf = w_out[0, ATTN_WIDTH:].astype(BF16)
    w_cq_bf = w_cq[0].astype(BF16)
    w_co_bf = w_co[0].astype(BF16)
    w_mkv_bf = jnp.concatenate([w_mk[0], w_mv[0]], axis=1).astype(BF16)
    w_glu_bf = w_glu[0].astype(BF16)
    row = lambda a: a.reshape(1, -1)

    a_re, a_im, bb_re, bb_im = _discretize(lam_re[0], lam_im[0], log_dt[0], b_re[0].transpose(0, 2, 1),
                                           b_im[0].transpose(0, 2, 1))
    hg = SSM_GROUPS // 2
    wb = jnp.stack([jnp.concatenate([_block_diag(bb_re[h * hg:(h + 1) * hg]),
                                     _block_diag(bb_im[h * hg:(h + 1) * hg])], axis=1) for h in range(2)]).astype(BF16)
    cre_t = c_re[0].transpose(0, 2, 1)
    cim_t = c_im[0].transpose(0, 2, 1)
    wc = jnp.stack([jnp.concatenate([_block_diag(cre_t[h * hg:(h + 1) * hg]),
                                     -_block_diag(cim_t[h * hg:(h + 1) * hg])], axis=0) for h in range(2)]).astype(BF16)
    ssm_w = (wb, wc, a_re.reshape(1, SSM_LANES), a_im.reshape(1, SSM_LANES), row(d_skip[0]), w_glu_bf,
             row(g_ssm_out[0]))

    xp = x_prompt.reshape(tp, D_MODEL)
    xs_ = x_sample.reshape(ts, D_MODEL)

    qp, kp, vp, up = _inproj(xp, row(g_mix[0]), w_in_bf, 512)
    qs, ks, vs, us = _inproj(xs_, row(g_mix[0]), w_in_bf, 256)

    att_p = _attn_prompt(sink[0], qp, kp, vp, row(g_attn_out[0]), bp, lp)
    ck = cache_win_k[0].reshape(bs, WINDOW, KV_WIDTH)
    cv = cache_win_v[0].reshape(bs, WINDOW, KV_WIDTH)
    att_s = _attn_sample(sink[0], qs, ks, vs, ck, cv, row(g_attn_out[0]), bs, ls, 16)

    zero_state = jnp.zeros((bp, SSM_LANES), F32)
    ssm_p, p_sre, p_sim = _ssm(up.reshape(bp, lp, SSM_WIDTH), zero_state, zero_state, ssm_w, 32)
    ssm_s, s_sre, s_sim = _ssm(us.reshape(bs, ls, SSM_WIDTH), state_ssm_re[0].reshape(bs, SSM_LANES),
                               state_ssm_im[0].reshape(bs, SSM_LANES), ssm_w, ls)

    h1p, qcp = _outproj(xp, att_p, ssm_p.reshape(tp, SSM_WIDTH), w_att_bf, w_ssm_bf, row(g_ca[0]), w_cq_bf, 512)
    h1s, qcs = _outproj(xs_, att_s, ssm_s.reshape(ts, SSM_WIDTH), w_att_bf, w_ssm_bf, row(g_ca[0]), w_cq_bf, 256)

    mk_p, mv_p = _memkv(mem_prompt.reshape(bp * N_MEM, D_MODEL), row(g_mem[0]), w_mkv_bf, 512)

    no_counts = jnp.zeros((1, N_EXPERTS), F32)
    w_router_bf = w_router[0].astype(BF16)
    tt = tp + ts
    h2p, xn, tip, twp, cnt_p = _xattn(qcp, mk_p.reshape(bp, N_MEM, CA_WIDTH), mv_p.reshape(bp, N_MEM, CA_WIDTH),
                                      h1p, w_co_bf, row(g_moe[0]), w_router_bf, row(b_router[0]), no_counts,
                                      1, 512, tt, 0)
    h2s, xn, tis, tws, cnt = _xattn(qcs, cache_mem_k[0], cache_mem_v[0], h1s, w_co_bf, row(g_moe[0]),
                                    w_router_bf, row(b_router[0]), cnt_p, 8, ls, tt, tp, xn_prev=xn)

    m = tt * TOP_K
    pad_start, blk_e, w_idx, n_used, n_slots = _route(cnt, m)
    dest = jnp.concatenate([_dest(tip, pad_start), _dest(tis, pad_start)])
    src_tok = _slot_table(dest, n_slots)[1:, :MOE_BLOCK].reshape(n_slots)
    xs = _sc_gather(xn.reshape(tt, LANE_TILES, LANES), src_tok)
    ys = _experts_blocks(blk_e, w_idx, n_used, xs, w_gate[0], b_gate[0], w_up[0], b_up[0], w_down[0], b_down[0])
    yu = _sc_gather(ys, dest).reshape(m // (SUBLANES * TOP_K), SUBLANES, TOP_K, LANE_TILES, LANES)
    y_prompt = _combine(yu, h2p, twp, row(g_final), 256, 0).reshape(bp, lp, D_MODEL)
    y_sample = _combine(yu, h2s, tws, row(g_final), 256, tp).reshape(bs, ls, D_MODEL)

    kp4 = kp.reshape(bp, lp, N_KV_HEADS, HEAD_DIM)
    vp4 = vp.reshape(bp, lp, N_KV_HEADS, HEAD_DIM)
    p_win_k = kp4[:, -WINDOW:][None]
    p_win_v = vp4[:, -WINDOW:][None]
    s_win_k = jnp.concatenate([cache_win_k[0], ks.reshape(bs, ls, N_KV_HEADS, HEAD_DIM)], axis=1)[:, -WINDOW:][None]
    s_win_v = jnp.concatenate([cache_win_v[0], vs.reshape(bs, ls, N_KV_HEADS, HEAD_DIM)], axis=1)[:, -WINDOW:][None]
    st = lambda a, b: a.reshape(1, b, SSM_GROUPS, SSM_STATE)
    p_mem_k = mk_p.reshape(1, bp, N_MEM, CA_HEADS, CA_HEAD_DIM)
    p_mem_v = mv_p.reshape(1, bp, N_MEM, CA_HEADS, CA_HEAD_DIM)
    return (y_prompt, y_sample, p_win_k, p_win_v, st(p_sre, bp), st(p_sim, bp), p_mem_k, p_mem_v,
            s_win_k, s_win_v, st(s_sre, bs), st(s_sim, bs))
```

```python
import functools
import math

import jax
import jax.numpy as jnp
from jax import lax
from jax.experimental import pallas as pl
from jax.experimental.pallas import tpu as pltpu
from jax.experimental.pallas import tpu_sc as plsc

F32 = jnp.float32
BF16 = jnp.bfloat16

D_MODEL = 1024
HEAD_DIM = 64
N_Q_HEADS = 8
N_KV_HEADS = 2
Q_PER_KV = 4
ATTN_WIDTH = 512
KV_WIDTH = 128
WINDOW = 128
SSM_WIDTH = 512
SSM_GROUP = 16
SSM_GROUPS = 32
SSM_STATE = 64
SSM_LANES = SSM_GROUPS * SSM_STATE
SSM_HALF = SSM_LANES // 2
IN_WIDTH = ATTN_WIDTH + 2 * KV_WIDTH + SSM_WIDTH
N_MEM = 256
CA_HEADS = 4
CA_HEAD_DIM = 128
CA_WIDTH = 512
N_EXPERTS = 32
TOP_K = 4
SWIGLU_ALPHA = 1.702
SWIGLU_LIMIT = 7.0
RMS_EPS = 1e-5

SUBLANES = 8
LANES = 128
MOE_BLOCK = 256
DMA_BATCH = 16
TAB_RING = 4
ATTN_SEQS = 4
SC_CORES = 2
SC_WORKERS = 32
SC_ROWS = 64
DUMP_TILES = 2 * MOE_BLOCK
VMEM_LIMIT = 56 * 1024 * 1024


def _cparams(sem, vmem=None):
    return pltpu.CompilerParams(dimension_semantics=sem, vmem_limit_bytes=vmem)


def _rms(x, g):
    return x * lax.rsqrt(jnp.mean(x * x, axis=-1, keepdims=True) + RMS_EPS) * g


def _full(shape):
    return pl.BlockSpec(shape, lambda *_: (0,) * len(shape))


LANE_TILES = D_MODEL // LANES


def _store_row_tiles(ref, x):
    for c in range(LANE_TILES):
        ref[:, c] = x[:, c * LANES:(c + 1) * LANES].reshape(x.shape[0] // SUBLANES, SUBLANES, LANES)


def _load_row_tiles(ref):
    rows = ref.shape[0] * SUBLANES
    return jnp.concatenate([ref[:, c].reshape(rows, LANES) for c in range(LANE_TILES)], axis=1)


def _disc_kernel(lr_ref, li_ref, ldt_ref, bre_ref, bim_ref, are_ref, aim_ref, bbre_ref, bbim_ref):
    lr = lr_ref[...]
    li = li_ref[...]
    dt = jnp.exp(ldt_ref[...])
    mag = jnp.exp(lr * dt)
    ang = li * dt
    ab_re = mag * jnp.cos(ang)
    ab_im = mag * jnp.sin(ang)
    nr = ab_re - 1.0
    ni = ab_im
    den = lr * lr + li * li
    f_re = (nr * lr + ni * li) / den
    f_im = (ni * lr - nr * li) / den
    are_ref[...] = ab_re
    aim_ref[...] = ab_im
    br = bre_ref[...]
    bi = bim_ref[...]
    bbre_ref[...] = f_re[:, None, :] * br - f_im[:, None, :] * bi
    bbim_ref[...] = f_re[:, None, :] * bi + f_im[:, None, :] * br


def _discretize(lam_re, lam_im, log_dt, b_re_t, b_im_t):
    g, n = lam_re.shape
    c = b_re_t.shape[1]
    return pl.pallas_call(
        _disc_kernel,
        out_shape=(jax.ShapeDtypeStruct((g, n), F32), jax.ShapeDtypeStruct((g, n), F32),
                   jax.ShapeDtypeStruct((g, c, n), F32), jax.ShapeDtypeStruct((g, c, n), F32)),
        name="ssm_discretize",
    )(lam_re, lam_im, log_dt.reshape(g, 1), b_re_t, b_im_t)


def _inproj_kernel(x_ref, g_ref, w_ref, q_ref, k_ref, v_ref, u_ref):
    a = _rms(x_ref[...], g_ref[...]).astype(BF16)
    proj = jnp.dot(a, w_ref[...], preferred_element_type=F32)
    q_ref[...] = proj[:, :ATTN_WIDTH].astype(BF16)
    k_ref[...] = proj[:, ATTN_WIDTH:ATTN_WIDTH + KV_WIDTH]
    v_ref[...] = proj[:, ATTN_WIDTH + KV_WIDTH:ATTN_WIDTH + 2 * KV_WIDTH]
    u_ref[...] = proj[:, ATTN_WIDTH + 2 * KV_WIDTH:]


def _inproj(x, g_mix, w_in_bf, tm):
    t = x.shape[0]
    row = lambda w: pl.BlockSpec((tm, w), lambda i: (i, 0))
    return pl.pallas_call(
        _inproj_kernel,
        grid=(t // tm,),
        in_specs=[row(D_MODEL), _full((1, D_MODEL)), _full((D_MODEL, IN_WIDTH))],
        out_specs=(row(ATTN_WIDTH), row(KV_WIDTH), row(KV_WIDTH), row(SSM_WIDTH)),
        out_shape=(jax.ShapeDtypeStruct((t, ATTN_WIDTH), BF16), jax.ShapeDtypeStruct((t, KV_WIDTH), F32),
                   jax.ShapeDtypeStruct((t, KV_WIDTH), F32), jax.ShapeDtypeStruct((t, SSM_WIDTH), F32)),
        compiler_params=_cparams(("arbitrary",)),
        name="in_projection",
    )(x, g_mix, w_in_bf)


def _sink_softmax_pv(s, sink, v_bf):
    m = jnp.maximum(jnp.max(s, axis=-1, keepdims=True), sink)
    p = jnp.exp(s - m)
    denom = jnp.sum(p, axis=-1, keepdims=True) + jnp.exp(sink - m)
    return jnp.dot(p.astype(BF16), v_bf, preferred_element_type=F32) / denom


def _attn_prompt_kernel(sink_ref, q_ref, kc_ref, kp_ref, vc_ref, vp_ref, g_ref, o_ref):
    i = pl.program_id(1)
    q = q_ref[...]
    kk = jnp.concatenate([kp_ref[...], kc_ref[...]], axis=0).astype(BF16)
    vv = jnp.concatenate([vp_ref[...], vc_ref[...]], axis=0).astype(BF16)
    rows = Q_PER_KV * WINDOW
    qi = lax.broadcasted_iota(jnp.int32, (rows, 2 * WINDOW), 0) & (WINDOW - 1)
    kj = lax.broadcasted_iota(jnp.int32, (rows, 2 * WINDOW), 1)
    rel = qi + WINDOW - kj
    mask = (rel >= 0) & (rel < WINDOW) & ((kj >= WINDOW) | (i > 0))
    head_of_row = lax.broadcasted_iota(jnp.int32, (rows, 1), 0) // WINDOW
    low_q = lax.broadcasted_iota(jnp.int32, (WINDOW, LANES), 1) < HEAD_DIM
    zero_q = jnp.zeros((WINDOW, LANES), BF16)
    pairs = []
    for j in range(N_KV_HEADS):
        kh = kk[:, j * HEAD_DIM:(j + 1) * HEAD_DIM]
        vh = vv[:, j * HEAD_DIM:(j + 1) * HEAD_DIM]
        k2 = jnp.concatenate([kh, kh], axis=1)
        v2 = jnp.concatenate([vh, vh], axis=1)
        stacked = []
        for p in range(Q_PER_KV // 2):
            qp = q[:, (j * Q_PER_KV + 2 * p) * HEAD_DIM:(j * Q_PER_KV + 2 * p + 2) * HEAD_DIM]
            stacked += [jnp.where(low_q, qp, zero_q), jnp.where(low_q, zero_q, qp)]
        qs = jnp.concatenate(stacked, axis=0)
        sink = jnp.zeros((rows, 1), F32)
        for g in range(Q_PER_KV):
            sink = jnp.where(head_of_row == g, sink_ref[j * Q_PER_KV + g], sink)
        s = lax.dot_general(qs, k2, (((1,), (1,)), ((), ())), preferred_element_type=F32) * (HEAD_DIM ** -0.5)
        s = jnp.where(mask, s, -jnp.inf)
        o = _sink_softmax_pv(s, sink, v2)
        for p in range(Q_PER_KV // 2):
            pairs.append(jnp.where(low_q, o[2 * p * WINDOW:(2 * p + 1) * WINDOW],
                                   o[(2 * p + 1) * WINDOW:(2 * p + 2) * WINDOW]))
    att = jnp.concatenate(pairs, axis=-1)
    o_ref[...] = _rms(att, g_ref[...]).astype(BF16)


def _attn_prompt(sink, q, k, v, g_attn_out, b, l):
    nb = l // WINDOW
    cur = lambda w: pl.BlockSpec((WINDOW, w), lambda bi, i: (bi * nb + i, 0))
    prev = lambda w: pl.BlockSpec((WINDOW, w), lambda bi, i: (bi * nb + jnp.maximum(i - 1, 0), 0))
    return pl.pallas_call(
        _attn_prompt_kernel,
        grid=(b, nb),
        in_specs=[pl.BlockSpec(memory_space=pltpu.SMEM), cur(ATTN_WIDTH), cur(KV_WIDTH), prev(KV_WIDTH),
                  cur(KV_WIDTH), prev(KV_WIDTH), _full((1, ATTN_WIDTH))],
        out_specs=cur(ATTN_WIDTH),
        out_shape=jax.ShapeDtypeStruct((b * l, ATTN_WIDTH), BF16),
        compiler_params=_cparams(("arbitrary", "arbitrary")),
        name="window_attn_prompt",
    )(sink, q, k, k, v, v, g_attn_out)


def _attn_sample_kernel(sink_ref, q_ref, kn_ref, vn_ref, ck_ref, cv_ref, g_ref, o_ref, *, bt, t):
    nk = WINDOW + t
    sb = ATTN_SEQS
    keys = sb * nk
    cols = Q_PER_KV * sb * t
    kb = lax.broadcasted_iota(jnp.int32, (sb, nk, cols), 0).reshape(keys, cols)
    kk = lax.broadcasted_iota(jnp.int32, (sb, nk, cols), 1).reshape(keys, cols)
    col = lax.broadcasted_iota(jnp.int32, (keys, cols), 1) % (sb * t)
    cb, ct = col // t, col % t
    mask = (kb == cb) & (((kk < WINDOW) & (kk > ct)) | ((kk >= WINDOW) & ((kk - WINDOW) <= ct)))
    head_of_col = lax.broadcasted_iota(jnp.int32, (1, cols), 1) // (sb * t)
    low = lax.broadcasted_iota(jnp.int32, (sb * t, LANES), 1) < HEAD_DIM
    zero_q = jnp.zeros((sb * t, LANES), BF16)
    swap = lambda x: pltpu.roll(x, HEAD_DIM, 1)
    q_all = q_ref[...]
    kn_all = kn_ref[...]
    vn_all = vn_ref[...]
    outs_b = []
    for b0 in range(0, bt, sb):
        r0 = b0 * t
        newk = kn_all[r0:r0 + sb * t].reshape(sb, t, KV_WIDTH)
        newv = vn_all[r0:r0 + sb * t].reshape(sb, t, KV_WIDTH)
        k_all = jnp.concatenate([ck_ref[b0:b0 + sb], newk], axis=1).reshape(keys, KV_WIDTH).astype(BF16)
        v_all = jnp.concatenate([cv_ref[b0:b0 + sb], newv], axis=1).reshape(keys, KV_WIDTH).astype(BF16)
        head_out = []
        for j in range(N_KV_HEADS):
            mine = low if j == 0 else jnp.logical_not(low)
            qs = []
            for g in range(Q_PER_KV):
                h = j * Q_PER_KV + g
                tile = q_all[r0:r0 + sb * t, (h // 2) * LANES:(h // 2 + 1) * LANES]
                if h % 2 != j:
                    tile = swap(tile.astype(F32)).astype(BF16)
                qs.append(jnp.where(mine, tile, zero_q))
            qj = jnp.concatenate(qs, axis=0)
            sink = jnp.zeros((1, cols), F32)
            for g in range(Q_PER_KV):
                sink = jnp.where(head_of_col == g, sink_ref[j * Q_PER_KV + g], sink)
            s = lax.dot_general(k_all, qj, (((1,), (1,)), ((), ())), preferred_element_type=F32) * (HEAD_DIM ** -0.5)
            s = jnp.where(mask, s, -jnp.inf)
            m = jnp.maximum(jnp.max(s, axis=0, keepdims=True), sink)
            p = jnp.exp(s - m)
            p = p / (jnp.sum(p, axis=0, keepdims=True) + jnp.exp(sink - m))
            o = lax.dot_general(p.astype(BF16), v_all, (((0,), (0,)), ((), ())), preferred_element_type=F32)
            for g in range(Q_PER_KV):
                og = o[g * sb * t:(g + 1) * sb * t]
                head_out.append(og if g % 2 == j else swap(og))
        tiles = [jnp.where(low, head_out[2 * pr], head_out[2 * pr + 1]) for pr in range(N_Q_HEADS // 2)]
        outs_b.append(jnp.concatenate(tiles, axis=-1))
    att = jnp.concatenate(outs_b, axis=0)
    o_ref[...] = _rms(att, g_ref[...]).astype(BF16)


def _attn_sample(sink, q, k, v, cache_k, cache_v, g_attn_out, b, t, bt):
    row = lambda w: pl.BlockSpec((bt * t, w), lambda i: (i, 0))
    cache = pl.BlockSpec((bt, WINDOW, KV_WIDTH), lambda i: (i, 0, 0))
    return pl.pallas_call(
        functools.partial(_attn_sample_kernel, bt=bt, t=t),
        grid=(b // bt,),
        in_specs=[pl.BlockSpec(memory_space=pltpu.SMEM), row(ATTN_WIDTH), row(KV_WIDTH), row(KV_WIDTH),
                  cache, cache, _full((1, ATTN_WIDTH))],
        out_specs=row(ATTN_WIDTH),
        out_shape=jax.ShapeDtypeStruct((b * t, ATTN_WIDTH), BF16),
        compiler_params=_cparams(("arbitrary",)),
        name="window_attn_sample",
    )(sink, q, k, v, cache_k, cache_v, g_attn_out)


def _ssm_project(u_ref, perm_ref, wb_ref, vre, vim, utb, r):
    u = u_ref[...].reshape(r, SSM_WIDTH)
    u_hi = u.astype(BF16)
    u_lo = (u - u_hi.astype(F32)).astype(BF16)
    perm = perm_ref[...]
    u_tb = (jnp.dot(perm, u_hi, preferred_element_type=F32) + jnp.dot(perm, u_lo, preferred_element_type=F32))
    utb[...] = u_tb
    u_bf = u_tb.astype(BF16)
    half_w = SSM_WIDTH // 2
    for h in range(2):
        vv = jnp.dot(u_bf[:, h * half_w:(h + 1) * half_w], wb_ref[h], preferred_element_type=F32)
        vre[:, h * SSM_HALF:(h + 1) * SSM_HALF] = vv[:, :SSM_HALF]
        vim[:, h * SSM_HALF:(h + 1) * SSM_HALF] = vv[:, SSM_HALF:]


def _ssm_scan(are_ref, aim_ref, vre, vim, sre, sim, nb, s):
    lane_chunk = SSM_LANES // 2
    for rg in range(nb // SUBLANES):
        for lc in range(SSM_LANES // lane_chunk):
            lanes = slice(lc * lane_chunk, (lc + 1) * lane_chunk)
            rows0 = slice(rg * SUBLANES, (rg + 1) * SUBLANES)
            ar = jnp.broadcast_to(are_ref[:, lanes], (SUBLANES, lane_chunk))
            ai = jnp.broadcast_to(aim_ref[:, lanes], (SUBLANES, lane_chunk))

            def step(t, carry, lanes=lanes, rg=rg, ar=ar, ai=ai):
                hr, hi = carry
                row = pl.multiple_of(t * nb + rg * SUBLANES, SUBLANES)
                nhr = ar * hr - ai * hi + vre[pl.ds(row, SUBLANES), lanes]
                nhi = ar * hi + ai * hr + vim[pl.ds(row, SUBLANES), lanes]
                vre[pl.ds(row, SUBLANES), lanes] = nhr
                vim[pl.ds(row, SUBLANES), lanes] = nhi
                return nhr, nhi

            hr, hi = lax.fori_loop(0, s, step, (sre[rows0, lanes], sim[rows0, lanes]), unroll=True)
            sre[rows0, lanes] = hr
            sim[rows0, lanes] = hi


def _ssm_output(vre, vim, utb, permt_ref, wc_ref, d_ref, wglu_ref, g_ref, o_ref, nb, s):
    ys = []
    for h in range(2):
        hcat = jnp.concatenate([vre[:, h * SSM_HALF:(h + 1) * SSM_HALF], vim[:, h * SSM_HALF:(h + 1) * SSM_HALF]],
                               axis=1).astype(BF16)
        ys.append(jnp.dot(hcat, wc_ref[h], preferred_element_type=F32))
    y = jnp.concatenate(ys, axis=1) + d_ref[...] * utb[...]
    g = jax.nn.gelu(y)
    z = jnp.dot(g.astype(BF16), wglu_ref[...], preferred_element_type=F32)
    out = g * jax.nn.sigmoid(z)
    on = _rms(out, g_ref[...]).astype(BF16)
    o_bt = jnp.dot(permt_ref[...], on, preferred_element_type=F32).astype(BF16)
    o_ref[...] = o_bt.reshape(nb, s, SSM_WIDTH)


def _ssm_kernel(u_ref, perm_ref, permt_ref, wb_ref, wc_ref, are_ref, aim_ref, d_ref, wglu_ref, g_ref,
                h0re_ref, h0im_ref, o_ref, hre_ref, him_ref, vre, vim, utb, sre, sim, *, nb, s):
    c = pl.program_id(0)

    @pl.when(c == 0)
    def _():
        sre[...] = h0re_ref[...]
        sim[...] = h0im_ref[...]

    _ssm_project(u_ref, perm_ref, wb_ref, vre, vim, utb, nb * s)
    _ssm_scan(are_ref, aim_ref, vre, vim, sre, sim, nb, s)
    _ssm_output(vre, vim, utb, permt_ref, wc_ref, d_ref, wglu_ref, g_ref, o_ref, nb, s)

    @pl.when(c == pl.num_programs(0) - 1)
    def _():
        hre_ref[...] = sre[...]
        him_ref[...] = sim[...]


def _ssm(u, h0_re, h0_im, ssm_w, s):
    nb, l, _ = u.shape
    r = nb * s
    ridx = jnp.arange(r)
    src = (ridx % nb) * s + ridx // nb
    perm = (src[:, None] == ridx[None, :]).astype(BF16)
    wb, wc, a_re, a_im, d_skip, w_glu, g_out = ssm_w
    in_blk = out_blk = pl.BlockSpec((nb, s, SSM_WIDTH), lambda c: (0, c, 0))
    chunk_bufs = [pltpu.VMEM((r, SSM_LANES), F32), pltpu.VMEM((r, SSM_LANES), F32), pltpu.VMEM((r, SSM_WIDTH), F32)]
    st_blk = _full((nb, SSM_LANES))
    return pl.pallas_call(
        functools.partial(_ssm_kernel, nb=nb, s=s),
        grid=(l // s,),
        in_specs=[in_blk, _full((r, r)), _full((r, r)), _full(wb.shape), _full(wc.shape),
                  _full((1, SSM_LANES)), _full((1, SSM_LANES)), _full((1, SSM_WIDTH)),
                  _full((SSM_WIDTH, SSM_WIDTH)), _full((1, SSM_WIDTH)), st_blk, st_blk],
        out_specs=(out_blk, st_blk, st_blk),
        out_shape=(jax.ShapeDtypeStruct((nb, l, SSM_WIDTH), BF16),
                   jax.ShapeDtypeStruct((nb, SSM_LANES), F32), jax.ShapeDtypeStruct((nb, SSM_LANES), F32)),
        scratch_shapes=chunk_bufs + [pltpu.VMEM((nb, SSM_LANES), F32), pltpu.VMEM((nb, SSM_LANES), F32)],
        compiler_params=_cparams(("arbitrary",), VMEM_LIMIT),
        name="ssm_scan",
    )(u, perm, perm.T, wb, wc, a_re, a_im, d_skip, w_glu, g_out, h0_re, h0_im)


def _outproj_kernel(x_ref, att_ref, ssm_ref, wa_ref, ws_ref, gca_ref, wcq_ref, h_ref, qc_ref):
    h = (x_ref[...] + jnp.dot(att_ref[...], wa_ref[...], preferred_element_type=F32)
         + jnp.dot(ssm_ref[...], ws_ref[...], preferred_element_type=F32))
    h_ref[...] = h
    xn = _rms(h, gca_ref[...]).astype(BF16)
    qc_ref[...] = jnp.dot(xn, wcq_ref[...], preferred_element_type=F32).astype(BF16)


def _outproj(x, att, ssm, w_att, w_ssm, g_ca, w_cq, tm):
    t = x.shape[0]
    row = lambda w: pl.BlockSpec((tm, w), lambda i: (i, 0))
    return pl.pallas_call(
        _outproj_kernel,
        grid=(t // tm,),
        in_specs=[row(D_MODEL), row(ATTN_WIDTH), row(SSM_WIDTH), _full((ATTN_WIDTH, D_MODEL)),
                  _full((SSM_WIDTH, D_MODEL)), _full((1, D_MODEL)), _full((D_MODEL, CA_WIDTH))],
        out_specs=(row(D_MODEL), row(CA_WIDTH)),
        out_shape=(jax.ShapeDtypeStruct((t, D_MODEL), F32), jax.ShapeDtypeStruct((t, CA_WIDTH), BF16)),
        compiler_params=_cparams(("arbitrary",)),
        name="out_projection",
    )(x, att, ssm, w_att, w_ssm, g_ca, w_cq)


def _memkv_kernel(m_ref, g_ref, w_ref, k_ref, v_ref):
    m = _rms(m_ref[...], g_ref[...]).astype(BF16)
    kv = jnp.dot(m, w_ref[...], preferred_element_type=F32)
    k_ref[...] = kv[:, :CA_WIDTH]
    v_ref[...] = kv[:, CA_WIDTH:]


def _memkv(mem, g_mem, w_mkv_bf, tm):
    t = mem.shape[0]
    row = lambda w: pl.BlockSpec((tm, w), lambda i: (i, 0))
    return pl.pallas_call(
        _memkv_kernel,
        grid=(t // tm,),
        in_specs=[row(D_MODEL), _full((1, D_MODEL)), _full((D_MODEL, 2 * CA_WIDTH))],
        out_specs=(row(CA_WIDTH), row(CA_WIDTH)),
        out_shape=(jax.ShapeDtypeStruct((t, CA_WIDTH), F32), jax.ShapeDtypeStruct((t, CA_WIDTH), F32)),
        compiler_params=_cparams(("arbitrary",)),
        name="memory_kv",
    )(mem, g_mem, w_mkv_bf)


def _xattn_kernel(qc_ref, mk_ref, mv_ref, h_ref, wco_ref, gmoe_ref, wr_ref, br_ref, tri_ref, cin_ref, *refs,
                  nbat, rows, head_split, has_prev, tile_off):
    refs = refs[1:] if has_prev else refs
    h2_ref, xn_hbm, ti_ref, tw_ref, cout_ref, cnt, xn_buf, xn_sem = refs[:8]
    step = pl.program_id(0)
    nsteps = pl.num_programs(0)
    tm = nbat * rows

    def xn_copies(s):
        dst0 = tile_off + s * (tm // SUBLANES)
        return [pltpu.make_async_copy(xn_buf.at[:, :, sl, :], xn_hbm.at[pl.ds(dst0, tm // SUBLANES), sl], xn_sem)
                for sl in range(SUBLANES)]

    @pl.when(step == 0)
    def _():
        cnt[...] = cin_ref[...]

    if head_split:
        mbuf, msem = refs[8:]
        slot = step % 2

        def head_copies(s, sl):
            b0 = s * nbat
            return [pltpu.make_async_copy(src.at[pl.ds(b0, nbat), :, hd, :], mbuf.at[sl, kv, hd], msem.at[sl])
                    for kv, src in enumerate((mk_ref, mv_ref)) for hd in range(CA_HEADS)]

        @pl.when(step == 0)
        def _():
            for c in head_copies(0, 0):
                c.start()

        @pl.when(step + 1 < nsteps)
        def _():
            for c in head_copies(step + 1, 1 - slot):
                c.start()

        for c in head_copies(step, slot):
            c.wait()

    q_all = qc_ref[...]
    if head_split:
        keys = nbat * N_MEM
        own = (lax.broadcasted_iota(jnp.int32, (keys, tm), 0) // N_MEM
               == lax.broadcasted_iota(jnp.int32, (keys, tm), 1) // rows)
        outs = []
        for hd in range(CA_HEADS):
            sl = slice(hd * CA_HEAD_DIM, (hd + 1) * CA_HEAD_DIM)
            mk = mbuf[slot, 0, hd].reshape(keys, CA_HEAD_DIM).astype(BF16)
            mv = mbuf[slot, 1, hd].reshape(keys, CA_HEAD_DIM).astype(BF16)
            s = lax.dot_general(mk, q_all[:, sl], (((1,), (1,)), ((), ())),
                                preferred_element_type=F32) * (CA_HEAD_DIM ** -0.5)
            s = jnp.where(own, s, -jnp.inf)
            p = jnp.exp(s - jnp.max(s, axis=0, keepdims=True))
            p = p / jnp.sum(p, axis=0, keepdims=True)
            outs.append(lax.dot_general(p.astype(BF16), mv, (((0,), (0,)), ((), ())), preferred_element_type=F32))
        o = jnp.concatenate(outs, axis=-1).astype(BF16)
    else:
        outs_b = []
        for bi in range(nbat):
            q = q_all[bi * rows:(bi + 1) * rows]
            outs = []
            for hd in range(CA_HEADS):
                sl = slice(hd * CA_HEAD_DIM, (hd + 1) * CA_HEAD_DIM)
                mk = mk_ref[bi, :, sl].astype(BF16)
                mv = mv_ref[bi, :, sl].astype(BF16)
                s = lax.dot_general(q[:, sl], mk, (((1,), (1,)), ((), ())),
                                    preferred_element_type=F32) * (CA_HEAD_DIM ** -0.5)
                m = jnp.max(s, axis=-1, keepdims=True)
                p = jnp.exp(s - m)
                denom = jnp.sum(p, axis=-1, keepdims=True)
                outs.append(jnp.dot(p.astype(BF16), mv, preferred_element_type=F32) / denom)
            outs_b.append(jnp.concatenate(outs, axis=-1))
        o = jnp.concatenate(outs_b, axis=0).astype(BF16)
    h2 = h_ref[...] + jnp.dot(o, wco_ref[...], preferred_element_type=F32)
    h2_ref[...] = h2
    xn = _rms(h2, gmoe_ref[...])
    @pl.when(step >= 1)
    def _():
        for c in xn_copies(step - 1):
            c.wait()

    _store_row_tiles(xn_buf, xn)
    for c in xn_copies(step):
        c.start()

    @pl.when(step == nsteps - 1)
    def _():
        for c in xn_copies(step):
            c.wait()

    logits = jnp.dot(xn.astype(BF16), wr_ref[...], preferred_element_type=F32) + br_ref[...]
    n = logits.shape[0]
    eidx = lax.broadcasted_iota(jnp.int32, (n, N_EXPERTS), 1).astype(F32)
    lane = lax.broadcasted_iota(jnp.int32, (n, LANES), 1)
    ti = jnp.zeros((n, LANES), F32)
    tv = jnp.zeros((n, LANES), F32)
    work = logits
    vals = []
    hits = []
    for k in range(TOP_K):
        mx = jnp.max(work, axis=-1, keepdims=True)
        idx = jnp.min(jnp.where(work == mx, eidx, float(N_EXPERTS)), axis=-1, keepdims=True)
        vals.append(mx)
        hits.append(eidx == idx)
        ti = jnp.where(lane == k, idx, ti)
        work = jnp.where(hits[k], -jnp.inf, work)
    ex = [jnp.exp(v - vals[0]) for v in vals]
    tot = ex[0] + ex[1] + ex[2] + ex[3]
    for k in range(TOP_K):
        tv = jnp.where(lane == k, ex[k] / tot, tv)
    sel = jnp.zeros((n, N_EXPERTS), F32)
    for k in range(TOP_K):
        sel = sel + jnp.where(hits[k], 1.0, 0.0)
    before = jnp.dot(tri_ref[...], sel.astype(BF16), preferred_element_type=F32) + cnt[...]
    for k in range(TOP_K):
        rank = jnp.sum(jnp.where(hits[k], before, 0.0), axis=-1, keepdims=True)
        ti = jnp.where(lane == TOP_K + k, rank, ti)
    cnt[...] = cnt[...] + jnp.sum(sel, axis=0, keepdims=True)
    ti_ref[...] = ti.astype(jnp.int32)
    tw_ref[...] = tv

    @pl.when(step == pl.num_programs(0) - 1)
    def _():
        cout_ref[...] = cnt[...]


def _xattn(qc, mk, mv, h1, w_co, g_moe, w_router, b_router, counts_in, nbat, rows, t_total, t_off, xn_prev=None):
    t = qc.shape[0]
    tm = nbat * rows
    row = lambda w: pl.BlockSpec((tm, w), lambda i: (i, 0))
    seq_rows = t // mk.shape[0]
    head_split = mk.ndim == 4
    scratch = [pltpu.VMEM((1, N_EXPERTS), F32), pltpu.VMEM((tm // SUBLANES, LANE_TILES, SUBLANES, LANES), F32),
               pltpu.SemaphoreType.DMA(())]
    any_spec = pl.BlockSpec(memory_space=pl.ANY)
    prev_args, prev_specs, aliases = [], [], {}
    if xn_prev is not None:
        prev_args, prev_specs, aliases = [xn_prev], [any_spec], {10: 1}
    if head_split:
        assert seq_rows == rows
        mem = pl.BlockSpec(memory_space=pl.ANY)
        scratch += [pltpu.VMEM((2, 2, CA_HEADS, nbat, N_MEM, CA_HEAD_DIM), F32), pltpu.SemaphoreType.DMA((2,))]
    else:
        mem = pl.BlockSpec((nbat, N_MEM, CA_WIDTH), lambda i: ((i * tm) // (seq_rows * nbat), 0, 0))
    tri = jnp.tri(tm, k=-1, dtype=BF16)
    return pl.pallas_call(
        functools.partial(_xattn_kernel, nbat=nbat, rows=rows, head_split=head_split, has_prev=xn_prev is not None,
                          tile_off=t_off // SUBLANES),
        grid=(t // tm,),
        in_specs=[row(CA_WIDTH), mem, mem, row(D_MODEL), _full((CA_WIDTH, D_MODEL)), _full((1, D_MODEL)),
                  _full((D_MODEL, N_EXPERTS)), _full((1, N_EXPERTS)), _full((tm, tm)), _full((1, N_EXPERTS))]
        + prev_specs,
        out_specs=(row(D_MODEL), any_spec, row(LANES), row(LANES), _full((1, N_EXPERTS))),
        out_shape=(jax.ShapeDtypeStruct((t, D_MODEL), F32),
                   jax.ShapeDtypeStruct((t_total // SUBLANES, SUBLANES, LANE_TILES, LANES), F32),
                   jax.ShapeDtypeStruct((t, LANES), jnp.int32), jax.ShapeDtypeStruct((t, LANES), F32),
                   jax.ShapeDtypeStruct((1, N_EXPERTS), F32)),
        scratch_shapes=scratch,
        input_output_aliases=aliases,
        compiler_params=_cparams(("arbitrary",), VMEM_LIMIT),
        name="cross_attn_router",
    )(qc, mk, mv, h1, w_co, g_moe, w_router, b_router, tri, counts_in, *prev_args)


def _sc_gather(table, idx):
    n = idx.shape[0]
    per_worker = n // SC_WORKERS
    pieces = per_worker // SC_ROWS
    assert pieces * SC_ROWS * SC_WORKERS == n
    mesh = plsc.VectorSubcoreMesh(core_axis_name="c", subcore_axis_name="s")

    @functools.partial(pl.kernel, mesh=mesh, out_type=jax.ShapeDtypeStruct((n, LANE_TILES, LANES), F32),
                       scratch_types=[pltpu.VMEM((SC_ROWS,), jnp.int32), pltpu.VMEM((SC_ROWS, LANE_TILES, LANES), F32),
                                      pltpu.SemaphoreType.DMA],
                       name="moe_row_gather")
    def gather(table_hbm, idx_hbm, out_hbm, idx_v, rows_v, sem):
        worker = lax.axis_index("s") * SC_CORES + lax.axis_index("c")
        base = worker * per_worker

        @pl.loop(0, pieces)
        def _(piece):
            off = base + piece * SC_ROWS
            pltpu.sync_copy(idx_hbm.at[pl.ds(off, SC_ROWS)], idx_v)
            pltpu.async_copy(table_hbm.at[idx_v], rows_v, sem).wait()
            pltpu.sync_copy(rows_v, out_hbm.at[pl.ds(off, SC_ROWS)])

    return gather(table, idx)


def _expert_block_kernel(blk_e_ref, w_idx_ref, nused_ref, xs_hbm, wg_ref, bg_ref, wu_ref, bu_ref, wd_ref, bd_ref,
                         ys_hbm, wg_bf, wu_bf, wd_bf, xbuf, ybuf, xsem, ysem):
    i = pl.program_id(0)
    n_used = nused_ref[0]
    active = i < n_used
    slot = i % 2
    tiles = MOE_BLOCK // SUBLANES

    def x_copies(blk, sl):
        return [pltpu.make_async_copy(xs_hbm.at[pl.ds(blk * tiles, tiles), s], xbuf.at[sl, :, :, s, :], xsem.at[sl])
                for s in range(SUBLANES)]

    def y_copies(blk, sl):
        return [pltpu.make_async_copy(ybuf.at[sl, :, :, s, :], ys_hbm.at[pl.ds(blk * tiles, tiles), s], ysem.at[sl])
                for s in range(SUBLANES)]

    @pl.when((i == 0) & active)
    def _():
        for c in x_copies(0, 0):
            c.start()

    prev_e = blk_e_ref[jnp.maximum(i - 1, 0)]
    new_expert = (i == 0) | (blk_e_ref[i] != prev_e)

    @pl.when(active & new_expert)
    def _():
        wg_bf[...] = wg_ref[0].astype(BF16)
        wu_bf[...] = wu_ref[0].astype(BF16)
        wd_bf[...] = wd_ref[0].astype(BF16)

    @pl.when(active)
    def _():
        for c in x_copies(i, slot):
            c.wait()

        @pl.when(i + 1 < n_used)
        def _():
            for c in x_copies(i + 1, 1 - slot):
                c.start()

        x = _load_row_tiles(xbuf.at[slot]).astype(BF16)
        gate = jnp.minimum(jnp.dot(x, wg_bf[...], preferred_element_type=F32) + bg_ref[0], SWIGLU_LIMIT)
        up = jnp.clip(jnp.dot(x, wu_bf[...], preferred_element_type=F32) + bu_ref[0], -SWIGLU_LIMIT, SWIGLU_LIMIT)
        h = gate * jax.nn.sigmoid(SWIGLU_ALPHA * gate) * (up + 1.0)
        y = jnp.dot(h.astype(BF16), wd_bf[...], preferred_element_type=F32) + bd_ref[0]

        @pl.when(i >= 1)
        def _():
            for c in y_copies(i - 1, 1 - slot):
                c.wait()

        _store_row_tiles(ybuf.at[slot], y)
        for c in y_copies(i, slot):
            c.start()

        @pl.when(i == n_used - 1)
        def _():
            for c in y_copies(i, slot):
                c.wait()


def _experts_blocks(blk_e, w_idx, n_used, xs, w_gate, b_gate, w_up, b_up, w_down, b_down):
    n_slots = xs.shape[0]
    n_blocks = n_slots // MOE_BLOCK
    d_ff = w_gate.shape[-1]
    tiles = MOE_BLOCK // SUBLANES
    wspec = lambda a, b: pl.BlockSpec((1, a, b), lambda i, be, wi, nu: (wi[i], 0, 0))
    bspec = lambda b: pl.BlockSpec((1, 1, b), lambda i, be, wi, nu: (be[i], 0, 0))
    any_spec = pl.BlockSpec(memory_space=pl.ANY)
    grid_spec = pltpu.PrefetchScalarGridSpec(
        num_scalar_prefetch=3,
        grid=(n_blocks,),
        in_specs=[any_spec, wspec(D_MODEL, d_ff), bspec(d_ff), wspec(D_MODEL, d_ff), bspec(d_ff),
                  wspec(d_ff, D_MODEL), bspec(D_MODEL)],
        out_specs=any_spec,
        scratch_shapes=[pltpu.VMEM((D_MODEL, d_ff), BF16), pltpu.VMEM((D_MODEL, d_ff), BF16),
                        pltpu.VMEM((d_ff, D_MODEL), BF16),
                        pltpu.VMEM((2, tiles, LANE_TILES, SUBLANES, LANES), F32),
                        pltpu.VMEM((2, tiles, LANE_TILES, SUBLANES, LANES), F32),
                        pltpu.SemaphoreType.DMA((2,)), pltpu.SemaphoreType.DMA((2,))],
    )
    ys = pl.pallas_call(
        _expert_block_kernel,
        grid_spec=grid_spec,
        out_shape=jax.ShapeDtypeStruct((n_slots // SUBLANES, SUBLANES, LANE_TILES, LANES), F32),
        compiler_params=_cparams(("arbitrary",), VMEM_LIMIT),
        name="moe_experts",
    )(blk_e, w_idx, n_used, xs.reshape(n_slots // SUBLANES, SUBLANES, LANE_TILES, LANES), w_gate,
      b_gate.reshape(N_EXPERTS, 1, d_ff), w_up, b_up.reshape(N_EXPERTS, 1, d_ff), w_down,
      b_down.reshape(N_EXPERTS, 1, D_MODEL))
    return ys.reshape(n_slots, LANE_TILES, LANES)


def _expert_kernel(blk_e_ref, w_idx_ref, nused_ref, tab_hbm, xn_hbm, wg_ref, bg_ref, wu_ref, bu_ref, wd_ref, bd_ref,
                   yu_hbm, wg_bf, wu_bf, wd_bf, xbuf, ybuf, tab, gsem, ssem, tsem, *, n_blocks):
    i = pl.program_id(0)
    n_used = nused_ref[0]
    active = i < n_used
    slot = i % 2
    row_w = 2 * MOE_BLOCK

    def ring(b):
        return (b + TAB_RING) % TAB_RING

    def tab_copy(b):
        r = ring(b)
        return pltpu.make_async_copy(tab_hbm.at[jnp.minimum(b, n_blocks - 1) + 1], tab.at[pl.ds(r * row_w, row_w)],
                                     tsem.at[r])

    def block_wait(ref, sem):
        pltpu.make_async_copy(ref.at[pl.ds(0, MOE_BLOCK)], ref.at[pl.ds(0, MOE_BLOCK)], sem).wait()

    def issue_rows(off, copy):
        def body(it, carry):
            ids = [tab[off + it * DMA_BATCH + q] for q in range(DMA_BATCH)]
            for q in range(DMA_BATCH):
                copy(ids[q], it * (DMA_BATCH // SUBLANES) + q // SUBLANES, q % SUBLANES).start(priority=q % 2)
            return carry
        lax.fori_loop(0, MOE_BLOCK // DMA_BATCH, body, 0)

    def issue_gathers(b, sl):
        issue_rows(ring(b) * row_w,
                   lambda tok, j, s: pltpu.make_async_copy(xn_hbm.at[tok], xbuf.at[sl, j, :, s, :], gsem.at[sl]))

    def issue_scatters(b, sl):
        issue_rows(ring(b) * row_w + MOE_BLOCK,
                   lambda tile, j, s: pltpu.make_async_copy(ybuf.at[sl, j, :, s, :], yu_hbm.at[tile], ssem))

    @pl.when((i == 0) & active)
    def _():
        ybuf[...] = jnp.zeros(ybuf.shape, F32)
        for b in (-1, 0, 1):
            tab_copy(b).start()
            tab_copy(b).wait()
        tab_copy(2).start()
        issue_gathers(0, 0)

    prev_e = blk_e_ref[jnp.maximum(i - 1, 0)]
    new_expert = (i == 0) | (blk_e_ref[i] != prev_e)

    @pl.when(active & new_expert)
    def _():
        wg_bf[...] = wg_ref[0].astype(BF16)
        wu_bf[...] = wu_ref[0].astype(BF16)
        wd_bf[...] = wd_ref[0].astype(BF16)

    @pl.when(active)
    def _():
        block_wait(xn_hbm, gsem.at[slot])

        @pl.when(i >= 1)
        def _():
            tab_copy(i + 1).wait()
            tab_copy(i + 2).start()

        issue_gathers(i + 1, 1 - slot)
        issue_scatters(i - 1, 1 - slot)
        x = _load_row_tiles(xbuf.at[slot]).astype(BF16)
        gate = jnp.minimum(jnp.dot(x, wg_bf[...], preferred_element_type=F32) + bg_ref[0], SWIGLU_LIMIT)
        up = jnp.clip(jnp.dot(x, wu_bf[...], preferred_element_type=F32) + bu_ref[0], -SWIGLU_LIMIT, SWIGLU_LIMIT)
        h = gate * jax.nn.sigmoid(SWIGLU_ALPHA * gate) * (up + 1.0)
        y = jnp.dot(h.astype(BF16), wd_bf[...], preferred_element_type=F32) + bd_ref[0]
        _store_row_tiles(ybuf.at[slot], y)
        block_wait(yu_hbm, ssem)

        @pl.when(i == n_used - 1)
        def _():
            issue_scatters(i, slot)
            block_wait(yu_hbm, ssem)
            block_wait(xn_hbm, gsem.at[1 - slot])
            tab_copy(i + 2).wait()


def _experts(blk_e, w_idx, n_used, tab, xn, n_out_tiles, w_gate, b_gate, w_up, b_up, w_down, b_down):
    n_blocks = tab.shape[0] - 1
    d_ff = w_gate.shape[-1]
    tiles = MOE_BLOCK // SUBLANES
    wspec = lambda a, b: pl.BlockSpec((1, a, b), lambda i, be, wi, nu: (wi[i], 0, 0))
    bspec = lambda b: pl.BlockSpec((1, 1, b), lambda i, be, wi, nu: (be[i], 0, 0))
    any_spec = pl.BlockSpec(memory_space=pl.ANY)
    grid_spec = pltpu.PrefetchScalarGridSpec(
        num_scalar_prefetch=3,
        grid=(n_blocks,),
        in_specs=[any_spec, any_spec, wspec(D_MODEL, d_ff), bspec(d_ff), wspec(D_MODEL, d_ff), bspec(d_ff),
                  wspec(d_ff, D_MODEL), bspec(D_MODEL)],
        out_specs=any_spec,
        scratch_shapes=[pltpu.VMEM((D_MODEL, d_ff), BF16), pltpu.VMEM((D_MODEL, d_ff), BF16),
                        pltpu.VMEM((d_ff, D_MODEL), BF16),
                        pltpu.VMEM((2, tiles, LANE_TILES, SUBLANES, LANES), F32),
                        pltpu.VMEM((2, tiles, LANE_TILES, SUBLANES, LANES), F32),
                        pltpu.SMEM((TAB_RING * 2 * MOE_BLOCK,), jnp.int32),
                        pltpu.SemaphoreType.DMA((2,)), pltpu.SemaphoreType.DMA(()),
                        pltpu.SemaphoreType.DMA((TAB_RING,))],
    )
    return pl.pallas_call(
        functools.partial(_expert_kernel, n_blocks=n_blocks),
        grid_spec=grid_spec,
        out_shape=jax.ShapeDtypeStruct((n_out_tiles, LANE_TILES, LANES), F32),
        compiler_params=_cparams(("arbitrary",), VMEM_LIMIT),
        name="moe_experts",
    )(blk_e, w_idx, n_used, tab, xn, w_gate, b_gate.reshape(N_EXPERTS, 1, d_ff), w_up,
      b_up.reshape(N_EXPERTS, 1, d_ff), w_down, b_down.reshape(N_EXPERTS, 1, D_MODEL))


def _combine_kernel(yu_hbm, h_ref, tw_ref, g_ref, o_ref, buf, sem, *, tm, tile_off):
    i = pl.program_id(0)
    n = pl.num_programs(0)
    slot = i % 2
    tiles = tm // SUBLANES

    def copies(step, sl):
        t0 = tile_off + step * tiles
        return [pltpu.make_async_copy(yu_hbm.at[pl.ds(t0, tiles), s, k], buf.at[sl, k, :, :, s, :], sem.at[sl])
                for s in range(SUBLANES) for k in range(TOP_K)]

    @pl.when(i == 0)
    def _():
        for c in copies(0, 0):
            c.start()

    @pl.when(i + 1 < n)
    def _():
        for c in copies(i + 1, 1 - slot):
            c.start()

    for c in copies(i, slot):
        c.wait()
    tw = tw_ref[...]
    y = h_ref[...]
    for k in range(TOP_K):
        y = y + tw[:, k:k + 1] * _load_row_tiles(buf.at[slot, k])
    o_ref[...] = _rms(y, g_ref[...])


def _combine(yu, h2, tw, g_final, tm, t_off):
    t = h2.shape[0]
    row = lambda w: pl.BlockSpec((tm, w), lambda i: (i, 0))
    return pl.pallas_call(
        functools.partial(_combine_kernel, tm=tm, tile_off=t_off // SUBLANES),
        grid=(t // tm,),
        in_specs=[pl.BlockSpec(memory_space=pl.ANY), row(D_MODEL), row(LANES), _full((1, D_MODEL))],
        out_specs=row(D_MODEL),
        out_shape=jax.ShapeDtypeStruct((t, D_MODEL), F32),
        scratch_shapes=[pltpu.VMEM((2, TOP_K, tm // SUBLANES, LANE_TILES, SUBLANES, LANES), F32),
                        pltpu.SemaphoreType.DMA((2,))],
        compiler_params=_cparams(("arbitrary",), VMEM_LIMIT),
        name="moe_combine",
    )(yu, h2, tw, g_final)


def _route(counts, m):
    counts = counts.reshape(N_EXPERTS).astype(jnp.int32)
    padded = (counts + MOE_BLOCK - 1) // MOE_BLOCK * MOE_BLOCK
    pad_end = jnp.cumsum(padded)
    pad_start = pad_end - padded
    n_blocks = -(-(m + N_EXPERTS * (MOE_BLOCK - 1)) // MOE_BLOCK)
    blk_start = jnp.arange(n_blocks, dtype=jnp.int32) * MOE_BLOCK
    owner = lambda start: jnp.minimum(jnp.sum((pad_end[None, :] <= start[:, None]).astype(jnp.int32), axis=1),
                                      N_EXPERTS - 1).astype(jnp.int32)
    n_used = pad_end[-1] // MOE_BLOCK
    last_start = jnp.maximum(n_used - 1, 0) * MOE_BLOCK
    blk_e = owner(jnp.minimum(blk_start, last_start))
    first = jnp.concatenate([jnp.ones((1,), bool), blk_e[1:] != blk_e[:-1]])
    region_end = jnp.sum(jnp.where(blk_e[:, None] == jnp.arange(N_EXPERTS)[None, :], pad_end[None, :], 0), axis=1)
    w_idx = jnp.where(first, blk_e, owner(jnp.minimum(region_end, last_start))).astype(jnp.int32)
    return pad_start, blk_e, w_idx, n_used.astype(jnp.int32).reshape(1), n_blocks * MOE_BLOCK


def _dest(ti, pad_start):
    e = ti[:, :TOP_K]
    onehot = e[:, :, None] == jnp.arange(N_EXPERTS, dtype=jnp.int32)[None, None, :]
    start = jnp.sum(jnp.where(onehot, pad_start[None, None, :], 0), axis=-1)
    return (start + ti[:, TOP_K:2 * TOP_K]).reshape(-1)


def _slot_sources(dest, n_slots, n_tokens):
    m = dest.shape[0]
    filled_by = jnp.zeros((n_slots,), jnp.int32).at[dest].add(jnp.arange(m, dtype=jnp.int32) + 1) - 1
    return jnp.where(filled_by >= 0, filled_by // TOP_K, jnp.arange(n_slots, dtype=jnp.int32) % n_tokens)


def _slot_table(dest, n_slots):
    m = dest.shape[0]
    n_blocks = n_slots // MOE_BLOCK
    filled_by = jnp.zeros((n_slots,), jnp.int32).at[dest].add(jnp.arange(m, dtype=jnp.int32) + 1) - 1
    filled = filled_by >= 0
    src_tok = jnp.where(filled, filled_by, 0) // TOP_K
    dst_tile = jnp.where(filled, filled_by, m + jnp.arange(n_slots, dtype=jnp.int32) % DUMP_TILES)
    rows = jnp.concatenate([src_tok.reshape(n_blocks, MOE_BLOCK), dst_tile.reshape(n_blocks, MOE_BLOCK)], axis=1)
    before_first = jnp.concatenate([jnp.zeros((MOE_BLOCK,), jnp.int32),
                                    m + MOE_BLOCK + jnp.arange(MOE_BLOCK, dtype=jnp.int32)])[None]
    return jnp.concatenate([before_first, rows], axis=0).astype(jnp.int32)


def _block_diag(blocks):
    g, a, b = blocks.shape
    eye = jnp.eye(g, dtype=blocks.dtype)
    return (blocks[:, :, None, :] * eye[:, None, :, None]).reshape(g * a, g * b)


def kernel(x_prompt, x_sample, mem_prompt, cache_win_k, cache_win_v, state_ssm_re, state_ssm_im, cache_mem_k,
           cache_mem_v, g_mix, w_in, sink, lam_re, lam_im, log_dt, b_re, b_im, c_re, c_im, d_skip, w_glu,
           g_attn_out, g_ssm_out, w_out, g_ca, g_mem, w_mk, w_mv, w_cq, w_co, g_moe, w_router, b_router,
           w_gate, b_gate, w_up, b_up, w_down, b_down, g_final):
    depth = g_mix.shape[0]
    assert depth == 1
    bp, lp, _ = x_prompt.shape
    bs, ls, _ = x_sample.shape
    tp, ts = bp * lp, bs * ls

    w_in_bf = w_in[0].astype(BF16)
    w_att_bf = w_out[0, :ATTN_WIDTH].astype(BF16)
    w_ssm_bf = w_out[0, ATTN_WIDTH:].astype(BF16)
    w_cq_bf = w_cq[0].astype(BF16)
    w_co_bf = w_co[0].astype(BF16)
    w_mkv_bf = jnp.concatenate([w_mk[0], w_mv[0]], axis=1).astype(BF16)
    w_glu_bf = w_glu[0].astype(BF16)
    row = lambda a: a.reshape(1, -1)

    a_re, a_im, bb_re, bb_im = _discretize(lam_re[0], lam_im[0], log_dt[0], b_re[0].transpose(0, 2, 1),
                                           b_im[0].transpose(0, 2, 1))
    hg = SSM_GROUPS // 2
    wb = jnp.stack([jnp.concatenate([_block_diag(bb_re[h * hg:(h + 1) * hg]),
                                     _block_diag(bb_im[h * hg:(h + 1) * hg])], axis=1) for h in range(2)]).astype(BF16)
    cre_t = c_re[0].transpose(0, 2, 1)
    cim_t = c_im[0].transpose(0, 2, 1)
    wc = jnp.stack([jnp.concatenate([_block_diag(cre_t[h * hg:(h + 1) * hg]),
                                     -_block_diag(cim_t[h * hg:(h + 1) * hg])], axis=0) for h in range(2)]).astype(BF16)
    ssm_w = (wb, wc, a_re.reshape(1, SSM_LANES), a_im.reshape(1, SSM_LANES), row(d_skip[0]), w_glu_bf,
             row(g_ssm_out[0]))

    xp = x_prompt.reshape(tp, D_MODEL)
    xs_ = x_sample.reshape(ts, D_MODEL)

    qp, kp, vp, up = _inproj(xp, row(g_mix[0]), w_in_bf, 512)
    qs, ks, vs, us = _inproj(xs_, row(g_mix[0]), w_in_bf, 256)

    att_p = _attn_prompt(sink[0], qp, kp, vp, row(g_attn_out[0]), bp, lp)
    ck = cache_win_k[0].reshape(bs, WINDOW, KV_WIDTH)
    cv = cache_win_v[0].reshape(bs, WINDOW, KV_WIDTH)
    att_s = _attn_sample(sink[0], qs, ks, vs, ck, cv, row(g_attn_out[0]), bs, ls, 16)

    zero_state = jnp.zeros((bp, SSM_LANES), F32)
    ssm_p, p_sre, p_sim = _ssm(up.reshape(bp, lp, SSM_WIDTH), zero_state, zero_state, ssm_w, 32)
    ssm_s, s_sre, s_sim = _ssm(us.reshape(bs, ls, SSM_WIDTH), state_ssm_re[0].reshape(bs, SSM_LANES),
                               state_ssm_im[0].reshape(bs, SSM_LANES), ssm_w, ls)

    h1p, qcp = _outproj(xp, att_p, ssm_p.reshape(tp, SSM_WIDTH), w_att_bf, w_ssm_bf, row(g_ca[0]), w_cq_bf, 512)
    h1s, qcs = _outproj(xs_, att_s, ssm_s.reshape(ts, SSM_WIDTH), w_att_bf, w_ssm_bf, row(g_ca[0]), w_cq_bf, 256)

    mk_p, mv_p = _memkv(mem_prompt.reshape(bp * N_MEM, D_MODEL), row(g_mem[0]), w_mkv_bf, 512)

    no_counts = jnp.zeros((1, N_EXPERTS), F32)
    w_router_bf = w_router[0].astype(BF16)
    tt = tp + ts
    h2p, xn, tip, twp, cnt_p = _xattn(qcp, mk_p.reshape(bp, N_MEM, CA_WIDTH), mv_p.reshape(bp, N_MEM, CA_WIDTH),
                                      h1p, w_co_bf, row(g_moe[0]), w_router_bf, row(b_router[0]), no_counts,
                                      1, 512, tt, 0)
    h2s, xn, tis, tws, cnt = _xattn(qcs, cache_mem_k[0], cache_mem_v[0], h1s, w_co_bf, row(g_moe[0]),
                                    w_router_bf, row(b_router[0]), cnt_p, 8, ls, tt, tp, xn_prev=xn)

    m = tt * TOP_K
    pad_start, blk_e, w_idx, n_used, n_slots = _route(cnt, m)
    dest = jnp.concatenate([_dest(tip, pad_start), _dest(tis, pad_start)])
    src_tok = _slot_sources(dest, n_slots, tt)
    xs = _sc_gather(xn.reshape(tt, LANE_TILES, LANES), src_tok)
    ys = _experts_blocks(blk_e, w_idx, n_used, xs, w_gate[0], b_gate[0], w_up[0], b_up[0], w_down[0], b_down[0])
    yu = _sc_gather(ys, dest).reshape(m // (SUBLANES * TOP_K), SUBLANES, TOP_K, LANE_TILES, LANES)
    y_prompt = _combine(yu, h2p, twp, row(g_final), 256, 0).reshape(bp, lp, D_MODEL)
    y_sample = _combine(yu, h2s, tws, row(g_final), 256, tp).reshape(bs, ls, D_MODEL)

    kp4 = kp.reshape(bp, lp, N_KV_HEADS, HEAD_DIM)
    vp4 = vp.reshape(bp, lp, N_KV_HEADS, HEAD_DIM)
    p_win_k = kp4[:, -WINDOW:][None]
    p_win_v = vp4[:, -WINDOW:][None]
    s_win_k = jnp.concatenate([cache_win_k[0], ks.reshape(bs, ls, N_KV_HEADS, HEAD_DIM)], axis=1)[:, -WINDOW:][None]
    s_win_v = jnp.concatenate([cache_win_v[0], vs.reshape(bs, ls, N_KV_HEADS, HEAD_DIM)], axis=1)[:, -WINDOW:][None]
    st = lambda a, b: a.reshape(1, b, SSM_GROUPS, SSM_STATE)
    p_mem_k = mk_p.reshape(1, bp, N_MEM, CA_HEADS, CA_HEAD_DIM)
    p_mem_v = mv_p.reshape(1, bp, N_MEM, CA_HEADS, CA_HEAD_DIM)
    return (y_prompt, y_sample, p_win_k, p_win_v, st(p_sre, bp), st(p_sim, bp), p_mem_k, p_mem_v,
            s_win_k, s_win_v, st(s_sre, bs), st(s_sim, bs))
```

```python
import functools
import math

import jax
import jax.numpy as jnp
from jax import lax
from jax.experimental import pallas as pl
from jax.experimental.pallas import tpu as pltpu
from jax.experimental.pallas import tpu_sc as plsc

F32 = jnp.float32
BF16 = jnp.bfloat16

D_MODEL = 1024
HEAD_DIM = 64
N_Q_HEADS = 8
N_KV_HEADS = 2
Q_PER_KV = 4
ATTN_WIDTH = 512
KV_WIDTH = 128
WINDOW = 128
SSM_WIDTH = 512
SSM_GROUP = 16
SSM_GROUPS = 32
SSM_STATE = 64
SSM_LANES = SSM_GROUPS * SSM_STATE
SSM_HALF = SSM_LANES // 2
IN_WIDTH = ATTN_WIDTH + 2 * KV_WIDTH + SSM_WIDTH
N_MEM = 256
CA_HEADS = 4
CA_HEAD_DIM = 128
CA_WIDTH = 512
N_EXPERTS = 32
TOP_K = 4
SWIGLU_ALPHA = 1.702
SWIGLU_LIMIT = 7.0
RMS_EPS = 1e-5

SUBLANES = 8
LANES = 128
MOE_BLOCK = 256
DMA_BATCH = 16
TAB_RING = 4
ATTN_SEQS = 4
SC_CORES = 2
SC_WORKERS = 32
SC_ROWS = 64
DUMP_TILES = 2 * MOE_BLOCK
VMEM_LIMIT = 56 * 1024 * 1024


def _cparams(sem, vmem=None):
    return pltpu.CompilerParams(dimension_semantics=sem, vmem_limit_bytes=vmem)


def _rms(x, g):
    return x * lax.rsqrt(jnp.mean(x * x, axis=-1, keepdims=True) + RMS_EPS) * g


def _full(shape):
    return pl.BlockSpec(shape, lambda *_: (0,) * len(shape))


LANE_TILES = D_MODEL // LANES


def _store_row_tiles(ref, x):
    for c in range(LANE_TILES):
        ref[:, c] = x[:, c * LANES:(c + 1) * LANES].reshape(x.shape[0] // SUBLANES, SUBLANES, LANES)


def _load_row_tiles(ref):
    rows = ref.shape[0] * SUBLANES
    return jnp.concatenate([ref[:, c].reshape(rows, LANES) for c in range(LANE_TILES)], axis=1)


def _disc_kernel(lr_ref, li_ref, ldt_ref, bre_ref, bim_ref, are_ref, aim_ref, bbre_ref, bbim_ref):
    lr = lr_ref[...]
    li = li_ref[...]
    dt = jnp.exp(ldt_ref[...])
    mag = jnp.exp(lr * dt)
    ang = li * dt
    ab_re = mag * jnp.cos(ang)
    ab_im = mag * jnp.sin(ang)
    nr = ab_re - 1.0
    ni = ab_im
    den = lr * lr + li * li
    f_re = (nr * lr + ni * li) / den
    f_im = (ni * lr - nr * li) / den
    are_ref[...] = ab_re
    aim_ref[...] = ab_im
    br = bre_ref[...]
    bi = bim_ref[...]
    bbre_ref[...] = f_re[:, None, :] * br - f_im[:, None, :] * bi
    bbim_ref[...] = f_re[:, None, :] * bi + f_im[:, None, :] * br


def _discretize(lam_re, lam_im, log_dt, b_re_t, b_im_t):
    g, n = lam_re.shape
    c = b_re_t.shape[1]
    return pl.pallas_call(
        _disc_kernel,
        out_shape=(jax.ShapeDtypeStruct((g, n), F32), jax.ShapeDtypeStruct((g, n), F32),
                   jax.ShapeDtypeStruct((g, c, n), F32), jax.ShapeDtypeStruct((g, c, n), F32)),
        name="ssm_discretize",
    )(lam_re, lam_im, log_dt.reshape(g, 1), b_re_t, b_im_t)


def _inproj_kernel(x_ref, g_ref, w_ref, q_ref, k_ref, v_ref, u_ref):
    a = _rms(x_ref[...], g_ref[...]).astype(BF16)
    proj = jnp.dot(a, w_ref[...], preferred_element_type=F32)
    q_ref[...] = proj[:, :ATTN_WIDTH].astype(BF16)
    k_ref[...] = proj[:, ATTN_WIDTH:ATTN_WIDTH + KV_WIDTH]
    v_ref[...] = proj[:, ATTN_WIDTH + KV_WIDTH:ATTN_WIDTH + 2 * KV_WIDTH]
    u_ref[...] = proj[:, ATTN_WIDTH + 2 * KV_WIDTH:]


def _inproj(x, g_mix, w_in_bf, tm):
    t = x.shape[0]
    row = lambda w: pl.BlockSpec((tm, w), lambda i: (i, 0))
    return pl.pallas_call(
        _inproj_kernel,
        grid=(t // tm,),
        in_specs=[row(D_MODEL), _full((1, D_MODEL)), _full((D_MODEL, IN_WIDTH))],
        out_specs=(row(ATTN_WIDTH), row(KV_WIDTH), row(KV_WIDTH), row(SSM_WIDTH)),
        out_shape=(jax.ShapeDtypeStruct((t, ATTN_WIDTH), BF16), jax.ShapeDtypeStruct((t, KV_WIDTH), F32),
                   jax.ShapeDtypeStruct((t, KV_WIDTH), F32), jax.ShapeDtypeStruct((t, SSM_WIDTH), F32)),
        compiler_params=_cparams(("arbitrary",)),
        name="in_projection",
    )(x, g_mix, w_in_bf)


def _sink_softmax_pv(s, sink, v_bf):
    m = jnp.maximum(jnp.max(s, axis=-1, keepdims=True), sink)
    p = jnp.exp(s - m)
    denom = jnp.sum(p, axis=-1, keepdims=True) + jnp.exp(sink - m)
    return jnp.dot(p.astype(BF16), v_bf, preferred_element_type=F32) / denom


def _attn_prompt_kernel(sink_ref, q_ref, kc_ref, kp_ref, vc_ref, vp_ref, g_ref, o_ref):
    i = pl.program_id(1)
    q = q_ref[...]
    kk = jnp.concatenate([kp_ref[...], kc_ref[...]], axis=0).astype(BF16)
    vv = jnp.concatenate([vp_ref[...], vc_ref[...]], axis=0).astype(BF16)
    rows = Q_PER_KV * WINDOW
    qi = lax.broadcasted_iota(jnp.int32, (rows, 2 * WINDOW), 0) & (WINDOW - 1)
    kj = lax.broadcasted_iota(jnp.int32, (rows, 2 * WINDOW), 1)
    rel = qi + WINDOW - kj
    mask = (rel >= 0) & (rel < WINDOW) & ((kj >= WINDOW) | (i > 0))
    head_of_row = lax.broadcasted_iota(jnp.int32, (rows, 1), 0) // WINDOW
    low_q = lax.broadcasted_iota(jnp.int32, (WINDOW, LANES), 1) < HEAD_DIM
    zero_q = jnp.zeros((WINDOW, LANES), BF16)
    pairs = []
    for j in range(N_KV_HEADS):
        kh = kk[:, j * HEAD_DIM:(j + 1) * HEAD_DIM]
        vh = vv[:, j * HEAD_DIM:(j + 1) * HEAD_DIM]
        k2 = jnp.concatenate([kh, kh], axis=1)
        v2 = jnp.concatenate([vh, vh], axis=1)
        stacked = []
        for p in range(Q_PER_KV // 2):
            qp = q[:, (j * Q_PER_KV + 2 * p) * HEAD_DIM:(j * Q_PER_KV + 2 * p + 2) * HEAD_DIM]
            stacked += [jnp.where(low_q, qp, zero_q), jnp.where(low_q, zero_q, qp)]
        qs = jnp.concatenate(stacked, axis=0)
        sink = jnp.zeros((rows, 1), F32)
        for g in range(Q_PER_KV):
            sink = jnp.where(head_of_row == g, sink_ref[j * Q_PER_KV + g], sink)
        s = lax.dot_general(qs, k2, (((1,), (1,)), ((), ())), preferred_element_type=F32) * (HEAD_DIM ** -0.5)
        s = jnp.where(mask, s, -jnp.inf)
        o = _sink_softmax_pv(s, sink, v2)
        for p in range(Q_PER_KV // 2):
            pairs.append(jnp.where(low_q, o[2 * p * WINDOW:(2 * p + 1) * WINDOW],
                                   o[(2 * p + 1) * WINDOW:(2 * p + 2) * WINDOW]))
    att = jnp.concatenate(pairs, axis=-1)
    o_ref[...] = _rms(att, g_ref[...]).astype(BF16)


def _attn_prompt(sink, q, k, v, g_attn_out, b, l):
    nb = l // WINDOW
    cur = lambda w: pl.BlockSpec((WINDOW, w), lambda bi, i: (bi * nb + i, 0))
    prev = lambda w: pl.BlockSpec((WINDOW, w), lambda bi, i: (bi * nb + jnp.maximum(i - 1, 0), 0))
    return pl.pallas_call(
        _attn_prompt_kernel,
        grid=(b, nb),
        in_specs=[pl.BlockSpec(memory_space=pltpu.SMEM), cur(ATTN_WIDTH), cur(KV_WIDTH), prev(KV_WIDTH),
                  cur(KV_WIDTH), prev(KV_WIDTH), _full((1, ATTN_WIDTH))],
        out_specs=cur(ATTN_WIDTH),
        out_shape=jax.ShapeDtypeStruct((b * l, ATTN_WIDTH), BF16),
        compiler_params=_cparams(("arbitrary", "arbitrary")),
        name="window_attn_prompt",
    )(sink, q, k, k, v, v, g_attn_out)


def _attn_sample_kernel(sink_ref, q_ref, kn_ref, vn_ref, ck_ref, cv_ref, g_ref, o_ref, *, bt, t):
    nk = WINDOW + t
    sb = ATTN_SEQS
    keys = sb * nk
    cols = Q_PER_KV * sb * t
    kb = lax.broadcasted_iota(jnp.int32, (sb, nk, cols), 0).reshape(keys, cols)
    kk = lax.broadcasted_iota(jnp.int32, (sb, nk, cols), 1).reshape(keys, cols)
    col = lax.broadcasted_iota(jnp.int32, (keys, cols), 1) % (sb * t)
    cb, ct = col // t, col % t
    mask = (kb == cb) & (((kk < WINDOW) & (kk > ct)) | ((kk >= WINDOW) & ((kk - WINDOW) <= ct)))
    head_of_col = lax.broadcasted_iota(jnp.int32, (1, cols), 1) // (sb * t)
    low = lax.broadcasted_iota(jnp.int32, (sb * t, LANES), 1) < HEAD_DIM
    zero_q = jnp.zeros((sb * t, LANES), BF16)
    swap = lambda x: pltpu.roll(x, HEAD_DIM, 1)
    q_all = q_ref[...]
    kn_all = kn_ref[...]
    vn_all = vn_ref[...]
    outs_b = []
    for b0 in range(0, bt, sb):
        r0 = b0 * t
        newk = kn_all[r0:r0 + sb * t].reshape(sb, t, KV_WIDTH)
        newv = vn_all[r0:r0 + sb * t].reshape(sb, t, KV_WIDTH)
        k_all = jnp.concatenate([ck_ref[b0:b0 + sb], newk], axis=1).reshape(keys, KV_WIDTH).astype(BF16)
        v_all = jnp.concatenate([cv_ref[b0:b0 + sb], newv], axis=1).reshape(keys, KV_WIDTH).astype(BF16)
        head_out = []
        for j in range(N_KV_HEADS):
            mine = low if j == 0 else jnp.logical_not(low)
            qs = []
            for g in range(Q_PER_KV):
                h = j * Q_PER_KV + g
                tile = q_all[r0:r0 + sb * t, (h // 2) * LANES:(h // 2 + 1) * LANES]
                if h % 2 != j:
                    tile = swap(tile.astype(F32)).astype(BF16)
                qs.append(jnp.where(mine, tile, zero_q))
            qj = jnp.concatenate(qs, axis=0)
            sink = jnp.zeros((1, cols), F32)
            for g in range(Q_PER_KV):
                sink = jnp.where(head_of_col == g, sink_ref[j * Q_PER_KV + g], sink)
            s = lax.dot_general(k_all, qj, (((1,), (1,)), ((), ())), preferred_element_type=F32) * (HEAD_DIM ** -0.5)
            s = jnp.where(mask, s, -jnp.inf)
            m = jnp.maximum(jnp.max(s, axis=0, keepdims=True), sink)
            p = jnp.exp(s - m)
            p = p / (jnp.sum(p, axis=0, keepdims=True) + jnp.exp(sink - m))
            o = lax.dot_general(p.astype(BF16), v_all, (((0,), (0,)), ((), ())), preferred_element_type=F32)
            for g in range(Q_PER_KV):
                og = o[g * sb * t:(g + 1) * sb * t]
                head_out.append(og if g % 2 == j else swap(og))
        tiles = [jnp.where(low, head_out[2 * pr], head_out[2 * pr + 1]) for pr in range(N_Q_HEADS // 2)]
        outs_b.append(jnp.concatenate(tiles, axis=-1))
    att = jnp.concatenate(outs_b, axis=0)
    o_ref[...] = _rms(att, g_ref[...]).astype(BF16)


def _attn_sample(sink, q, k, v, cache_k, cache_v, g_attn_out, b, t, bt):
    row = lambda w: pl.BlockSpec((bt * t, w), lambda i: (i, 0))
    cache = pl.BlockSpec((bt, WINDOW, KV_WIDTH), lambda i: (i, 0, 0))
    return pl.pallas_call(
        functools.partial(_attn_sample_kernel, bt=bt, t=t),
        grid=(b // bt,),
        in_specs=[pl.BlockSpec(memory_space=pltpu.SMEM), row(ATTN_WIDTH), row(KV_WIDTH), row(KV_WIDTH),
                  cache, cache, _full((1, ATTN_WIDTH))],
        out_specs=row(ATTN_WIDTH),
        out_shape=jax.ShapeDtypeStruct((b * t, ATTN_WIDTH), BF16),
        compiler_params=_cparams(("arbitrary",)),
        name="window_attn_sample",
    )(sink, q, k, v, cache_k, cache_v, g_attn_out)


def _ssm_project(u_ref, perm_ref, wb_ref, vre, vim, utb, r):
    u = u_ref[...].reshape(r, SSM_WIDTH)
    u_hi = u.astype(BF16)
    u_lo = (u - u_hi.astype(F32)).astype(BF16)
    perm = perm_ref[...]
    u_tb = (jnp.dot(perm, u_hi, preferred_element_type=F32) + jnp.dot(perm, u_lo, preferred_element_type=F32))
    utb[...] = u_tb
    u_bf = u_tb.astype(BF16)
    half_w = SSM_WIDTH // 2
    for h in range(2):
        vv = jnp.dot(u_bf[:, h * half_w:(h + 1) * half_w], wb_ref[h], preferred_element_type=F32)
        vre[:, h * SSM_HALF:(h + 1) * SSM_HALF] = vv[:, :SSM_HALF]
        vim[:, h * SSM_HALF:(h + 1) * SSM_HALF] = vv[:, SSM_HALF:]


def _ssm_scan(are_ref, aim_ref, vre, vim, sre, sim, nb, s):
    lane_chunk = SSM_LANES // 2
    for rg in range(nb // SUBLANES):
        for lc in range(SSM_LANES // lane_chunk):
            lanes = slice(lc * lane_chunk, (lc + 1) * lane_chunk)
            rows0 = slice(rg * SUBLANES, (rg + 1) * SUBLANES)
            ar = jnp.broadcast_to(are_ref[:, lanes], (SUBLANES, lane_chunk))
            ai = jnp.broadcast_to(aim_ref[:, lanes], (SUBLANES, lane_chunk))

            def step(t, carry, lanes=lanes, rg=rg, ar=ar, ai=ai):
                hr, hi = carry
                row = pl.multiple_of(t * nb + rg * SUBLANES, SUBLANES)
                nhr = ar * hr - ai * hi + vre[pl.ds(row, SUBLANES), lanes]
                nhi = ar * hi + ai * hr + vim[pl.ds(row, SUBLANES), lanes]
                vre[pl.ds(row, SUBLANES), lanes] = nhr
                vim[pl.ds(row, SUBLANES), lanes] = nhi
                return nhr, nhi

            hr, hi = lax.fori_loop(0, s, step, (sre[rows0, lanes], sim[rows0, lanes]), unroll=True)
            sre[rows0, lanes] = hr
            sim[rows0, lanes] = hi


def _ssm_output(vre, vim, utb, permt_ref, wc_ref, d_ref, wglu_ref, g_ref, o_ref, nb, s):
    ys = []
    for h in range(2):
        hcat = jnp.concatenate([vre[:, h * SSM_HALF:(h + 1) * SSM_HALF], vim[:, h * SSM_HALF:(h + 1) * SSM_HALF]],
                               axis=1).astype(BF16)
        ys.append(jnp.dot(hcat, wc_ref[h], preferred_element_type=F32))
    y = jnp.concatenate(ys, axis=1) + d_ref[...] * utb[...]
    g = jax.nn.gelu(y)
    z = jnp.dot(g.astype(BF16), wglu_ref[...], preferred_element_type=F32)
    out = g * jax.nn.sigmoid(z)
    on = _rms(out, g_ref[...]).astype(BF16)
    o_bt = jnp.dot(permt_ref[...], on, preferred_element_type=F32).astype(BF16)
    o_ref[...] = o_bt.reshape(nb, s, SSM_WIDTH)


def _ssm_kernel(u_ref, perm_ref, permt_ref, wb_ref, wc_ref, are_ref, aim_ref, d_ref, wglu_ref, g_ref,
                h0re_ref, h0im_ref, o_ref, hre_ref, him_ref, vre, vim, utb, sre, sim, *, nb, s):
    c = pl.program_id(0)

    @pl.when(c == 0)
    def _():
        sre[...] = h0re_ref[...]
        sim[...] = h0im_ref[...]

    _ssm_project(u_ref, perm_ref, wb_ref, vre, vim, utb, nb * s)
    _ssm_scan(are_ref, aim_ref, vre, vim, sre, sim, nb, s)
    _ssm_output(vre, vim, utb, permt_ref, wc_ref, d_ref, wglu_ref, g_ref, o_ref, nb, s)

    @pl.when(c == pl.num_programs(0) - 1)
    def _():
        hre_ref[...] = sre[...]
        him_ref[...] = sim[...]


def _ssm(u, h0_re, h0_im, ssm_w, s):
    nb, l, _ = u.shape
    r = nb * s
    ridx = jnp.arange(r)
    src = (ridx % nb) * s + ridx // nb
    perm = (src[:, None] == ridx[None, :]).astype(BF16)
    wb, wc, a_re, a_im, d_skip, w_glu, g_out = ssm_w
    in_blk = out_blk = pl.BlockSpec((nb, s, SSM_WIDTH), lambda c: (0, c, 0))
    chunk_bufs = [pltpu.VMEM((r, SSM_LANES), F32), pltpu.VMEM((r, SSM_LANES), F32), pltpu.VMEM((r, SSM_WIDTH), F32)]
    st_blk = _full((nb, SSM_LANES))
    return pl.pallas_call(
        functools.partial(_ssm_kernel, nb=nb, s=s),
        grid=(l // s,),
        in_specs=[in_blk, _full((r, r)), _full((r, r)), _full(wb.shape), _full(wc.shape),
                  _full((1, SSM_LANES)), _full((1, SSM_LANES)), _full((1, SSM_WIDTH)),
                  _full((SSM_WIDTH, SSM_WIDTH)), _full((1, SSM_WIDTH)), st_blk, st_blk],
        out_specs=(out_blk, st_blk, st_blk),
        out_shape=(jax.ShapeDtypeStruct((nb, l, SSM_WIDTH), BF16),
                   jax.ShapeDtypeStruct((nb, SSM_LANES), F32), jax.ShapeDtypeStruct((nb, SSM_LANES), F32)),
        scratch_shapes=chunk_bufs + [pltpu.VMEM((nb, SSM_LANES), F32), pltpu.VMEM((nb, SSM_LANES), F32)],
        compiler_params=_cparams(("arbitrary",), VMEM_LIMIT),
        name="ssm_scan",
    )(u, perm, perm.T, wb, wc, a_re, a_im, d_skip, w_glu, g_out, h0_re, h0_im)


def _outproj_kernel(x_ref, att_ref, ssm_ref, wa_ref, ws_ref, gca_ref, wcq_ref, h_ref, qc_ref):
    h = (x_ref[...] + jnp.dot(att_ref[...], wa_ref[...], preferred_element_type=F32)
         + jnp.dot(ssm_ref[...], ws_ref[...], preferred_element_type=F32))
    h_ref[...] = h
    xn = _rms(h, gca_ref[...]).astype(BF16)
    qc_ref[...] = jnp.dot(xn, wcq_ref[...], preferred_element_type=F32).astype(BF16)


def _outproj(x, att, ssm, w_att, w_ssm, g_ca, w_cq, tm):
    t = x.shape[0]
    row = lambda w: pl.BlockSpec((tm, w), lambda i: (i, 0))
    return pl.pallas_call(
        _outproj_kernel,
        grid=(t // tm,),
        in_specs=[row(D_MODEL), row(ATTN_WIDTH), row(SSM_WIDTH), _full((ATTN_WIDTH, D_MODEL)),
                  _full((SSM_WIDTH, D_MODEL)), _full((1, D_MODEL)), _full((D_MODEL, CA_WIDTH))],
        out_specs=(row(D_MODEL), row(CA_WIDTH)),
        out_shape=(jax.ShapeDtypeStruct((t, D_MODEL), F32), jax.ShapeDtypeStruct((t, CA_WIDTH), BF16)),
        compiler_params=_cparams(("arbitrary",)),
        name="out_projection",
    )(x, att, ssm, w_att, w_ssm, g_ca, w_cq)


def _memkv_kernel(m_ref, g_ref, w_ref, k_ref, v_ref):
    m = _rms(m_ref[...], g_ref[...]).astype(BF16)
    kv = jnp.dot(m, w_ref[...], preferred_element_type=F32)
    k_ref[...] = kv[:, :CA_WIDTH]
    v_ref[...] = kv[:, CA_WIDTH:]


def _memkv(mem, g_mem, w_mkv_bf, tm):
    t = mem.shape[0]
    row = lambda w: pl.BlockSpec((tm, w), lambda i: (i, 0))
    return pl.pallas_call(
        _memkv_kernel,
        grid=(t // tm,),
        in_specs=[row(D_MODEL), _full((1, D_MODEL)), _full((D_MODEL, 2 * CA_WIDTH))],
        out_specs=(row(CA_WIDTH), row(CA_WIDTH)),
        out_shape=(jax.ShapeDtypeStruct((t, CA_WIDTH), F32), jax.ShapeDtypeStruct((t, CA_WIDTH), F32)),
        compiler_params=_cparams(("arbitrary",)),
        name="memory_kv",
    )(mem, g_mem, w_mkv_bf)


def _xattn_kernel(qc_ref, mk_ref, mv_ref, h_ref, wco_ref, gmoe_ref, wr_ref, br_ref, tri_ref, cin_ref, *refs,
                  nbat, rows, head_split, has_prev, tile_off):
    refs = refs[1:] if has_prev else refs
    h2_ref, xn_hbm, ti_ref, tw_ref, cout_ref, cnt, xn_buf, xn_sem = refs[:8]
    step = pl.program_id(0)
    nsteps = pl.num_programs(0)
    tm = nbat * rows

    def xn_copies(s):
        dst0 = tile_off + s * (tm // SUBLANES)
        return [pltpu.make_async_copy(xn_buf.at[:, :, sl, :], xn_hbm.at[pl.ds(dst0, tm // SUBLANES), sl], xn_sem)
                for sl in range(SUBLANES)]

    @pl.when(step == 0)
    def _():
        cnt[...] = cin_ref[...]

    if head_split:
        mbuf, msem = refs[8:]
        slot = step % 2

        def head_copies(s, sl):
            b0 = s * nbat
            return [pltpu.make_async_copy(src.at[pl.ds(b0, nbat), :, hd, :], mbuf.at[sl, kv, hd], msem.at[sl])
                    for kv, src in enumerate((mk_ref, mv_ref)) for hd in range(CA_HEADS)]

        @pl.when(step == 0)
        def _():
            for c in head_copies(0, 0):
                c.start()

        @pl.when(step + 1 < nsteps)
        def _():
            for c in head_copies(step + 1, 1 - slot):
                c.start()

        for c in head_copies(step, slot):
            c.wait()

    q_all = qc_ref[...]
    if head_split:
        keys = nbat * N_MEM
        own = (lax.broadcasted_iota(jnp.int32, (keys, tm), 0) // N_MEM
               == lax.broadcasted_iota(jnp.int32, (keys, tm), 1) // rows)
        outs = []
        for hd in range(CA_HEADS):
            sl = slice(hd * CA_HEAD_DIM, (hd + 1) * CA_HEAD_DIM)
            mk = mbuf[slot, 0, hd].reshape(keys, CA_HEAD_DIM).astype(BF16)
            mv = mbuf[slot, 1, hd].reshape(keys, CA_HEAD_DIM).astype(BF16)
            s = lax.dot_general(mk, q_all[:, sl], (((1,), (1,)), ((), ())),
                                preferred_element_type=F32) * (CA_HEAD_DIM ** -0.5)
            s = jnp.where(own, s, -jnp.inf)
            p = jnp.exp(s - jnp.max(s, axis=0, keepdims=True))
            p = p / jnp.sum(p, axis=0, keepdims=True)
            outs.append(lax.dot_general(p.astype(BF16), mv, (((0,), (0,)), ((), ())), preferred_element_type=F32))
        o = jnp.concatenate(outs, axis=-1).astype(BF16)
    else:
        outs_b = []
        for bi in range(nbat):
            q = q_all[bi * rows:(bi + 1) * rows]
            outs = []
            for hd in range(CA_HEADS):
                sl = slice(hd * CA_HEAD_DIM, (hd + 1) * CA_HEAD_DIM)
                mk = mk_ref[bi, :, sl].astype(BF16)
                mv = mv_ref[bi, :, sl].astype(BF16)
                s = lax.dot_general(q[:, sl], mk, (((1,), (1,)), ((), ())),
                                    preferred_element_type=F32) * (CA_HEAD_DIM ** -0.5)
                m = jnp.max(s, axis=-1, keepdims=True)
                p = jnp.exp(s - m)
                denom = jnp.sum(p, axis=-1, keepdims=True)
                outs.append(jnp.dot(p.astype(BF16), mv, preferred_element_type=F32) / denom)
            outs_b.append(jnp.concatenate(outs, axis=-1))
        o = jnp.concatenate(outs_b, axis=0).astype(BF16)
    h2 = h_ref[...] + jnp.dot(o, wco_ref[...], preferred_element_type=F32)
    h2_ref[...] = h2
    xn = _rms(h2, gmoe_ref[...])
    @pl.when(step >= 1)
    def _():
        for c in xn_copies(step - 1):
            c.wait()

    _store_row_tiles(xn_buf, xn)
    for c in xn_copies(step):
        c.start()

    @pl.when(step == nsteps - 1)
    def _():
        for c in xn_copies(step):
            c.wait()

    logits = jnp.dot(xn.astype(BF16), wr_ref[...], preferred_element_type=F32) + br_ref[...]
    n = logits.shape[0]
    eidx = lax.broadcasted_iota(jnp.int32, (n, N_EXPERTS), 1).astype(F32)
    lane = lax.broadcasted_iota(jnp.int32, (n, LANES), 1)
    ti = jnp.zeros((n, LANES), F32)
    tv = jnp.zeros((n, LANES), F32)
    work = logits
    vals = []
    hits = []
    for k in range(TOP_K):
        mx = jnp.max(work, axis=-1, keepdims=True)
        idx = jnp.min(jnp.where(work == mx, eidx, float(N_EXPERTS)), axis=-1, keepdims=True)
        vals.append(mx)
        hits.append(eidx == idx)
        ti = jnp.where(lane == k, idx, ti)
        work = jnp.where(hits[k], -jnp.inf, work)
    ex = [jnp.exp(v - vals[0]) for v in vals]
    tot = ex[0] + ex[1] + ex[2] + ex[3]
    for k in range(TOP_K):
        tv = jnp.where(lane == k, ex[k] / tot, tv)
    sel = jnp.zeros((n, N_EXPERTS), F32)
    for k in range(TOP_K):
        sel = sel + jnp.where(hits[k], 1.0, 0.0)
    before = jnp.dot(tri_ref[...], sel.astype(BF16), preferred_element_type=F32) + cnt[...]
    for k in range(TOP_K):
        rank = jnp.sum(jnp.where(hits[k], before, 0.0), axis=-1, keepdims=True)
        ti = jnp.where(lane == TOP_K + k, rank, ti)
    cnt[...] = cnt[...] + jnp.sum(sel, axis=0, keepdims=True)
    ti_ref[...] = ti.astype(jnp.int32)
    tw_ref[...] = tv

    @pl.when(step == pl.num_programs(0) - 1)
    def _():
        cout_ref[...] = cnt[...]


def _xattn(qc, mk, mv, h1, w_co, g_moe, w_router, b_router, counts_in, nbat, rows, t_total, t_off, xn_prev=None):
    t = qc.shape[0]
    tm = nbat * rows
    row = lambda w: pl.BlockSpec((tm, w), lambda i: (i, 0))
    seq_rows = t // mk.shape[0]
    head_split = mk.ndim == 4
    scratch = [pltpu.VMEM((1, N_EXPERTS), F32), pltpu.VMEM((tm // SUBLANES, LANE_TILES, SUBLANES, LANES), F32),
               pltpu.SemaphoreType.DMA(())]
    any_spec = pl.BlockSpec(memory_space=pl.ANY)
    prev_args, prev_specs, aliases = [], [], {}
    if xn_prev is not None:
        prev_args, prev_specs, aliases = [xn_prev], [any_spec], {10: 1}
    if head_split:
        assert seq_rows == rows
        mem = pl.BlockSpec(memory_space=pl.ANY)
        scratch += [pltpu.VMEM((2, 2, CA_HEADS, nbat, N_MEM, CA_HEAD_DIM), F32), pltpu.SemaphoreType.DMA((2,))]
    else:
        mem = pl.BlockSpec((nbat, N_MEM, CA_WIDTH), lambda i: ((i * tm) // (seq_rows * nbat), 0, 0))
    tri = jnp.tri(tm, k=-1, dtype=BF16)
    return pl.pallas_call(
        functools.partial(_xattn_kernel, nbat=nbat, rows=rows, head_split=head_split, has_prev=xn_prev is not None,
                          tile_off=t_off // SUBLANES),
        grid=(t // tm,),
        in_specs=[row(CA_WIDTH), mem, mem, row(D_MODEL), _full((CA_WIDTH, D_MODEL)), _full((1, D_MODEL)),
                  _full((D_MODEL, N_EXPERTS)), _full((1, N_EXPERTS)), _full((tm, tm)), _full((1, N_EXPERTS))]
        + prev_specs,
        out_specs=(row(D_MODEL), any_spec, row(LANES), row(LANES), _full((1, N_EXPERTS))),
        out_shape=(jax.ShapeDtypeStruct((t, D_MODEL), F32),
                   jax.ShapeDtypeStruct((t_total // SUBLANES, SUBLANES, LANE_TILES, LANES), F32),
                   jax.ShapeDtypeStruct((t, LANES), jnp.int32), jax.ShapeDtypeStruct((t, LANES), F32),
                   jax.ShapeDtypeStruct((1, N_EXPERTS), F32)),
        scratch_shapes=scratch,
        input_output_aliases=aliases,
        compiler_params=_cparams(("arbitrary",), VMEM_LIMIT),
        name="cross_attn_router",
    )(qc, mk, mv, h1, w_co, g_moe, w_router, b_router, tri, counts_in, *prev_args)


def _sc_gather(table, idx):
    n = idx.shape[0]
    per_worker = n // SC_WORKERS
    pieces = per_worker // SC_ROWS
    assert pieces * SC_ROWS * SC_WORKERS == n
    mesh = plsc.VectorSubcoreMesh(core_axis_name="c", subcore_axis_name="s")

    @functools.partial(pl.kernel, mesh=mesh, out_type=jax.ShapeDtypeStruct((n, LANE_TILES, LANES), F32),
                       scratch_types=[pltpu.VMEM((SC_ROWS,), jnp.int32), pltpu.VMEM((SC_ROWS, LANE_TILES, LANES), F32),
                                      pltpu.SemaphoreType.DMA],
                       name="moe_row_gather")
    def gather(table_hbm, idx_hbm, out_hbm, idx_v, rows_v, sem):
        worker = lax.axis_index("s") * SC_CORES + lax.axis_index("c")
        base = worker * per_worker

        @pl.loop(0, pieces)
        def _(piece):
            off = base + piece * SC_ROWS
            pltpu.sync_copy(idx_hbm.at[pl.ds(off, SC_ROWS)], idx_v)
            pltpu.async_copy(table_hbm.at[idx_v], rows_v, sem).wait()
            pltpu.sync_copy(rows_v, out_hbm.at[pl.ds(off, SC_ROWS)])

    return gather(table, idx)


def _sc_scatter(rows, dest_t, n_out):
    n = rows.shape[0]
    per_worker = n // SC_WORKERS
    step = SC_ROWS // 2
    pieces = per_worker // step
    assert pieces * step * SC_WORKERS == n
    mesh = plsc.VectorSubcoreMesh(core_axis_name="c", subcore_axis_name="s")

    @functools.partial(pl.kernel, mesh=mesh, out_type=jax.ShapeDtypeStruct((n_out, LANE_TILES, LANES), F32),
                       scratch_types=[pltpu.VMEM((step,), jnp.int32), pltpu.VMEM((step, LANE_TILES, LANES), F32)],
                       name="moe_row_scatter")
    def scatter(rows_hbm, idx_hbm, out_hbm, idx_v, rows_v):
        worker = lax.axis_index("s") * SC_CORES + lax.axis_index("c")
        base = worker * per_worker

        @pl.loop(0, pieces)
        def _(piece):
            off = base + piece * step
            pltpu.sync_copy(rows_hbm.at[pl.ds(off, step)], rows_v)
            for k in range(TOP_K):
                pltpu.sync_copy(idx_hbm.at[k, pl.ds(off, step)], idx_v)
                pltpu.sync_copy(rows_v, out_hbm.at[idx_v])

    return scatter(rows, dest_t)


def _expert_block_kernel(blk_e_ref, w_idx_ref, nused_ref, xs_hbm, wg_ref, bg_ref, wu_ref, bu_ref, wd_ref, bd_ref,
                         ys_hbm, wg_bf, wu_bf, wd_bf, xbuf, ybuf, xsem, ysem):
    i = pl.program_id(0)
    n_used = nused_ref[0]
    active = i < n_used
    slot = i % 2
    tiles = MOE_BLOCK // SUBLANES

    def x_copies(blk, sl):
        return [pltpu.make_async_copy(xs_hbm.at[pl.ds(blk * tiles, tiles), s], xbuf.at[sl, :, :, s, :], xsem.at[sl])
                for s in range(SUBLANES)]

    def y_copies(blk, sl):
        return [pltpu.make_async_copy(ybuf.at[sl, :, :, s, :], ys_hbm.at[pl.ds(blk * tiles, tiles), s], ysem.at[sl])
                for s in range(SUBLANES)]

    @pl.when((i == 0) & active)
    def _():
        for c in x_copies(0, 0):
            c.start()

    prev_e = blk_e_ref[jnp.maximum(i - 1, 0)]
    new_expert = (i == 0) | (blk_e_ref[i] != prev_e)

    @pl.when(active & new_expert)
    def _():
        wg_bf[...] = wg_ref[0].astype(BF16)
        wu_bf[...] = wu_ref[0].astype(BF16)
        wd_bf[...] = wd_ref[0].astype(BF16)

    @pl.when(active)
    def _():
        for c in x_copies(i, slot):
            c.wait()

        @pl.when(i + 1 < n_used)
        def _():
            for c in x_copies(i + 1, 1 - slot):
                c.start()

        x = _load_row_tiles(xbuf.at[slot]).astype(BF16)
        gate = jnp.minimum(jnp.dot(x, wg_bf[...], preferred_element_type=F32) + bg_ref[0], SWIGLU_LIMIT)
        up = jnp.clip(jnp.dot(x, wu_bf[...], preferred_element_type=F32) + bu_ref[0], -SWIGLU_LIMIT, SWIGLU_LIMIT)
        h = gate * jax.nn.sigmoid(SWIGLU_ALPHA * gate) * (up + 1.0)
        y = jnp.dot(h.astype(BF16), wd_bf[...], preferred_element_type=F32) + bd_ref[0]

        @pl.when(i >= 1)
        def _():
            for c in y_copies(i - 1, 1 - slot):
                c.wait()

        _store_row_tiles(ybuf.at[slot], y)
        for c in y_copies(i, slot):
            c.start()

        @pl.when(i == n_used - 1)
        def _():
            for c in y_copies(i, slot):
                c.wait()


def _experts_blocks(blk_e, w_idx, n_used, xs, w_gate, b_gate, w_up, b_up, w_down, b_down):
    n_slots = xs.shape[0]
    n_blocks = n_slots // MOE_BLOCK
    d_ff = w_gate.shape[-1]
    tiles = MOE_BLOCK // SUBLANES
    wspec = lambda a, b: pl.BlockSpec((1, a, b), lambda i, be, wi, nu: (wi[i], 0, 0))
    bspec = lambda b: pl.BlockSpec((1, 1, b), lambda i, be, wi, nu: (be[i], 0, 0))
    any_spec = pl.BlockSpec(memory_space=pl.ANY)
    grid_spec = pltpu.PrefetchScalarGridSpec(
        num_scalar_prefetch=3,
        grid=(n_blocks,),
        in_specs=[any_spec, wspec(D_MODEL, d_ff), bspec(d_ff), wspec(D_MODEL, d_ff), bspec(d_ff),
                  wspec(d_ff, D_MODEL), bspec(D_MODEL)],
        out_specs=any_spec,
        scratch_shapes=[pltpu.VMEM((D_MODEL, d_ff), BF16), pltpu.VMEM((D_MODEL, d_ff), BF16),
                        pltpu.VMEM((d_ff, D_MODEL), BF16),
                        pltpu.VMEM((2, tiles, LANE_TILES, SUBLANES, LANES), F32),
                        pltpu.VMEM((2, tiles, LANE_TILES, SUBLANES, LANES), F32),
                        pltpu.SemaphoreType.DMA((2,)), pltpu.SemaphoreType.DMA((2,))],
    )
    ys = pl.pallas_call(
        _expert_block_kernel,
        grid_spec=grid_spec,
        out_shape=jax.ShapeDtypeStruct((n_slots // SUBLANES, SUBLANES, LANE_TILES, LANES), F32),
        compiler_params=_cparams(("arbitrary",), VMEM_LIMIT),
        name="moe_experts",
    )(blk_e, w_idx, n_used, xs.reshape(n_slots // SUBLANES, SUBLANES, LANE_TILES, LANES), w_gate,
      b_gate.reshape(N_EXPERTS, 1, d_ff), w_up, b_up.reshape(N_EXPERTS, 1, d_ff), w_down,
      b_down.reshape(N_EXPERTS, 1, D_MODEL))
    return ys.reshape(n_slots, LANE_TILES, LANES)


def _expert_kernel(blk_e_ref, w_idx_ref, nused_ref, tab_hbm, xn_hbm, wg_ref, bg_ref, wu_ref, bu_ref, wd_ref, bd_ref,
                   yu_hbm, wg_bf, wu_bf, wd_bf, xbuf, ybuf, tab, gsem, ssem, tsem, *, n_blocks):
    i = pl.program_id(0)
    n_used = nused_ref[0]
    active = i < n_used
    slot = i % 2
    row_w = 2 * MOE_BLOCK

    def ring(b):
        return (b + TAB_RING) % TAB_RING

    def tab_copy(b):
        r = ring(b)
        return pltpu.make_async_copy(tab_hbm.at[jnp.minimum(b, n_blocks - 1) + 1], tab.at[pl.ds(r * row_w, row_w)],
                                     tsem.at[r])

    def block_wait(ref, sem):
        pltpu.make_async_copy(ref.at[pl.ds(0, MOE_BLOCK)], ref.at[pl.ds(0, MOE_BLOCK)], sem).wait()

    def issue_rows(off, copy):
        def body(it, carry):
            ids = [tab[off + it * DMA_BATCH + q] for q in range(DMA_BATCH)]
            for q in range(DMA_BATCH):
                copy(ids[q], it * (DMA_BATCH // SUBLANES) + q // SUBLANES, q % SUBLANES).start(priority=q % 2)
            return carry
        lax.fori_loop(0, MOE_BLOCK // DMA_BATCH, body, 0)

    def issue_gathers(b, sl):
        issue_rows(ring(b) * row_w,
                   lambda tok, j, s: pltpu.make_async_copy(xn_hbm.at[tok], xbuf.at[sl, j, :, s, :], gsem.at[sl]))

    def issue_scatters(b, sl):
        issue_rows(ring(b) * row_w + MOE_BLOCK,
                   lambda tile, j, s: pltpu.make_async_copy(ybuf.at[sl, j, :, s, :], yu_hbm.at[tile], ssem))

    @pl.when((i == 0) & active)
    def _():
        ybuf[...] = jnp.zeros(ybuf.shape, F32)
        for b in (-1, 0, 1):
            tab_copy(b).start()
            tab_copy(b).wait()
        tab_copy(2).start()
        issue_gathers(0, 0)

    prev_e = blk_e_ref[jnp.maximum(i - 1, 0)]
    new_expert = (i == 0) | (blk_e_ref[i] != prev_e)

    @pl.when(active & new_expert)
    def _():
        wg_bf[...] = wg_ref[0].astype(BF16)
        wu_bf[...] = wu_ref[0].astype(BF16)
        wd_bf[...] = wd_ref[0].astype(BF16)

    @pl.when(active)
    def _():
        block_wait(xn_hbm, gsem.at[slot])

        @pl.when(i >= 1)
        def _():
            tab_copy(i + 1).wait()
            tab_copy(i + 2).start()

        issue_gathers(i + 1, 1 - slot)
        issue_scatters(i - 1, 1 - slot)
        x = _load_row_tiles(xbuf.at[slot]).astype(BF16)
        gate = jnp.minimum(jnp.dot(x, wg_bf[...], preferred_element_type=F32) + bg_ref[0], SWIGLU_LIMIT)
        up = jnp.clip(jnp.dot(x, wu_bf[...], preferred_element_type=F32) + bu_ref[0], -SWIGLU_LIMIT, SWIGLU_LIMIT)
        h = gate * jax.nn.sigmoid(SWIGLU_ALPHA * gate) * (up + 1.0)
        y = jnp.dot(h.astype(BF16), wd_bf[...], preferred_element_type=F32) + bd_ref[0]
        _store_row_tiles(ybuf.at[slot], y)
        block_wait(yu_hbm, ssem)

        @pl.when(i == n_used - 1)
        def _():
            issue_scatters(i, slot)
            block_wait(yu_hbm, ssem)
            block_wait(xn_hbm, gsem.at[1 - slot])
            tab_copy(i + 2).wait()


def _experts(blk_e, w_idx, n_used, tab, xn, n_out_tiles, w_gate, b_gate, w_up, b_up, w_down, b_down):
    n_blocks = tab.shape[0] - 1
    d_ff = w_gate.shape[-1]
    tiles = MOE_BLOCK // SUBLANES
    wspec = lambda a, b: pl.BlockSpec((1, a, b), lambda i, be, wi, nu: (wi[i], 0, 0))
    bspec = lambda b: pl.BlockSpec((1, 1, b), lambda i, be, wi, nu: (be[i], 0, 0))
    any_spec = pl.BlockSpec(memory_space=pl.ANY)
    grid_spec = pltpu.PrefetchScalarGridSpec(
        num_scalar_prefetch=3,
        grid=(n_blocks,),
        in_specs=[any_spec, any_spec, wspec(D_MODEL, d_ff), bspec(d_ff), wspec(D_MODEL, d_ff), bspec(d_ff),
                  wspec(d_ff, D_MODEL), bspec(D_MODEL)],
        out_specs=any_spec,
        scratch_shapes=[pltpu.VMEM((D_MODEL, d_ff), BF16), pltpu.VMEM((D_MODEL, d_ff), BF16),
                        pltpu.VMEM((d_ff, D_MODEL), BF16),
                        pltpu.VMEM((2, tiles, LANE_TILES, SUBLANES, LANES), F32),
                        pltpu.VMEM((2, tiles, LANE_TILES, SUBLANES, LANES), F32),
                        pltpu.SMEM((TAB_RING * 2 * MOE_BLOCK,), jnp.int32),
                        pltpu.SemaphoreType.DMA((2,)), pltpu.SemaphoreType.DMA(()),
                        pltpu.SemaphoreType.DMA((TAB_RING,))],
    )
    return pl.pallas_call(
        functools.partial(_expert_kernel, n_blocks=n_blocks),
        grid_spec=grid_spec,
        out_shape=jax.ShapeDtypeStruct((n_out_tiles, LANE_TILES, LANES), F32),
        compiler_params=_cparams(("arbitrary",), VMEM_LIMIT),
        name="moe_experts",
    )(blk_e, w_idx, n_used, tab, xn, w_gate, b_gate.reshape(N_EXPERTS, 1, d_ff), w_up,
      b_up.reshape(N_EXPERTS, 1, d_ff), w_down, b_down.reshape(N_EXPERTS, 1, D_MODEL))


def _combine_kernel(yu_hbm, h_ref, tw_ref, g_ref, o_ref, buf, sem, *, tm, tile_off):
    i = pl.program_id(0)
    n = pl.num_programs(0)
    slot = i % 2
    tiles = tm // SUBLANES

    def copies(step, sl):
        t0 = tile_off + step * tiles
        return [pltpu.make_async_copy(yu_hbm.at[pl.ds(t0, tiles), s, k], buf.at[sl, k, :, :, s, :], sem.at[sl])
                for s in range(SUBLANES) for k in range(TOP_K)]

    @pl.when(i == 0)
    def _():
        for c in copies(0, 0):
            c.start()

    @pl.when(i + 1 < n)
    def _():
        for c in copies(i + 1, 1 - slot):
            c.start()

    for c in copies(i, slot):
        c.wait()
    tw = tw_ref[...]
    y = h_ref[...]
    for k in range(TOP_K):
        y = y + tw[:, k:k + 1] * _load_row_tiles(buf.at[slot, k])
    o_ref[...] = _rms(y, g_ref[...])


def _combine(yu, h2, tw, g_final, tm, t_off):
    t = h2.shape[0]
    row = lambda w: pl.BlockSpec((tm, w), lambda i: (i, 0))
    return pl.pallas_call(
        functools.partial(_combine_kernel, tm=tm, tile_off=t_off // SUBLANES),
        grid=(t // tm,),
        in_specs=[pl.BlockSpec(memory_space=pl.ANY), row(D_MODEL), row(LANES), _full((1, D_MODEL))],
        out_specs=row(D_MODEL),
        out_shape=jax.ShapeDtypeStruct((t, D_MODEL), F32),
        scratch_shapes=[pltpu.VMEM((2, TOP_K, tm // SUBLANES, LANE_TILES, SUBLANES, LANES), F32),
                        pltpu.SemaphoreType.DMA((2,))],
        compiler_params=_cparams(("arbitrary",), VMEM_LIMIT),
        name="moe_combine",
    )(yu, h2, tw, g_final)


def _route(counts, m):
    counts = counts.reshape(N_EXPERTS).astype(jnp.int32)
    padded = (counts + MOE_BLOCK - 1) // MOE_BLOCK * MOE_BLOCK
    pad_end = jnp.cumsum(padded)
    pad_start = pad_end - padded
    n_blocks = -(-(m + N_EXPERTS * (MOE_BLOCK - 1)) // MOE_BLOCK)
    blk_start = jnp.arange(n_blocks, dtype=jnp.int32) * MOE_BLOCK
    owner = lambda start: jnp.minimum(jnp.sum((pad_end[None, :] <= start[:, None]).astype(jnp.int32), axis=1),
                                      N_EXPERTS - 1).astype(jnp.int32)
    n_used = pad_end[-1] // MOE_BLOCK
    last_start = jnp.maximum(n_used - 1, 0) * MOE_BLOCK
    blk_e = owner(jnp.minimum(blk_start, last_start))
    first = jnp.concatenate([jnp.ones((1,), bool), blk_e[1:] != blk_e[:-1]])
    region_end = jnp.sum(jnp.where(blk_e[:, None] == jnp.arange(N_EXPERTS)[None, :], pad_end[None, :], 0), axis=1)
    w_idx = jnp.where(first, blk_e, owner(jnp.minimum(region_end, last_start))).astype(jnp.int32)
    return pad_start, blk_e, w_idx, n_used.astype(jnp.int32).reshape(1), n_blocks * MOE_BLOCK


def _dest(ti, pad_start):
    e = ti[:, :TOP_K]
    onehot = e[:, :, None] == jnp.arange(N_EXPERTS, dtype=jnp.int32)[None, None, :]
    start = jnp.sum(jnp.where(onehot, pad_start[None, None, :], 0), axis=-1)
    return (start + ti[:, TOP_K:2 * TOP_K]).reshape(-1)


def _slot_sources(dest, n_slots, n_tokens):
    m = dest.shape[0]
    filled_by = jnp.zeros((n_slots,), jnp.int32).at[dest].add(jnp.arange(m, dtype=jnp.int32) + 1) - 1
    return jnp.where(filled_by >= 0, filled_by // TOP_K, jnp.arange(n_slots, dtype=jnp.int32) % n_tokens)


def _slot_table(dest, n_slots):
    m = dest.shape[0]
    n_blocks = n_slots // MOE_BLOCK
    filled_by = jnp.zeros((n_slots,), jnp.int32).at[dest].add(jnp.arange(m, dtype=jnp.int32) + 1) - 1
    filled = filled_by >= 0
    src_tok = jnp.where(filled, filled_by, 0) // TOP_K
    dst_tile = jnp.where(filled, filled_by, m + jnp.arange(n_slots, dtype=jnp.int32) % DUMP_TILES)
    rows = jnp.concatenate([src_tok.reshape(n_blocks, MOE_BLOCK), dst_tile.reshape(n_blocks, MOE_BLOCK)], axis=1)
    before_first = jnp.concatenate([jnp.zeros((MOE_BLOCK,), jnp.int32),
                                    m + MOE_BLOCK + jnp.arange(MOE_BLOCK, dtype=jnp.int32)])[None]
    return jnp.concatenate([before_first, rows], axis=0).astype(jnp.int32)


def _block_diag(blocks):
    g, a, b = blocks.shape
    eye = jnp.eye(g, dtype=blocks.dtype)
    return (blocks[:, :, None, :] * eye[:, None, :, None]).reshape(g * a, g * b)


def kernel(x_prompt, x_sample, mem_prompt, cache_win_k, cache_win_v, state_ssm_re, state_ssm_im, cache_mem_k,
           cache_mem_v, g_mix, w_in, sink, lam_re, lam_im, log_dt, b_re, b_im, c_re, c_im, d_skip, w_glu,
           g_attn_out, g_ssm_out, w_out, g_ca, g_mem, w_mk, w_mv, w_cq, w_co, g_moe, w_router, b_router,
           w_gate, b_gate, w_up, b_up, w_down, b_down, g_final):
    depth = g_mix.shape[0]
    assert depth == 1
    bp, lp, _ = x_prompt.shape
    bs, ls, _ = x_sample.shape
    tp, ts = bp * lp, bs * ls

    w_in_bf = w_in[0].astype(BF16)
    w_att_bf = w_out[0, :ATTN_WIDTH].astype(BF16)
    w_ssm_bf = w_out[0, ATTN_WIDTH:].astype(BF16)
    w_cq_bf = w_cq[0].astype(BF16)
    w_co_bf = w_co[0].astype(BF16)
    w_mkv_bf = jnp.concatenate([w_mk[0], w_mv[0]], axis=1).astype(BF16)
    w_glu_bf = w_glu[0].astype(BF16)
    row = lambda a: a.reshape(1, -1)

    a_re, a_im, bb_re, bb_im = _discretize(lam_re[0], lam_im[0], log_dt[0], b_re[0].transpose(0, 2, 1),
                                           b_im[0].transpose(0, 2, 1))
    hg = SSM_GROUPS // 2
    wb = jnp.stack([jnp.concatenate([_block_diag(bb_re[h * hg:(h + 1) * hg]),
                                     _block_diag(bb_im[h * hg:(h + 1) * hg])], axis=1) for h in range(2)]).astype(BF16)
    cre_t = c_re[0].transpose(0, 2, 1)
    cim_t = c_im[0].transpose(0, 2, 1)
    wc = jnp.stack([jnp.concatenate([_block_diag(cre_t[h * hg:(h + 1) * hg]),
                                     -_block_diag(cim_t[h * hg:(h + 1) * hg])], axis=0) for h in range(2)]).astype(BF16)
    ssm_w = (wb, wc, a_re.reshape(1, SSM_LANES), a_im.reshape(1, SSM_LANES), row(d_skip[0]), w_glu_bf,
             row(g_ssm_out[0]))

    xp = x_prompt.reshape(tp, D_MODEL)
    xs_ = x_sample.reshape(ts, D_MODEL)

    qp, kp, vp, up = _inproj(xp, row(g_mix[0]), w_in_bf, 512)
    qs, ks, vs, us = _inproj(xs_, row(g_mix[0]), w_in_bf, 256)

    att_p = _attn_prompt(sink[0], qp, kp, vp, row(g_attn_out[0]), bp, lp)
    ck = cache_win_k[0].reshape(bs, WINDOW, KV_WIDTH)
    cv = cache_win_v[0].reshape(bs, WINDOW, KV_WIDTH)
    att_s = _attn_sample(sink[0], qs, ks, vs, ck, cv, row(g_attn_out[0]), bs, ls, 16)

    zero_state = jnp.zeros((bp, SSM_LANES), F32)
    ssm_p, p_sre, p_sim = _ssm(up.reshape(bp, lp, SSM_WIDTH), zero_state, zero_state, ssm_w, 32)
    ssm_s, s_sre, s_sim = _ssm(us.reshape(bs, ls, SSM_WIDTH), state_ssm_re[0].reshape(bs, SSM_LANES),
                               state_ssm_im[0].reshape(bs, SSM_LANES), ssm_w, ls)

    h1p, qcp = _outproj(xp, att_p, ssm_p.reshape(tp, SSM_WIDTH), w_att_bf, w_ssm_bf, row(g_ca[0]), w_cq_bf, 512)
    h1s, qcs = _outproj(xs_, att_s, ssm_s.reshape(ts, SSM_WIDTH), w_att_bf, w_ssm_bf, row(g_ca[0]), w_cq_bf, 256)

    mk_p, mv_p = _memkv(mem_prompt.reshape(bp * N_MEM, D_MODEL), row(g_mem[0]), w_mkv_bf, 512)

    no_counts = jnp.zeros((1, N_EXPERTS), F32)
    w_router_bf = w_router[0].astype(BF16)
    tt = tp + ts
    h2p, xn, tip, twp, cnt_p = _xattn(qcp, mk_p.reshape(bp, N_MEM, CA_WIDTH), mv_p.reshape(bp, N_MEM, CA_WIDTH),
                                      h1p, w_co_bf, row(g_moe[0]), w_router_bf, row(b_router[0]), no_counts,
                                      1, 512, tt, 0)
    h2s, xn, tis, tws, cnt = _xattn(qcs, cache_mem_k[0], cache_mem_v[0], h1s, w_co_bf, row(g_moe[0]),
                                    w_router_bf, row(b_router[0]), cnt_p, 8, ls, tt, tp, xn_prev=xn)

    m = tt * TOP_K
    pad_start, blk_e, w_idx, n_used, n_slots = _route(cnt, m)
    dest = jnp.concatenate([_dest(tip, pad_start), _dest(tis, pad_start)])
    xs = _sc_scatter(xn.reshape(tt, LANE_TILES, LANES), dest.reshape(tt, TOP_K).T, n_slots)
    ys = _experts_blocks(blk_e, w_idx, n_used, xs, w_gate[0], b_gate[0], w_up[0], b_up[0], w_down[0], b_down[0])
    yu = _sc_gather(ys, dest).reshape(m // (SUBLANES * TOP_K), SUBLANES, TOP_K, LANE_TILES, LANES)
    y_prompt = _combine(yu, h2p, twp, row(g_final), 256, 0).reshape(bp, lp, D_MODEL)
    y_sample = _combine(yu, h2s, tws, row(g_final), 256, tp).reshape(bs, ls, D_MODEL)

    kp4 = kp.reshape(bp, lp, N_KV_HEADS, HEAD_DIM)
    vp4 = vp.reshape(bp, lp, N_KV_HEADS, HEAD_DIM)
    p_win_k = kp4[:, -WINDOW:][None]
    p_win_v = vp4[:, -WINDOW:][None]
    s_win_k = jnp.concatenate([cache_win_k[0], ks.reshape(bs, ls, N_KV_HEADS, HEAD_DIM)], axis=1)[:, -WINDOW:][None]
    s_win_v = jnp.concatenate([cache_win_v[0], vs.reshape(bs, ls, N_KV_HEADS, HEAD_DIM)], axis=1)[:, -WINDOW:][None]
    st = lambda a, b: a.reshape(1, b, SSM_GROUPS, SSM_STATE)
    p_mem_k = mk_p.reshape(1, bp, N_MEM, CA_HEADS, CA_HEAD_DIM)
    p_mem_v = mv_p.reshape(1, bp, N_MEM, CA_HEADS, CA_HEAD_DIM)
    return (y_prompt, y_sample, p_win_k, p_win_v, st(p_sre, bp), st(p_sim, bp), p_mem_k, p_mem_v,
            s_win_k, s_win_v, st(s_sre, bs), st(s_sim, bs))
```

```python
import functools

import jax
import jax.numpy as jnp
from jax import lax
from jax.experimental import pallas as pl
from jax.experimental.pallas import tpu as pltpu
from jax.experimental.pallas import tpu_sc as plsc

F32 = jnp.float32
BF16 = jnp.bfloat16

D_MODEL = 1024
HEAD_DIM = 64
N_Q_HEADS = 8
N_KV_HEADS = 2
Q_PER_KV = 4
ATTN_WIDTH = 512
KV_WIDTH = 128
WINDOW = 128
SSM_WIDTH = 512
SSM_GROUP = 16
SSM_GROUPS = 32
SSM_STATE = 64
SSM_LANES = SSM_GROUPS * SSM_STATE
SSM_HALF = SSM_LANES // 2
IN_WIDTH = ATTN_WIDTH + 2 * KV_WIDTH + SSM_WIDTH
N_MEM = 256
CA_HEADS = 4
CA_HEAD_DIM = 128
CA_WIDTH = 512
N_EXPERTS = 32
TOP_K = 4
SWIGLU_ALPHA = 1.702
SWIGLU_LIMIT = 7.0
RMS_EPS = 1e-5

SUBLANES = 8
LANES = 128
MOE_BLOCK = 256
ATTN_SEQS = 4
SC_CORES = 2
SC_WORKERS = 32
SC_ROWS = 64
COMBINE_CHUNKS = 4
VMEM_LIMIT = 56 * 1024 * 1024


def _cparams(sem, vmem=None):
    return pltpu.CompilerParams(dimension_semantics=sem, vmem_limit_bytes=vmem)


def _rms(x, g):
    return x * lax.rsqrt(jnp.mean(x * x, axis=-1, keepdims=True) + RMS_EPS) * g


def _full(shape):
    return pl.BlockSpec(shape, lambda *_: (0,) * len(shape))


LANE_TILES = D_MODEL // LANES


def _store_row_tiles(ref, x):
    for c in range(LANE_TILES):
        ref[:, c] = x[:, c * LANES:(c + 1) * LANES].reshape(x.shape[0] // SUBLANES, SUBLANES, LANES)


def _load_row_tiles(ref):
    rows = ref.shape[0] * SUBLANES
    return jnp.concatenate([ref[:, c].reshape(rows, LANES) for c in range(LANE_TILES)], axis=1)


def _disc_kernel(lr_ref, li_ref, ldt_ref, bre_ref, bim_ref, are_ref, aim_ref, bbre_ref, bbim_ref):
    lr = lr_ref[...]
    li = li_ref[...]
    dt = jnp.exp(ldt_ref[...])
    mag = jnp.exp(lr * dt)
    ang = li * dt
    ab_re = mag * jnp.cos(ang)
    ab_im = mag * jnp.sin(ang)
    nr = ab_re - 1.0
    ni = ab_im
    den = lr * lr + li * li
    f_re = (nr * lr + ni * li) / den
    f_im = (ni * lr - nr * li) / den
    are_ref[...] = ab_re
    aim_ref[...] = ab_im
    br = bre_ref[...]
    bi = bim_ref[...]
    bbre_ref[...] = f_re[:, None, :] * br - f_im[:, None, :] * bi
    bbim_ref[...] = f_re[:, None, :] * bi + f_im[:, None, :] * br


def _discretize(lam_re, lam_im, log_dt, b_re_t, b_im_t):
    g, n = lam_re.shape
    c = b_re_t.shape[1]
    return pl.pallas_call(
        _disc_kernel,
        out_shape=(jax.ShapeDtypeStruct((g, n), F32), jax.ShapeDtypeStruct((g, n), F32),
                   jax.ShapeDtypeStruct((g, c, n), F32), jax.ShapeDtypeStruct((g, c, n), F32)),
        name="ssm_discretize",
    )(lam_re, lam_im, log_dt.reshape(g, 1), b_re_t, b_im_t)


def _inproj_kernel(x_ref, g_ref, w_ref, q_ref, k_ref, v_ref, u_ref):
    a = _rms(x_ref[...], g_ref[...]).astype(BF16)
    proj = jnp.dot(a, w_ref[...], preferred_element_type=F32)
    q_ref[...] = proj[:, :ATTN_WIDTH].astype(BF16)
    k_ref[...] = proj[:, ATTN_WIDTH:ATTN_WIDTH + KV_WIDTH]
    v_ref[...] = proj[:, ATTN_WIDTH + KV_WIDTH:ATTN_WIDTH + 2 * KV_WIDTH]
    u_ref[...] = proj[:, ATTN_WIDTH + 2 * KV_WIDTH:]


def _inproj(x, g_mix, w_in_bf, tm):
    t = x.shape[0]
    row = lambda w: pl.BlockSpec((tm, w), lambda i: (i, 0))
    return pl.pallas_call(
        _inproj_kernel,
        grid=(t // tm,),
        in_specs=[row(D_MODEL), _full((1, D_MODEL)), _full((D_MODEL, IN_WIDTH))],
        out_specs=(row(ATTN_WIDTH), row(KV_WIDTH), row(KV_WIDTH), row(SSM_WIDTH)),
        out_shape=(jax.ShapeDtypeStruct((t, ATTN_WIDTH), BF16), jax.ShapeDtypeStruct((t, KV_WIDTH), F32),
                   jax.ShapeDtypeStruct((t, KV_WIDTH), F32), jax.ShapeDtypeStruct((t, SSM_WIDTH), F32)),
        compiler_params=_cparams(("arbitrary",)),
        name="in_projection",
    )(x, g_mix, w_in_bf)


def _sink_softmax_pv(s, sink, v_bf):
    m = jnp.maximum(jnp.max(s, axis=-1, keepdims=True), sink)
    p = jnp.exp(s - m)
    denom = jnp.sum(p, axis=-1, keepdims=True) + jnp.exp(sink - m)
    return jnp.dot(p.astype(BF16), v_bf, preferred_element_type=F32) / denom


def _attn_prompt_kernel(sink_ref, q_ref, kc_ref, kp_ref, vc_ref, vp_ref, g_ref, o_ref):
    i = pl.program_id(1)
    q = q_ref[...]
    kk = jnp.concatenate([kp_ref[...], kc_ref[...]], axis=0).astype(BF16)
    vv = jnp.concatenate([vp_ref[...], vc_ref[...]], axis=0).astype(BF16)
    rows = Q_PER_KV * WINDOW
    qi = lax.broadcasted_iota(jnp.int32, (rows, 2 * WINDOW), 0) & (WINDOW - 1)
    kj = lax.broadcasted_iota(jnp.int32, (rows, 2 * WINDOW), 1)
    rel = qi + WINDOW - kj
    mask = (rel >= 0) & (rel < WINDOW) & ((kj >= WINDOW) | (i > 0))
    head_of_row = lax.broadcasted_iota(jnp.int32, (rows, 1), 0) // WINDOW
    low_q = lax.broadcasted_iota(jnp.int32, (WINDOW, LANES), 1) < HEAD_DIM
    zero_q = jnp.zeros((WINDOW, LANES), BF16)
    pairs = []
    for j in range(N_KV_HEADS):
        kh = kk[:, j * HEAD_DIM:(j + 1) * HEAD_DIM]
        vh = vv[:, j * HEAD_DIM:(j + 1) * HEAD_DIM]
        k2 = jnp.concatenate([kh, kh], axis=1)
        v2 = jnp.concatenate([vh, vh], axis=1)
        stacked = []
        for p in range(Q_PER_KV // 2):
            qp = q[:, (j * Q_PER_KV + 2 * p) * HEAD_DIM:(j * Q_PER_KV + 2 * p + 2) * HEAD_DIM]
            stacked += [jnp.where(low_q, qp, zero_q), jnp.where(low_q, zero_q, qp)]
        qs = jnp.concatenate(stacked, axis=0)
        sink = jnp.zeros((rows, 1), F32)
        for g in range(Q_PER_KV):
            sink = jnp.where(head_of_row == g, sink_ref[j * Q_PER_KV + g], sink)
        s = lax.dot_general(qs, k2, (((1,), (1,)), ((), ())), preferred_element_type=F32) * (HEAD_DIM ** -0.5)
        s = jnp.where(mask, s, -jnp.inf)
        o = _sink_softmax_pv(s, sink, v2)
        for p in range(Q_PER_KV // 2):
            pairs.append(jnp.where(low_q, o[2 * p * WINDOW:(2 * p + 1) * WINDOW],
                                   o[(2 * p + 1) * WINDOW:(2 * p + 2) * WINDOW]))
    att = jnp.concatenate(pairs, axis=-1)
    o_ref[...] = _rms(att, g_ref[...]).astype(BF16)


def _attn_prompt(sink, q, k, v, g_attn_out, b, l):
    nb = l // WINDOW
    cur = lambda w: pl.BlockSpec((WINDOW, w), lambda bi, i: (bi * nb + i, 0))
    prev = lambda w: pl.BlockSpec((WINDOW, w), lambda bi, i: (bi * nb + jnp.maximum(i - 1, 0), 0))
    return pl.pallas_call(
        _attn_prompt_kernel,
        grid=(b, nb),
        in_specs=[pl.BlockSpec(memory_space=pltpu.SMEM), cur(ATTN_WIDTH), cur(KV_WIDTH), prev(KV_WIDTH),
                  cur(KV_WIDTH), prev(KV_WIDTH), _full((1, ATTN_WIDTH))],
        out_specs=cur(ATTN_WIDTH),
        out_shape=jax.ShapeDtypeStruct((b * l, ATTN_WIDTH), BF16),
        compiler_params=_cparams(("arbitrary", "arbitrary")),
        name="window_attn_prompt",
    )(sink, q, k, k, v, v, g_attn_out)


def _attn_sample_kernel(sink_ref, q_ref, kn_ref, vn_ref, ck_ref, cv_ref, g_ref, o_ref, *, bt, t):
    nk = WINDOW + t
    sb = ATTN_SEQS
    keys = sb * nk
    cols = Q_PER_KV * sb * t
    kb = lax.broadcasted_iota(jnp.int32, (sb, nk, cols), 0).reshape(keys, cols)
    kk = lax.broadcasted_iota(jnp.int32, (sb, nk, cols), 1).reshape(keys, cols)
    col = lax.broadcasted_iota(jnp.int32, (keys, cols), 1) % (sb * t)
    cb, ct = col // t, col % t
    mask = (kb == cb) & (((kk < WINDOW) & (kk > ct)) | ((kk >= WINDOW) & ((kk - WINDOW) <= ct)))
    head_of_col = lax.broadcasted_iota(jnp.int32, (1, cols), 1) // (sb * t)
    low = lax.broadcasted_iota(jnp.int32, (sb * t, LANES), 1) < HEAD_DIM
    zero_q = jnp.zeros((sb * t, LANES), BF16)
    swap = lambda x: pltpu.roll(x, HEAD_DIM, 1)
    q_all = q_ref[...]
    kn_all = kn_ref[...]
    vn_all = vn_ref[...]
    outs_b = []
    for b0 in range(0, bt, sb):
        r0 = b0 * t
        newk = kn_all[r0:r0 + sb * t].reshape(sb, t, KV_WIDTH)
        newv = vn_all[r0:r0 + sb * t].reshape(sb, t, KV_WIDTH)
        k_all = jnp.concatenate([ck_ref[b0:b0 + sb], newk], axis=1).reshape(keys, KV_WIDTH).astype(BF16)
        v_all = jnp.concatenate([cv_ref[b0:b0 + sb], newv], axis=1).reshape(keys, KV_WIDTH).astype(BF16)
        head_out = []
        for j in range(N_KV_HEADS):
            mine = low if j == 0 else jnp.logical_not(low)
            qs = []
            for g in range(Q_PER_KV):
                h = j * Q_PER_KV + g
                tile = q_all[r0:r0 + sb * t, (h // 2) * LANES:(h // 2 + 1) * LANES]
                if h % 2 != j:
                    tile = swap(tile.astype(F32)).astype(BF16)
                qs.append(jnp.where(mine, tile, zero_q))
            qj = jnp.concatenate(qs, axis=0)
            sink = jnp.zeros((1, cols), F32)
            for g in range(Q_PER_KV):
                sink = jnp.where(head_of_col == g, sink_ref[j * Q_PER_KV + g], sink)
            s = lax.dot_general(k_all, qj, (((1,), (1,)), ((), ())), preferred_element_type=F32) * (HEAD_DIM ** -0.5)
            s = jnp.where(mask, s, -jnp.inf)
            m = jnp.maximum(jnp.max(s, axis=0, keepdims=True), sink)
            p = jnp.exp(s - m)
            p = p / (jnp.sum(p, axis=0, keepdims=True) + jnp.exp(sink - m))
            o = lax.dot_general(p.astype(BF16), v_all, (((0,), (0,)), ((), ())), preferred_element_type=F32)
            for g in range(Q_PER_KV):
                og = o[g * sb * t:(g + 1) * sb * t]
                head_out.append(og if g % 2 == j else swap(og))
        tiles = [jnp.where(low, head_out[2 * pr], head_out[2 * pr + 1]) for pr in range(N_Q_HEADS // 2)]
        outs_b.append(jnp.concatenate(tiles, axis=-1))
    att = jnp.concatenate(outs_b, axis=0)
    o_ref[...] = _rms(att, g_ref[...]).astype(BF16)


def _attn_sample(sink, q, k, v, cache_k, cache_v, g_attn_out, b, t, bt):
    row = lambda w: pl.BlockSpec((bt * t, w), lambda i: (i, 0))
    cache = pl.BlockSpec((bt, WINDOW, KV_WIDTH), lambda i: (i, 0, 0))
    return pl.pallas_call(
        functools.partial(_attn_sample_kernel, bt=bt, t=t),
        grid=(b // bt,),
        in_specs=[pl.BlockSpec(memory_space=pltpu.SMEM), row(ATTN_WIDTH), row(KV_WIDTH), row(KV_WIDTH),
                  cache, cache, _full((1, ATTN_WIDTH))],
        out_specs=row(ATTN_WIDTH),
        out_shape=jax.ShapeDtypeStruct((b * t, ATTN_WIDTH), BF16),
        compiler_params=_cparams(("arbitrary",)),
        name="window_attn_sample",
    )(sink, q, k, v, cache_k, cache_v, g_attn_out)


def _ssm_project(u_ref, perm_ref, wb_ref, vre, vim, utb, r):
    u = u_ref[...].reshape(r, SSM_WIDTH)
    u_hi = u.astype(BF16)
    u_lo = (u - u_hi.astype(F32)).astype(BF16)
    perm = perm_ref[...]
    u_tb = (jnp.dot(perm, u_hi, preferred_element_type=F32) + jnp.dot(perm, u_lo, preferred_element_type=F32))
    utb[...] = u_tb
    u_bf = u_tb.astype(BF16)
    half_w = SSM_WIDTH // 2
    for h in range(2):
        vv = jnp.dot(u_bf[:, h * half_w:(h + 1) * half_w], wb_ref[h], preferred_element_type=F32)
        vre[:, h * SSM_HALF:(h + 1) * SSM_HALF] = vv[:, :SSM_HALF]
        vim[:, h * SSM_HALF:(h + 1) * SSM_HALF] = vv[:, SSM_HALF:]


def _ssm_scan(are_ref, aim_ref, vre, vim, sre, sim, nb, s):
    lane_chunk = SSM_LANES // 2
    for rg in range(nb // SUBLANES):
        for lc in range(SSM_LANES // lane_chunk):
            lanes = slice(lc * lane_chunk, (lc + 1) * lane_chunk)
            rows0 = slice(rg * SUBLANES, (rg + 1) * SUBLANES)
            ar = jnp.broadcast_to(are_ref[:, lanes], (SUBLANES, lane_chunk))
            ai = jnp.broadcast_to(aim_ref[:, lanes], (SUBLANES, lane_chunk))

            def step(t, carry, lanes=lanes, rg=rg, ar=ar, ai=ai):
                hr, hi = carry
                row = pl.multiple_of(t * nb + rg * SUBLANES, SUBLANES)
                nhr = ar * hr - ai * hi + vre[pl.ds(row, SUBLANES), lanes]
                nhi = ar * hi + ai * hr + vim[pl.ds(row, SUBLANES), lanes]
                vre[pl.ds(row, SUBLANES), lanes] = nhr
                vim[pl.ds(row, SUBLANES), lanes] = nhi
                return nhr, nhi

            hr, hi = lax.fori_loop(0, s, step, (sre[rows0, lanes], sim[rows0, lanes]), unroll=True)
            sre[rows0, lanes] = hr
            sim[rows0, lanes] = hi


def _ssm_output(vre, vim, utb, permt_ref, wc_ref, d_ref, wglu_ref, g_ref, o_ref, nb, s):
    ys = []
    for h in range(2):
        hcat = jnp.concatenate([vre[:, h * SSM_HALF:(h + 1) * SSM_HALF], vim[:, h * SSM_HALF:(h + 1) * SSM_HALF]],
                               axis=1).astype(BF16)
        ys.append(jnp.dot(hcat, wc_ref[h], preferred_element_type=F32))
    y = jnp.concatenate(ys, axis=1) + d_ref[...] * utb[...]
    g = jax.nn.gelu(y)
    z = jnp.dot(g.astype(BF16), wglu_ref[...], preferred_element_type=F32)
    out = g * jax.nn.sigmoid(z)
    on = _rms(out, g_ref[...]).astype(BF16)
    o_bt = jnp.dot(permt_ref[...], on, preferred_element_type=F32).astype(BF16)
    o_ref[...] = o_bt.reshape(nb, s, SSM_WIDTH)


def _ssm_kernel(u_ref, perm_ref, permt_ref, wb_ref, wc_ref, are_ref, aim_ref, d_ref, wglu_ref, g_ref,
                h0re_ref, h0im_ref, o_ref, hre_ref, him_ref, vre, vim, utb, sre, sim, *, nb, s):
    c = pl.program_id(0)

    @pl.when(c == 0)
    def _():
        sre[...] = h0re_ref[...]
        sim[...] = h0im_ref[...]

    _ssm_project(u_ref, perm_ref, wb_ref, vre, vim, utb, nb * s)
    _ssm_scan(are_ref, aim_ref, vre, vim, sre, sim, nb, s)
    _ssm_output(vre, vim, utb, permt_ref, wc_ref, d_ref, wglu_ref, g_ref, o_ref, nb, s)

    @pl.when(c == pl.num_programs(0) - 1)
    def _():
        hre_ref[...] = sre[...]
        him_ref[...] = sim[...]


def _ssm(u, h0_re, h0_im, ssm_w, s):
    nb, l, _ = u.shape
    r = nb * s
    ridx = jnp.arange(r)
    src = (ridx % nb) * s + ridx // nb
    perm = (src[:, None] == ridx[None, :]).astype(BF16)
    wb, wc, a_re, a_im, d_skip, w_glu, g_out = ssm_w
    in_blk = out_blk = pl.BlockSpec((nb, s, SSM_WIDTH), lambda c: (0, c, 0))
    chunk_bufs = [pltpu.VMEM((r, SSM_LANES), F32), pltpu.VMEM((r, SSM_LANES), F32), pltpu.VMEM((r, SSM_WIDTH), F32)]
    st_blk = _full((nb, SSM_LANES))
    return pl.pallas_call(
        functools.partial(_ssm_kernel, nb=nb, s=s),
        grid=(l // s,),
        in_specs=[in_blk, _full((r, r)), _full((r, r)), _full(wb.shape), _full(wc.shape),
                  _full((1, SSM_LANES)), _full((1, SSM_LANES)), _full((1, SSM_WIDTH)),
                  _full((SSM_WIDTH, SSM_WIDTH)), _full((1, SSM_WIDTH)), st_blk, st_blk],
        out_specs=(out_blk, st_blk, st_blk),
        out_shape=(jax.ShapeDtypeStruct((nb, l, SSM_WIDTH), BF16),
                   jax.ShapeDtypeStruct((nb, SSM_LANES), F32), jax.ShapeDtypeStruct((nb, SSM_LANES), F32)),
        scratch_shapes=chunk_bufs + [pltpu.VMEM((nb, SSM_LANES), F32), pltpu.VMEM((nb, SSM_LANES), F32)],
        compiler_params=_cparams(("arbitrary",), VMEM_LIMIT),
        name="ssm_scan",
    )(u, perm, perm.T, wb, wc, a_re, a_im, d_skip, w_glu, g_out, h0_re, h0_im)


def _outproj_kernel(x_ref, att_ref, ssm_ref, wa_ref, ws_ref, gca_ref, wcq_ref, h_ref, qc_ref):
    h = (x_ref[...] + jnp.dot(att_ref[...], wa_ref[...], preferred_element_type=F32)
         + jnp.dot(ssm_ref[...], ws_ref[...], preferred_element_type=F32))
    h_ref[...] = h
    xn = _rms(h, gca_ref[...]).astype(BF16)
    qc_ref[...] = jnp.dot(xn, wcq_ref[...], preferred_element_type=F32).astype(BF16)


def _outproj(x, att, ssm, w_att, w_ssm, g_ca, w_cq, tm):
    t = x.shape[0]
    row = lambda w: pl.BlockSpec((tm, w), lambda i: (i, 0))
    return pl.pallas_call(
        _outproj_kernel,
        grid=(t // tm,),
        in_specs=[row(D_MODEL), row(ATTN_WIDTH), row(SSM_WIDTH), _full((ATTN_WIDTH, D_MODEL)),
                  _full((SSM_WIDTH, D_MODEL)), _full((1, D_MODEL)), _full((D_MODEL, CA_WIDTH))],
        out_specs=(row(D_MODEL), row(CA_WIDTH)),
        out_shape=(jax.ShapeDtypeStruct((t, D_MODEL), F32), jax.ShapeDtypeStruct((t, CA_WIDTH), BF16)),
        compiler_params=_cparams(("arbitrary",)),
        name="out_projection",
    )(x, att, ssm, w_att, w_ssm, g_ca, w_cq)


def _memkv_kernel(m_ref, g_ref, w_ref, k_ref, v_ref):
    m = _rms(m_ref[...], g_ref[...]).astype(BF16)
    kv = jnp.dot(m, w_ref[...], preferred_element_type=F32)
    k_ref[...] = kv[:, :CA_WIDTH]
    v_ref[...] = kv[:, CA_WIDTH:]


def _memkv(mem, g_mem, w_mkv_bf, tm):
    t = mem.shape[0]
    row = lambda w: pl.BlockSpec((tm, w), lambda i: (i, 0))
    return pl.pallas_call(
        _memkv_kernel,
        grid=(t // tm,),
        in_specs=[row(D_MODEL), _full((1, D_MODEL)), _full((D_MODEL, 2 * CA_WIDTH))],
        out_specs=(row(CA_WIDTH), row(CA_WIDTH)),
        out_shape=(jax.ShapeDtypeStruct((t, CA_WIDTH), F32), jax.ShapeDtypeStruct((t, CA_WIDTH), F32)),
        compiler_params=_cparams(("arbitrary",)),
        name="memory_kv",
    )(mem, g_mem, w_mkv_bf)


def _xattn_kernel(qc_ref, mk_ref, mv_ref, h_ref, wco_ref, gmoe_ref, wr_ref, br_ref, tri_ref, cin_ref, *refs,
                  nbat, rows, head_split, has_prev, tile_off):
    refs = refs[1:] if has_prev else refs
    h2_ref, xn_hbm, ti_ref, tw_ref, cout_ref, cnt, xn_buf, xn_sem = refs[:8]
    step = pl.program_id(0)
    nsteps = pl.num_programs(0)
    tm = nbat * rows

    def xn_copies(s):
        dst0 = tile_off + s * (tm // SUBLANES)
        return [pltpu.make_async_copy(xn_buf.at[:, :, sl, :], xn_hbm.at[pl.ds(dst0, tm // SUBLANES), sl], xn_sem)
                for sl in range(SUBLANES)]

    @pl.when(step == 0)
    def _():
        cnt[...] = cin_ref[...]

    if head_split:
        mbuf, msem = refs[8:]
        slot = step % 2

        def head_copies(s, sl):
            b0 = s * nbat
            return [pltpu.make_async_copy(src.at[pl.ds(b0, nbat), :, hd, :], mbuf.at[sl, kv, hd], msem.at[sl])
                    for kv, src in enumerate((mk_ref, mv_ref)) for hd in range(CA_HEADS)]

        @pl.when(step == 0)
        def _():
            for c in head_copies(0, 0):
                c.start()

        @pl.when(step + 1 < nsteps)
        def _():
            for c in head_copies(step + 1, 1 - slot):
                c.start()

        for c in head_copies(step, slot):
            c.wait()

    q_all = qc_ref[...]
    if head_split:
        keys = nbat * N_MEM
        own = (lax.broadcasted_iota(jnp.int32, (keys, tm), 0) // N_MEM
               == lax.broadcasted_iota(jnp.int32, (keys, tm), 1) // rows)
        outs = []
        for hd in range(CA_HEADS):
            sl = slice(hd * CA_HEAD_DIM, (hd + 1) * CA_HEAD_DIM)
            mk = mbuf[slot, 0, hd].reshape(keys, CA_HEAD_DIM).astype(BF16)
            mv = mbuf[slot, 1, hd].reshape(keys, CA_HEAD_DIM).astype(BF16)
            s = lax.dot_general(mk, q_all[:, sl], (((1,), (1,)), ((), ())),
                                preferred_element_type=F32) * (CA_HEAD_DIM ** -0.5)
            s = jnp.where(own, s, -jnp.inf)
            p = jnp.exp(s - jnp.max(s, axis=0, keepdims=True))
            p = p / jnp.sum(p, axis=0, keepdims=True)
            outs.append(lax.dot_general(p.astype(BF16), mv, (((0,), (0,)), ((), ())), preferred_element_type=F32))
        o = jnp.concatenate(outs, axis=-1).astype(BF16)
    else:
        outs_b = []
        for bi in range(nbat):
            q = q_all[bi * rows:(bi + 1) * rows]
            outs = []
            for hd in range(CA_HEADS):
                sl = slice(hd * CA_HEAD_DIM, (hd + 1) * CA_HEAD_DIM)
                mk = mk_ref[bi, :, sl].astype(BF16)
                mv = mv_ref[bi, :, sl].astype(BF16)
                s = lax.dot_general(q[:, sl], mk, (((1,), (1,)), ((), ())),
                                    preferred_element_type=F32) * (CA_HEAD_DIM ** -0.5)
                m = jnp.max(s, axis=-1, keepdims=True)
                p = jnp.exp(s - m)
                denom = jnp.sum(p, axis=-1, keepdims=True)
                outs.append(jnp.dot(p.astype(BF16), mv, preferred_element_type=F32) / denom)
            outs_b.append(jnp.concatenate(outs, axis=-1))
        o = jnp.concatenate(outs_b, axis=0).astype(BF16)
    h2 = h_ref[...] + jnp.dot(o, wco_ref[...], preferred_element_type=F32)
    h2_ref[...] = h2
    xn = _rms(h2, gmoe_ref[...])
    @pl.when(step >= 1)
    def _():
        for c in xn_copies(step - 1):
            c.wait()

    _store_row_tiles(xn_buf, xn)
    for c in xn_copies(step):
        c.start()

    @pl.when(step == nsteps - 1)
    def _():
        for c in xn_copies(step):
            c.wait()

    logits = jnp.dot(xn.astype(BF16), wr_ref[...], preferred_element_type=F32) + br_ref[...]
    n = logits.shape[0]
    eidx = lax.broadcasted_iota(jnp.int32, (n, N_EXPERTS), 1).astype(F32)
    lane = lax.broadcasted_iota(jnp.int32, (n, LANES), 1)
    ti = jnp.zeros((n, LANES), F32)
    tv = jnp.zeros((n, LANES), F32)
    work = logits
    vals = []
    hits = []
    for k in range(TOP_K):
        mx = jnp.max(work, axis=-1, keepdims=True)
        idx = jnp.min(jnp.where(work == mx, eidx, float(N_EXPERTS)), axis=-1, keepdims=True)
        vals.append(mx)
        hits.append(eidx == idx)
        ti = jnp.where(lane == k, idx, ti)
        work = jnp.where(hits[k], -jnp.inf, work)
    ex = [jnp.exp(v - vals[0]) for v in vals]
    tot = ex[0] + ex[1] + ex[2] + ex[3]
    for k in range(TOP_K):
        tv = jnp.where(lane == k, ex[k] / tot, tv)
    sel = jnp.zeros((n, N_EXPERTS), F32)
    for k in range(TOP_K):
        sel = sel + jnp.where(hits[k], 1.0, 0.0)
    before = jnp.dot(tri_ref[...], sel.astype(BF16), preferred_element_type=F32) + cnt[...]
    for k in range(TOP_K):
        rank = jnp.sum(jnp.where(hits[k], before, 0.0), axis=-1, keepdims=True)
        ti = jnp.where(lane == TOP_K + k, rank, ti)
    cnt[...] = cnt[...] + jnp.sum(sel, axis=0, keepdims=True)
    if tm % LANES == 0:
        ti_ref[...] = jnp.transpose(ti)[:2 * TOP_K].astype(jnp.int32)
    else:
        ti_ref[...] = ti.astype(jnp.int32)
    tw_ref[...] = tv

    @pl.when(step == pl.num_programs(0) - 1)
    def _():
        cout_ref[...] = cnt[...]


def _xattn(qc, mk, mv, h1, w_co, g_moe, w_router, b_router, counts_in, nbat, rows, t_total, t_off, xn_prev=None):
    t = qc.shape[0]
    tm = nbat * rows
    row = lambda w: pl.BlockSpec((tm, w), lambda i: (i, 0))
    seq_rows = t // mk.shape[0]
    head_split = mk.ndim == 4
    scratch = [pltpu.VMEM((1, N_EXPERTS), F32), pltpu.VMEM((tm // SUBLANES, LANE_TILES, SUBLANES, LANES), F32),
               pltpu.SemaphoreType.DMA(())]
    any_spec = pl.BlockSpec(memory_space=pl.ANY)
    prev_args, prev_specs, aliases = [], [], {}
    if xn_prev is not None:
        prev_args, prev_specs, aliases = [xn_prev], [any_spec], {10: 1}
    if head_split:
        assert seq_rows == rows
        mem = pl.BlockSpec(memory_space=pl.ANY)
        scratch += [pltpu.VMEM((2, 2, CA_HEADS, nbat, N_MEM, CA_HEAD_DIM), F32), pltpu.SemaphoreType.DMA((2,))]
    else:
        mem = pl.BlockSpec((nbat, N_MEM, CA_WIDTH), lambda i: ((i * tm) // (seq_rows * nbat), 0, 0))
    tri = jnp.tri(tm, k=-1, dtype=BF16)
    if tm % LANES == 0:
        ti_spec = pl.BlockSpec((2 * TOP_K, tm), lambda i: (0, i))
        ti_shape = jax.ShapeDtypeStruct((2 * TOP_K, t), jnp.int32)
    else:
        ti_spec, ti_shape = row(LANES), jax.ShapeDtypeStruct((t, LANES), jnp.int32)
    return pl.pallas_call(
        functools.partial(_xattn_kernel, nbat=nbat, rows=rows, head_split=head_split, has_prev=xn_prev is not None,
                          tile_off=t_off // SUBLANES),
        grid=(t // tm,),
        in_specs=[row(CA_WIDTH), mem, mem, row(D_MODEL), _full((CA_WIDTH, D_MODEL)), _full((1, D_MODEL)),
                  _full((D_MODEL, N_EXPERTS)), _full((1, N_EXPERTS)), _full((tm, tm)), _full((1, N_EXPERTS))]
        + prev_specs,
        out_specs=(row(D_MODEL), any_spec, ti_spec, row(LANES), _full((1, N_EXPERTS))),
        out_shape=(jax.ShapeDtypeStruct((t, D_MODEL), F32),
                   jax.ShapeDtypeStruct((t_total // SUBLANES, SUBLANES, LANE_TILES, LANES), F32),
                   ti_shape, jax.ShapeDtypeStruct((t, LANES), F32),
                   jax.ShapeDtypeStruct((1, N_EXPERTS), F32)),
        scratch_shapes=scratch,
        input_output_aliases=aliases,
        compiler_params=_cparams(("arbitrary",), VMEM_LIMIT),
        name="cross_attn_router",
    )(qc, mk, mv, h1, w_co, g_moe, w_router, b_router, tri, counts_in, *prev_args)


def _sc_gather(table, idx):
    n = idx.shape[0]
    per_worker = n // SC_WORKERS
    pieces = per_worker // SC_ROWS
    assert pieces * SC_ROWS * SC_WORKERS == n
    mesh = plsc.VectorSubcoreMesh(core_axis_name="c", subcore_axis_name="s")

    @functools.partial(pl.kernel, mesh=mesh, out_type=jax.ShapeDtypeStruct((n, LANE_TILES, LANES), F32),
                       scratch_types=[pltpu.VMEM((SC_ROWS,), jnp.int32), pltpu.VMEM((SC_ROWS, LANE_TILES, LANES), F32),
                                      pltpu.SemaphoreType.DMA],
                       name="moe_row_gather")
    def gather(table_hbm, idx_hbm, out_hbm, idx_v, rows_v, sem):
        worker = lax.axis_index("s") * SC_CORES + lax.axis_index("c")
        base = worker * per_worker

        @pl.loop(0, pieces)
        def _(piece):
            off = base + piece * SC_ROWS
            pltpu.sync_copy(idx_hbm.at[pl.ds(off, SC_ROWS)], idx_v)
            pltpu.async_copy(table_hbm.at[idx_v], rows_v, sem).wait()
            pltpu.sync_copy(rows_v, out_hbm.at[pl.ds(off, SC_ROWS)])

    return gather(table, idx)


def _sc_scatter(rows, dest_t, n_out):
    n = rows.shape[0]
    per_worker = n // SC_WORKERS
    step = SC_ROWS // 2
    pieces = per_worker // step
    assert pieces * step * SC_WORKERS == n
    mesh = plsc.VectorSubcoreMesh(core_axis_name="c", subcore_axis_name="s")

    @functools.partial(pl.kernel, mesh=mesh, out_type=jax.ShapeDtypeStruct((n_out, LANE_TILES, LANES), F32),
                       scratch_types=[pltpu.VMEM((step,), jnp.int32), pltpu.VMEM((step, LANE_TILES, LANES), F32)],
                       name="moe_row_scatter")
    def scatter(rows_hbm, idx_hbm, out_hbm, idx_v, rows_v):
        worker = lax.axis_index("s") * SC_CORES + lax.axis_index("c")
        base = worker * per_worker

        @pl.loop(0, pieces)
        def _(piece):
            off = base + piece * step
            pltpu.sync_copy(rows_hbm.at[pl.ds(off, step)], rows_v)
            for k in range(TOP_K):
                pltpu.sync_copy(idx_hbm.at[k, pl.ds(off, step)], idx_v)
                pltpu.sync_copy(rows_v, out_hbm.at[idx_v])

    return scatter(rows, dest_t)


def _expert_block_kernel(blk_e_ref, w_idx_ref, nused_ref, xs_hbm, wg_ref, bg_ref, wu_ref, bu_ref, wd_ref, bd_ref,
                         ys_hbm, wg_bf, wu_bf, wd_bf, xbuf, ybuf, xsem, ysem):
    i = pl.program_id(0)
    n_used = nused_ref[0]
    active = i < n_used
    slot = i % 2
    tiles = MOE_BLOCK // SUBLANES

    def x_copies(blk, sl):
        return [pltpu.make_async_copy(xs_hbm.at[pl.ds(blk * tiles, tiles), s], xbuf.at[sl, :, :, s, :], xsem.at[sl])
                for s in range(SUBLANES)]

    def y_copies(blk, sl):
        return [pltpu.make_async_copy(ybuf.at[sl, :, :, s, :], ys_hbm.at[pl.ds(blk * tiles, tiles), s], ysem.at[sl])
                for s in range(SUBLANES)]

    @pl.when((i == 0) & active)
    def _():
        for c in x_copies(0, 0):
            c.start()

    prev_e = blk_e_ref[jnp.maximum(i - 1, 0)]
    new_expert = (i == 0) | (blk_e_ref[i] != prev_e)

    @pl.when(active & new_expert)
    def _():
        wg_bf[...] = wg_ref[0].astype(BF16)
        wu_bf[...] = wu_ref[0].astype(BF16)
        wd_bf[...] = wd_ref[0].astype(BF16)

    @pl.when(active)
    def _():
        for c in x_copies(i, slot):
            c.wait()

        @pl.when(i + 1 < n_used)
        def _():
            for c in x_copies(i + 1, 1 - slot):
                c.start()

        x = _load_row_tiles(xbuf.at[slot]).astype(BF16)
        gate = jnp.minimum(jnp.dot(x, wg_bf[...], preferred_element_type=F32) + bg_ref[0], SWIGLU_LIMIT)
        up = jnp.clip(jnp.dot(x, wu_bf[...], preferred_element_type=F32) + bu_ref[0], -SWIGLU_LIMIT, SWIGLU_LIMIT)
        h = gate * jax.nn.sigmoid(SWIGLU_ALPHA * gate) * (up + 1.0)
        y = jnp.dot(h.astype(BF16), wd_bf[...], preferred_element_type=F32) + bd_ref[0]

        @pl.when(i >= 1)
        def _():
            for c in y_copies(i - 1, 1 - slot):
                c.wait()

        _store_row_tiles(ybuf.at[slot], y)
        for c in y_copies(i, slot):
            c.start()

        @pl.when(i == n_used - 1)
        def _():
            for c in y_copies(i, slot):
                c.wait()


def _experts_blocks(blk_e, w_idx, n_used, xs, w_gate, b_gate, w_up, b_up, w_down, b_down):
    n_slots = xs.shape[0]
    n_blocks = n_slots // MOE_BLOCK
    d_ff = w_gate.shape[-1]
    tiles = MOE_BLOCK // SUBLANES
    wspec = lambda a, b: pl.BlockSpec((1, a, b), lambda i, be, wi, nu: (wi[i], 0, 0))
    bspec = lambda b: pl.BlockSpec((1, 1, b), lambda i, be, wi, nu: (be[i], 0, 0))
    any_spec = pl.BlockSpec(memory_space=pl.ANY)
    grid_spec = pltpu.PrefetchScalarGridSpec(
        num_scalar_prefetch=3,
        grid=(n_blocks,),
        in_specs=[any_spec, wspec(D_MODEL, d_ff), bspec(d_ff), wspec(D_MODEL, d_ff), bspec(d_ff),
                  wspec(d_ff, D_MODEL), bspec(D_MODEL)],
        out_specs=any_spec,
        scratch_shapes=[pltpu.VMEM((D_MODEL, d_ff), BF16), pltpu.VMEM((D_MODEL, d_ff), BF16),
                        pltpu.VMEM((d_ff, D_MODEL), BF16),
                        pltpu.VMEM((2, tiles, LANE_TILES, SUBLANES, LANES), F32),
                        pltpu.VMEM((2, tiles, LANE_TILES, SUBLANES, LANES), F32),
                        pltpu.SemaphoreType.DMA((2,)), pltpu.SemaphoreType.DMA((2,))],
    )
    ys = pl.pallas_call(
        _expert_block_kernel,
        grid_spec=grid_spec,
        out_shape=jax.ShapeDtypeStruct((n_slots // SUBLANES, SUBLANES, LANE_TILES, LANES), F32),
        compiler_params=_cparams(("arbitrary",), VMEM_LIMIT),
        name="moe_experts",
    )(blk_e, w_idx, n_used, xs.reshape(n_slots // SUBLANES, SUBLANES, LANE_TILES, LANES), w_gate,
      b_gate.reshape(N_EXPERTS, 1, d_ff), w_up, b_up.reshape(N_EXPERTS, 1, d_ff), w_down,
      b_down.reshape(N_EXPERTS, 1, D_MODEL))
    return ys.reshape(n_slots, LANE_TILES, LANES)


def _combine_kernel(yu_hbm, h_ref, tw_ref, g_ref, *rest, tm):
    o_ref, buf, sem = rest[-3:]
    i = pl.program_id(0)
    n = pl.num_programs(0)
    slot = i % 2
    tiles = tm // SUBLANES

    def copies(step, sl):
        t0 = step * tiles
        return [pltpu.make_async_copy(yu_hbm.at[k, pl.ds(t0, tiles), s], buf.at[sl, k, :, :, s, :], sem.at[sl])
                for s in range(SUBLANES) for k in range(TOP_K)]

    @pl.when(i == 0)
    def _():
        for c in copies(0, 0):
            c.start()

    @pl.when(i + 1 < n)
    def _():
        for c in copies(i + 1, 1 - slot):
            c.start()

    for c in copies(i, slot):
        c.wait()
    tw = tw_ref[...]
    y = h_ref[...]
    for k in range(TOP_K):
        y = y + tw[:, k:k + 1] * _load_row_tiles(buf.at[slot, k])
    o_ref[...] = _rms(y, g_ref[...])


def _combine(yu, h2, tw, g_final, tm, row_off, y_prev=None):
    t = h2.shape[0]
    n_rows = yu.shape[1] * SUBLANES
    blk0 = row_off // tm
    row = lambda w: pl.BlockSpec((tm, w), lambda i: (i + blk0, 0))
    args, in_specs, aliases = [yu, h2, tw, g_final], [pl.BlockSpec(memory_space=pl.ANY), row(D_MODEL), row(LANES),
                                                      _full((1, D_MODEL))], {}
    if y_prev is not None:
        args.append(y_prev)
        in_specs.append(pl.BlockSpec(memory_space=pl.ANY))
        aliases = {4: 0}
    return pl.pallas_call(
        functools.partial(_combine_kernel, tm=tm),
        grid=(n_rows // tm,),
        in_specs=in_specs,
        out_specs=row(D_MODEL),
        out_shape=jax.ShapeDtypeStruct((t, D_MODEL), F32),
        scratch_shapes=[pltpu.VMEM((2, TOP_K, tm // SUBLANES, LANE_TILES, SUBLANES, LANES), F32),
                        pltpu.SemaphoreType.DMA((2,))],
        input_output_aliases=aliases,
        compiler_params=_cparams(("arbitrary",), VMEM_LIMIT),
        name="moe_combine",
    )(*args)


def _route(counts, m):
    counts = counts.reshape(N_EXPERTS).astype(jnp.int32)
    padded = (counts + MOE_BLOCK - 1) // MOE_BLOCK * MOE_BLOCK
    pad_end = jnp.cumsum(padded)
    pad_start = pad_end - padded
    n_blocks = -(-(m + N_EXPERTS * (MOE_BLOCK - 1)) // MOE_BLOCK)
    blk_start = jnp.arange(n_blocks, dtype=jnp.int32) * MOE_BLOCK
    owner = lambda start: jnp.minimum(jnp.sum((pad_end[None, :] <= start[:, None]).astype(jnp.int32), axis=1),
                                      N_EXPERTS - 1).astype(jnp.int32)
    n_used = pad_end[-1] // MOE_BLOCK
    last_start = jnp.maximum(n_used - 1, 0) * MOE_BLOCK
    blk_e = owner(jnp.minimum(blk_start, last_start))
    first = jnp.concatenate([jnp.ones((1,), bool), blk_e[1:] != blk_e[:-1]])
    region_end = jnp.sum(jnp.where(blk_e[:, None] == jnp.arange(N_EXPERTS)[None, :], pad_end[None, :], 0), axis=1)
    w_idx = jnp.where(first, blk_e, owner(jnp.minimum(region_end, last_start))).astype(jnp.int32)
    return pad_start, blk_e, w_idx, n_used.astype(jnp.int32).reshape(1), n_blocks * MOE_BLOCK


def _dest(ti, pad_start):
    if ti.shape[0] != 2 * TOP_K:
        ti = ti[:, :2 * TOP_K].T
    e = ti[:TOP_K]
    start = jnp.sum(jnp.where(e[None] == jnp.arange(N_EXPERTS, dtype=jnp.int32)[:, None, None],
                              pad_start[:, None, None], 0), axis=0)
    return start + ti[TOP_K:]


def _block_diag(blocks):
    g, a, b = blocks.shape
    eye = jnp.eye(g, dtype=blocks.dtype)
    return (blocks[:, :, None, :] * eye[:, None, :, None]).reshape(g * a, g * b)


def kernel(x_prompt, x_sample, mem_prompt, cache_win_k, cache_win_v, state_ssm_re, state_ssm_im, cache_mem_k,
           cache_mem_v, g_mix, w_in, sink, lam_re, lam_im, log_dt, b_re, b_im, c_re, c_im, d_skip, w_glu,
           g_attn_out, g_ssm_out, w_out, g_ca, g_mem, w_mk, w_mv, w_cq, w_co, g_moe, w_router, b_router,
           w_gate, b_gate, w_up, b_up, w_down, b_down, g_final):
    depth = g_mix.shape[0]
    assert depth == 1
    bp, lp, _ = x_prompt.shape
    bs, ls, _ = x_sample.shape
    tp, ts = bp * lp, bs * ls

    w_in_bf = w_in[0].astype(BF16)
    w_att_bf = w_out[0, :ATTN_WIDTH].astype(BF16)
    w_ssm_bf = w_out[0, ATTN_WIDTH:].astype(BF16)
    w_cq_bf = w_cq[0].astype(BF16)
    w_co_bf = w_co[0].astype(BF16)
    w_mkv_bf = jnp.concatenate([w_mk[0], w_mv[0]], axis=1).astype(BF16)
    w_glu_bf = w_glu[0].astype(BF16)
    row = lambda a: a.reshape(1, -1)

    a_re, a_im, bb_re, bb_im = _discretize(lam_re[0], lam_im[0], log_dt[0], b_re[0].transpose(0, 2, 1),
                                           b_im[0].transpose(0, 2, 1))
    hg = SSM_GROUPS // 2
    wb = jnp.stack([jnp.concatenate([_block_diag(bb_re[h * hg:(h + 1) * hg]),
                                     _block_diag(bb_im[h * hg:(h + 1) * hg])], axis=1) for h in range(2)]).astype(BF16)
    cre_t = c_re[0].transpose(0, 2, 1)
    cim_t = c_im[0].transpose(0, 2, 1)
    wc = jnp.stack([jnp.concatenate([_block_diag(cre_t[h * hg:(h + 1) * hg]),
                                     -_block_diag(cim_t[h * hg:(h + 1) * hg])], axis=0) for h in range(2)]).astype(BF16)
    ssm_w = (wb, wc, a_re.reshape(1, SSM_LANES), a_im.reshape(1, SSM_LANES), row(d_skip[0]), w_glu_bf,
             row(g_ssm_out[0]))

    xp = x_prompt.reshape(tp, D_MODEL)
    xs_ = x_sample.reshape(ts, D_MODEL)

    qp, kp, vp, up = _inproj(xp, row(g_mix[0]), w_in_bf, 512)
    qs, ks, vs, us = _inproj(xs_, row(g_mix[0]), w_in_bf, 256)

    att_p = _attn_prompt(sink[0], qp, kp, vp, row(g_attn_out[0]), bp, lp)
    ck = cache_win_k[0].reshape(bs, WINDOW, KV_WIDTH)
    cv = cache_win_v[0].reshape(bs, WINDOW, KV_WIDTH)
    att_s = _attn_sample(sink[0], qs, ks, vs, ck, cv, row(g_attn_out[0]), bs, ls, 16)

    zero_state = jnp.zeros((bp, SSM_LANES), F32)
    ssm_p, p_sre, p_sim = _ssm(up.reshape(bp, lp, SSM_WIDTH), zero_state, zero_state, ssm_w, 32)
    ssm_s, s_sre, s_sim = _ssm(us.reshape(bs, ls, SSM_WIDTH), state_ssm_re[0].reshape(bs, SSM_LANES),
                               state_ssm_im[0].reshape(bs, SSM_LANES), ssm_w, ls)

    h1p, qcp = _outproj(xp, att_p, ssm_p.reshape(tp, SSM_WIDTH), w_att_bf, w_ssm_bf, row(g_ca[0]), w_cq_bf, 512)
    h1s, qcs = _outproj(xs_, att_s, ssm_s.reshape(ts, SSM_WIDTH), w_att_bf, w_ssm_bf, row(g_ca[0]), w_cq_bf, 256)

    mk_p, mv_p = _memkv(mem_prompt.reshape(bp * N_MEM, D_MODEL), row(g_mem[0]), w_mkv_bf, 512)

    no_counts = jnp.zeros((1, N_EXPERTS), F32)
    w_router_bf = w_router[0].astype(BF16)
    tt = tp + ts
    h2p, xn, tip, twp, cnt_p = _xattn(qcp, mk_p.reshape(bp, N_MEM, CA_WIDTH), mv_p.reshape(bp, N_MEM, CA_WIDTH),
                                      h1p, w_co_bf, row(g_moe[0]), w_router_bf, row(b_router[0]), no_counts,
                                      1, 512, tt, 0)
    h2s, xn, tis, tws, cnt = _xattn(qcs, cache_mem_k[0], cache_mem_v[0], h1s, w_co_bf, row(g_moe[0]),
                                    w_router_bf, row(b_router[0]), cnt_p, 8, ls, tt, tp, xn_prev=xn)

    m = tt * TOP_K
    pad_start, blk_e, w_idx, n_used, n_slots = _route(cnt, m)
    dest = jnp.concatenate([_dest(tip, pad_start), _dest(tis, pad_start)], axis=1)
    xs = _sc_scatter(xn.reshape(tt, LANE_TILES, LANES), dest, n_slots)
    ys = _experts_blocks(blk_e, w_idx, n_used, xs, w_gate[0], b_gate[0], w_up[0], b_up[0], w_down[0], b_down[0])
    def gathered(t0, t1):
        rows = _sc_gather(ys, dest[:, t0:t1].reshape(-1))
        return rows.reshape(TOP_K, (t1 - t0) // SUBLANES, SUBLANES, LANE_TILES, LANES)

    y_prompt = None
    chunk = tp // COMBINE_CHUNKS
    for c in range(COMBINE_CHUNKS):
        y_prompt = _combine(gathered(c * chunk, (c + 1) * chunk), h2p, twp, row(g_final), 256, c * chunk,
                            y_prev=y_prompt)
    y_prompt = y_prompt.reshape(bp, lp, D_MODEL)
    y_sample = _combine(gathered(tp, tt), h2s, tws, row(g_final), 256, 0).reshape(bs, ls, D_MODEL)

    kp4 = kp.reshape(bp, lp, N_KV_HEADS, HEAD_DIM)
    vp4 = vp.reshape(bp, lp, N_KV_HEADS, HEAD_DIM)
    p_win_k = kp4[:, -WINDOW:][None]
    p_win_v = vp4[:, -WINDOW:][None]
    s_win_k = jnp.concatenate([cache_win_k[0], ks.reshape(bs, ls, N_KV_HEADS, HEAD_DIM)], axis=1)[:, -WINDOW:][None]
    s_win_v = jnp.concatenate([cache_win_v[0], vs.reshape(bs, ls, N_KV_HEADS, HEAD_DIM)], axis=1)[:, -WINDOW:][None]
    st = lambda a, b: a.reshape(1, b, SSM_GROUPS, SSM_STATE)
    p_mem_k = mk_p.reshape(1, bp, N_MEM, CA_HEADS, CA_HEAD_DIM)
    p_mem_v = mv_p.reshape(1, bp, N_MEM, CA_HEADS, CA_HEAD_DIM)
    return (y_prompt, y_sample, p_win_k, p_win_v, st(p_sre, bp), st(p_sim, bp), p_mem_k, p_mem_v,
            s_win_k, s_win_v, st(s_sre, bs), st(s_sim, bs))
```

```python
import functools

import jax
import jax.numpy as jnp
from jax import lax
from jax.experimental import pallas as pl
from jax.experimental.pallas import tpu as pltpu
from jax.experimental.pallas import tpu_sc as plsc

F32 = jnp.float32
BF16 = jnp.bfloat16

D_MODEL = 1024
HEAD_DIM = 64
N_Q_HEADS = 8
N_KV_HEADS = 2
Q_PER_KV = 4
ATTN_WIDTH = 512
KV_WIDTH = 128
WINDOW = 128
SSM_WIDTH = 512
SSM_GROUP = 16
SSM_GROUPS = 32
SSM_STATE = 64
SSM_LANES = SSM_GROUPS * SSM_STATE
SSM_HALF = SSM_LANES // 2
IN_WIDTH = ATTN_WIDTH + 2 * KV_WIDTH + SSM_WIDTH
N_MEM = 256
CA_HEADS = 4
CA_HEAD_DIM = 128
CA_WIDTH = 512
N_EXPERTS = 32
TOP_K = 4
SWIGLU_ALPHA = 1.702
SWIGLU_LIMIT = 7.0
RMS_EPS = 1e-5

SUBLANES = 8
LANES = 128
MOE_BLOCK = 256
ATTN_SEQS = 4
SC_CORES = 2
SC_WORKERS = 32
SC_ROWS = 64
COMBINE_CHUNKS = 4
VMEM_LIMIT = 56 * 1024 * 1024


def _cparams(sem, vmem=None):
    return pltpu.CompilerParams(dimension_semantics=sem, vmem_limit_bytes=vmem)


def _rms(x, g):
    return x * lax.rsqrt(jnp.mean(x * x, axis=-1, keepdims=True) + RMS_EPS) * g


def _full(shape):
    return pl.BlockSpec(shape, lambda *_: (0,) * len(shape))


LANE_TILES = D_MODEL // LANES


def _store_row_tiles(ref, x):
    for c in range(LANE_TILES):
        ref[:, c] = x[:, c * LANES:(c + 1) * LANES].reshape(x.shape[0] // SUBLANES, SUBLANES, LANES)


def _load_row_tiles(ref):
    rows = ref.shape[0] * SUBLANES
    return jnp.concatenate([ref[:, c].reshape(rows, LANES) for c in range(LANE_TILES)], axis=1)


def _disc_kernel(lr_ref, li_ref, ldt_ref, bre_ref, bim_ref, are_ref, aim_ref, bbre_ref, bbim_ref):
    lr = lr_ref[...]
    li = li_ref[...]
    dt = jnp.exp(ldt_ref[...])
    mag = jnp.exp(lr * dt)
    ang = li * dt
    ab_re = mag * jnp.cos(ang)
    ab_im = mag * jnp.sin(ang)
    nr = ab_re - 1.0
    ni = ab_im
    den = lr * lr + li * li
    f_re = (nr * lr + ni * li) / den
    f_im = (ni * lr - nr * li) / den
    are_ref[...] = ab_re
    aim_ref[...] = ab_im
    br = bre_ref[...]
    bi = bim_ref[...]
    bbre_ref[...] = f_re[:, None, :] * br - f_im[:, None, :] * bi
    bbim_ref[...] = f_re[:, None, :] * bi + f_im[:, None, :] * br


def _discretize(lam_re, lam_im, log_dt, b_re_t, b_im_t):
    g, n = lam_re.shape
    c = b_re_t.shape[1]
    return pl.pallas_call(
        _disc_kernel,
        out_shape=(jax.ShapeDtypeStruct((g, n), F32), jax.ShapeDtypeStruct((g, n), F32),
                   jax.ShapeDtypeStruct((g, c, n), F32), jax.ShapeDtypeStruct((g, c, n), F32)),
        name="ssm_discretize",
    )(lam_re, lam_im, log_dt.reshape(g, 1), b_re_t, b_im_t)


def _inproj_kernel(x_ref, g_ref, w_ref, q_ref, k_ref, v_ref, u_ref):
    a = _rms(x_ref[...], g_ref[...]).astype(BF16)
    proj = jnp.dot(a, w_ref[...], preferred_element_type=F32)
    q_ref[...] = proj[:, :ATTN_WIDTH].astype(BF16)
    k_ref[...] = proj[:, ATTN_WIDTH:ATTN_WIDTH + KV_WIDTH]
    v_ref[...] = proj[:, ATTN_WIDTH + KV_WIDTH:ATTN_WIDTH + 2 * KV_WIDTH]
    u_ref[...] = proj[:, ATTN_WIDTH + 2 * KV_WIDTH:]


def _inproj(x, g_mix, w_in_bf, tm):
    t = x.shape[0]
    row = lambda w: pl.BlockSpec((tm, w), lambda i: (i, 0))
    return pl.pallas_call(
        _inproj_kernel,
        grid=(t // tm,),
        in_specs=[row(D_MODEL), _full((1, D_MODEL)), _full((D_MODEL, IN_WIDTH))],
        out_specs=(row(ATTN_WIDTH), row(KV_WIDTH), row(KV_WIDTH), row(SSM_WIDTH)),
        out_shape=(jax.ShapeDtypeStruct((t, ATTN_WIDTH), BF16), jax.ShapeDtypeStruct((t, KV_WIDTH), F32),
                   jax.ShapeDtypeStruct((t, KV_WIDTH), F32), jax.ShapeDtypeStruct((t, SSM_WIDTH), F32)),
        compiler_params=_cparams(("arbitrary",)),
        name="in_projection",
    )(x, g_mix, w_in_bf)


def _sink_softmax_pv(s, sink, v_bf):
    m = jnp.maximum(jnp.max(s, axis=-1, keepdims=True), sink)
    p = jnp.exp(s - m)
    denom = jnp.sum(p, axis=-1, keepdims=True) + jnp.exp(sink - m)
    return jnp.dot(p.astype(BF16), v_bf, preferred_element_type=F32) / denom


def _attn_prompt_kernel(sink_ref, q_ref, kc_ref, kp_ref, vc_ref, vp_ref, g_ref, o_ref):
    i = pl.program_id(1)
    q = q_ref[...]
    kk = jnp.concatenate([kp_ref[...], kc_ref[...]], axis=0).astype(BF16)
    vv = jnp.concatenate([vp_ref[...], vc_ref[...]], axis=0).astype(BF16)
    rows = Q_PER_KV * WINDOW
    qi = lax.broadcasted_iota(jnp.int32, (rows, 2 * WINDOW), 0) & (WINDOW - 1)
    kj = lax.broadcasted_iota(jnp.int32, (rows, 2 * WINDOW), 1)
    rel = qi + WINDOW - kj
    mask = (rel >= 0) & (rel < WINDOW) & ((kj >= WINDOW) | (i > 0))
    head_of_row = lax.broadcasted_iota(jnp.int32, (rows, 1), 0) // WINDOW
    low_q = lax.broadcasted_iota(jnp.int32, (WINDOW, LANES), 1) < HEAD_DIM
    zero_q = jnp.zeros((WINDOW, LANES), BF16)
    pairs = []
    for j in range(N_KV_HEADS):
        kh = kk[:, j * HEAD_DIM:(j + 1) * HEAD_DIM]
        vh = vv[:, j * HEAD_DIM:(j + 1) * HEAD_DIM]
        k2 = jnp.concatenate([kh, kh], axis=1)
        v2 = jnp.concatenate([vh, vh], axis=1)
        stacked = []
        for p in range(Q_PER_KV // 2):
            qp = q[:, (j * Q_PER_KV + 2 * p) * HEAD_DIM:(j * Q_PER_KV + 2 * p + 2) * HEAD_DIM]
            stacked += [jnp.where(low_q, qp, zero_q), jnp.where(low_q, zero_q, qp)]
        qs = jnp.concatenate(stacked, axis=0)
        sink = jnp.zeros((rows, 1), F32)
        for g in range(Q_PER_KV):
            sink = jnp.where(head_of_row == g, sink_ref[j * Q_PER_KV + g], sink)
        s = lax.dot_general(qs, k2, (((1,), (1,)), ((), ())), preferred_element_type=F32) * (HEAD_DIM ** -0.5)
        s = jnp.where(mask, s, -jnp.inf)
        o = _sink_softmax_pv(s, sink, v2)
        for p in range(Q_PER_KV // 2):
            pairs.append(jnp.where(low_q, o[2 * p * WINDOW:(2 * p + 1) * WINDOW],
                                   o[(2 * p + 1) * WINDOW:(2 * p + 2) * WINDOW]))
    att = jnp.concatenate(pairs, axis=-1)
    o_ref[...] = _rms(att, g_ref[...]).astype(BF16)


def _attn_prompt(sink, q, k, v, g_attn_out, b, l):
    nb = l // WINDOW
    cur = lambda w: pl.BlockSpec((WINDOW, w), lambda bi, i: (bi * nb + i, 0))
    prev = lambda w: pl.BlockSpec((WINDOW, w), lambda bi, i: (bi * nb + jnp.maximum(i - 1, 0), 0))
    return pl.pallas_call(
        _attn_prompt_kernel,
        grid=(b, nb),
        in_specs=[pl.BlockSpec(memory_space=pltpu.SMEM), cur(ATTN_WIDTH), cur(KV_WIDTH), prev(KV_WIDTH),
                  cur(KV_WIDTH), prev(KV_WIDTH), _full((1, ATTN_WIDTH))],
        out_specs=cur(ATTN_WIDTH),
        out_shape=jax.ShapeDtypeStruct((b * l, ATTN_WIDTH), BF16),
        compiler_params=_cparams(("arbitrary", "arbitrary")),
        name="window_attn_prompt",
    )(sink, q, k, k, v, v, g_attn_out)


def _attn_sample_kernel(sink_ref, q_ref, kn_ref, vn_ref, ck_ref, cv_ref, g_ref, o_ref, *, bt, t):
    nk = WINDOW + t
    sb = ATTN_SEQS
    keys = sb * nk
    cols = Q_PER_KV * sb * t
    kb = lax.broadcasted_iota(jnp.int32, (sb, nk, cols), 0).reshape(keys, cols)
    kk = lax.broadcasted_iota(jnp.int32, (sb, nk, cols), 1).reshape(keys, cols)
    col = lax.broadcasted_iota(jnp.int32, (keys, cols), 1) % (sb * t)
    cb, ct = col // t, col % t
    mask = (kb == cb) & (((kk < WINDOW) & (kk > ct)) | ((kk >= WINDOW) & ((kk - WINDOW) <= ct)))
    head_of_col = lax.broadcasted_iota(jnp.int32, (1, cols), 1) // (sb * t)
    low = lax.broadcasted_iota(jnp.int32, (sb * t, LANES), 1) < HEAD_DIM
    zero_q = jnp.zeros((sb * t, LANES), BF16)
    swap = lambda x: pltpu.roll(x, HEAD_DIM, 1)
    q_all = q_ref[...]
    kn_all = kn_ref[...]
    vn_all = vn_ref[...]
    outs_b = []
    for b0 in range(0, bt, sb):
        r0 = b0 * t
        newk = kn_all[r0:r0 + sb * t].reshape(sb, t, KV_WIDTH)
        newv = vn_all[r0:r0 + sb * t].reshape(sb, t, KV_WIDTH)
        k_all = jnp.concatenate([ck_ref[b0:b0 + sb], newk], axis=1).reshape(keys, KV_WIDTH).astype(BF16)
        v_all = jnp.concatenate([cv_ref[b0:b0 + sb], newv], axis=1).reshape(keys, KV_WIDTH).astype(BF16)
        head_out = []
        for j in range(N_KV_HEADS):
            mine = low if j == 0 else jnp.logical_not(low)
            qs = []
            for g in range(Q_PER_KV):
                h = j * Q_PER_KV + g
                tile = q_all[r0:r0 + sb * t, (h // 2) * LANES:(h // 2 + 1) * LANES]
                if h % 2 != j:
                    tile = swap(tile.astype(F32)).astype(BF16)
                qs.append(jnp.where(mine, tile, zero_q))
            qj = jnp.concatenate(qs, axis=0)
            sink = jnp.zeros((1, cols), F32)
            for g in range(Q_PER_KV):
                sink = jnp.where(head_of_col == g, sink_ref[j * Q_PER_KV + g], sink)
            s = lax.dot_general(k_all, qj, (((1,), (1,)), ((), ())), preferred_element_type=F32) * (HEAD_DIM ** -0.5)
            s = jnp.where(mask, s, -jnp.inf)
            m = jnp.maximum(jnp.max(s, axis=0, keepdims=True), sink)
            p = jnp.exp(s - m)
            p = p / (jnp.sum(p, axis=0, keepdims=True) + jnp.exp(sink - m))
            o = lax.dot_general(p.astype(BF16), v_all, (((0,), (0,)), ((), ())), preferred_element_type=F32)
            for g in range(Q_PER_KV):
                og = o[g * sb * t:(g + 1) * sb * t]
                head_out.append(og if g % 2 == j else swap(og))
        tiles = [jnp.where(low, head_out[2 * pr], head_out[2 * pr + 1]) for pr in range(N_Q_HEADS // 2)]
        outs_b.append(jnp.concatenate(tiles, axis=-1))
    att = jnp.concatenate(outs_b, axis=0)
    o_ref[...] = _rms(att, g_ref[...]).astype(BF16)


def _attn_sample(sink, q, k, v, cache_k, cache_v, g_attn_out, b, t, bt):
    row = lambda w: pl.BlockSpec((bt * t, w), lambda i: (i, 0))
    cache = pl.BlockSpec((bt, WINDOW, KV_WIDTH), lambda i: (i, 0, 0))
    return pl.pallas_call(
        functools.partial(_attn_sample_kernel, bt=bt, t=t),
        grid=(b // bt,),
        in_specs=[pl.BlockSpec(memory_space=pltpu.SMEM), row(ATTN_WIDTH), row(KV_WIDTH), row(KV_WIDTH),
                  cache, cache, _full((1, ATTN_WIDTH))],
        out_specs=row(ATTN_WIDTH),
        out_shape=jax.ShapeDtypeStruct((b * t, ATTN_WIDTH), BF16),
        compiler_params=_cparams(("arbitrary",)),
        name="window_attn_sample",
    )(sink, q, k, v, cache_k, cache_v, g_attn_out)


def _ssm_project(u_ref, perm_ref, wb_ref, vre, vim, utb, r):
    u = u_ref[...].reshape(r, SSM_WIDTH)
    u_hi = u.astype(BF16)
    u_lo = (u - u_hi.astype(F32)).astype(BF16)
    perm = perm_ref[...]
    u_tb = (jnp.dot(perm, u_hi, preferred_element_type=F32) + jnp.dot(perm, u_lo, preferred_element_type=F32))
    utb[...] = u_tb
    u_bf = u_tb.astype(BF16)
    half_w = SSM_WIDTH // 2
    for h in range(2):
        vv = jnp.dot(u_bf[:, h * half_w:(h + 1) * half_w], wb_ref[h], preferred_element_type=F32)
        vre[:, h * SSM_HALF:(h + 1) * SSM_HALF] = vv[:, :SSM_HALF]
        vim[:, h * SSM_HALF:(h + 1) * SSM_HALF] = vv[:, SSM_HALF:]


def _ssm_scan(are_ref, aim_ref, vre, vim, sre, sim, nb, s):
    lane_chunk = SSM_LANES // 2
    for rg in range(nb // SUBLANES):
        for lc in range(SSM_LANES // lane_chunk):
            lanes = slice(lc * lane_chunk, (lc + 1) * lane_chunk)
            rows0 = slice(rg * SUBLANES, (rg + 1) * SUBLANES)
            ar = jnp.broadcast_to(are_ref[:, lanes], (SUBLANES, lane_chunk))
            ai = jnp.broadcast_to(aim_ref[:, lanes], (SUBLANES, lane_chunk))

            def step(t, carry, lanes=lanes, rg=rg, ar=ar, ai=ai):
                hr, hi = carry
                row = pl.multiple_of(t * nb + rg * SUBLANES, SUBLANES)
                nhr = ar * hr - ai * hi + vre[pl.ds(row, SUBLANES), lanes]
                nhi = ar * hi + ai * hr + vim[pl.ds(row, SUBLANES), lanes]
                vre[pl.ds(row, SUBLANES), lanes] = nhr
                vim[pl.ds(row, SUBLANES), lanes] = nhi
                return nhr, nhi

            hr, hi = lax.fori_loop(0, s, step, (sre[rows0, lanes], sim[rows0, lanes]), unroll=True)
            sre[rows0, lanes] = hr
            sim[rows0, lanes] = hi


def _ssm_output(vre, vim, utb, permt_ref, wc_ref, d_ref, wglu_ref, g_ref, o_ref, nb, s):
    ys = []
    for h in range(2):
        hcat = jnp.concatenate([vre[:, h * SSM_HALF:(h + 1) * SSM_HALF], vim[:, h * SSM_HALF:(h + 1) * SSM_HALF]],
                               axis=1).astype(BF16)
        ys.append(jnp.dot(hcat, wc_ref[h], preferred_element_type=F32))
    y = jnp.concatenate(ys, axis=1) + d_ref[...] * utb[...]
    g = jax.nn.gelu(y)
    z = jnp.dot(g.astype(BF16), wglu_ref[...], preferred_element_type=F32)
    out = g * jax.nn.sigmoid(z)
    on = _rms(out, g_ref[...]).astype(BF16)
    o_bt = jnp.dot(permt_ref[...], on, preferred_element_type=F32).astype(BF16)
    o_ref[...] = o_bt.reshape(nb, s, SSM_WIDTH)


def _ssm_kernel(u_ref, perm_ref, permt_ref, wb_ref, wc_ref, are_ref, aim_ref, d_ref, wglu_ref, g_ref,
                h0re_ref, h0im_ref, o_ref, hre_ref, him_ref, vre, vim, utb, sre, sim, *, nb, s):
    c = pl.program_id(0)

    @pl.when(c == 0)
    def _():
        sre[...] = h0re_ref[...]
        sim[...] = h0im_ref[...]

    _ssm_project(u_ref, perm_ref, wb_ref, vre, vim, utb, nb * s)
    _ssm_scan(are_ref, aim_ref, vre, vim, sre, sim, nb, s)
    _ssm_output(vre, vim, utb, permt_ref, wc_ref, d_ref, wglu_ref, g_ref, o_ref, nb, s)

    @pl.when(c == pl.num_programs(0) - 1)
    def _():
        hre_ref[...] = sre[...]
        him_ref[...] = sim[...]


def _ssm(u, h0_re, h0_im, ssm_w, s):
    nb, l, _ = u.shape
    r = nb * s
    ridx = jnp.arange(r)
    src = (ridx % nb) * s + ridx // nb
    perm = (src[:, None] == ridx[None, :]).astype(BF16)
    wb, wc, a_re, a_im, d_skip, w_glu, g_out = ssm_w
    in_blk = out_blk = pl.BlockSpec((nb, s, SSM_WIDTH), lambda c: (0, c, 0))
    chunk_bufs = [pltpu.VMEM((r, SSM_LANES), F32), pltpu.VMEM((r, SSM_LANES), F32), pltpu.VMEM((r, SSM_WIDTH), F32)]
    st_blk = _full((nb, SSM_LANES))
    return pl.pallas_call(
        functools.partial(_ssm_kernel, nb=nb, s=s),
        grid=(l // s,),
        in_specs=[in_blk, _full((r, r)), _full((r, r)), _full(wb.shape), _full(wc.shape),
                  _full((1, SSM_LANES)), _full((1, SSM_LANES)), _full((1, SSM_WIDTH)),
                  _full((SSM_WIDTH, SSM_WIDTH)), _full((1, SSM_WIDTH)), st_blk, st_blk],
        out_specs=(out_blk, st_blk, st_blk),
        out_shape=(jax.ShapeDtypeStruct((nb, l, SSM_WIDTH), BF16),
                   jax.ShapeDtypeStruct((nb, SSM_LANES), F32), jax.ShapeDtypeStruct((nb, SSM_LANES), F32)),
        scratch_shapes=chunk_bufs + [pltpu.VMEM((nb, SSM_LANES), F32), pltpu.VMEM((nb, SSM_LANES), F32)],
        compiler_params=_cparams(("arbitrary",), VMEM_LIMIT),
        name="ssm_scan",
    )(u, perm, perm.T, wb, wc, a_re, a_im, d_skip, w_glu, g_out, h0_re, h0_im)


def _outproj_kernel(x_ref, att_ref, ssm_ref, wa_ref, ws_ref, gca_ref, wcq_ref, h_ref, qc_ref):
    h = (x_ref[...] + jnp.dot(att_ref[...], wa_ref[...], preferred_element_type=F32)
         + jnp.dot(ssm_ref[...], ws_ref[...], preferred_element_type=F32))
    h_ref[...] = h
    xn = _rms(h, gca_ref[...]).astype(BF16)
    qc_ref[...] = jnp.dot(xn, wcq_ref[...], preferred_element_type=F32).astype(BF16)


def _outproj(x, att, ssm, w_att, w_ssm, g_ca, w_cq, tm):
    t = x.shape[0]
    row = lambda w: pl.BlockSpec((tm, w), lambda i: (i, 0))
    return pl.pallas_call(
        _outproj_kernel,
        grid=(t // tm,),
        in_specs=[row(D_MODEL), row(ATTN_WIDTH), row(SSM_WIDTH), _full((ATTN_WIDTH, D_MODEL)),
                  _full((SSM_WIDTH, D_MODEL)), _full((1, D_MODEL)), _full((D_MODEL, CA_WIDTH))],
        out_specs=(row(D_MODEL), row(CA_WIDTH)),
        out_shape=(jax.ShapeDtypeStruct((t, D_MODEL), F32), jax.ShapeDtypeStruct((t, CA_WIDTH), BF16)),
        compiler_params=_cparams(("arbitrary",)),
        name="out_projection",
    )(x, att, ssm, w_att, w_ssm, g_ca, w_cq)


def _memkv_kernel(m_ref, g_ref, w_ref, k_ref, v_ref):
    m = _rms(m_ref[...], g_ref[...]).astype(BF16)
    kv = jnp.dot(m, w_ref[...], preferred_element_type=F32)
    k_ref[...] = kv[:, :CA_WIDTH]
    v_ref[...] = kv[:, CA_WIDTH:]


def _memkv(mem, g_mem, w_mkv_bf, tm):
    t = mem.shape[0]
    row = lambda w: pl.BlockSpec((tm, w), lambda i: (i, 0))
    return pl.pallas_call(
        _memkv_kernel,
        grid=(t // tm,),
        in_specs=[row(D_MODEL), _full((1, D_MODEL)), _full((D_MODEL, 2 * CA_WIDTH))],
        out_specs=(row(CA_WIDTH), row(CA_WIDTH)),
        out_shape=(jax.ShapeDtypeStruct((t, CA_WIDTH), F32), jax.ShapeDtypeStruct((t, CA_WIDTH), F32)),
        compiler_params=_cparams(("arbitrary",)),
        name="memory_kv",
    )(mem, g_mem, w_mkv_bf)


def _xattn_kernel(qc_ref, mk_ref, mv_ref, h_ref, wco_ref, gmoe_ref, wr_ref, br_ref, tri_ref, cin_ref, *refs,
                  nbat, rows, head_split, has_prev, tile_off):
    refs = refs[1:] if has_prev else refs
    h2_ref, xn_hbm, ti_ref, tw_ref, cout_ref, cnt, xn_buf, xn_sem = refs[:8]
    step = pl.program_id(0)
    nsteps = pl.num_programs(0)
    tm = nbat * rows

    def xn_copies(s):
        dst0 = tile_off + s * (tm // SUBLANES)
        return [pltpu.make_async_copy(xn_buf.at[:, :, sl, :], xn_hbm.at[pl.ds(dst0, tm // SUBLANES), sl], xn_sem)
                for sl in range(SUBLANES)]

    @pl.when(step == 0)
    def _():
        cnt[...] = cin_ref[...]

    if head_split:
        mbuf, msem = refs[8:]
        slot = step % 2

        def head_copies(s, sl):
            b0 = s * nbat
            return [pltpu.make_async_copy(src.at[pl.ds(b0, nbat), :, hd, :], mbuf.at[sl, kv, hd], msem.at[sl])
                    for kv, src in enumerate((mk_ref, mv_ref)) for hd in range(CA_HEADS)]

        @pl.when(step == 0)
        def _():
            for c in head_copies(0, 0):
                c.start()

        @pl.when(step + 1 < nsteps)
        def _():
            for c in head_copies(step + 1, 1 - slot):
                c.start()

        for c in head_copies(step, slot):
            c.wait()

    q_all = qc_ref[...]
    if head_split:
        keys = nbat * N_MEM
        own = (lax.broadcasted_iota(jnp.int32, (keys, tm), 0) // N_MEM
               == lax.broadcasted_iota(jnp.int32, (keys, tm), 1) // rows)
        outs = []
        for hd in range(CA_HEADS):
            sl = slice(hd * CA_HEAD_DIM, (hd + 1) * CA_HEAD_DIM)
            mk = mbuf[slot, 0, hd].reshape(keys, CA_HEAD_DIM).astype(BF16)
            mv = mbuf[slot, 1, hd].reshape(keys, CA_HEAD_DIM).astype(BF16)
            s = lax.dot_general(mk, q_all[:, sl], (((1,), (1,)), ((), ())),
                                preferred_element_type=F32) * (CA_HEAD_DIM ** -0.5)
            s = jnp.where(own, s, -jnp.inf)
            p = jnp.exp(s - jnp.max(s, axis=0, keepdims=True))
            p = p / jnp.sum(p, axis=0, keepdims=True)
            outs.append(lax.dot_general(p.astype(BF16), mv, (((0,), (0,)), ((), ())), preferred_element_type=F32))
        o = jnp.concatenate(outs, axis=-1).astype(BF16)
    else:
        outs_b = []
        for bi in range(nbat):
            q = q_all[bi * rows:(bi + 1) * rows]
            outs = []
            for hd in range(CA_HEADS):
                sl = slice(hd * CA_HEAD_DIM, (hd + 1) * CA_HEAD_DIM)
                mk = mk_ref[bi, :, sl].astype(BF16)
                mv = mv_ref[bi, :, sl].astype(BF16)
                s = lax.dot_general(q[:, sl], mk, (((1,), (1,)), ((), ())),
                                    preferred_element_type=F32) * (CA_HEAD_DIM ** -0.5)
                m = jnp.max(s, axis=-1, keepdims=True)
                p = jnp.exp(s - m)
                denom = jnp.sum(p, axis=-1, keepdims=True)
                outs.append(jnp.dot(p.astype(BF16), mv, preferred_element_type=F32) / denom)
            outs_b.append(jnp.concatenate(outs, axis=-1))
        o = jnp.concatenate(outs_b, axis=0).astype(BF16)
    h2 = h_ref[...] + jnp.dot(o, wco_ref[...], preferred_element_type=F32)
    h2_ref[...] = h2
    xn = _rms(h2, gmoe_ref[...])
    @pl.when(step >= 1)
    def _():
        for c in xn_copies(step - 1):
            c.wait()

    _store_row_tiles(xn_buf, xn)
    for c in xn_copies(step):
        c.start()

    @pl.when(step == nsteps - 1)
    def _():
        for c in xn_copies(step):
            c.wait()

    logits = jnp.dot(xn.astype(BF16), wr_ref[...], preferred_element_type=F32) + br_ref[...]
    n = logits.shape[0]
    eidx = lax.broadcasted_iota(jnp.int32, (n, N_EXPERTS), 1).astype(F32)
    lane = lax.broadcasted_iota(jnp.int32, (n, LANES), 1)
    ti = jnp.zeros((n, LANES), F32)
    tv = jnp.zeros((n, LANES), F32)
    work = logits
    vals = []
    hits = []
    for k in range(TOP_K):
        mx = jnp.max(work, axis=-1, keepdims=True)
        idx = jnp.min(jnp.where(work == mx, eidx, float(N_EXPERTS)), axis=-1, keepdims=True)
        vals.append(mx)
        hits.append(eidx == idx)
        ti = jnp.where(lane == k, idx, ti)
        work = jnp.where(hits[k], -jnp.inf, work)
    ex = [jnp.exp(v - vals[0]) for v in vals]
    tot = ex[0] + ex[1] + ex[2] + ex[3]
    for k in range(TOP_K):
        tv = jnp.where(lane == k, ex[k] / tot, tv)
    sel = jnp.zeros((n, N_EXPERTS), F32)
    for k in range(TOP_K):
        sel = sel + jnp.where(hits[k], 1.0, 0.0)
    before = jnp.dot(tri_ref[...], sel.astype(BF16), preferred_element_type=F32) + cnt[...]
    for k in range(TOP_K):
        rank = jnp.sum(jnp.where(hits[k], before, 0.0), axis=-1, keepdims=True)
        ti = jnp.where(lane == TOP_K + k, rank, ti)
    cnt[...] = cnt[...] + jnp.sum(sel, axis=0, keepdims=True)
    if tm % LANES == 0:
        ti_ref[...] = jnp.transpose(ti)[:2 * TOP_K].astype(jnp.int32)
    else:
        ti_ref[...] = ti.astype(jnp.int32)
    tw_ref[...] = tv

    @pl.when(step == pl.num_programs(0) - 1)
    def _():
        cout_ref[...] = cnt[...]


def _xattn(qc, mk, mv, h1, w_co, g_moe, w_router, b_router, counts_in, nbat, rows, t_total, t_off, xn_prev=None):
    t = qc.shape[0]
    tm = nbat * rows
    row = lambda w: pl.BlockSpec((tm, w), lambda i: (i, 0))
    seq_rows = t // mk.shape[0]
    head_split = mk.ndim == 4
    scratch = [pltpu.VMEM((1, N_EXPERTS), F32), pltpu.VMEM((tm // SUBLANES, LANE_TILES, SUBLANES, LANES), F32),
               pltpu.SemaphoreType.DMA(())]
    any_spec = pl.BlockSpec(memory_space=pl.ANY)
    prev_args, prev_specs, aliases = [], [], {}
    if xn_prev is not None:
        prev_args, prev_specs, aliases = [xn_prev], [any_spec], {10: 1}
    if head_split:
        assert seq_rows == rows
        mem = pl.BlockSpec(memory_space=pl.ANY)
        scratch += [pltpu.VMEM((2, 2, CA_HEADS, nbat, N_MEM, CA_HEAD_DIM), F32), pltpu.SemaphoreType.DMA((2,))]
    else:
        mem = pl.BlockSpec((nbat, N_MEM, CA_WIDTH), lambda i: ((i * tm) // (seq_rows * nbat), 0, 0))
    tri = jnp.tri(tm, k=-1, dtype=BF16)
    if tm % LANES == 0:
        ti_spec = pl.BlockSpec((2 * TOP_K, tm), lambda i: (0, i))
        ti_shape = jax.ShapeDtypeStruct((2 * TOP_K, t), jnp.int32)
    else:
        ti_spec, ti_shape = row(LANES), jax.ShapeDtypeStruct((t, LANES), jnp.int32)
    return pl.pallas_call(
        functools.partial(_xattn_kernel, nbat=nbat, rows=rows, head_split=head_split, has_prev=xn_prev is not None,
                          tile_off=t_off // SUBLANES),
        grid=(t // tm,),
        in_specs=[row(CA_WIDTH), mem, mem, row(D_MODEL), _full((CA_WIDTH, D_MODEL)), _full((1, D_MODEL)),
                  _full((D_MODEL, N_EXPERTS)), _full((1, N_EXPERTS)), _full((tm, tm)), _full((1, N_EXPERTS))]
        + prev_specs,
        out_specs=(row(D_MODEL), any_spec, ti_spec, row(LANES), _full((1, N_EXPERTS))),
        out_shape=(jax.ShapeDtypeStruct((t, D_MODEL), F32),
                   jax.ShapeDtypeStruct((t_total // SUBLANES, SUBLANES, LANE_TILES, LANES), F32),
                   ti_shape, jax.ShapeDtypeStruct((t, LANES), F32),
                   jax.ShapeDtypeStruct((1, N_EXPERTS), F32)),
        scratch_shapes=scratch,
        input_output_aliases=aliases,
        compiler_params=_cparams(("arbitrary",), VMEM_LIMIT),
        name="cross_attn_router",
    )(qc, mk, mv, h1, w_co, g_moe, w_router, b_router, tri, counts_in, *prev_args)


def _sc_gather(table, idx):
    n = idx.shape[0]
    per_worker = n // SC_WORKERS
    pieces = per_worker // SC_ROWS
    assert pieces * SC_ROWS * SC_WORKERS == n
    mesh = plsc.VectorSubcoreMesh(core_axis_name="c", subcore_axis_name="s")

    @functools.partial(pl.kernel, mesh=mesh, out_type=jax.ShapeDtypeStruct((n, LANE_TILES, LANES), F32),
                       scratch_types=[pltpu.VMEM((SC_ROWS,), jnp.int32), pltpu.VMEM((SC_ROWS, LANE_TILES, LANES), F32),
                                      pltpu.SemaphoreType.DMA],
                       name="moe_row_gather")
    def gather(table_hbm, idx_hbm, out_hbm, idx_v, rows_v, sem):
        worker = lax.axis_index("s") * SC_CORES + lax.axis_index("c")
        base = worker * per_worker

        @pl.loop(0, pieces)
        def _(piece):
            off = base + piece * SC_ROWS
            pltpu.sync_copy(idx_hbm.at[pl.ds(off, SC_ROWS)], idx_v)
            pltpu.async_copy(table_hbm.at[idx_v], rows_v, sem).wait()
            pltpu.sync_copy(rows_v, out_hbm.at[pl.ds(off, SC_ROWS)])

    return gather(table, idx)


def _sc_scatter(rows, dest_t, n_out):
    n = rows.shape[0]
    per_worker = n // SC_WORKERS
    step = SC_ROWS // 2
    pieces = per_worker // step
    assert pieces * step * SC_WORKERS == n
    mesh = plsc.VectorSubcoreMesh(core_axis_name="c", subcore_axis_name="s")

    @functools.partial(pl.kernel, mesh=mesh, out_type=jax.ShapeDtypeStruct((n_out, LANE_TILES, LANES), F32),
                       scratch_types=[pltpu.VMEM((TOP_K, step), jnp.int32), pltpu.VMEM((step, LANE_TILES, LANES), F32),
                                      pltpu.SemaphoreType.DMA, pltpu.SemaphoreType.DMA],
                       name="moe_row_scatter")
    def scatter(rows_hbm, idx_hbm, out_hbm, idx_v, rows_v, sem_in, sem_out):
        worker = lax.axis_index("s") * SC_CORES + lax.axis_index("c")
        base = worker * per_worker

        @pl.loop(0, pieces)
        def _(piece):
            off = base + piece * step
            loads = [pltpu.async_copy(rows_hbm.at[pl.ds(off, step)], rows_v, sem_in)]
            loads += [pltpu.async_copy(idx_hbm.at[k, pl.ds(off, step)], idx_v.at[k], sem_in) for k in range(TOP_K)]
            for c in loads:
                c.wait()
            stores = [pltpu.async_copy(rows_v, out_hbm.at[idx_v.at[k]], sem_out) for k in range(TOP_K)]
            for c in stores:
                c.wait()

    return scatter(rows, dest_t)


def _expert_block_kernel(blk_e_ref, w_idx_ref, nused_ref, xs_hbm, wg_ref, bg_ref, wu_ref, bu_ref, wd_ref, bd_ref,
                         ys_hbm, wg_bf, wu_bf, wd_bf, xbuf, ybuf, xsem, ysem):
    i = pl.program_id(0)
    n_used = nused_ref[0]
    active = i < n_used
    slot = i % 2
    tiles = MOE_BLOCK // SUBLANES

    def x_copies(blk, sl):
        return [pltpu.make_async_copy(xs_hbm.at[pl.ds(blk * tiles, tiles), s], xbuf.at[sl, :, :, s, :], xsem.at[sl])
                for s in range(SUBLANES)]

    def y_copies(blk, sl):
        return [pltpu.make_async_copy(ybuf.at[sl, :, :, s, :], ys_hbm.at[pl.ds(blk * tiles, tiles), s], ysem.at[sl])
                for s in range(SUBLANES)]

    @pl.when((i == 0) & active)
    def _():
        for c in x_copies(0, 0):
            c.start()

    prev_e = blk_e_ref[jnp.maximum(i - 1, 0)]
    new_expert = (i == 0) | (blk_e_ref[i] != prev_e)

    @pl.when(active & new_expert)
    def _():
        wg_bf[...] = wg_ref[0].astype(BF16)
        wu_bf[...] = wu_ref[0].astype(BF16)
        wd_bf[...] = wd_ref[0].astype(BF16)

    @pl.when(active)
    def _():
        for c in x_copies(i, slot):
            c.wait()

        @pl.when(i + 1 < n_used)
        def _():
            for c in x_copies(i + 1, 1 - slot):
                c.start()

        x = _load_row_tiles(xbuf.at[slot]).astype(BF16)
        gate = jnp.minimum(jnp.dot(x, wg_bf[...], preferred_element_type=F32) + bg_ref[0], SWIGLU_LIMIT)
        up = jnp.clip(jnp.dot(x, wu_bf[...], preferred_element_type=F32) + bu_ref[0], -SWIGLU_LIMIT, SWIGLU_LIMIT)
        h = gate * jax.nn.sigmoid(SWIGLU_ALPHA * gate) * (up + 1.0)
        y = jnp.dot(h.astype(BF16), wd_bf[...], preferred_element_type=F32) + bd_ref[0]

        @pl.when(i >= 1)
        def _():
            for c in y_copies(i - 1, 1 - slot):
                c.wait()

        _store_row_tiles(ybuf.at[slot], y)
        for c in y_copies(i, slot):
            c.start()

        @pl.when(i == n_used - 1)
        def _():
            for c in y_copies(i, slot):
                c.wait()


def _experts_blocks(blk_e, w_idx, n_used, xs, w_gate, b_gate, w_up, b_up, w_down, b_down):
    n_slots = xs.shape[0]
    n_blocks = n_slots // MOE_BLOCK
    d_ff = w_gate.shape[-1]
    tiles = MOE_BLOCK // SUBLANES
    wspec = lambda a, b: pl.BlockSpec((1, a, b), lambda i, be, wi, nu: (wi[i], 0, 0))
    bspec = lambda b: pl.BlockSpec((1, 1, b), lambda i, be, wi, nu: (be[i], 0, 0))
    any_spec = pl.BlockSpec(memory_space=pl.ANY)
    grid_spec = pltpu.PrefetchScalarGridSpec(
        num_scalar_prefetch=3,
        grid=(n_blocks,),
        in_specs=[any_spec, wspec(D_MODEL, d_ff), bspec(d_ff), wspec(D_MODEL, d_ff), bspec(d_ff),
                  wspec(d_ff, D_MODEL), bspec(D_MODEL)],
        out_specs=any_spec,
        scratch_shapes=[pltpu.VMEM((D_MODEL, d_ff), BF16), pltpu.VMEM((D_MODEL, d_ff), BF16),
                        pltpu.VMEM((d_ff, D_MODEL), BF16),
                        pltpu.VMEM((2, tiles, LANE_TILES, SUBLANES, LANES), F32),
                        pltpu.VMEM((2, tiles, LANE_TILES, SUBLANES, LANES), F32),
                        pltpu.SemaphoreType.DMA((2,)), pltpu.SemaphoreType.DMA((2,))],
    )
    ys = pl.pallas_call(
        _expert_block_kernel,
        grid_spec=grid_spec,
        out_shape=jax.ShapeDtypeStruct((n_slots // SUBLANES, SUBLANES, LANE_TILES, LANES), F32),
        compiler_params=_cparams(("arbitrary",), VMEM_LIMIT),
        name="moe_experts",
    )(blk_e, w_idx, n_used, xs.reshape(n_slots // SUBLANES, SUBLANES, LANE_TILES, LANES), w_gate,
      b_gate.reshape(N_EXPERTS, 1, d_ff), w_up, b_up.reshape(N_EXPERTS, 1, d_ff), w_down,
      b_down.reshape(N_EXPERTS, 1, D_MODEL))
    return ys.reshape(n_slots, LANE_TILES, LANES)


def _combine_kernel(yu_hbm, h_ref, tw_ref, g_ref, *rest, tm):
    o_ref, buf, sem = rest[-3:]
    i = pl.program_id(0)
    n = pl.num_programs(0)
    slot = i % 2
    tiles = tm // SUBLANES

    def copies(step, sl):
        t0 = step * tiles
        return [pltpu.make_async_copy(yu_hbm.at[k, pl.ds(t0, tiles), s], buf.at[sl, k, :, :, s, :], sem.at[sl])
                for s in range(SUBLANES) for k in range(TOP_K)]

    @pl.when(i == 0)
    def _():
        for c in copies(0, 0):
            c.start()

    @pl.when(i + 1 < n)
    def _():
        for c in copies(i + 1, 1 - slot):
            c.start()

    for c in copies(i, slot):
        c.wait()
    tw = tw_ref[...]
    y = h_ref[...]
    for k in range(TOP_K):
        y = y + tw[:, k:k + 1] * _load_row_tiles(buf.at[slot, k])
    o_ref[...] = _rms(y, g_ref[...])


def _combine(yu, h2, tw, g_final, tm, row_off, y_prev=None):
    t = h2.shape[0]
    n_rows = yu.shape[1] * SUBLANES
    blk0 = row_off // tm
    row = lambda w: pl.BlockSpec((tm, w), lambda i: (i + blk0, 0))
    args, in_specs, aliases = [yu, h2, tw, g_final], [pl.BlockSpec(memory_space=pl.ANY), row(D_MODEL), row(LANES),
                                                      _full((1, D_MODEL))], {}
    if y_prev is not None:
        args.append(y_prev)
        in_specs.append(pl.BlockSpec(memory_space=pl.ANY))
        aliases = {4: 0}
    return pl.pallas_call(
        functools.partial(_combine_kernel, tm=tm),
        grid=(n_rows // tm,),
        in_specs=in_specs,
        out_specs=row(D_MODEL),
        out_shape=jax.ShapeDtypeStruct((t, D_MODEL), F32),
        scratch_shapes=[pltpu.VMEM((2, TOP_K, tm // SUBLANES, LANE_TILES, SUBLANES, LANES), F32),
                        pltpu.SemaphoreType.DMA((2,))],
        input_output_aliases=aliases,
        compiler_params=_cparams(("arbitrary",), VMEM_LIMIT),
        name="moe_combine",
    )(*args)


def _route(counts, m):
    counts = counts.reshape(N_EXPERTS).astype(jnp.int32)
    padded = (counts + MOE_BLOCK - 1) // MOE_BLOCK * MOE_BLOCK
    pad_end = jnp.cumsum(padded)
    pad_start = pad_end - padded
    n_blocks = -(-(m + N_EXPERTS * (MOE_BLOCK - 1)) // MOE_BLOCK)
    blk_start = jnp.arange(n_blocks, dtype=jnp.int32) * MOE_BLOCK
    owner = lambda start: jnp.minimum(jnp.sum((pad_end[None, :] <= start[:, None]).astype(jnp.int32), axis=1),
                                      N_EXPERTS - 1).astype(jnp.int32)
    n_used = pad_end[-1] // MOE_BLOCK
    last_start = jnp.maximum(n_used - 1, 0) * MOE_BLOCK
    blk_e = owner(jnp.minimum(blk_start, last_start))
    first = jnp.concatenate([jnp.ones((1,), bool), blk_e[1:] != blk_e[:-1]])
    region_end = jnp.sum(jnp.where(blk_e[:, None] == jnp.arange(N_EXPERTS)[None, :], pad_end[None, :], 0), axis=1)
    w_idx = jnp.where(first, blk_e, owner(jnp.minimum(region_end, last_start))).astype(jnp.int32)
    return pad_start, blk_e, w_idx, n_used.astype(jnp.int32).reshape(1), n_blocks * MOE_BLOCK


def _dest(ti, pad_start):
    if ti.shape[0] != 2 * TOP_K:
        ti = ti[:, :2 * TOP_K].T
    e = ti[:TOP_K]
    start = jnp.sum(jnp.where(e[None] == jnp.arange(N_EXPERTS, dtype=jnp.int32)[:, None, None],
                              pad_start[:, None, None], 0), axis=0)
    return start + ti[TOP_K:]


def _block_diag(blocks):
    g, a, b = blocks.shape
    eye = jnp.eye(g, dtype=blocks.dtype)
    return (blocks[:, :, None, :] * eye[:, None, :, None]).reshape(g * a, g * b)


def kernel(x_prompt, x_sample, mem_prompt, cache_win_k, cache_win_v, state_ssm_re, state_ssm_im, cache_mem_k,
           cache_mem_v, g_mix, w_in, sink, lam_re, lam_im, log_dt, b_re, b_im, c_re, c_im, d_skip, w_glu,
           g_attn_out, g_ssm_out, w_out, g_ca, g_mem, w_mk, w_mv, w_cq, w_co, g_moe, w_router, b_router,
           w_gate, b_gate, w_up, b_up, w_down, b_down, g_final):
    depth = g_mix.shape[0]
    assert depth == 1
    bp, lp, _ = x_prompt.shape
    bs, ls, _ = x_sample.shape
    tp, ts = bp * lp, bs * ls

    w_in_bf = w_in[0].astype(BF16)
    w_att_bf = w_out[0, :ATTN_WIDTH].astype(BF16)
    w_ssm_bf = w_out[0, ATTN_WIDTH:].astype(BF16)
    w_cq_bf = w_cq[0].astype(BF16)
    w_co_bf = w_co[0].astype(BF16)
    w_mkv_bf = jnp.concatenate([w_mk[0], w_mv[0]], axis=1).astype(BF16)
    w_glu_bf = w_glu[0].astype(BF16)
    row = lambda a: a.reshape(1, -1)

    a_re, a_im, bb_re, bb_im = _discretize(lam_re[0], lam_im[0], log_dt[0], b_re[0].transpose(0, 2, 1),
                                           b_im[0].transpose(0, 2, 1))
    hg = SSM_GROUPS // 2
    wb = jnp.stack([jnp.concatenate([_block_diag(bb_re[h * hg:(h + 1) * hg]),
                                     _block_diag(bb_im[h * hg:(h + 1) * hg])], axis=1) for h in range(2)]).astype(BF16)
    cre_t = c_re[0].transpose(0, 2, 1)
    cim_t = c_im[0].transpose(0, 2, 1)
    wc = jnp.stack([jnp.concatenate([_block_diag(cre_t[h * hg:(h + 1) * hg]),
                                     -_block_diag(cim_t[h * hg:(h + 1) * hg])], axis=0) for h in range(2)]).astype(BF16)
    ssm_w = (wb, wc, a_re.reshape(1, SSM_LANES), a_im.reshape(1, SSM_LANES), row(d_skip[0]), w_glu_bf,
             row(g_ssm_out[0]))

    xp = x_prompt.reshape(tp, D_MODEL)
    xs_ = x_sample.reshape(ts, D_MODEL)

    qp, kp, vp, up = _inproj(xp, row(g_mix[0]), w_in_bf, 512)
    qs, ks, vs, us = _inproj(xs_, row(g_mix[0]), w_in_bf, 256)

    att_p = _attn_prompt(sink[0], qp, kp, vp, row(g_attn_out[0]), bp, lp)
    ck = cache_win_k[0].reshape(bs, WINDOW, KV_WIDTH)
    cv = cache_win_v[0].reshape(bs, WINDOW, KV_WIDTH)
    att_s = _attn_sample(sink[0], qs, ks, vs, ck, cv, row(g_attn_out[0]), bs, ls, 16)

    zero_state = jnp.zeros((bp, SSM_LANES), F32)
    ssm_p, p_sre, p_sim = _ssm(up.reshape(bp, lp, SSM_WIDTH), zero_state, zero_state, ssm_w, 32)
    ssm_s, s_sre, s_sim = _ssm(us.reshape(bs, ls, SSM_WIDTH), state_ssm_re[0].reshape(bs, SSM_LANES),
                               state_ssm_im[0].reshape(bs, SSM_LANES), ssm_w, ls)

    h1p, qcp = _outproj(xp, att_p, ssm_p.reshape(tp, SSM_WIDTH), w_att_bf, w_ssm_bf, row(g_ca[0]), w_cq_bf, 512)
    h1s, qcs = _outproj(xs_, att_s, ssm_s.reshape(ts, SSM_WIDTH), w_att_bf, w_ssm_bf, row(g_ca[0]), w_cq_bf, 256)

    mk_p, mv_p = _memkv(mem_prompt.reshape(bp * N_MEM, D_MODEL), row(g_mem[0]), w_mkv_bf, 512)

    no_counts = jnp.zeros((1, N_EXPERTS), F32)
    w_router_bf = w_router[0].astype(BF16)
    tt = tp + ts
    h2p, xn, tip, twp, cnt_p = _xattn(qcp, mk_p.reshape(bp, N_MEM, CA_WIDTH), mv_p.reshape(bp, N_MEM, CA_WIDTH),
                                      h1p, w_co_bf, row(g_moe[0]), w_router_bf, row(b_router[0]), no_counts,
                                      1, 512, tt, 0)
    h2s, xn, tis, tws, cnt = _xattn(qcs, cache_mem_k[0], cache_mem_v[0], h1s, w_co_bf, row(g_moe[0]),
                                    w_router_bf, row(b_router[0]), cnt_p, 8, ls, tt, tp, xn_prev=xn)

    m = tt * TOP_K
    pad_start, blk_e, w_idx, n_used, n_slots = _route(cnt, m)
    dest = jnp.concatenate([_dest(tip, pad_start), _dest(tis, pad_start)], axis=1)
    xs = _sc_scatter(xn.reshape(tt, LANE_TILES, LANES), dest, n_slots)
    ys = _experts_blocks(blk_e, w_idx, n_used, xs, w_gate[0], b_gate[0], w_up[0], b_up[0], w_down[0], b_down[0])
    def gathered(t0, t1):
        rows = _sc_gather(ys, dest[:, t0:t1].reshape(-1))
        return rows.reshape(TOP_K, (t1 - t0) // SUBLANES, SUBLANES, LANE_TILES, LANES)

    y_prompt = None
    chunk = tp // COMBINE_CHUNKS
    for c in range(COMBINE_CHUNKS):
        y_prompt = _combine(gathered(c * chunk, (c + 1) * chunk), h2p, twp, row(g_final), 256, c * chunk,
                            y_prev=y_prompt)
    y_prompt = y_prompt.reshape(bp, lp, D_MODEL)
    y_sample = _combine(gathered(tp, tt), h2s, tws, row(g_final), 256, 0).reshape(bs, ls, D_MODEL)

    kp4 = kp.reshape(bp, lp, N_KV_HEADS, HEAD_DIM)
    vp4 = vp.reshape(bp, lp, N_KV_HEADS, HEAD_DIM)
    p_win_k = kp4[:, -WINDOW:][None]
    p_win_v = vp4[:, -WINDOW:][None]
    s_win_k = jnp.concatenate([cache_win_k[0], ks.reshape(bs, ls, N_KV_HEADS, HEAD_DIM)], axis=1)[:, -WINDOW:][None]
    s_win_v = jnp.concatenate([cache_win_v[0], vs.reshape(bs, ls, N_KV_HEADS, HEAD_DIM)], axis=1)[:, -WINDOW:][None]
    st = lambda a, b: a.reshape(1, b, SSM_GROUPS, SSM_STATE)
    p_mem_k = mk_p.reshape(1, bp, N_MEM, CA_HEADS, CA_HEAD_DIM)
    p_mem_v = mv_p.reshape(1, bp, N_MEM, CA_HEADS, CA_HEAD_DIM)
    return (y_prompt, y_sample, p_win_k, p_win_v, st(p_sre, bp), st(p_sim, bp), p_mem_k, p_mem_v,
            s_win_k, s_win_v, st(s_sre, bs), st(s_sim, bs))
```

```python
import functools
import math

import jax
import jax.numpy as jnp
from jax import lax
from jax.experimental import pallas as pl
from jax.experimental.pallas import tpu as pltpu
from jax.experimental.pallas import tpu_sc as plsc

F32 = jnp.float32
BF16 = jnp.bfloat16

D_MODEL = 1024
HEAD_DIM = 64
N_Q_HEADS = 8
N_KV_HEADS = 2
Q_PER_KV = 4
ATTN_WIDTH = 512
KV_WIDTH = 128
WINDOW = 128
SSM_WIDTH = 512
SSM_GROUP = 16
SSM_GROUPS = 32
SSM_STATE = 64
SSM_LANES = SSM_GROUPS * SSM_STATE
SSM_HALF = SSM_LANES // 2
IN_WIDTH = ATTN_WIDTH + 2 * KV_WIDTH + SSM_WIDTH
N_MEM = 256
CA_HEADS = 4
CA_HEAD_DIM = 128
CA_WIDTH = 512
N_EXPERTS = 32
TOP_K = 4
SWIGLU_ALPHA = 1.702
SWIGLU_LIMIT = 7.0
RMS_EPS = 1e-5

assert math.frexp(HEAD_DIM ** -0.5)[0] == 0.5

SUBLANES = 8
LANES = 128
MOE_BLOCK = 256
ATTN_SEQS = 4
SC_CORES = 2
SC_WORKERS = 32
SC_ROWS = 64
COMBINE_CHUNKS = 4
VMEM_LIMIT = 56 * 1024 * 1024


def _cparams(sem, vmem=None):
    return pltpu.CompilerParams(dimension_semantics=sem, vmem_limit_bytes=vmem)


def _rms(x, g):
    return x * lax.rsqrt(jnp.mean(x * x, axis=-1, keepdims=True) + RMS_EPS) * g


def _full(shape):
    return pl.BlockSpec(shape, lambda *_: (0,) * len(shape))


LANE_TILES = D_MODEL // LANES


def _store_row_tiles(ref, x):
    for c in range(LANE_TILES):
        ref[:, c] = x[:, c * LANES:(c + 1) * LANES].reshape(x.shape[0] // SUBLANES, SUBLANES, LANES)


def _load_row_tiles(ref):
    rows = ref.shape[0] * SUBLANES
    return jnp.concatenate([ref[:, c].reshape(rows, LANES) for c in range(LANE_TILES)], axis=1)


def _disc_kernel(lr_ref, li_ref, ldt_ref, bre_ref, bim_ref, are_ref, aim_ref, bbre_ref, bbim_ref):
    lr = lr_ref[...]
    li = li_ref[...]
    dt = jnp.exp(ldt_ref[...])
    mag = jnp.exp(lr * dt)
    ang = li * dt
    ab_re = mag * jnp.cos(ang)
    ab_im = mag * jnp.sin(ang)
    nr = ab_re - 1.0
    ni = ab_im
    den = lr * lr + li * li
    f_re = (nr * lr + ni * li) / den
    f_im = (ni * lr - nr * li) / den
    are_ref[...] = ab_re
    aim_ref[...] = ab_im
    br = bre_ref[...]
    bi = bim_ref[...]
    bbre_ref[...] = f_re[:, None, :] * br - f_im[:, None, :] * bi
    bbim_ref[...] = f_re[:, None, :] * bi + f_im[:, None, :] * br


def _discretize(lam_re, lam_im, log_dt, b_re_t, b_im_t):
    g, n = lam_re.shape
    c = b_re_t.shape[1]
    return pl.pallas_call(
        _disc_kernel,
        out_shape=(jax.ShapeDtypeStruct((g, n), F32), jax.ShapeDtypeStruct((g, n), F32),
                   jax.ShapeDtypeStruct((g, c, n), F32), jax.ShapeDtypeStruct((g, c, n), F32)),
        name="ssm_discretize",
    )(lam_re, lam_im, log_dt.reshape(g, 1), b_re_t, b_im_t)


def _inproj_kernel(x_ref, g_ref, w_ref, q_ref, k_ref, v_ref, u_ref):
    a = _rms(x_ref[...], g_ref[...]).astype(BF16)
    proj = jnp.dot(a, w_ref[...], preferred_element_type=F32)
    q_ref[...] = (proj[:, :ATTN_WIDTH] * (HEAD_DIM ** -0.5)).astype(BF16)
    k_ref[...] = proj[:, ATTN_WIDTH:ATTN_WIDTH + KV_WIDTH]
    v_ref[...] = proj[:, ATTN_WIDTH + KV_WIDTH:ATTN_WIDTH + 2 * KV_WIDTH]
    u_ref[...] = proj[:, ATTN_WIDTH + 2 * KV_WIDTH:]


def _inproj(x, g_mix, w_in_bf, tm):
    t = x.shape[0]
    row = lambda w: pl.BlockSpec((tm, w), lambda i: (i, 0))
    return pl.pallas_call(
        _inproj_kernel,
        grid=(t // tm,),
        in_specs=[row(D_MODEL), _full((1, D_MODEL)), _full((D_MODEL, IN_WIDTH))],
        out_specs=(row(ATTN_WIDTH), row(KV_WIDTH), row(KV_WIDTH), row(SSM_WIDTH)),
        out_shape=(jax.ShapeDtypeStruct((t, ATTN_WIDTH), BF16), jax.ShapeDtypeStruct((t, KV_WIDTH), F32),
                   jax.ShapeDtypeStruct((t, KV_WIDTH), F32), jax.ShapeDtypeStruct((t, SSM_WIDTH), F32)),
        compiler_params=_cparams(("arbitrary",)),
        name="in_projection",
    )(x, g_mix, w_in_bf)


def _sink_softmax_pv(s, sink, v_bf):
    m = jnp.maximum(jnp.max(s, axis=-1, keepdims=True), sink)
    p = jnp.exp(s - m)
    denom = jnp.sum(p, axis=-1, keepdims=True) + jnp.exp(sink - m)
    return jnp.dot(p.astype(BF16), v_bf, preferred_element_type=F32) / denom


def _attn_prompt_kernel(sink_ref, q_ref, kc_ref, kp_ref, vc_ref, vp_ref, g_ref, o_ref):
    i = pl.program_id(1)
    q = q_ref[...]
    kk = jnp.concatenate([kp_ref[...], kc_ref[...]], axis=0).astype(BF16)
    vv = jnp.concatenate([vp_ref[...], vc_ref[...]], axis=0).astype(BF16)
    rows = Q_PER_KV * WINDOW
    qi = lax.broadcasted_iota(jnp.int32, (rows, 2 * WINDOW), 0) & (WINDOW - 1)
    kj = lax.broadcasted_iota(jnp.int32, (rows, 2 * WINDOW), 1)
    rel = qi + WINDOW - kj
    mask = (rel >= 0) & (rel < WINDOW) & ((kj >= WINDOW) | (i > 0))
    head_of_row = lax.broadcasted_iota(jnp.int32, (rows, 1), 0) // WINDOW
    low_q = lax.broadcasted_iota(jnp.int32, (WINDOW, LANES), 1) < HEAD_DIM
    zero_q = jnp.zeros((WINDOW, LANES), BF16)
    pairs = []
    for j in range(N_KV_HEADS):
        kh = kk[:, j * HEAD_DIM:(j + 1) * HEAD_DIM]
        vh = vv[:, j * HEAD_DIM:(j + 1) * HEAD_DIM]
        k2 = jnp.concatenate([kh, kh], axis=1)
        v2 = jnp.concatenate([vh, vh], axis=1)
        stacked = []
        for p in range(Q_PER_KV // 2):
            qp = q[:, (j * Q_PER_KV + 2 * p) * HEAD_DIM:(j * Q_PER_KV + 2 * p + 2) * HEAD_DIM]
            stacked += [jnp.where(low_q, qp, zero_q), jnp.where(low_q, zero_q, qp)]
        qs = jnp.concatenate(stacked, axis=0)
        sink = jnp.zeros((rows, 1), F32)
        for g in range(Q_PER_KV):
            sink = jnp.where(head_of_row == g, sink_ref[j * Q_PER_KV + g], sink)
        s = lax.dot_general(qs, k2, (((1,), (1,)), ((), ())), preferred_element_type=F32)
        s = jnp.where(mask, s, -jnp.inf)
        o = _sink_softmax_pv(s, sink, v2)
        for p in range(Q_PER_KV // 2):
            pairs.append(jnp.where(low_q, o[2 * p * WINDOW:(2 * p + 1) * WINDOW],
                                   o[(2 * p + 1) * WINDOW:(2 * p + 2) * WINDOW]))
    att = jnp.concatenate(pairs, axis=-1)
    o_ref[...] = _rms(att, g_ref[...]).astype(BF16)


def _attn_prompt(sink, q, k, v, g_attn_out, b, l):
    nb = l // WINDOW
    cur = lambda w: pl.BlockSpec((WINDOW, w), lambda bi, i: (bi * nb + i, 0))
    prev = lambda w: pl.BlockSpec((WINDOW, w), lambda bi, i: (bi * nb + jnp.maximum(i - 1, 0), 0))
    return pl.pallas_call(
        _attn_prompt_kernel,
        grid=(b, nb),
        in_specs=[pl.BlockSpec(memory_space=pltpu.SMEM), cur(ATTN_WIDTH), cur(KV_WIDTH), prev(KV_WIDTH),
                  cur(KV_WIDTH), prev(KV_WIDTH), _full((1, ATTN_WIDTH))],
        out_specs=cur(ATTN_WIDTH),
        out_shape=jax.ShapeDtypeStruct((b * l, ATTN_WIDTH), BF16),
        compiler_params=_cparams(("arbitrary", "arbitrary")),
        name="window_attn_prompt",
    )(sink, q, k, k, v, v, g_attn_out)


def _attn_sample_kernel(sink_ref, q_ref, kn_ref, vn_ref, ck_ref, cv_ref, g_ref, o_ref, *, bt, t):
    nk = WINDOW + t
    sb = ATTN_SEQS
    keys = sb * nk
    cols = Q_PER_KV * sb * t
    kb = lax.broadcasted_iota(jnp.int32, (sb, nk, cols), 0).reshape(keys, cols)
    kk = lax.broadcasted_iota(jnp.int32, (sb, nk, cols), 1).reshape(keys, cols)
    col = lax.broadcasted_iota(jnp.int32, (keys, cols), 1) % (sb * t)
    cb, ct = col // t, col % t
    mask = (kb == cb) & (((kk < WINDOW) & (kk > ct)) | ((kk >= WINDOW) & ((kk - WINDOW) <= ct)))
    head_of_col = lax.broadcasted_iota(jnp.int32, (1, cols), 1) // (sb * t)
    low = lax.broadcasted_iota(jnp.int32, (sb * t, LANES), 1) < HEAD_DIM
    zero_q = jnp.zeros((sb * t, LANES), BF16)
    swap = lambda x: pltpu.roll(x, HEAD_DIM, 1)
    q_all = q_ref[...]
    kn_all = kn_ref[...]
    vn_all = vn_ref[...]
    outs_b = []
    for b0 in range(0, bt, sb):
        r0 = b0 * t
        newk = kn_all[r0:r0 + sb * t].reshape(sb, t, KV_WIDTH)
        newv = vn_all[r0:r0 + sb * t].reshape(sb, t, KV_WIDTH)
        k_all = jnp.concatenate([ck_ref[b0:b0 + sb], newk], axis=1).reshape(keys, KV_WIDTH).astype(BF16)
        v_all = jnp.concatenate([cv_ref[b0:b0 + sb], newv], axis=1).reshape(keys, KV_WIDTH).astype(BF16)
        head_out = []
        for j in range(N_KV_HEADS):
            mine = low if j == 0 else jnp.logical_not(low)
            qs = []
            for g in range(Q_PER_KV):
                h = j * Q_PER_KV + g
                tile = q_all[r0:r0 + sb * t, (h // 2) * LANES:(h // 2 + 1) * LANES]
                if h % 2 != j:
                    tile = swap(tile.astype(F32)).astype(BF16)
                qs.append(jnp.where(mine, tile, zero_q))
            qj = jnp.concatenate(qs, axis=0)
            sink = jnp.zeros((1, cols), F32)
            for g in range(Q_PER_KV):
                sink = jnp.where(head_of_col == g, sink_ref[j * Q_PER_KV + g], sink)
            s = lax.dot_general(k_all, qj, (((1,), (1,)), ((), ())), preferred_element_type=F32)
            s = jnp.where(mask, s, -jnp.inf)
            m = jnp.maximum(jnp.max(s, axis=0, keepdims=True), sink)
            p = jnp.exp(s - m)
            p = p / (jnp.sum(p, axis=0, keepdims=True) + jnp.exp(sink - m))
            o = lax.dot_general(p.astype(BF16), v_all, (((0,), (0,)), ((), ())), preferred_element_type=F32)
            for g in range(Q_PER_KV):
                og = o[g * sb * t:(g + 1) * sb * t]
                head_out.append(og if g % 2 == j else swap(og))
        tiles = [jnp.where(low, head_out[2 * pr], head_out[2 * pr + 1]) for pr in range(N_Q_HEADS // 2)]
        outs_b.append(jnp.concatenate(tiles, axis=-1))
    att = jnp.concatenate(outs_b, axis=0)
    o_ref[...] = _rms(att, g_ref[...]).astype(BF16)


def _attn_sample(sink, q, k, v, cache_k, cache_v, g_attn_out, b, t, bt):
    row = lambda w: pl.BlockSpec((bt * t, w), lambda i: (i, 0))
    cache = pl.BlockSpec((bt, WINDOW, KV_WIDTH), lambda i: (i, 0, 0))
    return pl.pallas_call(
        functools.partial(_attn_sample_kernel, bt=bt, t=t),
        grid=(b // bt,),
        in_specs=[pl.BlockSpec(memory_space=pltpu.SMEM), row(ATTN_WIDTH), row(KV_WIDTH), row(KV_WIDTH),
                  cache, cache, _full((1, ATTN_WIDTH))],
        out_specs=row(ATTN_WIDTH),
        out_shape=jax.ShapeDtypeStruct((b * t, ATTN_WIDTH), BF16),
        compiler_params=_cparams(("arbitrary",)),
        name="window_attn_sample",
    )(sink, q, k, v, cache_k, cache_v, g_attn_out)


def _ssm_project(u_ref, perm_ref, wb_ref, vre, vim, utb, r):
    u = u_ref[...].reshape(r, SSM_WIDTH)
    u_hi = u.astype(BF16)
    u_lo = (u - u_hi.astype(F32)).astype(BF16)
    perm = perm_ref[...]
    u_tb = (jnp.dot(perm, u_hi, preferred_element_type=F32) + jnp.dot(perm, u_lo, preferred_element_type=F32))
    utb[...] = u_tb
    u_bf = u_tb.astype(BF16)
    half_w = SSM_WIDTH // 2
    for h in range(2):
        vv = jnp.dot(u_bf[:, h * half_w:(h + 1) * half_w], wb_ref[h], preferred_element_type=F32)
        vre[:, h * SSM_HALF:(h + 1) * SSM_HALF] = vv[:, :SSM_HALF]
        vim[:, h * SSM_HALF:(h + 1) * SSM_HALF] = vv[:, SSM_HALF:]


def _ssm_scan(are_ref, aim_ref, vre, vim, sre, sim, nb, s):
    lane_chunk = SSM_LANES // 2
    for rg in range(nb // SUBLANES):
        for lc in range(SSM_LANES // lane_chunk):
            lanes = slice(lc * lane_chunk, (lc + 1) * lane_chunk)
            rows0 = slice(rg * SUBLANES, (rg + 1) * SUBLANES)
            ar = jnp.broadcast_to(are_ref[:, lanes], (SUBLANES, lane_chunk))
            ai = jnp.broadcast_to(aim_ref[:, lanes], (SUBLANES, lane_chunk))

            def step(t, carry, lanes=lanes, rg=rg, ar=ar, ai=ai):
                hr, hi = carry
                row = pl.multiple_of(t * nb + rg * SUBLANES, SUBLANES)
                nhr = ar * hr - ai * hi + vre[pl.ds(row, SUBLANES), lanes]
                nhi = ar * hi + ai * hr + vim[pl.ds(row, SUBLANES), lanes]
                vre[pl.ds(row, SUBLANES), lanes] = nhr
                vim[pl.ds(row, SUBLANES), lanes] = nhi
                return nhr, nhi

            hr, hi = lax.fori_loop(0, s, step, (sre[rows0, lanes], sim[rows0, lanes]), unroll=True)
            sre[rows0, lanes] = hr
            sim[rows0, lanes] = hi


def _ssm_output(vre, vim, utb, permt_ref, wc_ref, d_ref, wglu_ref, g_ref, o_ref, nb, s):
    ys = []
    for h in range(2):
        hcat = jnp.concatenate([vre[:, h * SSM_HALF:(h + 1) * SSM_HALF], vim[:, h * SSM_HALF:(h + 1) * SSM_HALF]],
                               axis=1).astype(BF16)
        ys.append(jnp.dot(hcat, wc_ref[h], preferred_element_type=F32))
    y = jnp.concatenate(ys, axis=1) + d_ref[...] * utb[...]
    g = jax.nn.gelu(y)
    z = jnp.dot(g.astype(BF16), wglu_ref[...], preferred_element_type=F32)
    out = g * jax.nn.sigmoid(z)
    on = _rms(out, g_ref[...]).astype(BF16)
    o_bt = jnp.dot(permt_ref[...], on, preferred_element_type=F32).astype(BF16)
    o_ref[...] = o_bt.reshape(nb, s, SSM_WIDTH)


def _ssm_kernel(u_ref, perm_ref, permt_ref, wb_ref, wc_ref, are_ref, aim_ref, d_ref, wglu_ref, g_ref,
                h0re_ref, h0im_ref, o_ref, hre_ref, him_ref, vre, vim, utb, sre, sim, *, nb, s):
    c = pl.program_id(0)

    @pl.when(c == 0)
    def _():
        sre[...] = h0re_ref[...]
        sim[...] = h0im_ref[...]

    _ssm_project(u_ref, perm_ref, wb_ref, vre, vim, utb, nb * s)
    _ssm_scan(are_ref, aim_ref, vre, vim, sre, sim, nb, s)
    _ssm_output(vre, vim, utb, permt_ref, wc_ref, d_ref, wglu_ref, g_ref, o_ref, nb, s)

    @pl.when(c == pl.num_programs(0) - 1)
    def _():
        hre_ref[...] = sre[...]
        him_ref[...] = sim[...]


def _ssm(u, h0_re, h0_im, ssm_w, s):
    nb, l, _ = u.shape
    r = nb * s
    ridx = jnp.arange(r)
    src = (ridx % nb) * s + ridx // nb
    perm = (src[:, None] == ridx[None, :]).astype(BF16)
    wb, wc, a_re, a_im, d_skip, w_glu, g_out = ssm_w
    in_blk = out_blk = pl.BlockSpec((nb, s, SSM_WIDTH), lambda c: (0, c, 0))
    chunk_bufs = [pltpu.VMEM((r, SSM_LANES), F32), pltpu.VMEM((r, SSM_LANES), F32), pltpu.VMEM((r, SSM_WIDTH), F32)]
    st_blk = _full((nb, SSM_LANES))
    return pl.pallas_call(
        functools.partial(_ssm_kernel, nb=nb, s=s),
        grid=(l // s,),
        in_specs=[in_blk, _full((r, r)), _full((r, r)), _full(wb.shape), _full(wc.shape),
                  _full((1, SSM_LANES)), _full((1, SSM_LANES)), _full((1, SSM_WIDTH)),
                  _full((SSM_WIDTH, SSM_WIDTH)), _full((1, SSM_WIDTH)), st_blk, st_blk],
        out_specs=(out_blk, st_blk, st_blk),
        out_shape=(jax.ShapeDtypeStruct((nb, l, SSM_WIDTH), BF16),
                   jax.ShapeDtypeStruct((nb, SSM_LANES), F32), jax.ShapeDtypeStruct((nb, SSM_LANES), F32)),
        scratch_shapes=chunk_bufs + [pltpu.VMEM((nb, SSM_LANES), F32), pltpu.VMEM((nb, SSM_LANES), F32)],
        compiler_params=_cparams(("arbitrary",), VMEM_LIMIT),
        name="ssm_scan",
    )(u, perm, perm.T, wb, wc, a_re, a_im, d_skip, w_glu, g_out, h0_re, h0_im)


def _outproj_kernel(x_ref, att_ref, ssm_ref, wa_ref, ws_ref, gca_ref, wcq_ref, h_ref, qc_ref):
    h = (x_ref[...] + jnp.dot(att_ref[...], wa_ref[...], preferred_element_type=F32)
         + jnp.dot(ssm_ref[...], ws_ref[...], preferred_element_type=F32))
    h_ref[...] = h
    xn = _rms(h, gca_ref[...]).astype(BF16)
    qc_ref[...] = jnp.dot(xn, wcq_ref[...], preferred_element_type=F32).astype(BF16)


def _outproj(x, att, ssm, w_att, w_ssm, g_ca, w_cq, tm):
    t = x.shape[0]
    row = lambda w: pl.BlockSpec((tm, w), lambda i: (i, 0))
    return pl.pallas_call(
        _outproj_kernel,
        grid=(t // tm,),
        in_specs=[row(D_MODEL), row(ATTN_WIDTH), row(SSM_WIDTH), _full((ATTN_WIDTH, D_MODEL)),
                  _full((SSM_WIDTH, D_MODEL)), _full((1, D_MODEL)), _full((D_MODEL, CA_WIDTH))],
        out_specs=(row(D_MODEL), row(CA_WIDTH)),
        out_shape=(jax.ShapeDtypeStruct((t, D_MODEL), F32), jax.ShapeDtypeStruct((t, CA_WIDTH), BF16)),
        compiler_params=_cparams(("arbitrary",)),
        name="out_projection",
    )(x, att, ssm, w_att, w_ssm, g_ca, w_cq)


def _memkv_kernel(m_ref, g_ref, w_ref, k_ref, v_ref):
    m = _rms(m_ref[...], g_ref[...]).astype(BF16)
    kv = jnp.dot(m, w_ref[...], preferred_element_type=F32)
    k_ref[...] = kv[:, :CA_WIDTH]
    v_ref[...] = kv[:, CA_WIDTH:]


def _memkv(mem, g_mem, w_mkv_bf, tm):
    t = mem.shape[0]
    row = lambda w: pl.BlockSpec((tm, w), lambda i: (i, 0))
    return pl.pallas_call(
        _memkv_kernel,
        grid=(t // tm,),
        in_specs=[row(D_MODEL), _full((1, D_MODEL)), _full((D_MODEL, 2 * CA_WIDTH))],
        out_specs=(row(CA_WIDTH), row(CA_WIDTH)),
        out_shape=(jax.ShapeDtypeStruct((t, CA_WIDTH), F32), jax.ShapeDtypeStruct((t, CA_WIDTH), F32)),
        compiler_params=_cparams(("arbitrary",)),
        name="memory_kv",
    )(mem, g_mem, w_mkv_bf)


def _xattn_kernel(qc_ref, mk_ref, mv_ref, h_ref, wco_ref, gmoe_ref, wr_ref, br_ref, tri_ref, cin_ref, *refs,
                  nbat, rows, head_split, has_prev, tile_off):
    refs = refs[1:] if has_prev else refs
    h2_ref, xn_hbm, ti_ref, tw_ref, cout_ref, cnt, xn_buf, xn_sem = refs[:8]
    step = pl.program_id(0)
    nsteps = pl.num_programs(0)
    tm = nbat * rows

    def xn_copies(s):
        dst0 = tile_off + s * (tm // SUBLANES)
        return [pltpu.make_async_copy(xn_buf.at[:, :, sl, :], xn_hbm.at[pl.ds(dst0, tm // SUBLANES), sl], xn_sem)
                for sl in range(SUBLANES)]

    @pl.when(step == 0)
    def _():
        cnt[...] = cin_ref[...]

    if head_split:
        mbuf, msem = refs[8:]
        slot = step % 2

        def head_copies(s, sl):
            b0 = s * nbat
            return [pltpu.make_async_copy(src.at[pl.ds(b0, nbat), :, hd, :], mbuf.at[sl, kv, hd], msem.at[sl])
                    for kv, src in enumerate((mk_ref, mv_ref)) for hd in range(CA_HEADS)]

        @pl.when(step == 0)
        def _():
            for c in head_copies(0, 0):
                c.start()

        @pl.when(step + 1 < nsteps)
        def _():
            for c in head_copies(step + 1, 1 - slot):
                c.start()

        for c in head_copies(step, slot):
            c.wait()

    q_all = qc_ref[...]
    if head_split:
        keys = nbat * N_MEM
        own = (lax.broadcasted_iota(jnp.int32, (keys, tm), 0) // N_MEM
               == lax.broadcasted_iota(jnp.int32, (keys, tm), 1) // rows)
        outs = []
        for hd in range(CA_HEADS):
            sl = slice(hd * CA_HEAD_DIM, (hd + 1) * CA_HEAD_DIM)
            mk = mbuf[slot, 0, hd].reshape(keys, CA_HEAD_DIM).astype(BF16)
            mv = mbuf[slot, 1, hd].reshape(keys, CA_HEAD_DIM).astype(BF16)
            s = lax.dot_general(mk, q_all[:, sl], (((1,), (1,)), ((), ())),
                                preferred_element_type=F32) * (CA_HEAD_DIM ** -0.5)
            s = jnp.where(own, s, -jnp.inf)
            p = jnp.exp(s - jnp.max(s, axis=0, keepdims=True))
            p = p / jnp.sum(p, axis=0, keepdims=True)
            outs.append(lax.dot_general(p.astype(BF16), mv, (((0,), (0,)), ((), ())), preferred_element_type=F32))
        o = jnp.concatenate(outs, axis=-1).astype(BF16)
    else:
        outs_b = []
        for bi in range(nbat):
            q = q_all[bi * rows:(bi + 1) * rows]
            outs = []
            for hd in range(CA_HEADS):
                sl = slice(hd * CA_HEAD_DIM, (hd + 1) * CA_HEAD_DIM)
                mk = mk_ref[bi, :, sl].astype(BF16)
                mv = mv_ref[bi, :, sl].astype(BF16)
                s = lax.dot_general(q[:, sl], mk, (((1,), (1,)), ((), ())),
                                    preferred_element_type=F32) * (CA_HEAD_DIM ** -0.5)
                m = jnp.max(s, axis=-1, keepdims=True)
                p = jnp.exp(s - m)
                denom = jnp.sum(p, axis=-1, keepdims=True)
                outs.append(jnp.dot(p.astype(BF16), mv, preferred_element_type=F32) / denom)
            outs_b.append(jnp.concatenate(outs, axis=-1))
        o = jnp.concatenate(outs_b, axis=0).astype(BF16)
    h2 = h_ref[...] + jnp.dot(o, wco_ref[...], preferred_element_type=F32)
    h2_ref[...] = h2
    xn = _rms(h2, gmoe_ref[...])
    @pl.when(step >= 1)
    def _():
        for c in xn_copies(step - 1):
            c.wait()

    _store_row_tiles(xn_buf, xn)
    for c in xn_copies(step):
        c.start()

    @pl.when(step == nsteps - 1)
    def _():
        for c in xn_copies(step):
            c.wait()

    logits = jnp.dot(xn.astype(BF16), wr_ref[...], preferred_element_type=F32) + br_ref[...]
    n = logits.shape[0]
    eidx = lax.broadcasted_iota(jnp.int32, (n, N_EXPERTS), 1).astype(F32)
    lane = lax.broadcasted_iota(jnp.int32, (n, LANES), 1)
    ti = jnp.zeros((n, LANES), F32)
    tv = jnp.zeros((n, LANES), F32)
    work = logits
    vals = []
    hits = []
    for k in range(TOP_K):
        mx = jnp.max(work, axis=-1, keepdims=True)
        idx = jnp.min(jnp.where(work == mx, eidx, float(N_EXPERTS)), axis=-1, keepdims=True)
        vals.append(mx)
        hits.append(eidx == idx)
        ti = jnp.where(lane == k, idx, ti)
        work = jnp.where(hits[k], -jnp.inf, work)
    ex = [jnp.exp(v - vals[0]) for v in vals]
    tot = ex[0] + ex[1] + ex[2] + ex[3]
    for k in range(TOP_K):
        tv = jnp.where(lane == k, ex[k] / tot, tv)
    sel = jnp.zeros((n, N_EXPERTS), F32)
    for k in range(TOP_K):
        sel = sel + jnp.where(hits[k], 1.0, 0.0)
    before = jnp.dot(tri_ref[...], sel.astype(BF16), preferred_element_type=F32) + cnt[...]
    for k in range(TOP_K):
        rank = jnp.sum(jnp.where(hits[k], before, 0.0), axis=-1, keepdims=True)
        ti = jnp.where(lane == TOP_K + k, rank, ti)
    cnt[...] = cnt[...] + jnp.sum(sel, axis=0, keepdims=True)
    if tm % LANES == 0:
        ti_ref[...] = jnp.transpose(ti)[:2 * TOP_K].astype(jnp.int32)
    else:
        ti_ref[...] = ti.astype(jnp.int32)
    tw_ref[...] = tv

    @pl.when(step == pl.num_programs(0) - 1)
    def _():
        cout_ref[...] = cnt[...]


def _xattn(qc, mk, mv, h1, w_co, g_moe, w_router, b_router, counts_in, nbat, rows, t_total, t_off, xn_prev=None):
    t = qc.shape[0]
    tm = nbat * rows
    row = lambda w: pl.BlockSpec((tm, w), lambda i: (i, 0))
    seq_rows = t // mk.shape[0]
    head_split = mk.ndim == 4
    scratch = [pltpu.VMEM((1, N_EXPERTS), F32), pltpu.VMEM((tm // SUBLANES, LANE_TILES, SUBLANES, LANES), F32),
               pltpu.SemaphoreType.DMA(())]
    any_spec = pl.BlockSpec(memory_space=pl.ANY)
    prev_args, prev_specs, aliases = [], [], {}
    if xn_prev is not None:
        prev_args, prev_specs, aliases = [xn_prev], [any_spec], {10: 1}
    if head_split:
        assert seq_rows == rows
        mem = pl.BlockSpec(memory_space=pl.ANY)
        scratch += [pltpu.VMEM((2, 2, CA_HEADS, nbat, N_MEM, CA_HEAD_DIM), F32), pltpu.SemaphoreType.DMA((2,))]
    else:
        mem = pl.BlockSpec((nbat, N_MEM, CA_WIDTH), lambda i: ((i * tm) // (seq_rows * nbat), 0, 0))
    tri = jnp.tri(tm, k=-1, dtype=BF16)
    if tm % LANES == 0:
        ti_spec = pl.BlockSpec((2 * TOP_K, tm), lambda i: (0, i))
        ti_shape = jax.ShapeDtypeStruct((2 * TOP_K, t), jnp.int32)
    else:
        ti_spec, ti_shape = row(LANES), jax.ShapeDtypeStruct((t, LANES), jnp.int32)
    return pl.pallas_call(
        functools.partial(_xattn_kernel, nbat=nbat, rows=rows, head_split=head_split, has_prev=xn_prev is not None,
                          tile_off=t_off // SUBLANES),
        grid=(t // tm,),
        in_specs=[row(CA_WIDTH), mem, mem, row(D_MODEL), _full((CA_WIDTH, D_MODEL)), _full((1, D_MODEL)),
                  _full((D_MODEL, N_EXPERTS)), _full((1, N_EXPERTS)), _full((tm, tm)), _full((1, N_EXPERTS))]
        + prev_specs,
        out_specs=(row(D_MODEL), any_spec, ti_spec, row(LANES), _full((1, N_EXPERTS))),
        out_shape=(jax.ShapeDtypeStruct((t, D_MODEL), F32),
                   jax.ShapeDtypeStruct((t_total // SUBLANES, SUBLANES, LANE_TILES, LANES), F32),
                   ti_shape, jax.ShapeDtypeStruct((t, LANES), F32),
                   jax.ShapeDtypeStruct((1, N_EXPERTS), F32)),
        scratch_shapes=scratch,
        input_output_aliases=aliases,
        compiler_params=_cparams(("arbitrary",), VMEM_LIMIT),
        name="cross_attn_router",
    )(qc, mk, mv, h1, w_co, g_moe, w_router, b_router, tri, counts_in, *prev_args)


def _sc_gather(table, idx):
    n = idx.shape[0]
    per_worker = n // SC_WORKERS
    pieces = per_worker // SC_ROWS
    assert pieces * SC_ROWS * SC_WORKERS == n
    mesh = plsc.VectorSubcoreMesh(core_axis_name="c", subcore_axis_name="s")

    @functools.partial(pl.kernel, mesh=mesh, out_type=jax.ShapeDtypeStruct((n, LANE_TILES, LANES), F32),
                       scratch_types=[pltpu.VMEM((SC_ROWS,), jnp.int32), pltpu.VMEM((SC_ROWS, LANE_TILES, LANES), F32),
                                      pltpu.SemaphoreType.DMA],
                       name="moe_row_gather")
    def gather(table_hbm, idx_hbm, out_hbm, idx_v, rows_v, sem):
        worker = lax.axis_index("s") * SC_CORES + lax.axis_index("c")
        base = worker * per_worker

        @pl.loop(0, pieces)
        def _(piece):
            off = base + piece * SC_ROWS
            pltpu.sync_copy(idx_hbm.at[pl.ds(off, SC_ROWS)], idx_v)
            pltpu.async_copy(table_hbm.at[idx_v], rows_v, sem).wait()
            pltpu.sync_copy(rows_v, out_hbm.at[pl.ds(off, SC_ROWS)])

    return gather(table, idx)


def _sc_scatter(rows, dest_t, n_out):
    n = rows.shape[0]
    per_worker = n // SC_WORKERS
    step = SC_ROWS // 2
    pieces = per_worker // step
    assert pieces * step * SC_WORKERS == n
    mesh = plsc.VectorSubcoreMesh(core_axis_name="c", subcore_axis_name="s")

    @functools.partial(pl.kernel, mesh=mesh, out_type=jax.ShapeDtypeStruct((n_out, LANE_TILES, LANES), F32),
                       scratch_types=[pltpu.VMEM((TOP_K, step), jnp.int32), pltpu.VMEM((step, LANE_TILES, LANES), F32),
                                      pltpu.SemaphoreType.DMA, pltpu.SemaphoreType.DMA],
                       name="moe_row_scatter")
    def scatter(rows_hbm, idx_hbm, out_hbm, idx_v, rows_v, sem_in, sem_out):
        worker = lax.axis_index("s") * SC_CORES + lax.axis_index("c")
        base = worker * per_worker

        @pl.loop(0, pieces)
        def _(piece):
            off = base + piece * step
            loads = [pltpu.async_copy(rows_hbm.at[pl.ds(off, step)], rows_v, sem_in)]
            loads += [pltpu.async_copy(idx_hbm.at[k, pl.ds(off, step)], idx_v.at[k], sem_in) for k in range(TOP_K)]
            for c in loads:
                c.wait()
            stores = [pltpu.async_copy(rows_v, out_hbm.at[idx_v.at[k]], sem_out) for k in range(TOP_K)]
            for c in stores:
                c.wait()

    return scatter(rows, dest_t)


def _expert_block_kernel(blk_e_ref, w_idx_ref, nused_ref, xs_hbm, wg_ref, bg_ref, wu_ref, bu_ref, wd_ref, bd_ref,
                         ys_hbm, wg_bf, wu_bf, wd_bf, xbuf, ybuf, xsem, ysem):
    i = pl.program_id(0)
    n_used = nused_ref[0]
    active = i < n_used
    slot = i % 2
    tiles = MOE_BLOCK // SUBLANES

    def x_copies(blk, sl):
        return [pltpu.make_async_copy(xs_hbm.at[pl.ds(blk * tiles, tiles), s], xbuf.at[sl, :, :, s, :], xsem.at[sl])
                for s in range(SUBLANES)]

    def y_copies(blk, sl):
        return [pltpu.make_async_copy(ybuf.at[sl, :, :, s, :], ys_hbm.at[pl.ds(blk * tiles, tiles), s], ysem.at[sl])
                for s in range(SUBLANES)]

    @pl.when((i == 0) & active)
    def _():
        for c in x_copies(0, 0):
            c.start()

    prev_e = blk_e_ref[jnp.maximum(i - 1, 0)]
    new_expert = (i == 0) | (blk_e_ref[i] != prev_e)

    @pl.when(active & new_expert)
    def _():
        wg_bf[...] = wg_ref[0].astype(BF16)
        wu_bf[...] = wu_ref[0].astype(BF16)
        wd_bf[...] = wd_ref[0].astype(BF16)

    @pl.when(active)
    def _():
        for c in x_copies(i, slot):
            c.wait()

        @pl.when(i + 1 < n_used)
        def _():
            for c in x_copies(i + 1, 1 - slot):
                c.start()

        x = _load_row_tiles(xbuf.at[slot]).astype(BF16)
        gate = jnp.minimum(jnp.dot(x, wg_bf[...], preferred_element_type=F32) + bg_ref[0], SWIGLU_LIMIT)
        up = jnp.clip(jnp.dot(x, wu_bf[...], preferred_element_type=F32) + bu_ref[0], -SWIGLU_LIMIT, SWIGLU_LIMIT)
        h = gate * jax.nn.sigmoid(SWIGLU_ALPHA * gate) * (up + 1.0)
        y = jnp.dot(h.astype(BF16), wd_bf[...], preferred_element_type=F32) + bd_ref[0]

        @pl.when(i >= 1)
        def _():
            for c in y_copies(i - 1, 1 - slot):
                c.wait()

        _store_row_tiles(ybuf.at[slot], y)
        for c in y_copies(i, slot):
            c.start()

        @pl.when(i == n_used - 1)
        def _():
            for c in y_copies(i, slot):
                c.wait()


def _experts_blocks(blk_e, w_idx, n_used, xs, w_gate, b_gate, w_up, b_up, w_down, b_down):
    n_slots = xs.shape[0]
    n_blocks = n_slots // MOE_BLOCK
    d_ff = w_gate.shape[-1]
    tiles = MOE_BLOCK // SUBLANES
    wspec = lambda a, b: pl.BlockSpec((1, a, b), lambda i, be, wi, nu: (wi[i], 0, 0))
    bspec = lambda b: pl.BlockSpec((1, 1, b), lambda i, be, wi, nu: (be[i], 0, 0))
    any_spec = pl.BlockSpec(memory_space=pl.ANY)
    grid_spec = pltpu.PrefetchScalarGridSpec(
        num_scalar_prefetch=3,
        grid=(n_blocks,),
        in_specs=[any_spec, wspec(D_MODEL, d_ff), bspec(d_ff), wspec(D_MODEL, d_ff), bspec(d_ff),
                  wspec(d_ff, D_MODEL), bspec(D_MODEL)],
        out_specs=any_spec,
        scratch_shapes=[pltpu.VMEM((D_MODEL, d_ff), BF16), pltpu.VMEM((D_MODEL, d_ff), BF16),
                        pltpu.VMEM((d_ff, D_MODEL), BF16),
                        pltpu.VMEM((2, tiles, LANE_TILES, SUBLANES, LANES), F32),
                        pltpu.VMEM((2, tiles, LANE_TILES, SUBLANES, LANES), F32),
                        pltpu.SemaphoreType.DMA((2,)), pltpu.SemaphoreType.DMA((2,))],
    )
    ys = pl.pallas_call(
        _expert_block_kernel,
        grid_spec=grid_spec,
        out_shape=jax.ShapeDtypeStruct((n_slots // SUBLANES, SUBLANES, LANE_TILES, LANES), F32),
        compiler_params=_cparams(("arbitrary",), VMEM_LIMIT),
        name="moe_experts",
    )(blk_e, w_idx, n_used, xs.reshape(n_slots // SUBLANES, SUBLANES, LANE_TILES, LANES), w_gate,
      b_gate.reshape(N_EXPERTS, 1, d_ff), w_up, b_up.reshape(N_EXPERTS, 1, d_ff), w_down,
      b_down.reshape(N_EXPERTS, 1, D_MODEL))
    return ys.reshape(n_slots, LANE_TILES, LANES)


def _combine_kernel(yu_hbm, h_ref, tw_ref, g_ref, *rest, tm):
    o_ref, buf, sem = rest[-3:]
    i = pl.program_id(0)
    n = pl.num_programs(0)
    slot = i % 2
    tiles = tm // SUBLANES

    def copies(step, sl):
        t0 = step * tiles
        return [pltpu.make_async_copy(yu_hbm.at[k, pl.ds(t0, tiles), s], buf.at[sl, k, :, :, s, :], sem.at[sl])
                for s in range(SUBLANES) for k in range(TOP_K)]

    @pl.when(i == 0)
    def _():
        for c in copies(0, 0):
            c.start()

    @pl.when(i + 1 < n)
    def _():
        for c in copies(i + 1, 1 - slot):
            c.start()

    for c in copies(i, slot):
        c.wait()
    tw = tw_ref[...]
    y = h_ref[...]
    for k in range(TOP_K):
        y = y + tw[:, k:k + 1] * _load_row_tiles(buf.at[slot, k])
    o_ref[...] = _rms(y, g_ref[...])


def _combine(yu, h2, tw, g_final, tm, row_off, y_prev=None):
    t = h2.shape[0]
    n_rows = yu.shape[1] * SUBLANES
    blk0 = row_off // tm
    row = lambda w: pl.BlockSpec((tm, w), lambda i: (i + blk0, 0))
    args, in_specs, aliases = [yu, h2, tw, g_final], [pl.BlockSpec(memory_space=pl.ANY), row(D_MODEL), row(LANES),
                                                      _full((1, D_MODEL))], {}
    if y_prev is not None:
        args.append(y_prev)
        in_specs.append(pl.BlockSpec(memory_space=pl.ANY))
        aliases = {4: 0}
    return pl.pallas_call(
        functools.partial(_combine_kernel, tm=tm),
        grid=(n_rows // tm,),
        in_specs=in_specs,
        out_specs=row(D_MODEL),
        out_shape=jax.ShapeDtypeStruct((t, D_MODEL), F32),
        scratch_shapes=[pltpu.VMEM((2, TOP_K, tm // SUBLANES, LANE_TILES, SUBLANES, LANES), F32),
                        pltpu.SemaphoreType.DMA((2,))],
        input_output_aliases=aliases,
        compiler_params=_cparams(("arbitrary",), VMEM_LIMIT),
        name="moe_combine",
    )(*args)


def _route(counts, m):
    counts = counts.reshape(N_EXPERTS).astype(jnp.int32)
    padded = (counts + MOE_BLOCK - 1) // MOE_BLOCK * MOE_BLOCK
    pad_end = jnp.cumsum(padded)
    pad_start = pad_end - padded
    n_blocks = -(-(m + N_EXPERTS * (MOE_BLOCK - 1)) // MOE_BLOCK)
    blk_start = jnp.arange(n_blocks, dtype=jnp.int32) * MOE_BLOCK
    owner = lambda start: jnp.minimum(jnp.sum((pad_end[None, :] <= start[:, None]).astype(jnp.int32), axis=1),
                                      N_EXPERTS - 1).astype(jnp.int32)
    n_used = pad_end[-1] // MOE_BLOCK
    last_start = jnp.maximum(n_used - 1, 0) * MOE_BLOCK
    blk_e = owner(jnp.minimum(blk_start, last_start))
    first = jnp.concatenate([jnp.ones((1,), bool), blk_e[1:] != blk_e[:-1]])
    region_end = jnp.sum(jnp.where(blk_e[:, None] == jnp.arange(N_EXPERTS)[None, :], pad_end[None, :], 0), axis=1)
    w_idx = jnp.where(first, blk_e, owner(jnp.minimum(region_end, last_start))).astype(jnp.int32)
    return pad_start, blk_e, w_idx, n_used.astype(jnp.int32).reshape(1), n_blocks * MOE_BLOCK


def _dest(ti, pad_start):
    if ti.shape[0] != 2 * TOP_K:
        ti = ti[:, :2 * TOP_K].T
    e = ti[:TOP_K]
    start = jnp.sum(jnp.where(e[None] == jnp.arange(N_EXPERTS, dtype=jnp.int32)[:, None, None],
                              pad_start[:, None, None], 0), axis=0)
    return start + ti[TOP_K:]


def _block_diag(blocks):
    g, a, b = blocks.shape
    eye = jnp.eye(g, dtype=blocks.dtype)
    return (blocks[:, :, None, :] * eye[:, None, :, None]).reshape(g * a, g * b)


def kernel(x_prompt, x_sample, mem_prompt, cache_win_k, cache_win_v, state_ssm_re, state_ssm_im, cache_mem_k,
           cache_mem_v, g_mix, w_in, sink, lam_re, lam_im, log_dt, b_re, b_im, c_re, c_im, d_skip, w_glu,
           g_attn_out, g_ssm_out, w_out, g_ca, g_mem, w_mk, w_mv, w_cq, w_co, g_moe, w_router, b_router,
           w_gate, b_gate, w_up, b_up, w_down, b_down, g_final):
    depth = g_mix.shape[0]
    assert depth == 1
    bp, lp, _ = x_prompt.shape
    bs, ls, _ = x_sample.shape
    tp, ts = bp * lp, bs * ls

    w_in_bf = w_in[0].astype(BF16)
    w_att_bf = w_out[0, :ATTN_WIDTH].astype(BF16)
    w_ssm_bf = w_out[0, ATTN_WIDTH:].astype(BF16)
    w_cq_bf = w_cq[0].astype(BF16)
    w_co_bf = w_co[0].astype(BF16)
    w_mkv_bf = jnp.concatenate([w_mk[0], w_mv[0]], axis=1).astype(BF16)
    w_glu_bf = w_glu[0].astype(BF16)
    row = lambda a: a.reshape(1, -1)

    a_re, a_im, bb_re, bb_im = _discretize(lam_re[0], lam_im[0], log_dt[0], b_re[0].transpose(0, 2, 1),
                                           b_im[0].transpose(0, 2, 1))
    hg = SSM_GROUPS // 2
    wb = jnp.stack([jnp.concatenate([_block_diag(bb_re[h * hg:(h + 1) * hg]),
                                     _block_diag(bb_im[h * hg:(h + 1) * hg])], axis=1) for h in range(2)]).astype(BF16)
    cre_t = c_re[0].transpose(0, 2, 1)
    cim_t = c_im[0].transpose(0, 2, 1)
    wc = jnp.stack([jnp.concatenate([_block_diag(cre_t[h * hg:(h + 1) * hg]),
                                     -_block_diag(cim_t[h * hg:(h + 1) * hg])], axis=0) for h in range(2)]).astype(BF16)
    ssm_w = (wb, wc, a_re.reshape(1, SSM_LANES), a_im.reshape(1, SSM_LANES), row(d_skip[0]), w_glu_bf,
             row(g_ssm_out[0]))

    xp = x_prompt.reshape(tp, D_MODEL)
    xs_ = x_sample.reshape(ts, D_MODEL)

    qp, kp, vp, up = _inproj(xp, row(g_mix[0]), w_in_bf, 512)
    qs, ks, vs, us = _inproj(xs_, row(g_mix[0]), w_in_bf, 256)

    att_p = _attn_prompt(sink[0], qp, kp, vp, row(g_attn_out[0]), bp, lp)
    ck = cache_win_k[0].reshape(bs, WINDOW, KV_WIDTH)
    cv = cache_win_v[0].reshape(bs, WINDOW, KV_WIDTH)
    att_s = _attn_sample(sink[0], qs, ks, vs, ck, cv, row(g_attn_out[0]), bs, ls, 16)

    zero_state = jnp.zeros((bp, SSM_LANES), F32)
    ssm_p, p_sre, p_sim = _ssm(up.reshape(bp, lp, SSM_WIDTH), zero_state, zero_state, ssm_w, 32)
    ssm_s, s_sre, s_sim = _ssm(us.reshape(bs, ls, SSM_WIDTH), state_ssm_re[0].reshape(bs, SSM_LANES),
                               state_ssm_im[0].reshape(bs, SSM_LANES), ssm_w, ls)

    h1p, qcp = _outproj(xp, att_p, ssm_p.reshape(tp, SSM_WIDTH), w_att_bf, w_ssm_bf, row(g_ca[0]), w_cq_bf, 512)
    h1s, qcs = _outproj(xs_, att_s, ssm_s.reshape(ts, SSM_WIDTH), w_att_bf, w_ssm_bf, row(g_ca[0]), w_cq_bf, 256)

    mk_p, mv_p = _memkv(mem_prompt.reshape(bp * N_MEM, D_MODEL), row(g_mem[0]), w_mkv_bf, 512)

    no_counts = jnp.zeros((1, N_EXPERTS), F32)
    w_router_bf = w_router[0].astype(BF16)
    tt = tp + ts
    h2p, xn, tip, twp, cnt_p = _xattn(qcp, mk_p.reshape(bp, N_MEM, CA_WIDTH), mv_p.reshape(bp, N_MEM, CA_WIDTH),
                                      h1p, w_co_bf, row(g_moe[0]), w_router_bf, row(b_router[0]), no_counts,
                                      1, 512, tt, 0)
    h2s, xn, tis, tws, cnt = _xattn(qcs, cache_mem_k[0], cache_mem_v[0], h1s, w_co_bf, row(g_moe[0]),
                                    w_router_bf, row(b_router[0]), cnt_p, 8, ls, tt, tp, xn_prev=xn)

    m = tt * TOP_K
    pad_start, blk_e, w_idx, n_used, n_slots = _route(cnt, m)
    dest = jnp.concatenate([_dest(tip, pad_start), _dest(tis, pad_start)], axis=1)
    xs = _sc_scatter(xn.reshape(tt, LANE_TILES, LANES), dest, n_slots)
    ys = _experts_blocks(blk_e, w_idx, n_used, xs, w_gate[0], b_gate[0], w_up[0], b_up[0], w_down[0], b_down[0])
    def gathered(t0, t1):
        rows = _sc_gather(ys, dest[:, t0:t1].reshape(-1))
        return rows.reshape(TOP_K, (t1 - t0) // SUBLANES, SUBLANES, LANE_TILES, LANES)

    y_prompt = None
    chunk = tp // COMBINE_CHUNKS
    for c in range(COMBINE_CHUNKS):
        y_prompt = _combine(gathered(c * chunk, (c + 1) * chunk), h2p, twp, row(g_final), 256, c * chunk,
                            y_prev=y_prompt)
    y_prompt = y_prompt.reshape(bp, lp, D_MODEL)
    y_sample = _combine(gathered(tp, tt), h2s, tws, row(g_final), 256, 0).reshape(bs, ls, D_MODEL)

    kp4 = kp.reshape(bp, lp, N_KV_HEADS, HEAD_DIM)
    vp4 = vp.reshape(bp, lp, N_KV_HEADS, HEAD_DIM)
    p_win_k = kp4[:, -WINDOW:][None]
    p_win_v = vp4[:, -WINDOW:][None]
    s_win_k = jnp.concatenate([cache_win_k[0], ks.reshape(bs, ls, N_KV_HEADS, HEAD_DIM)], axis=1)[:, -WINDOW:][None]
    s_win_v = jnp.concatenate([cache_win_v[0], vs.reshape(bs, ls, N_KV_HEADS, HEAD_DIM)], axis=1)[:, -WINDOW:][None]
    st = lambda a, b: a.reshape(1, b, SSM_GROUPS, SSM_STATE)
    p_mem_k = mk_p.reshape(1, bp, N_MEM, CA_HEADS, CA_HEAD_DIM)
    p_mem_v = mv_p.reshape(1, bp, N_MEM, CA_HEADS, CA_HEAD_DIM)
    return (y_prompt, y_sample, p_win_k, p_win_v, st(p_sre, bp), st(p_sim, bp), p_mem_k, p_mem_v,
            s_win_k, s_win_v, st(s_sre, bs), st(s_sim, bs))
```

```python
import functools
import math

import jax
import jax.numpy as jnp
from jax import lax
from jax.experimental import pallas as pl
from jax.experimental.pallas import tpu as pltpu
from jax.experimental.pallas import tpu_sc as plsc

F32 = jnp.float32
BF16 = jnp.bfloat16

D_MODEL = 1024
HEAD_DIM = 64
N_Q_HEADS = 8
N_KV_HEADS = 2
Q_PER_KV = 4
ATTN_WIDTH = 512
KV_WIDTH = 128
WINDOW = 128
SSM_WIDTH = 512
SSM_GROUP = 16
SSM_GROUPS = 32
SSM_STATE = 64
SSM_LANES = SSM_GROUPS * SSM_STATE
SSM_HALF = SSM_LANES // 2
IN_WIDTH = ATTN_WIDTH + 2 * KV_WIDTH + SSM_WIDTH
N_MEM = 256
CA_HEADS = 4
CA_HEAD_DIM = 128
CA_WIDTH = 512
N_EXPERTS = 32
TOP_K = 4
SWIGLU_ALPHA = 1.702
SWIGLU_LIMIT = 7.0
RMS_EPS = 1e-5

assert math.frexp(HEAD_DIM ** -0.5)[0] == 0.5

SUBLANES = 8
LANES = 128
MOE_BLOCK = 256
ATTN_SEQS = 4
SC_CORES = 2
SC_WORKERS = 32
SC_ROWS = 64
COMBINE_CHUNKS = 4
VMEM_LIMIT = 56 * 1024 * 1024


def _cparams(sem, vmem=None):
    return pltpu.CompilerParams(dimension_semantics=sem, vmem_limit_bytes=vmem)


def _rms(x, g):
    return x * lax.rsqrt(jnp.mean(x * x, axis=-1, keepdims=True) + RMS_EPS) * g


def _full(shape):
    return pl.BlockSpec(shape, lambda *_: (0,) * len(shape))


LANE_TILES = D_MODEL // LANES


def _store_row_tiles(ref, x):
    for c in range(LANE_TILES):
        ref[:, c] = x[:, c * LANES:(c + 1) * LANES].reshape(x.shape[0] // SUBLANES, SUBLANES, LANES)


def _load_row_tiles(ref):
    rows = ref.shape[0] * SUBLANES
    return jnp.concatenate([ref[:, c].reshape(rows, LANES) for c in range(LANE_TILES)], axis=1)


def _start_alternating(copies):
    for j, c in enumerate(copies):
        c.start(priority=j % 2)


def _disc_kernel(lr_ref, li_ref, ldt_ref, bre_ref, bim_ref, are_ref, aim_ref, bbre_ref, bbim_ref):
    lr = lr_ref[...]
    li = li_ref[...]
    dt = jnp.exp(ldt_ref[...])
    mag = jnp.exp(lr * dt)
    ang = li * dt
    ab_re = mag * jnp.cos(ang)
    ab_im = mag * jnp.sin(ang)
    nr = ab_re - 1.0
    ni = ab_im
    den = lr * lr + li * li
    f_re = (nr * lr + ni * li) / den
    f_im = (ni * lr - nr * li) / den
    are_ref[...] = ab_re
    aim_ref[...] = ab_im
    br = bre_ref[...]
    bi = bim_ref[...]
    bbre_ref[...] = f_re[:, None, :] * br - f_im[:, None, :] * bi
    bbim_ref[...] = f_re[:, None, :] * bi + f_im[:, None, :] * br


def _discretize(lam_re, lam_im, log_dt, b_re_t, b_im_t):
    g, n = lam_re.shape
    c = b_re_t.shape[1]
    return pl.pallas_call(
        _disc_kernel,
        out_shape=(jax.ShapeDtypeStruct((g, n), F32), jax.ShapeDtypeStruct((g, n), F32),
                   jax.ShapeDtypeStruct((g, c, n), F32), jax.ShapeDtypeStruct((g, c, n), F32)),
        name="ssm_discretize",
    )(lam_re, lam_im, log_dt.reshape(g, 1), b_re_t, b_im_t)


def _inproj_kernel(x_ref, g_ref, w_ref, q_ref, k_ref, v_ref, u_ref):
    a = _rms(x_ref[...], g_ref[...]).astype(BF16)
    proj = jnp.dot(a, w_ref[...], preferred_element_type=F32)
    q_ref[...] = (proj[:, :ATTN_WIDTH] * (HEAD_DIM ** -0.5)).astype(BF16)
    k_ref[...] = proj[:, ATTN_WIDTH:ATTN_WIDTH + KV_WIDTH]
    v_ref[...] = proj[:, ATTN_WIDTH + KV_WIDTH:ATTN_WIDTH + 2 * KV_WIDTH]
    u_ref[...] = proj[:, ATTN_WIDTH + 2 * KV_WIDTH:]


def _inproj(x, g_mix, w_in_bf, tm):
    t = x.shape[0]
    row = lambda w: pl.BlockSpec((tm, w), lambda i: (i, 0))
    return pl.pallas_call(
        _inproj_kernel,
        grid=(t // tm,),
        in_specs=[row(D_MODEL), _full((1, D_MODEL)), _full((D_MODEL, IN_WIDTH))],
        out_specs=(row(ATTN_WIDTH), row(KV_WIDTH), row(KV_WIDTH), row(SSM_WIDTH)),
        out_shape=(jax.ShapeDtypeStruct((t, ATTN_WIDTH), BF16), jax.ShapeDtypeStruct((t, KV_WIDTH), F32),
                   jax.ShapeDtypeStruct((t, KV_WIDTH), F32), jax.ShapeDtypeStruct((t, SSM_WIDTH), F32)),
        compiler_params=_cparams(("arbitrary",)),
        name="in_projection",
    )(x, g_mix, w_in_bf)


def _sink_softmax_pv(s, sink, v_bf):
    m = jnp.maximum(jnp.max(s, axis=-1, keepdims=True), sink)
    p = jnp.exp(s - m)
    denom = jnp.sum(p, axis=-1, keepdims=True) + jnp.exp(sink - m)
    return jnp.dot(p.astype(BF16), v_bf, preferred_element_type=F32) / denom


def _attn_prompt_kernel(sink_ref, q_ref, kc_ref, kp_ref, vc_ref, vp_ref, g_ref, o_ref):
    i = pl.program_id(1)
    q = q_ref[...]
    kk = jnp.concatenate([kp_ref[...], kc_ref[...]], axis=0).astype(BF16)
    vv = jnp.concatenate([vp_ref[...], vc_ref[...]], axis=0).astype(BF16)
    rows = Q_PER_KV * WINDOW
    qi = lax.broadcasted_iota(jnp.int32, (rows, 2 * WINDOW), 0) & (WINDOW - 1)
    kj = lax.broadcasted_iota(jnp.int32, (rows, 2 * WINDOW), 1)
    rel = qi + WINDOW - kj
    mask = (rel >= 0) & (rel < WINDOW) & ((kj >= WINDOW) | (i > 0))
    head_of_row = lax.broadcasted_iota(jnp.int32, (rows, 1), 0) // WINDOW
    low_q = lax.broadcasted_iota(jnp.int32, (WINDOW, LANES), 1) < HEAD_DIM
    zero_q = jnp.zeros((WINDOW, LANES), BF16)
    pairs = []
    for j in range(N_KV_HEADS):
        kh = kk[:, j * HEAD_DIM:(j + 1) * HEAD_DIM]
        vh = vv[:, j * HEAD_DIM:(j + 1) * HEAD_DIM]
        k2 = jnp.concatenate([kh, kh], axis=1)
        v2 = jnp.concatenate([vh, vh], axis=1)
        stacked = []
        for p in range(Q_PER_KV // 2):
            qp = q[:, (j * Q_PER_KV + 2 * p) * HEAD_DIM:(j * Q_PER_KV + 2 * p + 2) * HEAD_DIM]
            stacked += [jnp.where(low_q, qp, zero_q), jnp.where(low_q, zero_q, qp)]
        qs = jnp.concatenate(stacked, axis=0)
        sink = jnp.zeros((rows, 1), F32)
        for g in range(Q_PER_KV):
            sink = jnp.where(head_of_row == g, sink_ref[j * Q_PER_KV + g], sink)
        s = lax.dot_general(qs, k2, (((1,), (1,)), ((), ())), preferred_element_type=F32)
        s = jnp.where(mask, s, -jnp.inf)
        o = _sink_softmax_pv(s, sink, v2)
        for p in range(Q_PER_KV // 2):
            pairs.append(jnp.where(low_q, o[2 * p * WINDOW:(2 * p + 1) * WINDOW],
                                   o[(2 * p + 1) * WINDOW:(2 * p + 2) * WINDOW]))
    att = jnp.concatenate(pairs, axis=-1)
    o_ref[...] = _rms(att, g_ref[...]).astype(BF16)


def _attn_prompt(sink, q, k, v, g_attn_out, b, l):
    nb = l // WINDOW
    cur = lambda w: pl.BlockSpec((WINDOW, w), lambda bi, i: (bi * nb + i, 0))
    prev = lambda w: pl.BlockSpec((WINDOW, w), lambda bi, i: (bi * nb + jnp.maximum(i - 1, 0), 0))
    return pl.pallas_call(
        _attn_prompt_kernel,
        grid=(b, nb),
        in_specs=[pl.BlockSpec(memory_space=pltpu.SMEM), cur(ATTN_WIDTH), cur(KV_WIDTH), prev(KV_WIDTH),
                  cur(KV_WIDTH), prev(KV_WIDTH), _full((1, ATTN_WIDTH))],
        out_specs=cur(ATTN_WIDTH),
        out_shape=jax.ShapeDtypeStruct((b * l, ATTN_WIDTH), BF16),
        compiler_params=_cparams(("arbitrary", "arbitrary")),
        name="window_attn_prompt",
    )(sink, q, k, k, v, v, g_attn_out)


def _attn_sample_kernel(sink_ref, q_ref, kn_ref, vn_ref, ck_ref, cv_ref, g_ref, o_ref, *, bt, t):
    nk = WINDOW + t
    sb = ATTN_SEQS
    keys = sb * nk
    cols = Q_PER_KV * sb * t
    kb = lax.broadcasted_iota(jnp.int32, (sb, nk, cols), 0).reshape(keys, cols)
    kk = lax.broadcasted_iota(jnp.int32, (sb, nk, cols), 1).reshape(keys, cols)
    col = lax.broadcasted_iota(jnp.int32, (keys, cols), 1) % (sb * t)
    cb, ct = col // t, col % t
    mask = (kb == cb) & (((kk < WINDOW) & (kk > ct)) | ((kk >= WINDOW) & ((kk - WINDOW) <= ct)))
    head_of_col = lax.broadcasted_iota(jnp.int32, (1, cols), 1) // (sb * t)
    low = lax.broadcasted_iota(jnp.int32, (sb * t, LANES), 1) < HEAD_DIM
    zero_q = jnp.zeros((sb * t, LANES), BF16)
    swap = lambda x: pltpu.roll(x, HEAD_DIM, 1)
    q_all = q_ref[...]
    kn_all = kn_ref[...]
    vn_all = vn_ref[...]
    outs_b = []
    for b0 in range(0, bt, sb):
        r0 = b0 * t
        newk = kn_all[r0:r0 + sb * t].reshape(sb, t, KV_WIDTH)
        newv = vn_all[r0:r0 + sb * t].reshape(sb, t, KV_WIDTH)
        k_all = jnp.concatenate([ck_ref[b0:b0 + sb], newk], axis=1).reshape(keys, KV_WIDTH).astype(BF16)
        v_all = jnp.concatenate([cv_ref[b0:b0 + sb], newv], axis=1).reshape(keys, KV_WIDTH).astype(BF16)
        head_out = []
        for j in range(N_KV_HEADS):
            mine = low if j == 0 else jnp.logical_not(low)
            qs = []
            for g in range(Q_PER_KV):
                h = j * Q_PER_KV + g
                tile = q_all[r0:r0 + sb * t, (h // 2) * LANES:(h // 2 + 1) * LANES]
                if h % 2 != j:
                    tile = swap(tile.astype(F32)).astype(BF16)
                qs.append(jnp.where(mine, tile, zero_q))
            qj = jnp.concatenate(qs, axis=0)
            sink = jnp.zeros((1, cols), F32)
            for g in range(Q_PER_KV):
                sink = jnp.where(head_of_col == g, sink_ref[j * Q_PER_KV + g], sink)
            s = lax.dot_general(k_all, qj, (((1,), (1,)), ((), ())), preferred_element_type=F32)
            s = jnp.where(mask, s, -jnp.inf)
            m = jnp.maximum(jnp.max(s, axis=0, keepdims=True), sink)
            p = jnp.exp(s - m)
            p = p / (jnp.sum(p, axis=0, keepdims=True) + jnp.exp(sink - m))
            o = lax.dot_general(p.astype(BF16), v_all, (((0,), (0,)), ((), ())), preferred_element_type=F32)
            for g in range(Q_PER_KV):
                og = o[g * sb * t:(g + 1) * sb * t]
                head_out.append(og if g % 2 == j else swap(og))
        tiles = [jnp.where(low, head_out[2 * pr], head_out[2 * pr + 1]) for pr in range(N_Q_HEADS // 2)]
        outs_b.append(jnp.concatenate(tiles, axis=-1))
    att = jnp.concatenate(outs_b, axis=0)
    o_ref[...] = _rms(att, g_ref[...]).astype(BF16)


def _attn_sample(sink, q, k, v, cache_k, cache_v, g_attn_out, b, t, bt):
    row = lambda w: pl.BlockSpec((bt * t, w), lambda i: (i, 0))
    cache = pl.BlockSpec((bt, WINDOW, KV_WIDTH), lambda i: (i, 0, 0))
    return pl.pallas_call(
        functools.partial(_attn_sample_kernel, bt=bt, t=t),
        grid=(b // bt,),
        in_specs=[pl.BlockSpec(memory_space=pltpu.SMEM), row(ATTN_WIDTH), row(KV_WIDTH), row(KV_WIDTH),
                  cache, cache, _full((1, ATTN_WIDTH))],
        out_specs=row(ATTN_WIDTH),
        out_shape=jax.ShapeDtypeStruct((b * t, ATTN_WIDTH), BF16),
        compiler_params=_cparams(("arbitrary",)),
        name="window_attn_sample",
    )(sink, q, k, v, cache_k, cache_v, g_attn_out)


def _ssm_project(u_ref, perm_ref, wb_ref, vre, vim, utb, r):
    u = u_ref[...].reshape(r, SSM_WIDTH)
    u_hi = u.astype(BF16)
    u_lo = (u - u_hi.astype(F32)).astype(BF16)
    perm = perm_ref[...]
    u_tb = (jnp.dot(perm, u_hi, preferred_element_type=F32) + jnp.dot(perm, u_lo, preferred_element_type=F32))
    utb[...] = u_tb
    u_bf = u_tb.astype(BF16)
    half_w = SSM_WIDTH // 2
    for h in range(2):
        vv = jnp.dot(u_bf[:, h * half_w:(h + 1) * half_w], wb_ref[h], preferred_element_type=F32)
        vre[:, h * SSM_HALF:(h + 1) * SSM_HALF] = vv[:, :SSM_HALF]
        vim[:, h * SSM_HALF:(h + 1) * SSM_HALF] = vv[:, SSM_HALF:]


def _ssm_scan(are_ref, aim_ref, vre, vim, sre, sim, nb, s):
    lane_chunk = SSM_LANES // 2
    for rg in range(nb // SUBLANES):
        for lc in range(SSM_LANES // lane_chunk):
            lanes = slice(lc * lane_chunk, (lc + 1) * lane_chunk)
            rows0 = slice(rg * SUBLANES, (rg + 1) * SUBLANES)
            ar = jnp.broadcast_to(are_ref[:, lanes], (SUBLANES, lane_chunk))
            ai = jnp.broadcast_to(aim_ref[:, lanes], (SUBLANES, lane_chunk))

            def step(t, carry, lanes=lanes, rg=rg, ar=ar, ai=ai):
                hr, hi = carry
                row = pl.multiple_of(t * nb + rg * SUBLANES, SUBLANES)
                nhr = ar * hr - ai * hi + vre[pl.ds(row, SUBLANES), lanes]
                nhi = ar * hi + ai * hr + vim[pl.ds(row, SUBLANES), lanes]
                vre[pl.ds(row, SUBLANES), lanes] = nhr
                vim[pl.ds(row, SUBLANES), lanes] = nhi
                return nhr, nhi

            hr, hi = lax.fori_loop(0, s, step, (sre[rows0, lanes], sim[rows0, lanes]), unroll=True)
            sre[rows0, lanes] = hr
            sim[rows0, lanes] = hi


def _ssm_output(vre, vim, utb, permt_ref, wc_ref, d_ref, wglu_ref, g_ref, o_ref, nb, s):
    ys = []
    for h in range(2):
        hcat = jnp.concatenate([vre[:, h * SSM_HALF:(h + 1) * SSM_HALF], vim[:, h * SSM_HALF:(h + 1) * SSM_HALF]],
                               axis=1).astype(BF16)
        ys.append(jnp.dot(hcat, wc_ref[h], preferred_element_type=F32))
    y = jnp.concatenate(ys, axis=1) + d_ref[...] * utb[...]
    g = jax.nn.gelu(y)
    z = jnp.dot(g.astype(BF16), wglu_ref[...], preferred_element_type=F32)
    out = g * jax.nn.sigmoid(z)
    on = _rms(out, g_ref[...]).astype(BF16)
    o_bt = jnp.dot(permt_ref[...], on, preferred_element_type=F32).astype(BF16)
    o_ref[...] = o_bt.reshape(nb, s, SSM_WIDTH)


def _ssm_kernel(u_ref, perm_ref, permt_ref, wb_ref, wc_ref, are_ref, aim_ref, d_ref, wglu_ref, g_ref,
                h0re_ref, h0im_ref, o_ref, hre_ref, him_ref, vre, vim, utb, sre, sim, *, nb, s):
    c = pl.program_id(0)

    @pl.when(c == 0)
    def _():
        sre[...] = h0re_ref[...]
        sim[...] = h0im_ref[...]

    _ssm_project(u_ref, perm_ref, wb_ref, vre, vim, utb, nb * s)
    _ssm_scan(are_ref, aim_ref, vre, vim, sre, sim, nb, s)
    _ssm_output(vre, vim, utb, permt_ref, wc_ref, d_ref, wglu_ref, g_ref, o_ref, nb, s)

    @pl.when(c == pl.num_programs(0) - 1)
    def _():
        hre_ref[...] = sre[...]
        him_ref[...] = sim[...]


def _ssm(u, h0_re, h0_im, ssm_w, s):
    nb, l, _ = u.shape
    r = nb * s
    ridx = jnp.arange(r)
    src = (ridx % nb) * s + ridx // nb
    perm = (src[:, None] == ridx[None, :]).astype(BF16)
    wb, wc, a_re, a_im, d_skip, w_glu, g_out = ssm_w
    in_blk = out_blk = pl.BlockSpec((nb, s, SSM_WIDTH), lambda c: (0, c, 0))
    chunk_bufs = [pltpu.VMEM((r, SSM_LANES), F32), pltpu.VMEM((r, SSM_LANES), F32), pltpu.VMEM((r, SSM_WIDTH), F32)]
    st_blk = _full((nb, SSM_LANES))
    return pl.pallas_call(
        functools.partial(_ssm_kernel, nb=nb, s=s),
        grid=(l // s,),
        in_specs=[in_blk, _full((r, r)), _full((r, r)), _full(wb.shape), _full(wc.shape),
                  _full((1, SSM_LANES)), _full((1, SSM_LANES)), _full((1, SSM_WIDTH)),
                  _full((SSM_WIDTH, SSM_WIDTH)), _full((1, SSM_WIDTH)), st_blk, st_blk],
        out_specs=(out_blk, st_blk, st_blk),
        out_shape=(jax.ShapeDtypeStruct((nb, l, SSM_WIDTH), BF16),
                   jax.ShapeDtypeStruct((nb, SSM_LANES), F32), jax.ShapeDtypeStruct((nb, SSM_LANES), F32)),
        scratch_shapes=chunk_bufs + [pltpu.VMEM((nb, SSM_LANES), F32), pltpu.VMEM((nb, SSM_LANES), F32)],
        compiler_params=_cparams(("arbitrary",), VMEM_LIMIT),
        name="ssm_scan",
    )(u, perm, perm.T, wb, wc, a_re, a_im, d_skip, w_glu, g_out, h0_re, h0_im)


def _outproj_kernel(x_ref, att_ref, ssm_ref, wa_ref, ws_ref, gca_ref, wcq_ref, h_ref, qc_ref):
    h = (x_ref[...] + jnp.dot(att_ref[...], wa_ref[...], preferred_element_type=F32)
         + jnp.dot(ssm_ref[...], ws_ref[...], preferred_element_type=F32))
    h_ref[...] = h
    xn = _rms(h, gca_ref[...]).astype(BF16)
    qc_ref[...] = jnp.dot(xn, wcq_ref[...], preferred_element_type=F32).astype(BF16)


def _outproj(x, att, ssm, w_att, w_ssm, g_ca, w_cq, tm):
    t = x.shape[0]
    row = lambda w: pl.BlockSpec((tm, w), lambda i: (i, 0))
    return pl.pallas_call(
        _outproj_kernel,
        grid=(t // tm,),
        in_specs=[row(D_MODEL), row(ATTN_WIDTH), row(SSM_WIDTH), _full((ATTN_WIDTH, D_MODEL)),
                  _full((SSM_WIDTH, D_MODEL)), _full((1, D_MODEL)), _full((D_MODEL, CA_WIDTH))],
        out_specs=(row(D_MODEL), row(CA_WIDTH)),
        out_shape=(jax.ShapeDtypeStruct((t, D_MODEL), F32), jax.ShapeDtypeStruct((t, CA_WIDTH), BF16)),
        compiler_params=_cparams(("arbitrary",)),
        name="out_projection",
    )(x, att, ssm, w_att, w_ssm, g_ca, w_cq)


def _memkv_kernel(m_ref, g_ref, w_ref, k_ref, v_ref):
    m = _rms(m_ref[...], g_ref[...]).astype(BF16)
    kv = jnp.dot(m, w_ref[...], preferred_element_type=F32)
    k_ref[...] = kv[:, :CA_WIDTH]
    v_ref[...] = kv[:, CA_WIDTH:]


def _memkv(mem, g_mem, w_mkv_bf, tm):
    t = mem.shape[0]
    row = lambda w: pl.BlockSpec((tm, w), lambda i: (i, 0))
    return pl.pallas_call(
        _memkv_kernel,
        grid=(t // tm,),
        in_specs=[row(D_MODEL), _full((1, D_MODEL)), _full((D_MODEL, 2 * CA_WIDTH))],
        out_specs=(row(CA_WIDTH), row(CA_WIDTH)),
        out_shape=(jax.ShapeDtypeStruct((t, CA_WIDTH), F32), jax.ShapeDtypeStruct((t, CA_WIDTH), F32)),
        compiler_params=_cparams(("arbitrary",)),
        name="memory_kv",
    )(mem, g_mem, w_mkv_bf)


def _xattn_kernel(qc_ref, mk_ref, mv_ref, h_ref, wco_ref, gmoe_ref, wr_ref, br_ref, tri_ref, cin_ref, *refs,
                  nbat, rows, head_split, has_prev, tile_off):
    refs = refs[1:] if has_prev else refs
    h2_ref, xn_hbm, ti_ref, tw_ref, cout_ref, cnt, xn_buf, xn_sem = refs[:8]
    step = pl.program_id(0)
    nsteps = pl.num_programs(0)
    tm = nbat * rows

    def xn_copies(s):
        dst0 = tile_off + s * (tm // SUBLANES)
        return [pltpu.make_async_copy(xn_buf.at[:, :, sl, :], xn_hbm.at[pl.ds(dst0, tm // SUBLANES), sl], xn_sem)
                for sl in range(SUBLANES)]

    @pl.when(step == 0)
    def _():
        cnt[...] = cin_ref[...]

    if head_split:
        mbuf, msem = refs[8:]
        slot = step % 2

        def head_copies(s, sl):
            b0 = s * nbat
            return [pltpu.make_async_copy(src.at[pl.ds(b0, nbat), :, hd, :], mbuf.at[sl, kv, hd], msem.at[sl])
                    for kv, src in enumerate((mk_ref, mv_ref)) for hd in range(CA_HEADS)]

        @pl.when(step == 0)
        def _():
            _start_alternating(head_copies(0, 0))

        @pl.when(step + 1 < nsteps)
        def _():
            _start_alternating(head_copies(step + 1, 1 - slot))

        for c in head_copies(step, slot):
            c.wait()

    q_all = qc_ref[...]
    if head_split:
        keys = nbat * N_MEM
        own = (lax.broadcasted_iota(jnp.int32, (keys, tm), 0) // N_MEM
               == lax.broadcasted_iota(jnp.int32, (keys, tm), 1) // rows)
        outs = []
        for hd in range(CA_HEADS):
            sl = slice(hd * CA_HEAD_DIM, (hd + 1) * CA_HEAD_DIM)
            mk = mbuf[slot, 0, hd].reshape(keys, CA_HEAD_DIM).astype(BF16)
            mv = mbuf[slot, 1, hd].reshape(keys, CA_HEAD_DIM).astype(BF16)
            s = lax.dot_general(mk, q_all[:, sl], (((1,), (1,)), ((), ())),
                                preferred_element_type=F32) * (CA_HEAD_DIM ** -0.5)
            s = jnp.where(own, s, -jnp.inf)
            p = jnp.exp(s - jnp.max(s, axis=0, keepdims=True))
            p = p / jnp.sum(p, axis=0, keepdims=True)
            outs.append(lax.dot_general(p.astype(BF16), mv, (((0,), (0,)), ((), ())), preferred_element_type=F32))
        o = jnp.concatenate(outs, axis=-1).astype(BF16)
    else:
        outs_b = []
        for bi in range(nbat):
            q = q_all[bi * rows:(bi + 1) * rows]
            outs = []
            for hd in range(CA_HEADS):
                sl = slice(hd * CA_HEAD_DIM, (hd + 1) * CA_HEAD_DIM)
                mk = mk_ref[bi, :, sl].astype(BF16)
                mv = mv_ref[bi, :, sl].astype(BF16)
                s = lax.dot_general(q[:, sl], mk, (((1,), (1,)), ((), ())),
                                    preferred_element_type=F32) * (CA_HEAD_DIM ** -0.5)
                m = jnp.max(s, axis=-1, keepdims=True)
                p = jnp.exp(s - m)
                denom = jnp.sum(p, axis=-1, keepdims=True)
                outs.append(jnp.dot(p.astype(BF16), mv, preferred_element_type=F32) / denom)
            outs_b.append(jnp.concatenate(outs, axis=-1))
        o = jnp.concatenate(outs_b, axis=0).astype(BF16)
    h2 = h_ref[...] + jnp.dot(o, wco_ref[...], preferred_element_type=F32)
    h2_ref[...] = h2
    xn = _rms(h2, gmoe_ref[...])
    @pl.when(step >= 1)
    def _():
        for c in xn_copies(step - 1):
            c.wait()

    _store_row_tiles(xn_buf, xn)
    for c in xn_copies(step):
        c.start()

    @pl.when(step == nsteps - 1)
    def _():
        for c in xn_copies(step):
            c.wait()

    logits = jnp.dot(xn.astype(BF16), wr_ref[...], preferred_element_type=F32) + br_ref[...]
    n = logits.shape[0]
    eidx = lax.broadcasted_iota(jnp.int32, (n, N_EXPERTS), 1).astype(F32)
    lane = lax.broadcasted_iota(jnp.int32, (n, LANES), 1)
    ti = jnp.zeros((n, LANES), F32)
    tv = jnp.zeros((n, LANES), F32)
    work = logits
    vals = []
    hits = []
    for k in range(TOP_K):
        mx = jnp.max(work, axis=-1, keepdims=True)
        idx = jnp.min(jnp.where(work == mx, eidx, float(N_EXPERTS)), axis=-1, keepdims=True)
        vals.append(mx)
        hits.append(eidx == idx)
        ti = jnp.where(lane == k, idx, ti)
        work = jnp.where(hits[k], -jnp.inf, work)
    ex = [jnp.exp(v - vals[0]) for v in vals]
    tot = ex[0] + ex[1] + ex[2] + ex[3]
    for k in range(TOP_K):
        tv = jnp.where(lane == k, ex[k] / tot, tv)
    sel = jnp.zeros((n, N_EXPERTS), F32)
    for k in range(TOP_K):
        sel = sel + jnp.where(hits[k], 1.0, 0.0)
    before = jnp.dot(tri_ref[...], sel.astype(BF16), preferred_element_type=F32) + cnt[...]
    for k in range(TOP_K):
        rank = jnp.sum(jnp.where(hits[k], before, 0.0), axis=-1, keepdims=True)
        ti = jnp.where(lane == TOP_K + k, rank, ti)
    cnt[...] = cnt[...] + jnp.sum(sel, axis=0, keepdims=True)
    if tm % LANES == 0:
        ti_ref[...] = jnp.transpose(ti)[:2 * TOP_K].astype(jnp.int32)
    else:
        ti_ref[...] = ti.astype(jnp.int32)
    tw_ref[...] = tv

    @pl.when(step == pl.num_programs(0) - 1)
    def _():
        cout_ref[...] = cnt[...]


def _xattn(qc, mk, mv, h1, w_co, g_moe, w_router, b_router, counts_in, nbat, rows, t_total, t_off, xn_prev=None):
    t = qc.shape[0]
    tm = nbat * rows
    row = lambda w: pl.BlockSpec((tm, w), lambda i: (i, 0))
    seq_rows = t // mk.shape[0]
    head_split = mk.ndim == 4
    scratch = [pltpu.VMEM((1, N_EXPERTS), F32), pltpu.VMEM((tm // SUBLANES, LANE_TILES, SUBLANES, LANES), F32),
               pltpu.SemaphoreType.DMA(())]
    any_spec = pl.BlockSpec(memory_space=pl.ANY)
    prev_args, prev_specs, aliases = [], [], {}
    if xn_prev is not None:
        prev_args, prev_specs, aliases = [xn_prev], [any_spec], {10: 1}
    if head_split:
        assert seq_rows == rows
        mem = pl.BlockSpec(memory_space=pl.ANY)
        scratch += [pltpu.VMEM((2, 2, CA_HEADS, nbat, N_MEM, CA_HEAD_DIM), F32), pltpu.SemaphoreType.DMA((2,))]
    else:
        mem = pl.BlockSpec((nbat, N_MEM, CA_WIDTH), lambda i: ((i * tm) // (seq_rows * nbat), 0, 0))
    tri = jnp.tri(tm, k=-1, dtype=BF16)
    if tm % LANES == 0:
        ti_spec = pl.BlockSpec((2 * TOP_K, tm), lambda i: (0, i))
        ti_shape = jax.ShapeDtypeStruct((2 * TOP_K, t), jnp.int32)
    else:
        ti_spec, ti_shape = row(LANES), jax.ShapeDtypeStruct((t, LANES), jnp.int32)
    return pl.pallas_call(
        functools.partial(_xattn_kernel, nbat=nbat, rows=rows, head_split=head_split, has_prev=xn_prev is not None,
                          tile_off=t_off // SUBLANES),
        grid=(t // tm,),
        in_specs=[row(CA_WIDTH), mem, mem, row(D_MODEL), _full((CA_WIDTH, D_MODEL)), _full((1, D_MODEL)),
                  _full((D_MODEL, N_EXPERTS)), _full((1, N_EXPERTS)), _full((tm, tm)), _full((1, N_EXPERTS))]
        + prev_specs,
        out_specs=(row(D_MODEL), any_spec, ti_spec, row(LANES), _full((1, N_EXPERTS))),
        out_shape=(jax.ShapeDtypeStruct((t, D_MODEL), F32),
                   jax.ShapeDtypeStruct((t_total // SUBLANES, SUBLANES, LANE_TILES, LANES), F32),
                   ti_shape, jax.ShapeDtypeStruct((t, LANES), F32),
                   jax.ShapeDtypeStruct((1, N_EXPERTS), F32)),
        scratch_shapes=scratch,
        input_output_aliases=aliases,
        compiler_params=_cparams(("arbitrary",), VMEM_LIMIT),
        name="cross_attn_router",
    )(qc, mk, mv, h1, w_co, g_moe, w_router, b_router, tri, counts_in, *prev_args)


def _sc_gather(table, idx):
    n = idx.shape[0]
    per_worker = n // SC_WORKERS
    pieces = per_worker // SC_ROWS
    assert pieces * SC_ROWS * SC_WORKERS == n
    mesh = plsc.VectorSubcoreMesh(core_axis_name="c", subcore_axis_name="s")

    @functools.partial(pl.kernel, mesh=mesh, out_type=jax.ShapeDtypeStruct((n, LANE_TILES, LANES), F32),
                       scratch_types=[pltpu.VMEM((SC_ROWS,), jnp.int32), pltpu.VMEM((SC_ROWS, LANE_TILES, LANES), F32),
                                      pltpu.SemaphoreType.DMA],
                       name="moe_row_gather")
    def gather(table_hbm, idx_hbm, out_hbm, idx_v, rows_v, sem):
        worker = lax.axis_index("s") * SC_CORES + lax.axis_index("c")
        base = worker * per_worker

        @pl.loop(0, pieces)
        def _(piece):
            off = base + piece * SC_ROWS
            pltpu.sync_copy(idx_hbm.at[pl.ds(off, SC_ROWS)], idx_v)
            pltpu.async_copy(table_hbm.at[idx_v], rows_v, sem).wait()
            pltpu.sync_copy(rows_v, out_hbm.at[pl.ds(off, SC_ROWS)])

    return gather(table, idx)


def _sc_scatter(rows, dest_t, n_out):
    n = rows.shape[0]
    per_worker = n // SC_WORKERS
    step = SC_ROWS // 2
    pieces = per_worker // step
    assert pieces * step * SC_WORKERS == n
    mesh = plsc.VectorSubcoreMesh(core_axis_name="c", subcore_axis_name="s")

    @functools.partial(pl.kernel, mesh=mesh, out_type=jax.ShapeDtypeStruct((n_out, LANE_TILES, LANES), F32),
                       scratch_types=[pltpu.VMEM((TOP_K, step), jnp.int32), pltpu.VMEM((step, LANE_TILES, LANES), F32),
                                      pltpu.SemaphoreType.DMA, pltpu.SemaphoreType.DMA],
                       name="moe_row_scatter")
    def scatter(rows_hbm, idx_hbm, out_hbm, idx_v, rows_v, sem_in, sem_out):
        worker = lax.axis_index("s") * SC_CORES + lax.axis_index("c")
        base = worker * per_worker

        @pl.loop(0, pieces)
        def _(piece):
            off = base + piece * step
            loads = [pltpu.async_copy(rows_hbm.at[pl.ds(off, step)], rows_v, sem_in)]
            loads += [pltpu.async_copy(idx_hbm.at[k, pl.ds(off, step)], idx_v.at[k], sem_in) for k in range(TOP_K)]
            for c in loads:
                c.wait()
            stores = [pltpu.async_copy(rows_v, out_hbm.at[idx_v.at[k]], sem_out) for k in range(TOP_K)]
            for c in stores:
                c.wait()

    return scatter(rows, dest_t)


def _expert_block_kernel(blk_e_ref, w_idx_ref, nused_ref, xs_hbm, wg_ref, bg_ref, wu_ref, bu_ref, wd_ref, bd_ref,
                         ys_hbm, wg_bf, wu_bf, wd_bf, xbuf, ybuf, xsem, ysem):
    i = pl.program_id(0)
    n_used = nused_ref[0]
    active = i < n_used
    slot = i % 2
    tiles = MOE_BLOCK // SUBLANES

    def x_copies(blk, sl):
        return [pltpu.make_async_copy(xs_hbm.at[pl.ds(blk * tiles, tiles), s], xbuf.at[sl, :, :, s, :], xsem.at[sl])
                for s in range(SUBLANES)]

    def y_copies(blk, sl):
        return [pltpu.make_async_copy(ybuf.at[sl, :, :, s, :], ys_hbm.at[pl.ds(blk * tiles, tiles), s], ysem.at[sl])
                for s in range(SUBLANES)]

    @pl.when((i == 0) & active)
    def _():
        for c in x_copies(0, 0):
            c.start()

    prev_e = blk_e_ref[jnp.maximum(i - 1, 0)]
    new_expert = (i == 0) | (blk_e_ref[i] != prev_e)

    @pl.when(active & new_expert)
    def _():
        wg_bf[...] = wg_ref[0].astype(BF16)
        wu_bf[...] = wu_ref[0].astype(BF16)
        wd_bf[...] = wd_ref[0].astype(BF16)

    @pl.when(active)
    def _():
        for c in x_copies(i, slot):
            c.wait()

        @pl.when(i + 1 < n_used)
        def _():
            for c in x_copies(i + 1, 1 - slot):
                c.start()

        x = _load_row_tiles(xbuf.at[slot]).astype(BF16)
        gate = jnp.minimum(jnp.dot(x, wg_bf[...], preferred_element_type=F32) + bg_ref[0], SWIGLU_LIMIT)
        up = jnp.clip(jnp.dot(x, wu_bf[...], preferred_element_type=F32) + bu_ref[0], -SWIGLU_LIMIT, SWIGLU_LIMIT)
        h = gate * jax.nn.sigmoid(SWIGLU_ALPHA * gate) * (up + 1.0)
        y = jnp.dot(h.astype(BF16), wd_bf[...], preferred_element_type=F32) + bd_ref[0]

        @pl.when(i >= 1)
        def _():
            for c in y_copies(i - 1, 1 - slot):
                c.wait()

        _store_row_tiles(ybuf.at[slot], y)
        _start_alternating(y_copies(i, slot))

        @pl.when(i == n_used - 1)
        def _():
            for c in y_copies(i, slot):
                c.wait()


def _experts_blocks(blk_e, w_idx, n_used, xs, w_gate, b_gate, w_up, b_up, w_down, b_down):
    n_slots = xs.shape[0]
    n_blocks = n_slots // MOE_BLOCK
    d_ff = w_gate.shape[-1]
    tiles = MOE_BLOCK // SUBLANES
    wspec = lambda a, b: pl.BlockSpec((1, a, b), lambda i, be, wi, nu: (wi[i], 0, 0))
    bspec = lambda b: pl.BlockSpec((1, 1, b), lambda i, be, wi, nu: (be[i], 0, 0))
    any_spec = pl.BlockSpec(memory_space=pl.ANY)
    grid_spec = pltpu.PrefetchScalarGridSpec(
        num_scalar_prefetch=3,
        grid=(n_blocks,),
        in_specs=[any_spec, wspec(D_MODEL, d_ff), bspec(d_ff), wspec(D_MODEL, d_ff), bspec(d_ff),
                  wspec(d_ff, D_MODEL), bspec(D_MODEL)],
        out_specs=any_spec,
        scratch_shapes=[pltpu.VMEM((D_MODEL, d_ff), BF16), pltpu.VMEM((D_MODEL, d_ff), BF16),
                        pltpu.VMEM((d_ff, D_MODEL), BF16),
                        pltpu.VMEM((2, tiles, LANE_TILES, SUBLANES, LANES), F32),
                        pltpu.VMEM((2, tiles, LANE_TILES, SUBLANES, LANES), F32),
                        pltpu.SemaphoreType.DMA((2,)), pltpu.SemaphoreType.DMA((2,))],
    )
    ys = pl.pallas_call(
        _expert_block_kernel,
        grid_spec=grid_spec,
        out_shape=jax.ShapeDtypeStruct((n_slots // SUBLANES, SUBLANES, LANE_TILES, LANES), F32),
        compiler_params=_cparams(("arbitrary",), VMEM_LIMIT),
        name="moe_experts",
    )(blk_e, w_idx, n_used, xs.reshape(n_slots // SUBLANES, SUBLANES, LANE_TILES, LANES), w_gate,
      b_gate.reshape(N_EXPERTS, 1, d_ff), w_up, b_up.reshape(N_EXPERTS, 1, d_ff), w_down,
      b_down.reshape(N_EXPERTS, 1, D_MODEL))
    return ys.reshape(n_slots, LANE_TILES, LANES)


def _combine_kernel(yu_hbm, h_ref, tw_ref, g_ref, *rest, tm):
    o_ref, buf, sem = rest[-3:]
    i = pl.program_id(0)
    n = pl.num_programs(0)
    slot = i % 2
    tiles = tm // SUBLANES

    def copies(step, sl):
        t0 = step * tiles
        return [pltpu.make_async_copy(yu_hbm.at[k, pl.ds(t0, tiles), s], buf.at[sl, k, :, :, s, :], sem.at[sl])
                for s in range(SUBLANES) for k in range(TOP_K)]

    @pl.when(i == 0)
    def _():
        _start_alternating(copies(0, 0))

    @pl.when(i + 1 < n)
    def _():
        _start_alternating(copies(i + 1, 1 - slot))

    for c in copies(i, slot):
        c.wait()
    tw = tw_ref[...]
    y = h_ref[...]
    for k in range(TOP_K):
        y = y + tw[:, k:k + 1] * _load_row_tiles(buf.at[slot, k])
    o_ref[...] = _rms(y, g_ref[...])


def _combine(yu, h2, tw, g_final, tm, row_off, y_prev=None):
    t = h2.shape[0]
    n_rows = yu.shape[1] * SUBLANES
    blk0 = row_off // tm
    row = lambda w: pl.BlockSpec((tm, w), lambda i: (i + blk0, 0))
    args, in_specs, aliases = [yu, h2, tw, g_final], [pl.BlockSpec(memory_space=pl.ANY), row(D_MODEL), row(LANES),
                                                      _full((1, D_MODEL))], {}
    if y_prev is not None:
        args.append(y_prev)
        in_specs.append(pl.BlockSpec(memory_space=pl.ANY))
        aliases = {4: 0}
    return pl.pallas_call(
        functools.partial(_combine_kernel, tm=tm),
        grid=(n_rows // tm,),
        in_specs=in_specs,
        out_specs=row(D_MODEL),
        out_shape=jax.ShapeDtypeStruct((t, D_MODEL), F32),
        scratch_shapes=[pltpu.VMEM((2, TOP_K, tm // SUBLANES, LANE_TILES, SUBLANES, LANES), F32),
                        pltpu.SemaphoreType.DMA((2,))],
        input_output_aliases=aliases,
        compiler_params=_cparams(("arbitrary",), VMEM_LIMIT),
        name="moe_combine",
    )(*args)


def _route(counts, m):
    counts = counts.reshape(N_EXPERTS).astype(jnp.int32)
    padded = (counts + MOE_BLOCK - 1) // MOE_BLOCK * MOE_BLOCK
    pad_end = jnp.cumsum(padded)
    pad_start = pad_end - padded
    n_blocks = -(-(m + N_EXPERTS * (MOE_BLOCK - 1)) // MOE_BLOCK)
    blk_start = jnp.arange(n_blocks, dtype=jnp.int32) * MOE_BLOCK
    owner = lambda start: jnp.minimum(jnp.sum((pad_end[None, :] <= start[:, None]).astype(jnp.int32), axis=1),
                                      N_EXPERTS - 1).astype(jnp.int32)
    n_used = pad_end[-1] // MOE_BLOCK
    last_start = jnp.maximum(n_used - 1, 0) * MOE_BLOCK
    blk_e = owner(jnp.minimum(blk_start, last_start))
    first = jnp.concatenate([jnp.ones((1,), bool), blk_e[1:] != blk_e[:-1]])
    region_end = jnp.sum(jnp.where(blk_e[:, None] == jnp.arange(N_EXPERTS)[None, :], pad_end[None, :], 0), axis=1)
    w_idx = jnp.where(first, blk_e, owner(jnp.minimum(region_end, last_start))).astype(jnp.int32)
    return pad_start, blk_e, w_idx, n_used.astype(jnp.int32).reshape(1), n_blocks * MOE_BLOCK


def _dest(ti, pad_start):
    if ti.shape[0] != 2 * TOP_K:
        ti = ti[:, :2 * TOP_K].T
    e = ti[:TOP_K]
    start = jnp.sum(jnp.where(e[None] == jnp.arange(N_EXPERTS, dtype=jnp.int32)[:, None, None],
                              pad_start[:, None, None], 0), axis=0)
    return start + ti[TOP_K:]


def _block_diag(blocks):
    g, a, b = blocks.shape
    eye = jnp.eye(g, dtype=blocks.dtype)
    return (blocks[:, :, None, :] * eye[:, None, :, None]).reshape(g * a, g * b)


def kernel(x_prompt, x_sample, mem_prompt, cache_win_k, cache_win_v, state_ssm_re, state_ssm_im, cache_mem_k,
           cache_mem_v, g_mix, w_in, sink, lam_re, lam_im, log_dt, b_re, b_im, c_re, c_im, d_skip, w_glu,
           g_attn_out, g_ssm_out, w_out, g_ca, g_mem, w_mk, w_mv, w_cq, w_co, g_moe, w_router, b_router,
           w_gate, b_gate, w_up, b_up, w_down, b_down, g_final):
    depth = g_mix.shape[0]
    assert depth == 1
    bp, lp, _ = x_prompt.shape
    bs, ls, _ = x_sample.shape
    tp, ts = bp * lp, bs * ls

    w_in_bf = w_in[0].astype(BF16)
    w_att_bf = w_out[0, :ATTN_WIDTH].astype(BF16)
    w_ssm_bf = w_out[0, ATTN_WIDTH:].astype(BF16)
    w_cq_bf = w_cq[0].astype(BF16)
    w_co_bf = w_co[0].astype(BF16)
    w_mkv_bf = jnp.concatenate([w_mk[0], w_mv[0]], axis=1).astype(BF16)
    w_glu_bf = w_glu[0].astype(BF16)
    row = lambda a: a.reshape(1, -1)

    a_re, a_im, bb_re, bb_im = _discretize(lam_re[0], lam_im[0], log_dt[0], b_re[0].transpose(0, 2, 1),
                                           b_im[0].transpose(0, 2, 1))
    hg = SSM_GROUPS // 2
    wb = jnp.stack([jnp.concatenate([_block_diag(bb_re[h * hg:(h + 1) * hg]),
                                     _block_diag(bb_im[h * hg:(h + 1) * hg])], axis=1) for h in range(2)]).astype(BF16)
    cre_t = c_re[0].transpose(0, 2, 1)
    cim_t = c_im[0].transpose(0, 2, 1)
    wc = jnp.stack([jnp.concatenate([_block_diag(cre_t[h * hg:(h + 1) * hg]),
                                     -_block_diag(cim_t[h * hg:(h + 1) * hg])], axis=0) for h in range(2)]).astype(BF16)
    ssm_w = (wb, wc, a_re.reshape(1, SSM_LANES), a_im.reshape(1, SSM_LANES), row(d_skip[0]), w_glu_bf,
             row(g_ssm_out[0]))

    xp = x_prompt.reshape(tp, D_MODEL)
    xs_ = x_sample.reshape(ts, D_MODEL)

    qp, kp, vp, up = _inproj(xp, row(g_mix[0]), w_in_bf, 512)
    qs, ks, vs, us = _inproj(xs_, row(g_mix[0]), w_in_bf, 256)

    att_p = _attn_prompt(sink[0], qp, kp, vp, row(g_attn_out[0]), bp, lp)
    ck = cache_win_k[0].reshape(bs, WINDOW, KV_WIDTH)
    cv = cache_win_v[0].reshape(bs, WINDOW, KV_WIDTH)
    att_s = _attn_sample(sink[0], qs, ks, vs, ck, cv, row(g_attn_out[0]), bs, ls, 16)

    zero_state = jnp.zeros((bp, SSM_LANES), F32)
    ssm_p, p_sre, p_sim = _ssm(up.reshape(bp, lp, SSM_WIDTH), zero_state, zero_state, ssm_w, 32)
    ssm_s, s_sre, s_sim = _ssm(us.reshape(bs, ls, SSM_WIDTH), state_ssm_re[0].reshape(bs, SSM_LANES),
                               state_ssm_im[0].reshape(bs, SSM_LANES), ssm_w, ls)

    h1p, qcp = _outproj(xp, att_p, ssm_p.reshape(tp, SSM_WIDTH), w_att_bf, w_ssm_bf, row(g_ca[0]), w_cq_bf, 512)
    h1s, qcs = _outproj(xs_, att_s, ssm_s.reshape(ts, SSM_WIDTH), w_att_bf, w_ssm_bf, row(g_ca[0]), w_cq_bf, 256)

    mk_p, mv_p = _memkv(mem_prompt.reshape(bp * N_MEM, D_MODEL), row(g_mem[0]), w_mkv_bf, 512)

    no_counts = jnp.zeros((1, N_EXPERTS), F32)
    w_router_bf = w_router[0].astype(BF16)
    tt = tp + ts
    h2p, xn, tip, twp, cnt_p = _xattn(qcp, mk_p.reshape(bp, N_MEM, CA_WIDTH), mv_p.reshape(bp, N_MEM, CA_WIDTH),
                                      h1p, w_co_bf, row(g_moe[0]), w_router_bf, row(b_router[0]), no_counts,
                                      1, 512, tt, 0)
    h2s, xn, tis, tws, cnt = _xattn(qcs, cache_mem_k[0], cache_mem_v[0], h1s, w_co_bf, row(g_moe[0]),
                                    w_router_bf, row(b_router[0]), cnt_p, 8, ls, tt, tp, xn_prev=xn)

    m = tt * TOP_K
    pad_start, blk_e, w_idx, n_used, n_slots = _route(cnt, m)
    dest = jnp.concatenate([_dest(tip, pad_start), _dest(tis, pad_start)], axis=1)
    xs = _sc_scatter(xn.reshape(tt, LANE_TILES, LANES), dest, n_slots)
    ys = _experts_blocks(blk_e, w_idx, n_used, xs, w_gate[0], b_gate[0], w_up[0], b_up[0], w_down[0], b_down[0])
    def gathered(t0, t1):
        rows = _sc_gather(ys, dest[:, t0:t1].reshape(-1))
        return rows.reshape(TOP_K, (t1 - t0) // SUBLANES, SUBLANES, LANE_TILES, LANES)

    y_prompt = None
    chunk = tp // COMBINE_CHUNKS
    for c in range(COMBINE_CHUNKS):
        y_prompt = _combine(gathered(c * chunk, (c + 1) * chunk), h2p, twp, row(g_final), 256, c * chunk,
                            y_prev=y_prompt)
    y_prompt = y_prompt.reshape(bp, lp, D_MODEL)
    y_sample = _combine(gathered(tp, tt), h2s, tws, row(g_final), 256, 0).reshape(bs, ls, D_MODEL)

    kp4 = kp.reshape(bp, lp, N_KV_HEADS, HEAD_DIM)
    vp4 = vp.reshape(bp, lp, N_KV_HEADS, HEAD_DIM)
    p_win_k = kp4[:, -WINDOW:][None]
    p_win_v = vp4[:, -WINDOW:][None]
    s_win_k = jnp.concatenate([cache_win_k[0], ks.reshape(bs, ls, N_KV_HEADS, HEAD_DIM)], axis=1)[:, -WINDOW:][None]
    s_win_v = jnp.concatenate([cache_win_v[0], vs.reshape(bs, ls, N_KV_HEADS, HEAD_DIM)], axis=1)[:, -WINDOW:][None]
    st = lambda a, b: a.reshape(1, b, SSM_GROUPS, SSM_STATE)
    p_mem_k = mk_p.reshape(1, bp, N_MEM, CA_HEADS, CA_HEAD_DIM)
    p_mem_v = mv_p.reshape(1, bp, N_MEM, CA_HEADS, CA_HEAD_DIM)
    return (y_prompt, y_sample, p_win_k, p_win_v, st(p_sre, bp), st(p_sim, bp), p_mem_k, p_mem_v,
            s_win_k, s_win_v, st(s_sre, bs), st(s_sim, bs))
```
